```python
import jax, jax.numpy as jnp
from jax import lax
import numpy as np

D_MODEL = 1024
BATCH = 8
SEQ = 8192
DEPTH = 4

N_META = 16
BLOCK = 128
PAD = BLOCK - N_META
MLA_HEADS = 8
MLA_NOPE = 64
MLA_ROPE = 32
MLA_QK = MLA_NOPE + MLA_ROPE
MLA_V = 64
Q_LORA = 384
KV_LORA = 256
ROPE_BASE = 10000.0
FOX_HEADS = 8
FOX_DIM = 64
CONV_WIDTH = 3
D_FF = 4 * D_MODEL
EPS = 1e-6
NEG = -1e30
N_EVEN = (DEPTH + 1) // 2
N_ODD = DEPTH // 2
ATTN_SPLITS = (Q_LORA, KV_LORA, MLA_ROPE, FOX_HEADS * FOX_DIM, FOX_HEADS * FOX_DIM, FOX_HEADS * FOX_DIM, FOX_HEADS)
ATTN_IN = Q_LORA + KV_LORA + MLA_ROPE + 3 * FOX_HEADS * FOX_DIM + FOX_HEADS
MIX_OUT = MLA_HEADS * MLA_V + FOX_HEADS * FOX_DIM

kernel_name = "hybrid_mla_fox_shortconv_trunk"


def _offsets(sizes):
    out, acc = [], 0
    for s in sizes[:-1]:
        acc += s
        out.append(acc)
    return out


def rms_norm(x, g):
    xf = x.astype(jnp.float32)
    y = xf * lax.rsqrt(jnp.mean(xf * xf, axis=-1, keepdims=True) + EPS)
    return (y * g.astype(jnp.float32)).astype(x.dtype)


def rope_tables(length):
    pos = jnp.arange(length, dtype=jnp.float32)
    inv_freq = ROPE_BASE ** (-jnp.arange(0, MLA_ROPE, 2, dtype=jnp.float32) / MLA_ROPE)
    ang = pos[:, None] * inv_freq[None, :]
    return jnp.cos(ang), jnp.sin(ang)


def rope_tail(x, cos, sin):
    x_nope = x[..., :MLA_NOPE]
    xr = x[..., MLA_NOPE:].astype(jnp.float32)
    x1, x2 = xr[..., : MLA_ROPE // 2], xr[..., MLA_ROPE // 2:]
    c, s = cos[None, :, None, :], sin[None, :, None, :]
    rot = jnp.concatenate([x1 * c - x2 * s, x2 * c + x1 * s], axis=-1).astype(x.dtype)
    return jnp.concatenate([x_nope, rot], axis=-1)


def pad_front(x):
    return jnp.pad(x, [(0, 0), (PAD, 0)] + [(0, 0)] * (x.ndim - 2))


def blocked_causal_attention(q, k, v, scale, cum_log_f=None):
    b, lp, h, dk = q.shape
    nb = lp // BLOCK
    key_pos = jnp.arange(lp)
    qb = q.reshape(b, nb, BLOCK, h, dk).transpose(1, 0, 2, 3, 4)
    use_decay = cum_log_f is not None
    if use_decay:
        f_bh = cum_log_f.transpose(0, 2, 1)
        f_q = f_bh.reshape(b, h, nb, BLOCK).transpose(2, 0, 1, 3)
        xs = (jnp.arange(nb), qb, f_q)
    else:
        xs = (jnp.arange(nb), qb)

    def one_block(args):
        if use_decay:
            i, q_blk, fq = args
        else:
            i, q_blk = args
        s = jnp.einsum('bqhd,bkhd->bhqk', q_blk, k, preferred_element_type=jnp.float32) * scale
        if use_decay:
            s = s + fq[..., :, None] - f_bh[:, :, None, :]
        q_pos = i * BLOCK + jnp.arange(BLOCK)
        mask = (key_pos[None, :] <= q_pos[:, None]) & (key_pos[None, :] >= PAD)
        s = jnp.where(mask[None, None], s, NEG)
        p = jax.nn.softmax(s, axis=-1)
        return jnp.einsum('bhqk,bkhd->bqhd', p.astype(v.dtype), v)

    out = lax.map(one_block, xs)
    return out.transpose(1, 0, 2, 3, 4).reshape(b, lp, h, v.shape[-1])


def attention_mixer(h, cos, sin, w_in, g_cq, w_uq, g_ckv, w_ukv, g_q_mla, g_k_mla,
                    g_q_fox, g_k_fox, b_forget, w_out):
    b, l, _ = h.shape
    z = h @ w_in
    c_q, c_kv, k_pe, fq, fk, fv, f_logit = jnp.split(z, _offsets(ATTN_SPLITS), axis=-1)

    q = (rms_norm(c_q, g_cq) @ w_uq).reshape(b, l, MLA_HEADS, MLA_QK)
    kv = (rms_norm(c_kv, g_ckv) @ w_ukv).reshape(b, l, MLA_HEADS, MLA_NOPE + MLA_V)
    k_nope, v_mla = kv[..., :MLA_NOPE], kv[..., MLA_NOPE:]
    k_rope = jnp.broadcast_to(k_pe[:, :, None, :], (b, l, MLA_HEADS, MLA_ROPE))
    k = jnp.concatenate([k_nope, k_rope], axis=-1)
    q = rope_tail(rms_norm(q, g_q_mla), cos, sin)
    k = rope_tail(rms_norm(k, g_k_mla), cos, sin)
    o_mla = blocked_causal_attention(pad_front(q), pad_front(k), pad_front(v_mla),
                                     MLA_QK ** -0.5)[:, PAD:]
    o_mla = o_mla.reshape(b, l, MLA_HEADS * MLA_V)

    qf = rms_norm(fq.reshape(b, l, FOX_HEADS, FOX_DIM), g_q_fox)
    kf = rms_norm(fk.reshape(b, l, FOX_HEADS, FOX_DIM), g_k_fox)
    vf = fv.reshape(b, l, FOX_HEADS, FOX_DIM)
    log_f = jax.nn.log_sigmoid(f_logit.astype(jnp.float32) + b_forget.astype(jnp.float32))
    cum_log_f = jnp.cumsum(pad_front(log_f), axis=1)
    o_fox = blocked_causal_attention(pad_front(qf), pad_front(kf), pad_front(vf),
                                     FOX_DIM ** -0.5, cum_log_f)[:, PAD:]
    o_fox = o_fox.reshape(b, l, FOX_HEADS * FOX_DIM)

    return jnp.concatenate([o_mla, o_fox], axis=-1) @ w_out


def short_conv_mixer(h, w_in, conv_w, w_out):
    z = h @ w_in
    gate_b, gate_c, u = jnp.split(z, 3, axis=-1)
    g = gate_c * u
    y = lax.conv_general_dilated(
        g, conv_w[:, None, :].astype(g.dtype), window_strides=(1,),
        padding=[(CONV_WIDTH - 1, 0)], dimension_numbers=('NWC', 'WIO', 'NWC'),
        feature_group_count=D_MODEL)
    return (gate_b * y) @ w_out


def sq_relu_mlp(h, w_up, w_down):
    return jnp.square(jax.nn.relu(h @ w_up)) @ w_down


def _fwd_setup_inputs(seed: int = 0) -> dict:
    key = jax.random.key(seed)
    ks = iter(jax.random.split(key, 32))

    def nrm(shape, scale):
        return jax.random.normal(next(ks), shape, jnp.float32) * scale

    def gain(shape):
        return 1.0 + 0.02 * jax.random.normal(next(ks), shape, jnp.float32)

    out_scale = (2.0 * DEPTH) ** -0.5
    return {
        "x": nrm((BATCH, SEQ, D_MODEL), 1.0),
        "meta_tokens": nrm((N_META, D_MODEL), 1.0),
        "g_mix": gain((DEPTH, D_MODEL)),
        "g_mlp": gain((DEPTH, D_MODEL)),
        "w_in_attn": nrm((N_EVEN, D_MODEL, ATTN_IN), D_MODEL ** -0.5),
        "g_cq": gain((N_EVEN, Q_LORA)),
        "w_uq": nrm((N_EVEN, Q_LORA, MLA_HEADS * MLA_QK), Q_LORA ** -0.5),
        "g_ckv": gain((N_EVEN, KV_LORA)),
        "w_ukv": nrm((N_EVEN, KV_LORA, MLA_HEADS * (MLA_NOPE + MLA_V)), KV_LORA ** -0.5),
        "g_q_mla": gain((N_EVEN, MLA_QK)),
        "g_k_mla": gain((N_EVEN, MLA_QK)),
        "g_q_fox": gain((N_EVEN, FOX_DIM)),
        "g_k_fox": gain((N_EVEN, FOX_DIM)),
        "b_forget": 2.0 + nrm((N_EVEN, FOX_HEADS), 0.1),
        "w_out_attn": nrm((N_EVEN, MIX_OUT, D_MODEL), MIX_OUT ** -0.5 * out_scale),
        "w_in_conv": nrm((N_ODD, D_MODEL, 3 * D_MODEL), D_MODEL ** -0.5),
        "conv_w": nrm((N_ODD, CONV_WIDTH, D_MODEL), CONV_WIDTH ** -0.5),
        "w_out_conv": nrm((N_ODD, D_MODEL, D_MODEL), D_MODEL ** -0.5 * out_scale),
        "w_mlp_up": nrm((DEPTH, D_MODEL, D_FF), D_MODEL ** -0.5),
        "w_mlp_down": nrm((DEPTH, D_FF, D_MODEL), D_FF ** -0.5 * out_scale),
    }


def _fwd_reference(x, meta_tokens, g_mix, g_mlp, w_in_attn, g_cq, w_uq, g_ckv, w_ukv,
              g_q_mla, g_k_mla, g_q_fox, g_k_fox, b_forget, w_out_attn,
              w_in_conv, conv_w, w_out_conv, w_mlp_up, w_mlp_down):
    b = x.shape[0]
    meta = jnp.broadcast_to(meta_tokens.astype(x.dtype)[None], (b, N_META, D_MODEL))
    h = jnp.concatenate([meta, x], axis=1)
    cos, sin = rope_tables(h.shape[1])
    for layer in range(DEPTH):
        j = layer // 2
        hn = rms_norm(h, g_mix[layer])
        if layer % 2 == 0:
            h = h + attention_mixer(hn, cos, sin, w_in_attn[j], g_cq[j], w_uq[j], g_ckv[j],
                                    w_ukv[j], g_q_mla[j], g_k_mla[j], g_q_fox[j], g_k_fox[j],
                                    b_forget[j], w_out_attn[j])
        else:
            h = h + short_conv_mixer(hn, w_in_conv[j], conv_w[j], w_out_conv[j])
        h = h + sq_relu_mlp(rms_norm(h, g_mlp[layer]), w_mlp_up[layer], w_mlp_down[layer])
    return h[:, N_META:]


import jax as _jax
import jax.numpy as _jnp

TWIN_FORMAT = 'train_step'
FWD_PARAMS = ['x', 'meta_tokens', 'g_mix', 'g_mlp', 'w_in_attn', 'g_cq', 'w_uq', 'g_ckv', 'w_ukv', 'g_q_mla', 'g_k_mla', 'g_q_fox', 'g_k_fox', 'b_forget', 'w_out_attn', 'w_in_conv', 'conv_w', 'w_out_conv', 'w_mlp_up', 'w_mlp_down']
TWIN_WEIGHTS = ['meta_tokens', 'g_mix', 'g_mlp', 'w_in_attn', 'g_cq', 'w_uq', 'g_ckv', 'w_ukv', 'g_q_mla', 'g_k_mla', 'g_q_fox', 'g_k_fox', 'b_forget', 'w_out_attn', 'w_in_conv', 'conv_w', 'w_out_conv', 'w_mlp_up', 'w_mlp_down']
TWIN_DIFF_INPUT = 'x'
TWIN_INPUTS = ['x', 'meta_tokens', 'g_mix', 'g_mlp', 'w_in_attn', 'g_cq', 'w_uq', 'g_ckv', 'w_ukv', 'g_q_mla', 'g_k_mla', 'g_q_fox', 'g_k_fox', 'b_forget', 'w_out_attn', 'w_in_conv', 'conv_w', 'w_out_conv', 'w_mlp_up', 'w_mlp_down', 'loss_target', 'm_meta_tokens', 'm_g_mix', 'm_g_mlp', 'm_w_in_attn', 'm_g_cq', 'm_w_uq', 'm_g_ckv', 'm_w_ukv', 'm_g_q_mla', 'm_g_k_mla', 'm_g_q_fox', 'm_g_k_fox', 'm_b_forget', 'm_w_out_attn', 'm_w_in_conv', 'm_conv_w', 'm_w_out_conv', 'm_w_mlp_up', 'm_w_mlp_down', 'v_meta_tokens', 'v_g_mix', 'v_g_mlp', 'v_w_in_attn', 'v_g_cq', 'v_w_uq', 'v_g_ckv', 'v_w_ukv', 'v_g_q_mla', 'v_g_k_mla', 'v_g_q_fox', 'v_g_k_fox', 'v_b_forget', 'v_w_out_attn', 'v_w_in_conv', 'v_conv_w', 'v_w_out_conv', 'v_w_mlp_up', 'v_w_mlp_down']
TWIN_OUTPUTS = ['loss', 'grad_x', 'grad_meta_tokens', 'grad_g_mix', 'grad_g_mlp', 'grad_w_in_attn', 'grad_g_cq', 'grad_w_uq', 'grad_g_ckv', 'grad_w_ukv', 'grad_g_q_mla', 'grad_g_k_mla', 'grad_g_q_fox', 'grad_g_k_fox', 'grad_b_forget', 'grad_w_out_attn', 'grad_w_in_conv', 'grad_conv_w', 'grad_w_out_conv', 'grad_w_mlp_up', 'grad_w_mlp_down', 'delta_meta_tokens', 'delta_g_mix', 'delta_g_mlp', 'delta_w_in_attn', 'delta_g_cq', 'delta_w_uq', 'delta_g_ckv', 'delta_w_ukv', 'delta_g_q_mla', 'delta_g_k_mla', 'delta_g_q_fox', 'delta_g_k_fox', 'delta_b_forget', 'delta_w_out_attn', 'delta_w_in_conv', 'delta_conv_w', 'delta_w_out_conv', 'delta_w_mlp_up', 'delta_w_mlp_down', 'new_m_meta_tokens', 'new_m_g_mix', 'new_m_g_mlp', 'new_m_w_in_attn', 'new_m_g_cq', 'new_m_w_uq', 'new_m_g_ckv', 'new_m_w_ukv', 'new_m_g_q_mla', 'new_m_g_k_mla', 'new_m_g_q_fox', 'new_m_g_k_fox', 'new_m_b_forget', 'new_m_w_out_attn', 'new_m_w_in_conv', 'new_m_conv_w', 'new_m_w_out_conv', 'new_m_w_mlp_up', 'new_m_w_mlp_down', 'new_v_meta_tokens', 'new_v_g_mix', 'new_v_g_mlp', 'new_v_w_in_attn', 'new_v_g_cq', 'new_v_w_uq', 'new_v_g_ckv', 'new_v_w_ukv', 'new_v_g_q_mla', 'new_v_g_k_mla', 'new_v_g_q_fox', 'new_v_g_k_fox', 'new_v_b_forget', 'new_v_w_out_attn', 'new_v_w_in_conv', 'new_v_conv_w', 'new_v_w_out_conv', 'new_v_w_mlp_up', 'new_v_w_mlp_down']
TWIN_LEAF_KINDS = {'loss': 'loss', 'grad_x': 'grad_x', 'grad_meta_tokens': 'grad_w', 'grad_g_mix': 'grad_w', 'grad_g_mlp': 'grad_w', 'grad_w_in_attn': 'grad_w', 'grad_g_cq': 'grad_w', 'grad_w_uq': 'grad_w', 'grad_g_ckv': 'grad_w', 'grad_w_ukv': 'grad_w', 'grad_g_q_mla': 'grad_w', 'grad_g_k_mla': 'grad_w', 'grad_g_q_fox': 'grad_w', 'grad_g_k_fox': 'grad_w', 'grad_b_forget': 'grad_w', 'grad_w_out_attn': 'grad_w', 'grad_w_in_conv': 'grad_w', 'grad_conv_w': 'grad_w', 'grad_w_out_conv': 'grad_w', 'grad_w_mlp_up': 'grad_w', 'grad_w_mlp_down': 'grad_w', 'delta_meta_tokens': 'delta_w', 'delta_g_mix': 'delta_w', 'delta_g_mlp': 'delta_w', 'delta_w_in_attn': 'delta_w', 'delta_g_cq': 'delta_w', 'delta_w_uq': 'delta_w', 'delta_g_ckv': 'delta_w', 'delta_w_ukv': 'delta_w', 'delta_g_q_mla': 'delta_w', 'delta_g_k_mla': 'delta_w', 'delta_g_q_fox': 'delta_w', 'delta_g_k_fox': 'delta_w', 'delta_b_forget': 'delta_w', 'delta_w_out_attn': 'delta_w', 'delta_w_in_conv': 'delta_w', 'delta_conv_w': 'delta_w', 'delta_w_out_conv': 'delta_w', 'delta_w_mlp_up': 'delta_w', 'delta_w_mlp_down': 'delta_w', 'new_m_meta_tokens': 'new_m', 'new_m_g_mix': 'new_m', 'new_m_g_mlp': 'new_m', 'new_m_w_in_attn': 'new_m', 'new_m_g_cq': 'new_m', 'new_m_w_uq': 'new_m', 'new_m_g_ckv': 'new_m', 'new_m_w_ukv': 'new_m', 'new_m_g_q_mla': 'new_m', 'new_m_g_k_mla': 'new_m', 'new_m_g_q_fox': 'new_m', 'new_m_g_k_fox': 'new_m', 'new_m_b_forget': 'new_m', 'new_m_w_out_attn': 'new_m', 'new_m_w_in_conv': 'new_m', 'new_m_conv_w': 'new_m', 'new_m_w_out_conv': 'new_m', 'new_m_w_mlp_up': 'new_m', 'new_m_w_mlp_down': 'new_m', 'new_v_meta_tokens': 'new_v', 'new_v_g_mix': 'new_v', 'new_v_g_mlp': 'new_v', 'new_v_w_in_attn': 'new_v', 'new_v_g_cq': 'new_v', 'new_v_w_uq': 'new_v', 'new_v_g_ckv': 'new_v', 'new_v_w_ukv': 'new_v', 'new_v_g_q_mla': 'new_v', 'new_v_g_k_mla': 'new_v', 'new_v_g_q_fox': 'new_v', 'new_v_g_k_fox': 'new_v', 'new_v_b_forget': 'new_v', 'new_v_w_out_attn': 'new_v', 'new_v_w_in_conv': 'new_v', 'new_v_conv_w': 'new_v', 'new_v_w_out_conv': 'new_v', 'new_v_w_mlp_up': 'new_v', 'new_v_w_mlp_down': 'new_v'}


def _forward(args):
    return _fwd_reference(*[args[k] for k in FWD_PARAMS])


def _output_shape():
    def fwd():
        inp = _fwd_setup_inputs(0)
        return _fwd_reference(*[inp[k] for k in FWD_PARAMS])
    out = _jax.eval_shape(fwd)
    return out.shape, out.dtype

N_MICROBATCH = 1
ADAM_LR = 0.001
ADAM_B1 = 0.9
ADAM_B2 = 0.999
ADAM_EPS = 1e-08
ADAM_WD = 0.01
ADAM_STEP = 10
PER_EXAMPLE_BATCH_AXIS = {'x': 0, 'loss_target': 0}
SHARED_INPUTS = []
_WEIGHT_DTYPES = {'meta_tokens': _jnp.float32, 'g_mix': _jnp.float32, 'g_mlp': _jnp.float32, 'w_in_attn': _jnp.float32, 'g_cq': _jnp.float32, 'w_uq': _jnp.float32, 'g_ckv': _jnp.float32, 'w_ukv': _jnp.float32, 'g_q_mla': _jnp.float32, 'g_k_mla': _jnp.float32, 'g_q_fox': _jnp.float32, 'g_k_fox': _jnp.float32, 'b_forget': _jnp.float32, 'w_out_attn': _jnp.float32, 'w_in_conv': _jnp.float32, 'conv_w': _jnp.float32, 'w_out_conv': _jnp.float32, 'w_mlp_up': _jnp.float32, 'w_mlp_down': _jnp.float32}
MOMENT_SCALE = {'meta_tokens': 2.791893e-02, 'g_mix': 1.699998e+01, 'g_mlp': 2.451846e+01, 'w_in_attn': 8.738769e-01, 'g_cq': 5.255008e-02, 'w_uq': 3.981757e-02, 'g_ckv': 1.725793e+00, 'w_ukv': 8.830604e-01, 'g_q_mla': 1.919079e-01, 'g_k_mla': 1.923636e-01, 'g_q_fox': 4.065855e+00, 'g_k_fox': 4.059655e+00, 'b_forget': 2.660311e+01, 'w_out_attn': 3.507671e+00, 'w_in_conv': 4.620199e-01, 'conv_w': 4.548123e+00, 'w_out_conv': 8.770296e-01, 'w_mlp_up': 8.753895e-01, 'w_mlp_down': 1.427796e+01}


def _to_microbatches(a, axis):
    t = _jnp.moveaxis(a, axis, 0)
    t = t.reshape((N_MICROBATCH, t.shape[0] // N_MICROBATCH) + t.shape[1:])
    return _jnp.moveaxis(t, 1, axis + 1)


def setup_inputs(seed: int = 0) -> dict:
    inp = _fwd_setup_inputs(seed)
    key = _jax.random.fold_in(_jax.random.key(seed), 7919)
    shape, _ = _output_shape()
    out = dict(inp)
    out["loss_target"] = _jax.random.normal(_jax.random.fold_in(key, 0), shape, _jnp.float32)
    for i, name in enumerate(TWIN_WEIGHTS):
        w = inp[name].astype(_jnp.float32)
        if MOMENT_SCALE is None:
            s = _jnp.sqrt(_jnp.mean(_jnp.square(w)) + 1e-30)
        else:
            s = MOMENT_SCALE[name]
        km, kv = _jax.random.split(_jax.random.fold_in(key, i + 1))
        out[name] = w
        out["m_" + name] = s * _jax.random.normal(km, w.shape, _jnp.float32)
        out["v_" + name] = (s * s) * _jax.random.uniform(kv, w.shape, _jnp.float32, 0.5, 1.5)
    if N_MICROBATCH > 1:
        for name, axis in PER_EXAMPLE_BATCH_AXIS.items():
            out[name] = _to_microbatches(out[name], axis)
    return {'x': out['x'], 'meta_tokens': out['meta_tokens'], 'g_mix': out['g_mix'], 'g_mlp': out['g_mlp'], 'w_in_attn': out['w_in_attn'], 'g_cq': out['g_cq'], 'w_uq': out['w_uq'], 'g_ckv': out['g_ckv'], 'w_ukv': out['w_ukv'], 'g_q_mla': out['g_q_mla'], 'g_k_mla': out['g_k_mla'], 'g_q_fox': out['g_q_fox'], 'g_k_fox': out['g_k_fox'], 'b_forget': out['b_forget'], 'w_out_attn': out['w_out_attn'], 'w_in_conv': out['w_in_conv'], 'conv_w': out['conv_w'], 'w_out_conv': out['w_out_conv'], 'w_mlp_up': out['w_mlp_up'], 'w_mlp_down': out['w_mlp_down'], 'loss_target': out['loss_target'], 'm_meta_tokens': out['m_meta_tokens'], 'm_g_mix': out['m_g_mix'], 'm_g_mlp': out['m_g_mlp'], 'm_w_in_attn': out['m_w_in_attn'], 'm_g_cq': out['m_g_cq'], 'm_w_uq': out['m_w_uq'], 'm_g_ckv': out['m_g_ckv'], 'm_w_ukv': out['m_w_ukv'], 'm_g_q_mla': out['m_g_q_mla'], 'm_g_k_mla': out['m_g_k_mla'], 'm_g_q_fox': out['m_g_q_fox'], 'm_g_k_fox': out['m_g_k_fox'], 'm_b_forget': out['m_b_forget'], 'm_w_out_attn': out['m_w_out_attn'], 'm_w_in_conv': out['m_w_in_conv'], 'm_conv_w': out['m_conv_w'], 'm_w_out_conv': out['m_w_out_conv'], 'm_w_mlp_up': out['m_w_mlp_up'], 'm_w_mlp_down': out['m_w_mlp_down'], 'v_meta_tokens': out['v_meta_tokens'], 'v_g_mix': out['v_g_mix'], 'v_g_mlp': out['v_g_mlp'], 'v_w_in_attn': out['v_w_in_attn'], 'v_g_cq': out['v_g_cq'], 'v_w_uq': out['v_w_uq'], 'v_g_ckv': out['v_g_ckv'], 'v_w_ukv': out['v_w_ukv'], 'v_g_q_mla': out['v_g_q_mla'], 'v_g_k_mla': out['v_g_k_mla'], 'v_g_q_fox': out['v_g_q_fox'], 'v_g_k_fox': out['v_g_k_fox'], 'v_b_forget': out['v_b_forget'], 'v_w_out_attn': out['v_w_out_attn'], 'v_w_in_conv': out['v_w_in_conv'], 'v_conv_w': out['v_conv_w'], 'v_w_out_conv': out['v_w_out_conv'], 'v_w_mlp_up': out['v_w_mlp_up'], 'v_w_mlp_down': out['v_w_mlp_down']}


def _loss(weights, diff, rest, loss_target):
    with _jax.named_scope("forward"):
        args = {**rest, TWIN_DIFF_INPUT: diff, **{k: w.astype(_WEIGHT_DTYPES[k]) for k, w in weights.items()}}
        y = _forward(args)
    with _jax.named_scope("loss_head"):
        err = _jnp.square(y.astype(_jnp.float32) - loss_target)
        return 0.5 * _jnp.sum(_jnp.mean(err, axis=-1)) if err.ndim else 0.5 * err


def _adamw(w, g, m, v):
    m = ADAM_B1 * m + (1.0 - ADAM_B1) * g
    v = ADAM_B2 * v + (1.0 - ADAM_B2) * _jnp.square(g)
    m_hat = m / (1.0 - ADAM_B1 ** ADAM_STEP)
    v_hat = v / (1.0 - ADAM_B2 ** ADAM_STEP)
    delta = -ADAM_LR * (m_hat / (_jnp.sqrt(v_hat) + ADAM_EPS) + ADAM_WD * w)
    return delta, m, v


def reference(x, meta_tokens, g_mix, g_mlp, w_in_attn, g_cq, w_uq, g_ckv, w_ukv, g_q_mla, g_k_mla, g_q_fox, g_k_fox, b_forget, w_out_attn, w_in_conv, conv_w, w_out_conv, w_mlp_up, w_mlp_down, loss_target, m_meta_tokens, m_g_mix, m_g_mlp, m_w_in_attn, m_g_cq, m_w_uq, m_g_ckv, m_w_ukv, m_g_q_mla, m_g_k_mla, m_g_q_fox, m_g_k_fox, m_b_forget, m_w_out_attn, m_w_in_conv, m_conv_w, m_w_out_conv, m_w_mlp_up, m_w_mlp_down, v_meta_tokens, v_g_mix, v_g_mlp, v_w_in_attn, v_g_cq, v_w_uq, v_g_ckv, v_w_ukv, v_g_q_mla, v_g_k_mla, v_g_q_fox, v_g_k_fox, v_b_forget, v_w_out_attn, v_w_in_conv, v_conv_w, v_w_out_conv, v_w_mlp_up, v_w_mlp_down):
    given = dict(x=x, meta_tokens=meta_tokens, g_mix=g_mix, g_mlp=g_mlp, w_in_attn=w_in_attn, g_cq=g_cq, w_uq=w_uq, g_ckv=g_ckv, w_ukv=w_ukv, g_q_mla=g_q_mla, g_k_mla=g_k_mla, g_q_fox=g_q_fox, g_k_fox=g_k_fox, b_forget=b_forget, w_out_attn=w_out_attn, w_in_conv=w_in_conv, conv_w=conv_w, w_out_conv=w_out_conv, w_mlp_up=w_mlp_up, w_mlp_down=w_mlp_down, loss_target=loss_target, m_meta_tokens=m_meta_tokens, m_g_mix=m_g_mix, m_g_mlp=m_g_mlp, m_w_in_attn=m_w_in_attn, m_g_cq=m_g_cq, m_w_uq=m_w_uq, m_g_ckv=m_g_ckv, m_w_ukv=m_w_ukv, m_g_q_mla=m_g_q_mla, m_g_k_mla=m_g_k_mla, m_g_q_fox=m_g_q_fox, m_g_k_fox=m_g_k_fox, m_b_forget=m_b_forget, m_w_out_attn=m_w_out_attn, m_w_in_conv=m_w_in_conv, m_conv_w=m_conv_w, m_w_out_conv=m_w_out_conv, m_w_mlp_up=m_w_mlp_up, m_w_mlp_down=m_w_mlp_down, v_meta_tokens=v_meta_tokens, v_g_mix=v_g_mix, v_g_mlp=v_g_mlp, v_w_in_attn=v_w_in_attn, v_g_cq=v_g_cq, v_w_uq=v_w_uq, v_g_ckv=v_g_ckv, v_w_ukv=v_w_ukv, v_g_q_mla=v_g_q_mla, v_g_k_mla=v_g_k_mla, v_g_q_fox=v_g_q_fox, v_g_k_fox=v_g_k_fox, v_b_forget=v_b_forget, v_w_out_attn=v_w_out_attn, v_w_in_conv=v_w_in_conv, v_conv_w=v_conv_w, v_w_out_conv=v_w_out_conv, v_w_mlp_up=v_w_mlp_up, v_w_mlp_down=v_w_mlp_down)
    weights = {n: given[n] for n in TWIN_WEIGHTS}
    shared = {n: given[n] for n in SHARED_INPUTS}
    per_example = {n: given[n] for n in ['x']}
    grad_fn = _jax.value_and_grad(_loss, argnums=(0, 1))

    def one_microbatch(ex, loss_target):
        ex = dict(ex)
        diff = ex.pop(TWIN_DIFF_INPUT)
        return grad_fn(weights, diff, {**shared, **ex}, loss_target)

    if N_MICROBATCH == 1:
        loss, (grad_w, grad_x) = one_microbatch(per_example, given["loss_target"])
    else:
        def body(carry, xs):
            loss_sum, grad_sum = carry
            l_k, (gw_k, gx_k) = one_microbatch(xs[0], xs[1])
            with _jax.named_scope("update"):
                return (loss_sum + l_k, _jax.tree.map(_jnp.add, grad_sum, gw_k)), gx_k

        init = (_jnp.zeros((), _jnp.float32), _jax.tree.map(_jnp.zeros_like, weights))
        (loss, grad_w), grad_x = _jax.lax.scan(body, init, (per_example, given["loss_target"]))
    with _jax.named_scope("update"):
        delta_w, new_m, new_v = {}, {}, {}
        for n in TWIN_WEIGHTS:
            delta_w[n], new_m[n], new_v[n] = _adamw(weights[n], grad_w[n], given["m_" + n], given["v_" + n])
    return (loss, grad_x, *[grad_w[n] for n in TWIN_WEIGHTS], *[delta_w[n] for n in TWIN_WEIGHTS],
            *[new_m[n] for n in TWIN_WEIGHTS], *[new_v[n] for n in TWIN_WEIGHTS])
```

```python
import functools
import math

import jax
import jax.numpy as jnp
import numpy as np
from jax import lax
from jax.experimental import pallas as pl
from jax.experimental.pallas import tpu as pltpu

F32 = jnp.float32
BF16 = jnp.bfloat16

N_DEV = 8
D_MODEL = 1024
DEPTH = 4
N_META = 16
BLOCK = 128
PAD = BLOCK - N_META
HEADS = 8
MLA_NOPE = 64
MLA_ROPE = 32
MLA_QK = MLA_NOPE + MLA_ROPE
MLA_V = 64
Q_LORA = 384
KV_LORA = 256
ROPE_BASE = 10000.0
FOX_DIM = 64
D_FF = 4 * D_MODEL
EPS = 1e-6
NEG = -1e30
LANES = 128
ATTN_IN_PAD = 2304
C_CQ, C_CKV, C_FQ, C_FK, C_FV, C_TAIL = 0, 384, 640, 1152, 1664, 2176
TAIL_F = MLA_ROPE

ADAM_LR = 0.001
ADAM_B1 = 0.9
ADAM_B2 = 0.999
ADAM_EPS = 1e-08
ADAM_WD = 0.01
ADAM_STEP = 10

VMEM_LIMIT_BYTES = 48 * 1024 * 1024
MESH = pl.DeviceIdType.MESH

SHARDED = (
    ("meta_tokens", 1), ("w_in_attn", 2), ("w_uq", 2), ("w_ukv", 2), ("w_out_attn", 1),
    ("w_in_conv", 2), ("conv_w", 2), ("w_out_conv", 1), ("w_mlp_up", 2), ("w_mlp_down", 1))
F32_GATHERED = ("meta_tokens", "conv_w")
REPLICATED = ("g_mix", "g_mlp", "g_cq", "g_ckv", "g_q_mla", "g_k_mla", "g_q_fox", "g_k_fox", "b_forget")
WEIGHTS = ("meta_tokens", "g_mix", "g_mlp", "w_in_attn", "g_cq", "w_uq", "g_ckv", "w_ukv", "g_q_mla",
           "g_k_mla", "g_q_fox", "g_k_fox", "b_forget", "w_out_attn", "w_in_conv", "conv_w",
           "w_out_conv", "w_mlp_up", "w_mlp_down")


def _params(*sem):
    return pltpu.CompilerParams(dimension_semantics=sem, vmem_limit_bytes=VMEM_LIMIT_BYTES)


def _row_tile(t):
    return 640 if (t % 640 == 0 and t > 640) else 128


def _pack(parts, rows_multiple, dtype):
    flat = jnp.concatenate([p.reshape(-1).astype(dtype) for p in parts])
    n = flat.shape[0]
    rows = -(-n // LANES)
    rows = -(-rows // rows_multiple) * rows_multiple
    return jnp.pad(flat, (0, rows * LANES - n)).reshape(rows, LANES)


def _pack_rows(parts, rows_multiple, dtype):
    flat = jnp.concatenate([p.reshape(N_DEV, -1).astype(dtype) for p in parts], axis=1)
    n = flat.shape[1]
    rows = -(-n // LANES)
    rows = -(-rows // rows_multiple) * rows_multiple
    return jnp.pad(flat, ((0, 0), (0, rows * LANES - n))).reshape(N_DEV, rows, LANES)


def _unpack(buf, shapes, lead=()):
    flat = buf.reshape(lead + (-1,))
    out, off = [], 0
    for s in shapes:
        n = math.prod(s)
        out.append(flat[..., off:off + n].reshape(lead + tuple(s)))
        off += n
    return out


def _to_shards(full, axis):
    s = full.shape
    return jnp.moveaxis(full.reshape(s[:axis] + (N_DEV, s[axis] // N_DEV) + s[axis + 1:]), axis, 0)


def _from_shards(g8, axis):
    m = jnp.moveaxis(g8, 0, axis)
    s = m.shape
    return m.reshape(s[:axis] + (s[axis] * s[axis + 1],) + s[axis + 2:])


def _all_gather(x):
    r, c_ = x.shape

    def body(x_ref, out_ref, send_sems, recv_sems, local_sem):
        x_, y_, c = lax.axis_index("x"), lax.axis_index("y"), lax.axis_index("c")
        me, sibling = (x_, y_, c), (x_, y_, 1 - c)
        chips = [(1 - x_, y_), (x_, 1 - y_), (1 - x_, 1 - y_)]

        def rows(px, py, pc):
            return out_ref.at[4 * px + 2 * py + pc]

        def copy(k, block, to, src=None):
            return pltpu.make_async_remote_copy(
                src_ref=rows(*block) if src is None else src, dst_ref=rows(*block),
                send_sem=send_sems.at[k], recv_sem=recv_sems.at[k], device_id=to, device_id_type=MESH)

        mine = pltpu.make_async_copy(x_ref, rows(*me), local_sem)
        mine.start()
        first = [copy(0, me, sibling, src=x_ref)]
        first += [copy(1 + j, me, (*chip, c), src=x_ref) for j, chip in enumerate(chips)]
        for cp in first:
            cp.start()
        passed = [copy(4 + j, (*chip, c), sibling) for j, chip in enumerate(chips)]
        for j, chip in enumerate(chips):
            copy(1 + j, (*chip, c), me).wait_recv()
            passed[j].start()
        copy(0, sibling, me).wait_recv()
        for j, chip in enumerate(chips):
            copy(4 + j, (*chip, 1 - c), me).wait_recv()
        for cp in first + passed:
            cp.wait_send()
        mine.wait()

    return pl.pallas_call(
        body, name="all_gather",
        out_shape=jax.ShapeDtypeStruct((N_DEV, r, c_), x.dtype),
        in_specs=[pl.BlockSpec(memory_space=pltpu.HBM)],
        out_specs=pl.BlockSpec(memory_space=pltpu.HBM),
        scratch_shapes=[pltpu.SemaphoreType.DMA((7,)), pltpu.SemaphoreType.DMA((7,)), pltpu.SemaphoreType.DMA],
    )(x)


def _all_to_all(x):
    _, r, c_ = x.shape

    def body(x_ref, out_ref, send_sems, recv_sems, local_sem):
        x_, y_, c = lax.axis_index("x"), lax.axis_index("y"), lax.axis_index("c")
        me = 4 * x_ + 2 * y_ + c
        mine = pltpu.make_async_copy(x_ref.at[me], out_ref.at[me], local_sem)
        mine.start()

        def peer(k):
            px = 1 - x_ if k & 4 else x_
            py = 1 - y_ if k & 2 else y_
            pc = 1 - c if k & 1 else c
            return px, py, pc

        def copy(k):
            px, py, pc = peer(k)
            return pltpu.make_async_remote_copy(
                src_ref=x_ref.at[4 * px + 2 * py + pc], dst_ref=out_ref.at[me],
                send_sem=send_sems.at[k - 1], recv_sem=recv_sems.at[k - 1], device_id=(px, py, pc),
                device_id_type=MESH)

        def arrival(k):
            px, py, pc = peer(k)
            slot = 4 * px + 2 * py + pc
            return pltpu.make_async_remote_copy(
                src_ref=x_ref.at[slot], dst_ref=out_ref.at[slot],
                send_sem=send_sems.at[k - 1], recv_sem=recv_sems.at[k - 1], device_id=(px, py, pc),
                device_id_type=MESH)

        sends = [copy(k) for k in range(1, N_DEV)]
        for cp in sends:
            cp.start()
        for k in range(1, N_DEV):
            arrival(k).wait_recv()
        for cp in sends:
            cp.wait_send()
        mine.wait()

    return pl.pallas_call(
        body, name="all_to_all",
        out_shape=jax.ShapeDtypeStruct(x.shape, x.dtype),
        in_specs=[pl.BlockSpec(memory_space=pltpu.HBM)],
        out_specs=pl.BlockSpec(memory_space=pltpu.HBM),
        scratch_shapes=[pltpu.SemaphoreType.DMA((7,)), pltpu.SemaphoreType.DMA((7,)), pltpu.SemaphoreType.DMA],
    )(x)


def _sum_blocks(x, rows_tile):
    _, r, _ = x.shape

    def body(x_ref, o_ref):
        acc = x_ref[0].astype(F32)
        for d in range(1, N_DEV):
            acc = acc + x_ref[d].astype(F32)
        o_ref[...] = acc

    return pl.pallas_call(
        body, name="sum_blocks", grid=(r // rows_tile,),
        in_specs=[pl.BlockSpec((N_DEV, rows_tile, LANES), lambda i: (0, i, 0))],
        out_specs=pl.BlockSpec((rows_tile, LANES), lambda i: (i, 0)),
        out_shape=jax.ShapeDtypeStruct((r, LANES), F32),
        compiler_params=_params("parallel"),
    )(x)


def _adamw(w, g, m, v, rows_tile):
    r = w.shape[0]
    c1 = 1.0 - ADAM_B1 ** ADAM_STEP
    c2 = 1.0 - ADAM_B2 ** ADAM_STEP

    def body(w_ref, g_ref, m_ref, v_ref, d_ref, mo_ref, vo_ref):
        g_ = g_ref[...]
        m_ = ADAM_B1 * m_ref[...] + (1.0 - ADAM_B1) * g_
        v_ = ADAM_B2 * v_ref[...] + (1.0 - ADAM_B2) * (g_ * g_)
        m_hat = m_ / c1
        v_hat = v_ / c2
        d_ref[...] = -ADAM_LR * (m_hat / (jnp.sqrt(v_hat) + ADAM_EPS) + ADAM_WD * w_ref[...])
        mo_ref[...] = m_
        vo_ref[...] = v_

    spec = pl.BlockSpec((rows_tile, LANES), lambda i: (i, 0))
    shape = jax.ShapeDtypeStruct((r, LANES), F32)
    return pl.pallas_call(
        body, name="adamw", grid=(r // rows_tile,), in_specs=[spec] * 4, out_specs=[spec] * 3,
        out_shape=[shape] * 3, compiler_params=_params("parallel"),
    )(w, g, m, v)


def _pick(n, prefs):
    for p in prefs:
        if n % p == 0:
            return p
    return n


def _mat_spec(arr, tr, tc, r_of, c_of):
    if arr.ndim == 2:
        return pl.BlockSpec((tr, tc), lambda i, j, k: (r_of(i, j, k), c_of(i, j, k)))
    per = arr.shape[2] // tc
    return pl.BlockSpec((None, tr, tc), lambda i, j, k: (c_of(i, j, k) // per, r_of(i, j, k), c_of(i, j, k) % per))


def _mm(a, b, *, ta=False, tb=False, out_dtype=F32, out_seg=None, res=None, epi=None, aux=None,
        tm=None, tn=None, tk=None, name="mm"):
    def dims(x):
        return (x.shape[0], x.shape[1]) if x.ndim == 2 else (x.shape[1], x.shape[0] * x.shape[2])
    ar, ac = dims(a)
    br, bc = dims(b)
    m, k = (ac, ar) if ta else (ar, ac)
    n, kb = (br, bc) if tb else (bc, br)
    assert k == kb, (a.shape, b.shape, ta, tb)
    tm = tm or _pick(m, (1024, 512, 384, 256, 128) if ta else (640, 512, 384, 256, 128))
    tn = tn or _pick(n, (1024, 768, 512, 384, 256, 128))
    tk = tk or _pick(k, (1024, 768, 640, 512, 384, 256, 128))
    if out_seg:
        assert (n // out_seg) % tn == 0
    for x, t in ((a, tm if ta else tk), (b, tk if tb else tn)):
        if x.ndim == 3:
            assert x.shape[2] % t == 0
    nk = k // tk
    gi, gj, gk = (lambda i, j, kk: i), (lambda i, j, kk: j), (lambda i, j, kk: kk)
    a_spec = _mat_spec(a, tk, tm, gk, gi) if ta else _mat_spec(a, tm, tk, gi, gk)
    b_spec = _mat_spec(b, tn, tk, gj, gk) if tb else _mat_spec(b, tk, tn, gk, gj)
    out_like = jnp.zeros((out_seg, 0, n // out_seg)) if out_seg else jnp.zeros((0, n))
    o_spec = _mat_spec(out_like, tm, tn, gi, gj)
    o_shape = (out_seg, m, n // out_seg) if out_seg else (m, n)
    dn = (((0 if ta else 1,), (1 if tb else 0,)), ((), ()))
    extra = [x for x in (res, aux) if x is not None]
    assert not (res is not None and aux is not None)
    n_out = 2 if epi == "relu2" else 1

    def body(*refs):
        a_ref, b_ref = refs[0], refs[1]
        x_ref = refs[2] if extra else None
        outs = refs[2 + len(extra):2 + len(extra) + n_out]
        acc_ref = refs[-1] if nk > 1 else None
        part = lax.dot_general(a_ref[...], b_ref[...], dn, preferred_element_type=F32)

        def finish(acc):
            if epi == "relu2":
                outs[0][...] = acc
                r = jnp.maximum(acc, 0.0)
                outs[1][...] = (r * r).astype(BF16)
            elif epi == "relu2_bwd":
                outs[0][...] = (acc * (2.0 * jnp.maximum(x_ref[...], 0.0))).astype(out_dtype)
            elif res is not None:
                outs[0][...] = (acc + x_ref[...]).astype(out_dtype)
            else:
                outs[0][...] = acc.astype(out_dtype)

        if nk == 1:
            finish(part)
        else:
            kk = pl.program_id(2)

            @pl.when(kk == 0)
            def _():
                acc_ref[...] = part

            @pl.when(kk > 0)
            def _():
                acc_ref[...] += part

            @pl.when(kk == nk - 1)
            def _():
                finish(acc_ref[...])

    if epi == "relu2":
        out_shape = [jax.ShapeDtypeStruct(o_shape, F32), jax.ShapeDtypeStruct(o_shape, BF16)]
        out_specs = [o_spec, o_spec]
    else:
        out_shape = jax.ShapeDtypeStruct(o_shape, out_dtype)
        out_specs = o_spec
    x_specs = [pl.BlockSpec((tm, tn), lambda i, j, kk: (i, j))] * len(extra)
    res_ = pl.pallas_call(
        body, name=name, grid=(m // tm, n // tn, nk),
        in_specs=[a_spec, b_spec] + x_specs, out_specs=out_specs, out_shape=out_shape,
        scratch_shapes=[pltpu.VMEM((tm, tn), F32)] if nk > 1 else [],
        compiler_params=_params("parallel", "parallel", "arbitrary"),
    )(a, b, *extra)
    return res_


def _rms_fwd(x, g):
    t, d = x.shape
    tm = _row_tile(t)

    def body(x_ref, g_ref, o_ref):
        x_ = x_ref[...]
        rstd = lax.rsqrt(jnp.mean(x_ * x_, axis=-1, keepdims=True) + EPS)
        o_ref[...] = (x_ * rstd * g_ref[...]).astype(BF16)

    return pl.pallas_call(
        body, name="rms_fwd", grid=(t // tm,),
        in_specs=[pl.BlockSpec((tm, d), lambda i: (i, 0)), pl.BlockSpec((1, d), lambda i: (0, 0))],
        out_specs=pl.BlockSpec((tm, d), lambda i: (i, 0)),
        out_shape=jax.ShapeDtypeStruct((t, d), BF16), compiler_params=_params("parallel"),
    )(x, g.reshape(1, d))


def _rms_bwd(x, g, dy, dres=None, want_bf16=False):
    t, d = x.shape
    tm = _row_tile(t)
    has_res = dres is not None

    def body(*refs):
        x_ref, g_ref, dy_ref = refs[:3]
        r_ref = refs[3] if has_res else None
        outs = refs[3 + has_res:]
        x_ = x_ref[...]
        rstd = lax.rsqrt(jnp.mean(x_ * x_, axis=-1, keepdims=True) + EPS)
        xh = x_ * rstd
        dy_ = dy_ref[...]
        dxh = dy_ * g_ref[...]
        dx = rstd * (dxh - xh * jnp.mean(dxh * xh, axis=-1, keepdims=True))
        if has_res:
            dx = dx + r_ref[...]
        outs[0][...] = dx
        if want_bf16:
            outs[1][...] = dx.astype(BF16)
        dg_ref = outs[-1]

        @pl.when(pl.program_id(0) == 0)
        def _():
            dg_ref[...] = jnp.zeros_like(dg_ref)

        dg_ref[...] += jnp.sum(dy_ * xh, axis=0, keepdims=True)

    row = pl.BlockSpec((tm, d), lambda i: (i, 0))
    vec = pl.BlockSpec((1, d), lambda i: (0, 0))
    out_shape = [jax.ShapeDtypeStruct((t, d), F32)] + ([jax.ShapeDtypeStruct((t, d), BF16)] if want_bf16 else []) \
        + [jax.ShapeDtypeStruct((1, d), F32)]
    out_specs = [row] + ([row] if want_bf16 else []) + [vec]
    return pl.pallas_call(
        body, name="rms_bwd", grid=(t // tm,),
        in_specs=[row, vec, row] + ([row] if has_res else []), out_specs=out_specs, out_shape=out_shape,
        compiler_params=_params("arbitrary"),
    )(x, g.reshape(1, d), dy, *([dres] if has_res else []))


def _swap_rope_halves(y):
    lane = lax.broadcasted_iota(jnp.int32, y.shape, 1)
    half = MLA_ROPE // 2
    swapped = jnp.where(lane < MLA_NOPE + half, pltpu.roll(y, LANES - half, axis=1), pltpu.roll(y, half, axis=1))
    return jnp.where((lane >= MLA_NOPE) & (lane < MLA_QK), swapped, 0.0)


def _head_norm_fwd(x, g, n_valid, scale, rope=None):
    h, t, w = x.shape
    tm = _row_tile(t)

    def body(*refs):
        x_ref, g_ref = refs[:2]
        o_ref = refs[-1]
        x_ = x_ref[...]
        rstd = lax.rsqrt(jnp.sum(x_ * x_, axis=-1, keepdims=True) * (1.0 / n_valid) + EPS)
        y = x_ * rstd * (g_ref[...] * scale)
        if rope is not None:
            y = y * refs[2][...] + _swap_rope_halves(y) * refs[3][...]
        o_ref[...] = y.astype(BF16)

    blk = pl.BlockSpec((None, tm, w), lambda hh, i: (hh, i, 0))
    tab = pl.BlockSpec((tm, w), lambda hh, i: (i, 0))
    return pl.pallas_call(
        body, name="head_norm_fwd", grid=(h, t // tm),
        in_specs=[blk, pl.BlockSpec((1, w), lambda hh, i: (0, 0))] + ([tab, tab] if rope is not None else []),
        out_specs=blk, out_shape=jax.ShapeDtypeStruct((h, t, w), BF16),
        compiler_params=_params("parallel", "parallel"),
    )(x, g, *(rope if rope is not None else ()))


def _head_norm_bwd(x, g, dout, n_valid, scale, rope=None):
    h, t, w = x.shape
    tm = BLOCK

    def body(*refs):
        x_ref, g_ref, do_ref = refs[:3]
        dx_ref, dsum_ref, dg_ref = refs[-3:]
        dy = do_ref[...]
        if rope is not None:
            c_, s_ = refs[3][...], refs[4][...]
            ds = (dy * s_[None]).reshape(h * tm, w)
            dy = dy * c_[None] + _swap_rope_halves(ds).reshape(h, tm, w)
        x_ = x_ref[...]
        rstd = lax.rsqrt(jnp.sum(x_ * x_, axis=-1, keepdims=True) * (1.0 / n_valid) + EPS)
        xh = x_ * rstd
        dxh = dy * (g_ref[...] * scale)[None]
        dx = rstd * (dxh - xh * (jnp.sum(dxh * xh, axis=-1, keepdims=True) * (1.0 / n_valid)))
        dx_ref[...] = dx
        dsum_ref[...] = jnp.sum(dx, axis=0)

        @pl.when(pl.program_id(0) == 0)
        def _():
            dg_ref[...] = jnp.zeros_like(dg_ref)

        dg_ref[...] += scale * jnp.sum(jnp.sum(dy * xh, axis=0), axis=0, keepdims=True)

    blk = pl.BlockSpec((h, tm, w), lambda i: (0, i, 0))
    tab = pl.BlockSpec((tm, w), lambda i: (i, 0))
    vec = pl.BlockSpec((1, w), lambda i: (0, 0))
    return pl.pallas_call(
        body, name="head_norm_bwd", grid=(t // tm,),
        in_specs=[blk, vec, blk] + ([tab, tab] if rope is not None else []),
        out_specs=[blk, tab, vec],
        out_shape=[jax.ShapeDtypeStruct((h, t, w), F32), jax.ShapeDtypeStruct((t, w), F32),
                   jax.ShapeDtypeStruct((1, w), F32)],
        compiler_params=_params("arbitrary"),
    )(x, g, dout, *(rope if rope is not None else ()))


def _tri(n, upper):
    r = lax.broadcasted_iota(jnp.int32, (n, n), 0)
    c = lax.broadcasted_iota(jnp.int32, (n, n), 1)
    return ((r <= c) if upper else (r >= c)).astype(F32)


def _gate_mask(shape, row0):
    lane = lax.broadcasted_iota(jnp.int32, shape, 1)
    row = row0 + lax.broadcasted_iota(jnp.int32, shape, 0)
    return (lane >= TAIL_F) & (lane < TAIL_F + HEADS) & (row >= PAD)


def _gate_fwd(z, bias):
    t = z.shape[0]
    tm = BLOCK
    tail = C_TAIL // LANES

    def body(z_ref, b_ref, o_ref, carry):
        i = pl.program_id(0)

        @pl.when(i == 0)
        def _():
            carry[...] = jnp.zeros_like(carry)

        x_ = z_ref[...] + b_ref[...]
        logf = jnp.minimum(x_, 0.0) - jnp.log1p(jnp.exp(-jnp.abs(x_)))
        logf = jnp.where(_gate_mask(logf.shape, i * tm), logf, 0.0)
        cum = jnp.dot(_tri(tm, False), logf, preferred_element_type=F32, precision=lax.Precision.HIGHEST) + carry[...]
        o_ref[...] = cum
        carry[...] = cum[tm - 1:tm, :]

    return pl.pallas_call(
        body, name="gate_fwd", grid=(t // tm,),
        in_specs=[pl.BlockSpec((tm, LANES), lambda i: (i, tail)), pl.BlockSpec((1, LANES), lambda i: (0, 0))],
        out_specs=pl.BlockSpec((tm, LANES), lambda i: (i, 0)),
        out_shape=jax.ShapeDtypeStruct((t, LANES), F32),
        scratch_shapes=[pltpu.VMEM((1, LANES), F32)], compiler_params=_params("arbitrary"),
    )(z, bias)


def _gate_bwd(z, bias, dcum):
    t = z.shape[0]
    tm = BLOCK
    nb = t // tm
    tail = C_TAIL // LANES

    def body(z_ref, b_ref, d_ref, o_ref, db_ref, carry):
        i = pl.program_id(0)

        @pl.when(i == 0)
        def _():
            carry[...] = jnp.zeros_like(carry)
            db_ref[...] = jnp.zeros_like(db_ref)

        rc = jnp.dot(_tri(tm, True), d_ref[...], preferred_element_type=F32, precision=lax.Precision.HIGHEST) + carry[...]
        carry[...] = rc[0:1, :]
        x_ = z_ref[...] + b_ref[...]
        sig_neg = 1.0 / (1.0 + jnp.exp(x_))
        dl = jnp.where(_gate_mask(rc.shape, (nb - 1 - i) * tm), rc * sig_neg, 0.0)
        o_ref[...] = dl
        db_ref[...] += jnp.sum(dl, axis=0, keepdims=True)

    return pl.pallas_call(
        body, name="gate_bwd", grid=(nb,),
        in_specs=[pl.BlockSpec((tm, LANES), lambda i: (nb - 1 - i, tail)), pl.BlockSpec((1, LANES), lambda i: (0, 0)),
                  pl.BlockSpec((tm, LANES), lambda i: (nb - 1 - i, 0))],
        out_specs=[pl.BlockSpec((tm, LANES), lambda i: (nb - 1 - i, 0)), pl.BlockSpec((1, LANES), lambda i: (0, 0))],
        out_shape=[jax.ShapeDtypeStruct((t, LANES), F32), jax.ShapeDtypeStruct((1, LANES), F32)],
        scratch_shapes=[pltpu.VMEM((1, LANES), F32)], compiler_params=_params("arbitrary"),
    )(z, bias, dcum)


def _pairs(nb, by_query):
    if by_query:
        pr = [(i, j) for i in range(nb) for j in range(i + 1)]
    else:
        pr = [(i, j) for j in range(nb) for i in range(j, nb)]
    return (jnp.asarray(np.array([p[0] for p in pr], np.int32)),
            jnp.asarray(np.array([p[1] for p in pr], np.int32)))


def _scores(q_ref, k_ref, fc_ref, fr_ref, i, j, tile, masked):
    s = lax.dot_general(q_ref[...], k_ref[...], (((1,), (1,)), ((), ())), preferred_element_type=F32)
    if fc_ref is not None:
        s = s + fc_ref[...] - fr_ref[...]
    if masked:
        qp = i * tile + lax.broadcasted_iota(jnp.int32, s.shape, 0)
        kp = j * tile + lax.broadcasted_iota(jnp.int32, s.shape, 1)
        s = jnp.where((kp <= qp) & (kp >= PAD), s, NEG)
    return s


def _attn_specs(tile, dk, dv, decay):
    q_of = lambda h, p, it, jt: (h, it[p], 0)
    k_of = lambda h, p, it, jt: (h, jt[p], 0)
    specs = dict(
        q=pl.BlockSpec((None, tile, dk), q_of), k=pl.BlockSpec((None, tile, dk), k_of),
        v=pl.BlockSpec((None, tile, dv), k_of), ov=pl.BlockSpec((None, tile, dv), q_of),
        col=pl.BlockSpec((None, tile, 1), q_of),
        fr=pl.BlockSpec((None, 1, tile), lambda h, p, it, jt: (h, 0, jt[p])))
    return specs


def _attn_fwd(q, k, v, fcol=None, frow=None):
    h, t, dk = q.shape
    dv = v.shape[-1]
    tile = _row_tile(t)
    nb = t // tile
    decay = fcol is not None
    it, jt = _pairs(nb, True)
    sp = _attn_specs(tile, dk, dv, decay)

    def body(it_ref, jt_ref, *refs):
        q_ref, k_ref, v_ref = refs[:3]
        fc_ref, fr_ref = (refs[3], refs[4]) if decay else (None, None)
        o_ref, lse_ref, m_sc, l_sc, acc_sc = refs[-5:]
        p = pl.program_id(1)
        i, j = it_ref[p], jt_ref[p]

        @pl.when(j == 0)
        def _():
            m_sc[...] = jnp.full_like(m_sc, NEG)
            l_sc[...] = jnp.zeros_like(l_sc)
            acc_sc[...] = jnp.zeros_like(acc_sc)

        def step(masked):
            s = _scores(q_ref, k_ref, fc_ref, fr_ref, i, j, tile, masked)
            m_prev = m_sc[...]
            m_new = jnp.maximum(m_prev, jnp.max(s, axis=-1, keepdims=True))
            alpha = jnp.exp(m_prev - m_new)
            e = jnp.exp(s - m_new)
            l_sc[...] = alpha * l_sc[...] + jnp.sum(e, axis=-1, keepdims=True)
            acc_sc[...] = alpha * acc_sc[...] + jnp.dot(e.astype(BF16), v_ref[...], preferred_element_type=F32)
            m_sc[...] = m_new

        edge = (j == i) | (j == 0)
        pl.when(edge)(lambda: step(True))
        pl.when(jnp.logical_not(edge))(lambda: step(False))

        @pl.when(j == i)
        def _():
            row = i * tile + lax.broadcasted_iota(jnp.int32, (tile, 1), 0)
            o_ref[...] = jnp.where(row >= PAD, acc_sc[...] / l_sc[...], 0.0)
            lse_ref[...] = m_sc[...] + jnp.log(l_sc[...])

    grid_spec = pltpu.PrefetchScalarGridSpec(
        num_scalar_prefetch=2, grid=(h, int(it.shape[0])),
        in_specs=[sp["q"], sp["k"], sp["v"]] + ([sp["col"], sp["fr"]] if decay else []),
        out_specs=[sp["ov"], sp["col"]],
        scratch_shapes=[pltpu.VMEM((tile, 1), F32), pltpu.VMEM((tile, 1), F32), pltpu.VMEM((tile, dv), F32)])
    return pl.pallas_call(
        body, name="attn_fwd", grid_spec=grid_spec,
        out_shape=[jax.ShapeDtypeStruct((h, t, dv), F32), jax.ShapeDtypeStruct((h, t, 1), F32)],
        compiler_params=_params("parallel", "arbitrary"),
    )(it, jt, q, k, v, *((fcol, frow) if decay else ()))


def _attn_dq(q, k, v, o, do, lse, fcol=None, frow=None):
    h, t, dk = q.shape
    dv = v.shape[-1]
    tile = _row_tile(t)
    nb = t // tile
    decay = fcol is not None
    it, jt = _pairs(nb, True)
    sp = _attn_specs(tile, dk, dv, decay)
    n_out = 2 if decay else 1

    def body(it_ref, jt_ref, *refs):
        q_ref, k_ref, v_ref, o_ref, do_ref, lse_ref = refs[:6]
        fc_ref, fr_ref = (refs[6], refs[7]) if decay else (None, None)
        outs = refs[-2 - 2 * n_out + 1:-1 - n_out]
        accs = refs[-1 - n_out:-1]
        delta_sc = refs[-1]
        p = pl.program_id(1)
        i, j = it_ref[p], jt_ref[p]

        @pl.when(j == 0)
        def _():
            for a_ in accs:
                a_[...] = jnp.zeros_like(a_)
            delta_sc[...] = jnp.sum(o_ref[...] * do_ref[...], axis=-1, keepdims=True)

        def step(masked):
            s = _scores(q_ref, k_ref, fc_ref, fr_ref, i, j, tile, masked)
            pr = jnp.exp(s - lse_ref[...])
            dp = lax.dot_general(do_ref[...].astype(BF16), v_ref[...], (((1,), (1,)), ((), ())),
                                 preferred_element_type=F32)
            ds = pr * (dp - delta_sc[...])
            accs[0][...] += jnp.dot(ds.astype(BF16), k_ref[...], preferred_element_type=F32)
            if decay:
                accs[1][...] += jnp.sum(ds, axis=-1, keepdims=True)

        edge = (j == i) | (j == 0)
        pl.when(edge)(lambda: step(True))
        pl.when(jnp.logical_not(edge))(lambda: step(False))

        @pl.when(j == i)
        def _():
            for o_, a_ in zip(outs, accs):
                o_[...] = a_[...]

    grid_spec = pltpu.PrefetchScalarGridSpec(
        num_scalar_prefetch=2, grid=(h, int(it.shape[0])),
        in_specs=[sp["q"], sp["k"], sp["v"], sp["ov"], sp["ov"], sp["col"]] + ([sp["col"], sp["fr"]] if decay else []),
        out_specs=[sp["q"]] + ([sp["col"]] if decay else []),
        scratch_shapes=[pltpu.VMEM((tile, dk), F32)] + ([pltpu.VMEM((tile, 1), F32)] if decay else [])
        + [pltpu.VMEM((tile, 1), F32)])
    out_shape = [jax.ShapeDtypeStruct((h, t, dk), F32)] + ([jax.ShapeDtypeStruct((h, t, 1), F32)] if decay else [])
    return pl.pallas_call(
        body, name="attn_dq", grid_spec=grid_spec, out_shape=out_shape,
        compiler_params=_params("parallel", "arbitrary"),
    )(it, jt, q, k, v, o, do, lse, *((fcol, frow) if decay else ()))


def _attn_dkv(q, k, v, o, do, lse, fcol=None, frow=None):
    h, t, dk = q.shape
    dv = v.shape[-1]
    tile = _row_tile(t)
    nb = t // tile
    decay = fcol is not None
    it, jt = _pairs(nb, False)
    sp = _attn_specs(tile, dk, dv, decay)

    def body(it_ref, jt_ref, *refs):
        q_ref, k_ref, v_ref, o_ref, do_ref, lse_ref = refs[:6]
        fc_ref, fr_ref = (refs[6], refs[7]) if decay else (None, None)
        n_out = 3 if decay else 2
        outs = refs[-2 * n_out:-n_out]
        accs = refs[-n_out:]
        p = pl.program_id(1)
        i, j = it_ref[p], jt_ref[p]

        @pl.when(i == j)
        def _():
            for a_ in accs:
                a_[...] = jnp.zeros_like(a_)

        def step(masked):
            s = _scores(q_ref, k_ref, fc_ref, fr_ref, i, j, tile, masked)
            pr = jnp.exp(s - lse_ref[...])
            do_ = do_ref[...]
            delta = jnp.sum(o_ref[...] * do_, axis=-1, keepdims=True)
            do_b = do_.astype(BF16)
            accs[1][...] += lax.dot_general(pr.astype(BF16), do_b, (((0,), (0,)), ((), ())),
                                            preferred_element_type=F32)
            dp = lax.dot_general(do_b, v_ref[...], (((1,), (1,)), ((), ())), preferred_element_type=F32)
            ds = pr * (dp - delta)
            accs[0][...] += lax.dot_general(ds.astype(BF16), q_ref[...], (((0,), (0,)), ((), ())),
                                            preferred_element_type=F32)
            if decay:
                accs[2][...] -= jnp.sum(ds, axis=0, keepdims=True)

        edge = (j == i) | (j == 0)
        pl.when(edge)(lambda: step(True))
        pl.when(jnp.logical_not(edge))(lambda: step(False))

        @pl.when(i == nb - 1)
        def _():
            for o_, a_ in zip(outs, accs):
                o_[...] = a_[...]

    grid_spec = pltpu.PrefetchScalarGridSpec(
        num_scalar_prefetch=2, grid=(h, int(it.shape[0])),
        in_specs=[sp["q"], sp["k"], sp["v"], sp["ov"], sp["ov"], sp["col"]] + ([sp["col"], sp["fr"]] if decay else []),
        out_specs=[sp["k"], sp["v"]] + ([sp["fr"]] if decay else []),
        scratch_shapes=[pltpu.VMEM((tile, dk), F32), pltpu.VMEM((tile, dv), F32)]
        + ([pltpu.VMEM((1, tile), F32)] if decay else []))
    out_shape = [jax.ShapeDtypeStruct((h, t, dk), F32), jax.ShapeDtypeStruct((h, t, dv), F32)] \
        + ([jax.ShapeDtypeStruct((h, 1, t), F32)] if decay else [])
    return pl.pallas_call(
        body, name="attn_dkv", grid_spec=grid_spec, out_shape=out_shape,
        compiler_params=_params("parallel", "arbitrary"),
    )(it, jt, q, k, v, o, do, lse, *((fcol, frow) if decay else ()))


CONV_COLS = 512


def _shift_down(g, prev, n):
    out = pltpu.roll(g, n, axis=0)
    row = lax.broadcasted_iota(jnp.int32, g.shape, 0)
    for r in range(n):
        out = jnp.where(row == r, prev[8 - n + r:8 - n + r + 1, :], out)
    return out


def _shift_up(g, nxt, n):
    tm = g.shape[0]
    out = pltpu.roll(g, tm - n, axis=0)
    row = lax.broadcasted_iota(jnp.int32, g.shape, 0)
    for r in range(n):
        out = jnp.where(row == tm - n + r, nxt[r:r + 1, :], out)
    return out


def _conv_fwd(z3, w):
    _, t, d = z3.shape
    tm = _row_tile(t)
    tc = CONV_COLS

    def body(z_ref, zp_ref, w_ref, o_ref):
        i = pl.program_id(1)
        g = z_ref[1] * z_ref[2]
        gp = jnp.where(i > 0, zp_ref[1] * zp_ref[2], 0.0)
        w_ = w_ref[...]
        y = w_[2:3] * g + w_[1:2] * _shift_down(g, gp, 1) + w_[0:1] * _shift_down(g, gp, 2)
        o_ref[...] = (z_ref[0] * y).astype(BF16)

    return pl.pallas_call(
        body, name="conv_fwd", grid=(d // tc, t // tm),
        in_specs=[pl.BlockSpec((3, tm, tc), lambda j, i: (0, i, j)),
                  pl.BlockSpec((3, 8, tc), lambda j, i: (0, jnp.maximum(i * (tm // 8) - 1, 0), j)),
                  pl.BlockSpec((3, tc), lambda j, i: (0, j))],
        out_specs=pl.BlockSpec((tm, tc), lambda j, i: (i, j)),
        out_shape=jax.ShapeDtypeStruct((t, d), BF16), compiler_params=_params("parallel", "parallel"),
    )(z3, z3, w)


def _conv_bwd(z3, w, dyb):
    _, t, d = z3.shape
    tm = _row_tile(t)
    tc = CONV_COLS
    ni = t // tm

    def body(z_ref, zp_ref, zn_ref, d_ref, dn_ref, w_ref, dz_ref, dw_ref):
        i = pl.program_id(1)
        gb, gc, u = z_ref[0], z_ref[1], z_ref[2]
        g = gc * u
        gp = jnp.where(i > 0, zp_ref[1] * zp_ref[2], 0.0)
        w_ = w_ref[...]
        g1, g2 = _shift_down(g, gp, 1), _shift_down(g, gp, 2)
        y = w_[2:3] * g + w_[1:2] * g1 + w_[0:1] * g2
        dyb_ = d_ref[...]
        dy = dyb_ * gb
        dyn = jnp.where(i < ni - 1, dn_ref[...] * zn_ref[0], 0.0)
        dg = w_[2:3] * dy + w_[1:2] * _shift_up(dy, dyn, 1) + w_[0:1] * _shift_up(dy, dyn, 2)
        dz_ref[0] = (dyb_ * y).astype(BF16)
        dz_ref[1] = (dg * u).astype(BF16)
        dz_ref[2] = (dg * gc).astype(BF16)

        @pl.when(i == 0)
        def _():
            dw_ref[...] = jnp.zeros_like(dw_ref)

        dw_ref[...] += jnp.concatenate([jnp.sum(dy * g2, axis=0, keepdims=True),
                                        jnp.sum(dy * g1, axis=0, keepdims=True),
                                        jnp.sum(dy * g, axis=0, keepdims=True)], axis=0)

    cur = pl.BlockSpec((3, tm, tc), lambda j, i: (0, i, j))
    return pl.pallas_call(
        body, name="conv_bwd", grid=(d // tc, ni),
        in_specs=[cur,
                  pl.BlockSpec((3, 8, tc), lambda j, i: (0, jnp.maximum(i * (tm // 8) - 1, 0), j)),
                  pl.BlockSpec((3, 8, tc), lambda j, i: (0, jnp.minimum((i + 1) * (tm // 8), t // 8 - 1), j)),
                  pl.BlockSpec((tm, tc), lambda j, i: (i, j)),
                  pl.BlockSpec((8, tc), lambda j, i: (jnp.minimum((i + 1) * (tm // 8), t // 8 - 1), j)),
                  pl.BlockSpec((3, tc), lambda j, i: (0, j))],
        out_specs=[cur, pl.BlockSpec((3, tc), lambda j, i: (0, j))],
        out_shape=[jax.ShapeDtypeStruct((3, t, d), BF16), jax.ShapeDtypeStruct((3, d), F32)],
        compiler_params=_params("parallel", "arbitrary"),
    )(z3, z3, z3, dyb, dyb, w)


def _loss_head(h, target):
    t, d = h.shape
    tm = BLOCK

    def body(h_ref, t_ref, dh_ref, dhb_ref, loss_ref):
        i = pl.program_id(0)

        @pl.when(i == 0)
        def _():
            loss_ref[...] = jnp.zeros_like(loss_ref)

        err = jnp.where(i > 0, h_ref[...] - t_ref[...], 0.0)
        dh = err * (1.0 / d)
        dh_ref[...] = dh
        dhb_ref[...] = dh.astype(BF16)
        loss_ref[...] += 0.5 * jnp.sum(jnp.sum(err * err, axis=-1, keepdims=True) * (1.0 / d), axis=0, keepdims=True)

    row = pl.BlockSpec((tm, d), lambda i: (i, 0))
    return pl.pallas_call(
        body, name="loss_head", grid=(t // tm,),
        in_specs=[row, pl.BlockSpec((tm, d), lambda i: (jnp.maximum(i - 1, 0), 0))],
        out_specs=[row, row, pl.BlockSpec((1, 1), lambda i: (0, 0))],
        out_shape=[jax.ShapeDtypeStruct((t, d), F32), jax.ShapeDtypeStruct((t, d), BF16),
                   jax.ShapeDtypeStruct((1, 1), F32)],
        compiler_params=_params("arbitrary"),
    )(h, target)


def _heads_major(x, width):
    t = x.shape[0]
    return jnp.transpose(x.reshape(t, HEADS, width), (1, 0, 2))


def _heads_minor(x):
    h, t, w = x.shape
    return jnp.transpose(x, (1, 0, 2)).reshape(t, h * w)


def _rope_tables(t):
    pos = jnp.arange(t, dtype=F32) - PAD
    inv_freq = ROPE_BASE ** (-jnp.arange(0, MLA_ROPE, 2, dtype=F32) / MLA_ROPE)
    ang = pos[:, None] * inv_freq[None, :]
    cos, sin = jnp.cos(ang), jnp.sin(ang)
    one, zero = jnp.ones((t, MLA_NOPE), F32), jnp.zeros((t, MLA_NOPE), F32)
    tail = jnp.zeros((t, LANES - MLA_QK), F32)
    return (jnp.concatenate([one, cos, cos, tail], axis=1), jnp.concatenate([zero, -sin, sin, tail], axis=1))


def _pad_lanes(x, width=LANES):
    return jnp.pad(x, [(0, 0)] * (x.ndim - 1) + [(0, width - x.shape[-1])])


def _permute_in_attn(w):
    return jnp.concatenate([w[:, :640], w[:, 672:2208], w[:, 640:672], w[:, 2208:2216],
                            jnp.zeros((w.shape[0], ATTN_IN_PAD - 2216), w.dtype)], axis=1)


def _unpermute_in_attn(dw):
    return jnp.concatenate([dw[:, :640], dw[:, 2176:2208], dw[:, 640:2176], dw[:, 2208:2216]], axis=1)


def _attn_layer_fwd(hn, wl, rope):
    t = hn.shape[0]
    z = _mm(hn, wl["w_in"], name="attn_in")
    cqn = _rms_fwd(z[:, C_CQ:C_CQ + Q_LORA], wl["g_cq"])
    ckvn = _rms_fwd(z[:, C_CKV:C_CKV + KV_LORA], wl["g_ckv"])
    qf = _mm(cqn, wl["w_uq"], name="mla_uq")
    kvf = _mm(ckvn, wl["w_ukv"], name="mla_ukv")
    kv3 = kvf.reshape(t, HEADS, MLA_NOPE + MLA_V)
    xq = _pad_lanes(_heads_major(qf, MLA_QK))
    k_pe = jnp.broadcast_to(z[None, :, C_TAIL:C_TAIL + MLA_ROPE], (HEADS, t, MLA_ROPE))
    xk = _pad_lanes(jnp.concatenate([jnp.transpose(kv3[:, :, :MLA_NOPE], (1, 0, 2)), k_pe], axis=-1))
    v_mla = jnp.transpose(kv3[:, :, MLA_NOPE:], (1, 0, 2)).astype(BF16)
    gq, gk = _pad_lanes(wl["g_q_mla"].reshape(1, -1)), _pad_lanes(wl["g_k_mla"].reshape(1, -1))
    q_mla = _head_norm_fwd(xq, gq, MLA_QK, MLA_QK ** -0.5, rope)
    k_mla = _head_norm_fwd(xk, gk, MLA_QK, 1.0, rope)
    o_mla, lse_mla = _attn_fwd(q_mla, k_mla, v_mla)
    xfq = _heads_major(z[:, C_FQ:C_FQ + HEADS * FOX_DIM], FOX_DIM)
    xfk = _heads_major(z[:, C_FK:C_FK + HEADS * FOX_DIM], FOX_DIM)
    v_fox = _heads_major(z[:, C_FV:C_FV + HEADS * FOX_DIM], FOX_DIM).astype(BF16)
    q_fox = _head_norm_fwd(xfq, wl["g_q_fox"].reshape(1, -1), FOX_DIM, FOX_DIM ** -0.5)
    k_fox = _head_norm_fwd(xfk, wl["g_k_fox"].reshape(1, -1), FOX_DIM, 1.0)
    bias = jnp.pad(wl["b_forget"].reshape(1, -1), ((0, 0), (TAIL_F, LANES - TAIL_F - HEADS)))
    cum = _gate_fwd(z, bias)
    f_ht = jnp.transpose(cum[:, TAIL_F:TAIL_F + HEADS])
    fcol, frow = f_ht[:, :, None], f_ht[:, None, :]
    o_fox, lse_fox = _attn_fwd(q_fox, k_fox, v_fox, fcol, frow)
    cat = jnp.concatenate([_heads_minor(o_mla), _heads_minor(o_fox)], axis=1).astype(BF16)
    saved = dict(z=z, cqn=cqn, ckvn=ckvn, xq=xq, xk=xk, v_mla=v_mla, q_mla=q_mla, k_mla=k_mla, o_mla=o_mla,
                 lse_mla=lse_mla, xfq=xfq, xfk=xfk, v_fox=v_fox, q_fox=q_fox, k_fox=k_fox, bias=bias,
                 fcol=fcol, frow=frow, o_fox=o_fox, lse_fox=lse_fox, cat=cat, gq=gq, gk=gk)
    return cat, saved


def _attn_layer_bwd(dcat, hn, wl, sv, rope):
    t = hn.shape[0]
    g = {}
    do_mla = _heads_major(dcat[:, :HEADS * MLA_V], MLA_V)
    do_fox = _heads_major(dcat[:, HEADS * MLA_V:], FOX_DIM)
    args = (sv["q_fox"], sv["k_fox"], sv["v_fox"], sv["o_fox"], do_fox, sv["lse_fox"], sv["fcol"], sv["frow"])
    dq_fox, dfcol = _attn_dq(*args)
    dk_fox, dv_fox, dfrow = _attn_dkv(*args)
    dcum = jnp.pad(jnp.transpose(dfrow[:, 0, :] + dfcol[:, :, 0]), ((0, 0), (TAIL_F, LANES - TAIL_F - HEADS)))
    dtail_f, dbias = _gate_bwd(sv["z"], sv["bias"], dcum)
    g["b_forget"] = dbias[0, TAIL_F:TAIL_F + HEADS]
    dxfq, _, dgq = _head_norm_bwd(sv["xfq"], wl["g_q_fox"].reshape(1, -1), dq_fox, FOX_DIM, FOX_DIM ** -0.5)
    dxfk, _, dgk = _head_norm_bwd(sv["xfk"], wl["g_k_fox"].reshape(1, -1), dk_fox, FOX_DIM, 1.0)
    g["g_q_fox"], g["g_k_fox"] = dgq[0], dgk[0]
    args = (sv["q_mla"], sv["k_mla"], sv["v_mla"], sv["o_mla"], do_mla, sv["lse_mla"])
    dq_mla, = _attn_dq(*args)
    dk_mla, dv_mla = _attn_dkv(*args)
    dxq, _, dgq = _head_norm_bwd(sv["xq"], sv["gq"], dq_mla, MLA_QK, MLA_QK ** -0.5, rope)
    dxk, dxk_sum, dgk = _head_norm_bwd(sv["xk"], sv["gk"], dk_mla, MLA_QK, 1.0, rope)
    g["g_q_mla"], g["g_k_mla"] = dgq[0, :MLA_QK], dgk[0, :MLA_QK]
    dqf = _heads_minor(dxq[:, :, :MLA_QK]).astype(BF16)
    dkvf = _heads_minor(jnp.concatenate([dxk[:, :, :MLA_NOPE], dv_mla], axis=-1)).astype(BF16)
    g["w_uq"] = _mm(sv["cqn"], dqf, ta=True, name="d_w_uq")
    g["w_ukv"] = _mm(sv["ckvn"], dkvf, ta=True, name="d_w_ukv")
    dcqn = _mm(dqf, wl["w_uq"], tb=True, name="d_cqn")
    dckvn = _mm(dkvf, wl["w_ukv"], tb=True, name="d_ckvn")
    z = sv["z"]
    dcq, dg_cq = _rms_bwd(z[:, C_CQ:C_CQ + Q_LORA], wl["g_cq"], dcqn)
    dckv, dg_ckv = _rms_bwd(z[:, C_CKV:C_CKV + KV_LORA], wl["g_ckv"], dckvn)
    g["g_cq"], g["g_ckv"] = dg_cq[0], dg_ckv[0]
    tail = jnp.concatenate([dxk_sum[:, MLA_NOPE:MLA_QK], dtail_f[:, TAIL_F:]], axis=1)
    dz = jnp.concatenate([dcq, dckv, _heads_minor(dxfq), _heads_minor(dxfk), _heads_minor(dv_fox), tail],
                         axis=1).astype(BF16)
    g["w_in"] = _mm(hn, dz, ta=True, name="d_w_in_attn")
    dhn = _mm(dz, wl["w_in"], tb=True, name="d_hn_attn")
    return dhn, g


def _local_step(x, target, w):
    seq = x.shape[0]
    t = seq + BLOCK
    rope = _rope_tables(t)
    h = jnp.concatenate([jnp.zeros((PAD, D_MODEL), F32), w["meta_tokens"], x], axis=0)
    tape = []
    for layer in range(DEPTH):
        j = layer // 2
        hn = _rms_fwd(h, w["g_mix"][layer])
        if layer % 2 == 0:
            wl = dict(w_in=w["w_in_attn"][j], g_cq=w["g_cq"][j], w_uq=w["w_uq"][j], g_ckv=w["g_ckv"][j],
                      w_ukv=w["w_ukv"][j], g_q_mla=w["g_q_mla"][j], g_k_mla=w["g_k_mla"][j],
                      g_q_fox=w["g_q_fox"][j], g_k_fox=w["g_k_fox"][j], b_forget=w["b_forget"][j])
            mixed, sv = _attn_layer_fwd(hn, wl, rope)
            h1 = _mm(mixed, w["w_out_attn"][j], res=h, name="attn_out")
        else:
            wl = None
            z3 = _mm(hn, w["w_in_conv"][j], out_seg=3, name="conv_in")
            mixed = _conv_fwd(z3, w["conv_w"][j])
            sv = dict(z3=z3)
            h1 = _mm(mixed, w["w_out_conv"][j], res=h, name="conv_out")
        hn2 = _rms_fwd(h1, w["g_mlp"][layer])
        u, act = _mm(hn2, w["w_mlp_up"][layer], epi="relu2", name="mlp_up")
        h2 = _mm(act, w["w_mlp_down"][layer], res=h1, name="mlp_down")
        tape.append(dict(h=h, hn=hn, wl=wl, sv=sv, mixed=mixed, h1=h1, hn2=hn2, u=u, act=act))
        h = h2

    dh, dh_b, loss = _loss_head(h, target)
    g = {n: [None] * (DEPTH if n in ("g_mix", "g_mlp", "w_mlp_up", "w_mlp_down") else DEPTH // 2)
         for n in WEIGHTS if n != "meta_tokens"}
    for layer in reversed(range(DEPTH)):
        j = layer // 2
        tp = tape[layer]
        g["w_mlp_down"][layer] = _mm(tp["act"], dh_b, ta=True, name="d_w_down")
        du = _mm(dh_b, w["w_mlp_down"][layer], tb=True, epi="relu2_bwd", aux=tp["u"], out_dtype=BF16, name="d_u")
        g["w_mlp_up"][layer] = _mm(tp["hn2"], du, ta=True, name="d_w_up")
        dhn2 = _mm(du, w["w_mlp_up"][layer], tb=True, name="d_hn2")
        dh1, dh1_b, dg = _rms_bwd(tp["h1"], w["g_mlp"][layer], dhn2, dres=dh, want_bf16=True)
        g["g_mlp"][layer] = dg[0]
        if layer % 2 == 0:
            g["w_out_attn"][j] = _mm(tp["mixed"], dh1_b, ta=True, name="d_w_out_attn")
            dcat = _mm(dh1_b, w["w_out_attn"][j], tb=True, name="d_cat")
            dhn, gl = _attn_layer_bwd(dcat, tp["hn"], tp["wl"], tp["sv"], rope)
            g["w_in_attn"][j] = _unpermute_in_attn(gl.pop("w_in"))
            for n, val in gl.items():
                g[n][j] = val
        else:
            g["w_out_conv"][j] = _mm(tp["mixed"], dh1_b, ta=True, name="d_w_out_conv")
            dyb = _mm(dh1_b, w["w_out_conv"][j], tb=True, name="d_yb")
            dz3, dcw = _conv_bwd(tp["sv"]["z3"], w["conv_w"][j], dyb)
            g["conv_w"][j] = dcw
            g["w_in_conv"][j] = _mm(tp["hn"], dz3, ta=True, name="d_w_in_conv")
            dhn = _mm(dz3, w["w_in_conv"][j], tb=True, name="d_hn_conv")
        dh, dh_b, dg = _rms_bwd(tp["h"], w["g_mix"][layer], dhn, dres=dh1, want_bf16=True)
        g["g_mix"][layer] = dg[0]
    grads = {n: jnp.stack(v) for n, v in g.items()}
    grads["meta_tokens"] = dh[PAD:BLOCK]
    return loss, dh[BLOCK:], grads


COMM_ROWS = 2048


def kernel(x, meta_tokens, g_mix, g_mlp, w_in_attn, g_cq, w_uq, g_ckv, w_ukv, g_q_mla, g_k_mla, g_q_fox, g_k_fox, b_forget, w_out_attn, w_in_conv, conv_w, w_out_conv, w_mlp_up, w_mlp_down, loss_target, m_meta_tokens, m_g_mix, m_g_mlp, m_w_in_attn, m_g_cq, m_w_uq, m_g_ckv, m_w_ukv, m_g_q_mla, m_g_k_mla, m_g_q_fox, m_g_k_fox, m_b_forget, m_w_out_attn, m_w_in_conv, m_conv_w, m_w_out_conv, m_w_mlp_up, m_w_mlp_down, v_meta_tokens, v_g_mix, v_g_mlp, v_w_in_attn, v_g_cq, v_w_uq, v_g_ckv, v_w_ukv, v_g_q_mla, v_g_k_mla, v_g_q_fox, v_g_k_fox, v_b_forget, v_w_out_attn, v_w_in_conv, v_conv_w, v_w_out_conv, v_w_mlp_up, v_w_mlp_down):
    args = dict(locals())
    local = {n: args[n] for n in WEIGHTS}
    mom = {n: args["m_" + n] for n in WEIGHTS}
    var = {n: args["v_" + n] for n in WEIGHTS}
    sharded = [n for n, _ in SHARDED]
    axis = dict(SHARDED)
    big = [n for n in sharded if n not in F32_GATHERED]

    shapes_big = [local[n].shape for n in big]
    shapes_f32 = [local[n].shape for n in F32_GATHERED]
    gathered_big = _all_gather(_pack([local[n] for n in big], 16, BF16))
    gathered_f32 = _all_gather(_pack([local[n] for n in F32_GATHERED], 8, F32))
    full = {n: local[n] for n in REPLICATED}
    for n, blocks in zip(big, _unpack(gathered_big, shapes_big, (N_DEV,))):
        full[n] = _from_shards(blocks, axis[n])
    for n, blocks in zip(F32_GATHERED, _unpack(gathered_f32, shapes_f32, (N_DEV,))):
        full[n] = _from_shards(blocks, axis[n])
    full["w_in_attn"] = jnp.stack([_permute_in_attn(full["w_in_attn"][j]) for j in range(DEPTH // 2)])

    loss_part, dx, grads = _local_step(x[0], loss_target[0], full)
    loss = lax.psum(loss_part[0, 0], ("x", "y", "c"))

    sent = _pack_rows([_to_shards(grads[n], axis[n]) for n in sharded], COMM_ROWS, BF16)
    g_sharded = _sum_blocks(_all_to_all(sent), COMM_ROWS)
    rep = _all_gather(_pack([grads[n] for n in REPLICATED], 8, F32))
    g_rep = _sum_blocks(rep, rep.shape[1])
    g_local = dict(zip(sharded, _unpack(g_sharded, [local[n].shape for n in sharded])))
    g_local.update(zip(REPLICATED, _unpack(g_rep, [local[n].shape for n in REPLICATED])))

    def flat(src, names, rows_multiple):
        return _pack([src[n] for n in names], rows_multiple, F32)

    upd = {}
    for names, g_flat, rows in ((sharded, g_sharded, COMM_ROWS), (REPLICATED, g_rep, g_rep.shape[0])):
        outs = _adamw(flat(local, names, rows), g_flat, flat(mom, names, rows), flat(var, names, rows), rows)
        for kind, buf in zip(("delta", "m", "v"), outs):
            upd.update({(kind, n): a for n, a in zip(names, _unpack(buf, [local[n].shape for n in names]))})

    return (loss, dx[None], *[g_local[n] for n in WEIGHTS], *[upd[("delta", n)] for n in WEIGHTS],
            *[upd[("m", n)] for n in WEIGHTS], *[upd[("v", n)] for n in WEIGHTS])
```

```python
import functools
import math

import jax
import jax.numpy as jnp
import numpy as np
from jax import lax
from jax.experimental import pallas as pl
from jax.experimental.pallas import tpu as pltpu

F32 = jnp.float32
BF16 = jnp.bfloat16

N_DEV = 8
D_MODEL = 1024
DEPTH = 4
N_META = 16
BLOCK = 128
PAD = BLOCK - N_META
HEADS = 8
MLA_NOPE = 64
MLA_ROPE = 32
MLA_QK = MLA_NOPE + MLA_ROPE
MLA_V = 64
Q_LORA = 384
KV_LORA = 256
ROPE_BASE = 10000.0
FOX_DIM = 64
D_FF = 4 * D_MODEL
EPS = 1e-6
NEG = -1e30
LANES = 128
ATTN_IN_PAD = 2304
C_CQ, C_CKV, C_FQ, C_FK, C_FV, C_TAIL = 0, 384, 640, 1152, 1664, 2176
TAIL_F = MLA_ROPE

ADAM_LR = 0.001
ADAM_B1 = 0.9
ADAM_B2 = 0.999
ADAM_EPS = 1e-08
ADAM_WD = 0.01
ADAM_STEP = 10

VMEM_LIMIT_BYTES = 48 * 1024 * 1024
MESH = pl.DeviceIdType.MESH

SHARDED = (
    ("meta_tokens", 1), ("w_in_attn", 2), ("w_uq", 2), ("w_ukv", 2), ("w_out_attn", 1),
    ("w_in_conv", 2), ("conv_w", 2), ("w_out_conv", 1), ("w_mlp_up", 2), ("w_mlp_down", 1))
F32_GATHERED = ("meta_tokens", "conv_w")
REPLICATED = ("g_mix", "g_mlp", "g_cq", "g_ckv", "g_q_mla", "g_k_mla", "g_q_fox", "g_k_fox", "b_forget")
WEIGHTS = ("meta_tokens", "g_mix", "g_mlp", "w_in_attn", "g_cq", "w_uq", "g_ckv", "w_ukv", "g_q_mla",
           "g_k_mla", "g_q_fox", "g_k_fox", "b_forget", "w_out_attn", "w_in_conv", "conv_w",
           "w_out_conv", "w_mlp_up", "w_mlp_down")


def _params(*sem):
    return pltpu.CompilerParams(dimension_semantics=sem, vmem_limit_bytes=VMEM_LIMIT_BYTES)


def _row_tile(t):
    return 640 if (t % 640 == 0 and t > 640) else 128


def _pack(parts, rows_multiple, dtype):
    flat = jnp.concatenate([p.reshape(-1).astype(dtype) for p in parts])
    n = flat.shape[0]
    rows = -(-n // LANES)
    rows = -(-rows // rows_multiple) * rows_multiple
    return jnp.pad(flat, (0, rows * LANES - n)).reshape(rows, LANES)


def _pack_rows(parts, rows_multiple, dtype):
    flat = jnp.concatenate([p.reshape(N_DEV, -1).astype(dtype) for p in parts], axis=1)
    n = flat.shape[1]
    rows = -(-n // LANES)
    rows = -(-rows // rows_multiple) * rows_multiple
    return jnp.pad(flat, ((0, 0), (0, rows * LANES - n))).reshape(N_DEV, rows, LANES)


def _unpack(buf, shapes, lead=()):
    flat = buf.reshape(lead + (-1,))
    out, off = [], 0
    for s in shapes:
        n = math.prod(s)
        out.append(flat[..., off:off + n].reshape(lead + tuple(s)))
        off += n
    return out


def _to_shards(full, axis):
    s = full.shape
    return jnp.moveaxis(full.reshape(s[:axis] + (N_DEV, s[axis] // N_DEV) + s[axis + 1:]), axis, 0)


def _from_shards(g8, axis):
    m = jnp.moveaxis(g8, 0, axis)
    s = m.shape
    return m.reshape(s[:axis] + (s[axis] * s[axis + 1],) + s[axis + 2:])


def _comm_call(body, name, xs, out_shapes):
    n = len(xs)
    hbm = pl.BlockSpec(memory_space=pltpu.HBM)
    return pl.pallas_call(
        body, name=name, out_shape=out_shapes, in_specs=[hbm] * n, out_specs=[hbm] * n,
        scratch_shapes=[pltpu.SemaphoreType.DMA((n, 7)), pltpu.SemaphoreType.DMA((n, 7)),
                        pltpu.SemaphoreType.DMA((n,))],
    )(*xs)


def _all_gather(xs):
    n = len(xs)

    def body(*refs):
        x_refs, out_refs = refs[:n], refs[n:2 * n]
        send_sems, recv_sems, local_sems = refs[2 * n:]
        x_, y_, c = lax.axis_index("x"), lax.axis_index("y"), lax.axis_index("c")
        me, sibling = (x_, y_, c), (x_, y_, 1 - c)
        chips = [(1 - x_, y_), (x_, 1 - y_), (1 - x_, 1 - y_)]

        def rows(a, px, py, pc):
            return out_refs[a].at[4 * px + 2 * py + pc]

        def copy(a, k, block, to, src=None):
            return pltpu.make_async_remote_copy(
                src_ref=rows(a, *block) if src is None else src, dst_ref=rows(a, *block),
                send_sem=send_sems.at[a, k], recv_sem=recv_sems.at[a, k], device_id=to, device_id_type=MESH)

        mine = [pltpu.make_async_copy(x_refs[a], rows(a, *me), local_sems.at[a]) for a in range(n)]
        for cp in mine:
            cp.start()
        first = []
        for a in range(n):
            first.append(copy(a, 0, me, sibling, src=x_refs[a]))
            first += [copy(a, 1 + j, me, (*chip, c), src=x_refs[a]) for j, chip in enumerate(chips)]
        for cp in first:
            cp.start()
        passed = []
        for j, chip in enumerate(chips):
            for a in range(n):
                copy(a, 1 + j, (*chip, c), me).wait_recv()
                passed.append(copy(a, 4 + j, (*chip, c), sibling))
                passed[-1].start()
        for a in range(n):
            copy(a, 0, sibling, me).wait_recv()
            for j, chip in enumerate(chips):
                copy(a, 4 + j, (*chip, 1 - c), me).wait_recv()
        for cp in first + passed:
            cp.wait_send()
        for cp in mine:
            cp.wait()

    return _comm_call(body, "all_gather", xs, [jax.ShapeDtypeStruct((N_DEV,) + x.shape, x.dtype) for x in xs])


def _all_to_all(xs):
    n = len(xs)

    def body(*refs):
        x_refs, out_refs = refs[:n], refs[n:2 * n]
        send_sems, recv_sems, local_sems = refs[2 * n:]
        x_, y_, c = lax.axis_index("x"), lax.axis_index("y"), lax.axis_index("c")
        me = 4 * x_ + 2 * y_ + c
        mine = [pltpu.make_async_copy(x_refs[a].at[me], out_refs[a].at[me], local_sems.at[a]) for a in range(n)]
        for cp in mine:
            cp.start()

        def peer(k):
            px = 1 - x_ if k & 4 else x_
            py = 1 - y_ if k & 2 else y_
            pc = 1 - c if k & 1 else c
            return px, py, pc

        def copy(a, k):
            px, py, pc = peer(k)
            return pltpu.make_async_remote_copy(
                src_ref=x_refs[a].at[4 * px + 2 * py + pc], dst_ref=out_refs[a].at[me],
                send_sem=send_sems.at[a, k - 1], recv_sem=recv_sems.at[a, k - 1], device_id=(px, py, pc),
                device_id_type=MESH)

        def arrival(a, k):
            px, py, pc = peer(k)
            slot = 4 * px + 2 * py + pc
            return pltpu.make_async_remote_copy(
                src_ref=x_refs[a].at[slot], dst_ref=out_refs[a].at[slot],
                send_sem=send_sems.at[a, k - 1], recv_sem=recv_sems.at[a, k - 1], device_id=(px, py, pc),
                device_id_type=MESH)

        sends = [copy(a, k) for k in range(1, N_DEV) for a in range(n)]
        for cp in sends:
            cp.start()
        for k in range(1, N_DEV):
            for a in range(n):
                arrival(a, k).wait_recv()
        for cp in sends:
            cp.wait_send()
        for cp in mine:
            cp.wait()

    return _comm_call(body, "all_to_all", xs, [jax.ShapeDtypeStruct(x.shape, x.dtype) for x in xs])


def _sum_blocks(x, rows_tile):
    _, r, c_ = x.shape

    def body(x_ref, o_ref):
        acc = x_ref[0].astype(F32)
        for d in range(1, N_DEV):
            acc = acc + x_ref[d].astype(F32)
        o_ref[...] = acc

    return pl.pallas_call(
        body, name="sum_blocks", grid=(r // rows_tile,),
        in_specs=[pl.BlockSpec((N_DEV, rows_tile, c_), lambda i: (0, i, 0))],
        out_specs=pl.BlockSpec((rows_tile, c_), lambda i: (i, 0)),
        out_shape=jax.ShapeDtypeStruct((r, c_), F32),
        compiler_params=_params("parallel"),
    )(x)


def _adamw(w, g, m, v, rows_tile):
    r, c_ = w.shape
    c1 = 1.0 - ADAM_B1 ** ADAM_STEP
    c2 = 1.0 - ADAM_B2 ** ADAM_STEP

    def body(w_ref, g_ref, m_ref, v_ref, d_ref, mo_ref, vo_ref):
        g_ = g_ref[...]
        m_ = ADAM_B1 * m_ref[...] + (1.0 - ADAM_B1) * g_
        v_ = ADAM_B2 * v_ref[...] + (1.0 - ADAM_B2) * (g_ * g_)
        m_hat = m_ / c1
        v_hat = v_ / c2
        d_ref[...] = -ADAM_LR * (m_hat / (jnp.sqrt(v_hat) + ADAM_EPS) + ADAM_WD * w_ref[...])
        mo_ref[...] = m_
        vo_ref[...] = v_

    spec = pl.BlockSpec((rows_tile, c_), lambda i: (i, 0))
    shape = jax.ShapeDtypeStruct((r, c_), F32)
    return pl.pallas_call(
        body, name="adamw", grid=(r // rows_tile,), in_specs=[spec] * 4, out_specs=[spec] * 3,
        out_shape=[shape] * 3, compiler_params=_params("parallel"),
    )(w, g, m, v)


def _pick(n, prefs):
    for p in prefs:
        if n % p == 0:
            return p
    return n


def _mat_spec(arr, tr, tc, r_of, c_of):
    if arr.ndim == 2:
        return pl.BlockSpec((tr, tc), lambda i, j, k: (r_of(i, j, k), c_of(i, j, k)))
    per = arr.shape[2] // tc
    return pl.BlockSpec((None, tr, tc), lambda i, j, k: (c_of(i, j, k) // per, r_of(i, j, k), c_of(i, j, k) % per))


def _mm(a, b, *, ta=False, tb=False, out_dtype=F32, out_seg=None, res=None, epi=None, aux=None,
        tm=None, tn=None, tk=None, name="mm"):
    def dims(x):
        return (x.shape[0], x.shape[1]) if x.ndim == 2 else (x.shape[1], x.shape[0] * x.shape[2])
    ar, ac = dims(a)
    br, bc = dims(b)
    m, k = (ac, ar) if ta else (ar, ac)
    n, kb = (br, bc) if tb else (bc, br)
    assert k == kb, (a.shape, b.shape, ta, tb)
    tm = tm or _pick(m, (1024, 512, 384, 256, 128) if ta else (640, 512, 384, 256, 128))
    tn = tn or _pick(n, (1024, 768, 512, 384, 256, 128))
    tk = tk or _pick(k, (1024, 768, 640, 512, 384, 256, 128))
    if out_seg:
        assert (n // out_seg) % tn == 0
    for x, t in ((a, tm if ta else tk), (b, tk if tb else tn)):
        if x.ndim == 3:
            assert x.shape[2] % t == 0
    nk = k // tk
    gi, gj, gk = (lambda j, i, kk: i), (lambda j, i, kk: j), (lambda j, i, kk: kk)
    a_spec = _mat_spec(a, tk, tm, gk, gi) if ta else _mat_spec(a, tm, tk, gi, gk)
    b_spec = _mat_spec(b, tn, tk, gj, gk) if tb else _mat_spec(b, tk, tn, gk, gj)
    out_like = jnp.zeros((out_seg, 0, n // out_seg)) if out_seg else jnp.zeros((0, n))
    o_spec = _mat_spec(out_like, tm, tn, gi, gj)
    o_shape = (out_seg, m, n // out_seg) if out_seg else (m, n)
    dn = (((0 if ta else 1,), (1 if tb else 0,)), ((), ()))
    extra = [x for x in (res, aux) if x is not None]
    assert not (res is not None and aux is not None)
    n_out = 2 if epi == "relu2" else 1

    def body(*refs):
        a_ref, b_ref = refs[0], refs[1]
        x_ref = refs[2] if extra else None
        outs = refs[2 + len(extra):2 + len(extra) + n_out]
        acc_ref = refs[-1] if nk > 1 else None
        part = lax.dot_general(a_ref[...], b_ref[...], dn, preferred_element_type=F32)

        def finish(acc):
            if epi == "relu2":
                outs[0][...] = acc
                r = jnp.maximum(acc, 0.0)
                outs[1][...] = (r * r).astype(BF16)
            elif epi == "relu2_bwd":
                outs[0][...] = (acc * (2.0 * jnp.maximum(x_ref[...], 0.0))).astype(out_dtype)
            elif res is not None:
                outs[0][...] = (acc + x_ref[...]).astype(out_dtype)
            else:
                outs[0][...] = acc.astype(out_dtype)

        if nk == 1:
            finish(part)
        else:
            kk = pl.program_id(2)

            @pl.when(kk == 0)
            def _():
                acc_ref[...] = part

            @pl.when(kk > 0)
            def _():
                acc_ref[...] += part

            @pl.when(kk == nk - 1)
            def _():
                finish(acc_ref[...])

    if epi == "relu2":
        out_shape = [jax.ShapeDtypeStruct(o_shape, F32), jax.ShapeDtypeStruct(o_shape, BF16)]
        out_specs = [o_spec, o_spec]
    else:
        out_shape = jax.ShapeDtypeStruct(o_shape, out_dtype)
        out_specs = o_spec
    x_specs = [pl.BlockSpec((tm, tn), lambda j, i, kk: (i, j))] * len(extra)
    res_ = pl.pallas_call(
        body, name=name, grid=(n // tn, m // tm, nk),
        in_specs=[a_spec, b_spec] + x_specs, out_specs=out_specs, out_shape=out_shape,
        scratch_shapes=[pltpu.VMEM((tm, tn), F32)] if nk > 1 else [],
        compiler_params=_params("parallel", "parallel", "arbitrary"),
    )(a, b, *extra)
    return res_


def _rms_fwd(x, g):
    t, d = x.shape
    tm = _row_tile(t)

    def body(x_ref, g_ref, o_ref):
        x_ = x_ref[...]
        rstd = lax.rsqrt(jnp.mean(x_ * x_, axis=-1, keepdims=True) + EPS)
        o_ref[...] = (x_ * rstd * g_ref[...]).astype(BF16)

    return pl.pallas_call(
        body, name="rms_fwd", grid=(t // tm,),
        in_specs=[pl.BlockSpec((tm, d), lambda i: (i, 0)), pl.BlockSpec((1, d), lambda i: (0, 0))],
        out_specs=pl.BlockSpec((tm, d), lambda i: (i, 0)),
        out_shape=jax.ShapeDtypeStruct((t, d), BF16), compiler_params=_params("parallel"),
    )(x, g.reshape(1, d))


def _rms_bwd(x, g, dy, dres=None, want_bf16=False):
    t, d = x.shape
    tm = _row_tile(t)
    has_res = dres is not None

    def body(*refs):
        x_ref, g_ref, dy_ref = refs[:3]
        r_ref = refs[3] if has_res else None
        outs = refs[3 + has_res:]
        x_ = x_ref[...]
        rstd = lax.rsqrt(jnp.mean(x_ * x_, axis=-1, keepdims=True) + EPS)
        xh = x_ * rstd
        dy_ = dy_ref[...]
        dxh = dy_ * g_ref[...]
        dx = rstd * (dxh - xh * jnp.mean(dxh * xh, axis=-1, keepdims=True))
        if has_res:
            dx = dx + r_ref[...]
        outs[0][...] = dx
        if want_bf16:
            outs[1][...] = dx.astype(BF16)
        dg_ref = outs[-1]

        @pl.when(pl.program_id(0) == 0)
        def _():
            dg_ref[...] = jnp.zeros_like(dg_ref)

        dg_ref[...] += jnp.sum(dy_ * xh, axis=0, keepdims=True)

    row = pl.BlockSpec((tm, d), lambda i: (i, 0))
    vec = pl.BlockSpec((1, d), lambda i: (0, 0))
    out_shape = [jax.ShapeDtypeStruct((t, d), F32)] + ([jax.ShapeDtypeStruct((t, d), BF16)] if want_bf16 else []) \
        + [jax.ShapeDtypeStruct((1, d), F32)]
    out_specs = [row] + ([row] if want_bf16 else []) + [vec]
    return pl.pallas_call(
        body, name="rms_bwd", grid=(t // tm,),
        in_specs=[row, vec, row] + ([row] if has_res else []), out_specs=out_specs, out_shape=out_shape,
        compiler_params=_params("arbitrary"),
    )(x, g.reshape(1, d), dy, *([dres] if has_res else []))


def _swap_rope_halves(y):
    lane = lax.broadcasted_iota(jnp.int32, y.shape, 1)
    half = MLA_ROPE // 2
    swapped = jnp.where(lane < MLA_NOPE + half, pltpu.roll(y, LANES - half, axis=1), pltpu.roll(y, half, axis=1))
    return jnp.where((lane >= MLA_NOPE) & (lane < MLA_QK), swapped, 0.0)


def _head_norm_fwd(x, g, n_valid, scale, rope=None):
    h, t, w = x.shape
    tm = _row_tile(t)

    def body(*refs):
        x_ref, g_ref = refs[:2]
        o_ref = refs[-1]
        x_ = x_ref[...]
        rstd = lax.rsqrt(jnp.sum(x_ * x_, axis=-1, keepdims=True) * (1.0 / n_valid) + EPS)
        y = x_ * rstd * (g_ref[...] * scale)
        if rope is not None:
            y = y * refs[2][...] + _swap_rope_halves(y) * refs[3][...]
        o_ref[...] = y.astype(BF16)

    blk = pl.BlockSpec((None, tm, w), lambda hh, i: (hh, i, 0))
    tab = pl.BlockSpec((tm, w), lambda hh, i: (i, 0))
    return pl.pallas_call(
        body, name="head_norm_fwd", grid=(h, t // tm),
        in_specs=[blk, pl.BlockSpec((1, w), lambda hh, i: (0, 0))] + ([tab, tab] if rope is not None else []),
        out_specs=blk, out_shape=jax.ShapeDtypeStruct((h, t, w), BF16),
        compiler_params=_params("parallel", "parallel"),
    )(x, g, *(rope if rope is not None else ()))


def _head_norm_bwd(x, g, dout, n_valid, scale, rope=None):
    h, t, w = x.shape
    tm = BLOCK

    def body(*refs):
        x_ref, g_ref, do_ref = refs[:3]
        dx_ref, dsum_ref, dg_ref = refs[-3:]
        dy = do_ref[...]
        if rope is not None:
            c_, s_ = refs[3][...], refs[4][...]
            ds = (dy * s_[None]).reshape(h * tm, w)
            dy = dy * c_[None] + _swap_rope_halves(ds).reshape(h, tm, w)
        x_ = x_ref[...]
        rstd = lax.rsqrt(jnp.sum(x_ * x_, axis=-1, keepdims=True) * (1.0 / n_valid) + EPS)
        xh = x_ * rstd
        dxh = dy * (g_ref[...] * scale)[None]
        dx = rstd * (dxh - xh * (jnp.sum(dxh * xh, axis=-1, keepdims=True) * (1.0 / n_valid)))
        dx_ref[...] = dx
        dsum_ref[...] = jnp.sum(dx, axis=0)

        @pl.when(pl.program_id(0) == 0)
        def _():
            dg_ref[...] = jnp.zeros_like(dg_ref)

        dg_ref[...] += scale * jnp.sum(jnp.sum(dy * xh, axis=0), axis=0, keepdims=True)

    blk = pl.BlockSpec((h, tm, w), lambda i: (0, i, 0))
    tab = pl.BlockSpec((tm, w), lambda i: (i, 0))
    vec = pl.BlockSpec((1, w), lambda i: (0, 0))
    return pl.pallas_call(
        body, name="head_norm_bwd", grid=(t // tm,),
        in_specs=[blk, vec, blk] + ([tab, tab] if rope is not None else []),
        out_specs=[blk, tab, vec],
        out_shape=[jax.ShapeDtypeStruct((h, t, w), F32), jax.ShapeDtypeStruct((t, w), F32),
                   jax.ShapeDtypeStruct((1, w), F32)],
        compiler_params=_params("arbitrary"),
    )(x, g, dout, *(rope if rope is not None else ()))


def _tri(n, upper):
    r = lax.broadcasted_iota(jnp.int32, (n, n), 0)
    c = lax.broadcasted_iota(jnp.int32, (n, n), 1)
    return ((r <= c) if upper else (r >= c)).astype(F32)


def _gate_mask(shape, row0):
    lane = lax.broadcasted_iota(jnp.int32, shape, 1)
    row = row0 + lax.broadcasted_iota(jnp.int32, shape, 0)
    return (lane >= TAIL_F) & (lane < TAIL_F + HEADS) & (row >= PAD)


def _gate_fwd(z, bias):
    t = z.shape[0]
    tm = BLOCK
    tail = C_TAIL // LANES

    def body(z_ref, b_ref, o_ref, carry):
        i = pl.program_id(0)

        @pl.when(i == 0)
        def _():
            carry[...] = jnp.zeros_like(carry)

        x_ = z_ref[...] + b_ref[...]
        logf = jnp.minimum(x_, 0.0) - jnp.log1p(jnp.exp(-jnp.abs(x_)))
        logf = jnp.where(_gate_mask(logf.shape, i * tm), logf, 0.0)
        cum = jnp.dot(_tri(tm, False), logf, preferred_element_type=F32, precision=lax.Precision.HIGHEST) + carry[...]
        o_ref[...] = cum
        carry[...] = cum[tm - 1:tm, :]

    return pl.pallas_call(
        body, name="gate_fwd", grid=(t // tm,),
        in_specs=[pl.BlockSpec((tm, LANES), lambda i: (i, tail)), pl.BlockSpec((1, LANES), lambda i: (0, 0))],
        out_specs=pl.BlockSpec((tm, LANES), lambda i: (i, 0)),
        out_shape=jax.ShapeDtypeStruct((t, LANES), F32),
        scratch_shapes=[pltpu.VMEM((1, LANES), F32)], compiler_params=_params("arbitrary"),
    )(z, bias)


def _gate_bwd(z, bias, dcum):
    t = z.shape[0]
    tm = BLOCK
    nb = t // tm
    tail = C_TAIL // LANES

    def body(z_ref, b_ref, d_ref, o_ref, db_ref, carry):
        i = pl.program_id(0)

        @pl.when(i == 0)
        def _():
            carry[...] = jnp.zeros_like(carry)
            db_ref[...] = jnp.zeros_like(db_ref)

        rc = jnp.dot(_tri(tm, True), d_ref[...], preferred_element_type=F32, precision=lax.Precision.HIGHEST) + carry[...]
        carry[...] = rc[0:1, :]
        x_ = z_ref[...] + b_ref[...]
        sig_neg = 1.0 / (1.0 + jnp.exp(x_))
        dl = jnp.where(_gate_mask(rc.shape, (nb - 1 - i) * tm), rc * sig_neg, 0.0)
        o_ref[...] = dl
        db_ref[...] += jnp.sum(dl, axis=0, keepdims=True)

    return pl.pallas_call(
        body, name="gate_bwd", grid=(nb,),
        in_specs=[pl.BlockSpec((tm, LANES), lambda i: (nb - 1 - i, tail)), pl.BlockSpec((1, LANES), lambda i: (0, 0)),
                  pl.BlockSpec((tm, LANES), lambda i: (nb - 1 - i, 0))],
        out_specs=[pl.BlockSpec((tm, LANES), lambda i: (nb - 1 - i, 0)), pl.BlockSpec((1, LANES), lambda i: (0, 0))],
        out_shape=[jax.ShapeDtypeStruct((t, LANES), F32), jax.ShapeDtypeStruct((1, LANES), F32)],
        scratch_shapes=[pltpu.VMEM((1, LANES), F32)], compiler_params=_params("arbitrary"),
    )(z, bias, dcum)


def _pairs(nb, by_query):
    if by_query:
        pr = [(i, j) for i in range(nb) for j in range(i + 1)]
    else:
        pr = [(i, j) for j in range(nb) for i in range(j, nb)]
    return (jnp.asarray(np.array([p[0] for p in pr], np.int32)),
            jnp.asarray(np.array([p[1] for p in pr], np.int32)))


HEADS_PER_STEP = 4


def _mask_scores(s, i, j, tile):
    qp = i * tile + lax.broadcasted_iota(jnp.int32, s.shape, 0)
    kp = j * tile + lax.broadcasted_iota(jnp.int32, s.shape, 1)
    return jnp.where((kp <= qp) & (kp >= PAD), s, NEG)


def _pipelined(n, front, back):
    nxt = front(0)
    for h in range(n):
        cur = nxt
        if h + 1 < n:
            nxt = front(h + 1)
        back(h, cur)


def _nt_dot(a, b):
    return lax.dot_general(a, b, (((1,), (1,)), ((), ())), preferred_element_type=F32)


def _tn_dot(a, b):
    return lax.dot_general(a, b, (((0,), (0,)), ((), ())), preferred_element_type=F32)


def _attn_specs(hb, tile, dk, dv):
    q_of = lambda h, p, it, jt: (h, it[p], 0)
    k_of = lambda h, p, it, jt: (h, jt[p], 0)
    return dict(
        q=pl.BlockSpec((hb, tile, dk), q_of), k=pl.BlockSpec((hb, tile, dk), k_of),
        v=pl.BlockSpec((hb, tile, dv), k_of), ov=pl.BlockSpec((hb, tile, dv), q_of),
        col=pl.BlockSpec((hb, tile, 1), q_of), colk=pl.BlockSpec((hb, tile, 1), k_of),
        fr=pl.BlockSpec((hb, 1, tile), lambda h, p, it, jt: (h, 0, jt[p])),
        rowq=pl.BlockSpec((hb, 1, tile), lambda h, p, it, jt: (h, 0, it[p])))


def _attn_fwd(q, k, v, key_bias=None):
    h, t, dk = q.shape
    dv = v.shape[-1]
    tile = _row_tile(t)
    nb = t // tile
    hb = HEADS_PER_STEP
    biased = key_bias is not None
    it, jt = _pairs(nb, True)
    sp = _attn_specs(hb, tile, dk, dv)

    def body(it_ref, jt_ref, *refs):
        q_ref, k_ref, v_ref = refs[:3]
        b_ref = refs[3] if biased else None
        o_ref, lse_ref, m_sc, l_sc, acc_sc = refs[-5:]
        p = pl.program_id(1)
        i, j = it_ref[p], jt_ref[p]

        @pl.when(j == 0)
        def _():
            m_sc[...] = jnp.full_like(m_sc, NEG)
            l_sc[...] = jnp.zeros_like(l_sc)
            acc_sc[...] = jnp.zeros_like(acc_sc)

        def step(masked):
            def front(n):
                return _nt_dot(q_ref[n], k_ref[n])

            def back(n, s):
                if biased:
                    s = s + b_ref[n]
                if masked:
                    s = _mask_scores(s, i, j, tile)
                m_prev = m_sc[n]
                m_new = jnp.maximum(m_prev, jnp.max(s, axis=-1, keepdims=True))
                alpha = jnp.exp(m_prev - m_new)
                e = jnp.exp(s - m_new)
                l_sc[n] = alpha * l_sc[n] + jnp.sum(e, axis=-1, keepdims=True)
                acc_sc[n] = alpha * acc_sc[n] + jnp.dot(e.astype(BF16), v_ref[n], preferred_element_type=F32)
                m_sc[n] = m_new

            _pipelined(hb, front, back)

        edge = (j == i) | (j == 0)
        pl.when(edge)(lambda: step(True))
        pl.when(jnp.logical_not(edge))(lambda: step(False))

        @pl.when(j == i)
        def _():
            row = i * tile + lax.broadcasted_iota(jnp.int32, (hb, tile, 1), 1)
            o_ref[...] = jnp.where(row >= PAD, acc_sc[...] / l_sc[...], 0.0)
            lse_ref[...] = m_sc[...] + jnp.log(l_sc[...])

    grid_spec = pltpu.PrefetchScalarGridSpec(
        num_scalar_prefetch=2, grid=(h // hb, int(it.shape[0])),
        in_specs=[sp["q"], sp["k"], sp["v"]] + ([sp["fr"]] if biased else []),
        out_specs=[sp["ov"], sp["col"]],
        scratch_shapes=[pltpu.VMEM((hb, tile, 1), F32), pltpu.VMEM((hb, tile, 1), F32),
                        pltpu.VMEM((hb, tile, dv), F32)])
    return pl.pallas_call(
        body, name="attn_fwd", grid_spec=grid_spec,
        out_shape=[jax.ShapeDtypeStruct((h, t, dv), F32), jax.ShapeDtypeStruct((h, t, 1), F32)],
        compiler_params=_params("parallel", "arbitrary"),
    )(it, jt, q, k, v, *((key_bias,) if biased else ()))


def _attn_dq(q, k, v, o, do, lse, key_bias=None):
    h, t, dk = q.shape
    dv = v.shape[-1]
    tile = _row_tile(t)
    nb = t // tile
    hb = HEADS_PER_STEP
    decay = key_bias is not None
    it, jt = _pairs(nb, True)
    sp = _attn_specs(hb, tile, dk, dv)
    n_out = 2 if decay else 1

    def body(it_ref, jt_ref, *refs):
        q_ref, k_ref, v_ref, o_ref, do_ref, lse_ref = refs[:6]
        b_ref = refs[6] if decay else None
        outs = refs[-2 - 2 * n_out:-2 - n_out]
        delta_ref = refs[-2 - n_out]
        accs = refs[-1 - n_out:-1]
        delta_sc = refs[-1]
        p = pl.program_id(1)
        i, j = it_ref[p], jt_ref[p]

        @pl.when(j == 0)
        def _():
            for a_ in accs:
                a_[...] = jnp.zeros_like(a_)
            delta_sc[...] = jnp.sum(o_ref[...] * do_ref[...], axis=-1, keepdims=True)

        def step(masked):
            def front(n):
                return _nt_dot(q_ref[n], k_ref[n]), _nt_dot(do_ref[n].astype(BF16), v_ref[n])

            def back(n, s_dp):
                s, dp = s_dp
                if decay:
                    s = s + b_ref[n]
                if masked:
                    s = _mask_scores(s, i, j, tile)
                ds = jnp.exp(s - lse_ref[n]) * (dp - delta_sc[n])
                accs[0][n] += jnp.dot(ds.astype(BF16), k_ref[n], preferred_element_type=F32)
                if decay:
                    accs[1][n] += jnp.sum(ds, axis=-1, keepdims=True)

            _pipelined(hb, front, back)

        edge = (j == i) | (j == 0)
        pl.when(edge)(lambda: step(True))
        pl.when(jnp.logical_not(edge))(lambda: step(False))

        @pl.when(j == i)
        def _():
            for o_, a_ in zip(outs, accs):
                o_[...] = a_[...]
            delta_ref[...] = delta_sc[...]

    grid_spec = pltpu.PrefetchScalarGridSpec(
        num_scalar_prefetch=2, grid=(h // hb, int(it.shape[0])),
        in_specs=[sp["q"], sp["k"], sp["v"], sp["ov"], sp["ov"], sp["col"]] + ([sp["fr"]] if decay else []),
        out_specs=[sp["q"]] + ([sp["col"]] if decay else []) + [sp["col"]],
        scratch_shapes=[pltpu.VMEM((hb, tile, dk), F32)] + ([pltpu.VMEM((hb, tile, 1), F32)] if decay else [])
        + [pltpu.VMEM((hb, tile, 1), F32)])
    out_shape = [jax.ShapeDtypeStruct((h, t, dk), F32)] + ([jax.ShapeDtypeStruct((h, t, 1), F32)] if decay else []) \
        + [jax.ShapeDtypeStruct((h, t, 1), F32)]
    return pl.pallas_call(
        body, name="attn_dq", grid_spec=grid_spec, out_shape=out_shape,
        compiler_params=_params("parallel", "arbitrary"),
    )(it, jt, q, k, v, o, do, lse, *((key_bias,) if decay else ()))


def _attn_dkv(q, k, v, do, lse_row, delta_row, key_bias=None):
    h, t, dk = q.shape
    dv = v.shape[-1]
    tile = _row_tile(t)
    nb = t // tile
    hb = HEADS_PER_STEP
    decay = key_bias is not None
    it, jt = _pairs(nb, False)
    sp = _attn_specs(hb, tile, dk, dv)

    def body(it_ref, jt_ref, *refs):
        q_ref, k_ref, v_ref, do_ref, lse_ref, delta_ref = refs[:6]
        b_ref = refs[6] if decay else None
        n_out = 3 if decay else 2
        outs = refs[-2 * n_out:-n_out]
        accs = refs[-n_out:]
        p = pl.program_id(1)
        i, j = it_ref[p], jt_ref[p]

        @pl.when(i == j)
        def _():
            for a_ in accs:
                a_[...] = jnp.zeros_like(a_)

        def step(masked):
            def front(n):
                return _nt_dot(k_ref[n], q_ref[n]), _nt_dot(v_ref[n], do_ref[n].astype(BF16))

            def back(n, s_dp):
                s, dp = s_dp
                if decay:
                    s = s + b_ref[n]
                if masked:
                    kp = j * tile + lax.broadcasted_iota(jnp.int32, s.shape, 0)
                    qp = i * tile + lax.broadcasted_iota(jnp.int32, s.shape, 1)
                    s = jnp.where((kp <= qp) & (kp >= PAD), s, NEG)
                pr = jnp.exp(s - lse_ref[n])
                ds = pr * (dp - delta_ref[n])
                accs[1][n] += jnp.dot(pr.astype(BF16), do_ref[n].astype(BF16), preferred_element_type=F32)
                accs[0][n] += jnp.dot(ds.astype(BF16), q_ref[n], preferred_element_type=F32)
                if decay:
                    accs[2][n] += jnp.sum(ds, axis=-1, keepdims=True)

            _pipelined(hb, front, back)

        edge = (j == i) | (j == 0)
        pl.when(edge)(lambda: step(True))
        pl.when(jnp.logical_not(edge))(lambda: step(False))

        @pl.when(i == nb - 1)
        def _():
            for o_, a_ in zip(outs, accs):
                o_[...] = a_[...]

    grid_spec = pltpu.PrefetchScalarGridSpec(
        num_scalar_prefetch=2, grid=(h // hb, int(it.shape[0])),
        in_specs=[sp["q"], sp["k"], sp["v"], sp["ov"], sp["rowq"], sp["rowq"]] + ([sp["colk"]] if decay else []),
        out_specs=[sp["k"], sp["v"]] + ([sp["colk"]] if decay else []),
        scratch_shapes=[pltpu.VMEM((hb, tile, dk), F32), pltpu.VMEM((hb, tile, dv), F32)]
        + ([pltpu.VMEM((hb, tile, 1), F32)] if decay else []))
    out_shape = [jax.ShapeDtypeStruct((h, t, dk), F32), jax.ShapeDtypeStruct((h, t, dv), F32)] \
        + ([jax.ShapeDtypeStruct((h, t, 1), F32)] if decay else [])
    return pl.pallas_call(
        body, name="attn_dkv", grid_spec=grid_spec, out_shape=out_shape,
        compiler_params=_params("parallel", "arbitrary"),
    )(it, jt, q, k, v, do, lse_row, delta_row, *((key_bias,) if decay else ()))


CONV_COLS = 512


def _shift_down(g, prev, n):
    out = pltpu.roll(g, n, axis=0)
    row = lax.broadcasted_iota(jnp.int32, g.shape, 0)
    for r in range(n):
        out = jnp.where(row == r, prev[8 - n + r:8 - n + r + 1, :], out)
    return out


def _shift_up(g, nxt, n):
    tm = g.shape[0]
    out = pltpu.roll(g, tm - n, axis=0)
    row = lax.broadcasted_iota(jnp.int32, g.shape, 0)
    for r in range(n):
        out = jnp.where(row == tm - n + r, nxt[r:r + 1, :], out)
    return out


def _conv_fwd(z3, w):
    _, t, d = z3.shape
    tm = _row_tile(t)
    tc = CONV_COLS

    def body(z_ref, zp_ref, w_ref, o_ref):
        i = pl.program_id(1)
        g = z_ref[1] * z_ref[2]
        gp = jnp.where(i > 0, zp_ref[1] * zp_ref[2], 0.0)
        w_ = w_ref[...]
        y = w_[2:3] * g + w_[1:2] * _shift_down(g, gp, 1) + w_[0:1] * _shift_down(g, gp, 2)
        o_ref[...] = (z_ref[0] * y).astype(BF16)

    return pl.pallas_call(
        body, name="conv_fwd", grid=(d // tc, t // tm),
        in_specs=[pl.BlockSpec((3, tm, tc), lambda j, i: (0, i, j)),
                  pl.BlockSpec((3, 8, tc), lambda j, i: (0, jnp.maximum(i * (tm // 8) - 1, 0), j)),
                  pl.BlockSpec((3, tc), lambda j, i: (0, j))],
        out_specs=pl.BlockSpec((tm, tc), lambda j, i: (i, j)),
        out_shape=jax.ShapeDtypeStruct((t, d), BF16), compiler_params=_params("parallel", "parallel"),
    )(z3, z3, w)


def _conv_bwd(z3, w, dyb):
    _, t, d = z3.shape
    tm = _row_tile(t)
    tc = CONV_COLS
    ni = t // tm

    def body(z_ref, zp_ref, zn_ref, d_ref, dn_ref, w_ref, dz_ref, dw_ref):
        i = pl.program_id(1)
        gb, gc, u = z_ref[0], z_ref[1], z_ref[2]
        g = gc * u
        gp = jnp.where(i > 0, zp_ref[1] * zp_ref[2], 0.0)
        w_ = w_ref[...]
        g1, g2 = _shift_down(g, gp, 1), _shift_down(g, gp, 2)
        y = w_[2:3] * g + w_[1:2] * g1 + w_[0:1] * g2
        dyb_ = d_ref[...]
        dy = dyb_ * gb
        dyn = jnp.where(i < ni - 1, dn_ref[...] * zn_ref[0], 0.0)
        dg = w_[2:3] * dy + w_[1:2] * _shift_up(dy, dyn, 1) + w_[0:1] * _shift_up(dy, dyn, 2)
        dz_ref[0] = (dyb_ * y).astype(BF16)
        dz_ref[1] = (dg * u).astype(BF16)
        dz_ref[2] = (dg * gc).astype(BF16)

        @pl.when(i == 0)
        def _():
            dw_ref[...] = jnp.zeros_like(dw_ref)

        dw_ref[...] += jnp.concatenate([jnp.sum(dy * g2, axis=0, keepdims=True),
                                        jnp.sum(dy * g1, axis=0, keepdims=True),
                                        jnp.sum(dy * g, axis=0, keepdims=True)], axis=0)

    cur = pl.BlockSpec((3, tm, tc), lambda j, i: (0, i, j))
    return pl.pallas_call(
        body, name="conv_bwd", grid=(d // tc, ni),
        in_specs=[cur,
                  pl.BlockSpec((3, 8, tc), lambda j, i: (0, jnp.maximum(i * (tm // 8) - 1, 0), j)),
                  pl.BlockSpec((3, 8, tc), lambda j, i: (0, jnp.minimum((i + 1) * (tm // 8), t // 8 - 1), j)),
                  pl.BlockSpec((tm, tc), lambda j, i: (i, j)),
                  pl.BlockSpec((8, tc), lambda j, i: (jnp.minimum((i + 1) * (tm // 8), t // 8 - 1), j)),
                  pl.BlockSpec((3, tc), lambda j, i: (0, j))],
        out_specs=[cur, pl.BlockSpec((3, tc), lambda j, i: (0, j))],
        out_shape=[jax.ShapeDtypeStruct((3, t, d), BF16), jax.ShapeDtypeStruct((3, d), F32)],
        compiler_params=_params("parallel", "arbitrary"),
    )(z3, z3, z3, dyb, dyb, w)


def _loss_head(h, target):
    t, d = h.shape
    tm = BLOCK

    def body(h_ref, t_ref, dh_ref, dhb_ref, loss_ref):
        i = pl.program_id(0)

        @pl.when(i == 0)
        def _():
            loss_ref[...] = jnp.zeros_like(loss_ref)

        err = jnp.where(i > 0, h_ref[...] - t_ref[...], 0.0)
        dh = err * (1.0 / d)
        dh_ref[...] = dh
        dhb_ref[...] = dh.astype(BF16)
        loss_ref[...] += 0.5 * jnp.sum(jnp.sum(err * err, axis=-1, keepdims=True) * (1.0 / d), axis=0, keepdims=True)

    row = pl.BlockSpec((tm, d), lambda i: (i, 0))
    return pl.pallas_call(
        body, name="loss_head", grid=(t // tm,),
        in_specs=[row, pl.BlockSpec((tm, d), lambda i: (jnp.maximum(i - 1, 0), 0))],
        out_specs=[row, row, pl.BlockSpec((1, 1), lambda i: (0, 0))],
        out_shape=[jax.ShapeDtypeStruct((t, d), F32), jax.ShapeDtypeStruct((t, d), BF16),
                   jax.ShapeDtypeStruct((1, 1), F32)],
        compiler_params=_params("arbitrary"),
    )(h, target)


def _heads_major(x, width):
    t = x.shape[0]
    return jnp.transpose(x.reshape(t, HEADS, width), (1, 0, 2))


def _heads_minor(x):
    h, t, w = x.shape
    return jnp.transpose(x, (1, 0, 2)).reshape(t, h * w)


def _rope_tables(t):
    pos = jnp.arange(t, dtype=F32) - PAD
    inv_freq = ROPE_BASE ** (-jnp.arange(0, MLA_ROPE, 2, dtype=F32) / MLA_ROPE)
    ang = pos[:, None] * inv_freq[None, :]
    cos, sin = jnp.cos(ang), jnp.sin(ang)
    one, zero = jnp.ones((t, MLA_NOPE), F32), jnp.zeros((t, MLA_NOPE), F32)
    tail = jnp.zeros((t, LANES - MLA_QK), F32)
    return (jnp.concatenate([one, cos, cos, tail], axis=1), jnp.concatenate([zero, -sin, sin, tail], axis=1))


def _pad_lanes(x, width=LANES):
    return jnp.pad(x, [(0, 0)] * (x.ndim - 1) + [(0, width - x.shape[-1])])


def _permute_in_attn(w):
    return jnp.concatenate([w[:, :640], w[:, 672:2208], w[:, 640:672], w[:, 2208:2216],
                            jnp.zeros((w.shape[0], ATTN_IN_PAD - 2216), w.dtype)], axis=1)


def _unpermute_in_attn(dw):
    return jnp.concatenate([dw[:, :640], dw[:, 2176:2208], dw[:, 640:2176], dw[:, 2208:2216]], axis=1)


def _attn_layer_fwd(hn, wl, rope):
    t = hn.shape[0]
    z = _mm(hn, wl["w_in"], name="attn_in")
    cqn = _rms_fwd(z[:, C_CQ:C_CQ + Q_LORA], wl["g_cq"])
    ckvn = _rms_fwd(z[:, C_CKV:C_CKV + KV_LORA], wl["g_ckv"])
    qf = _mm(cqn, wl["w_uq"], name="mla_uq")
    kvf = _mm(ckvn, wl["w_ukv"], name="mla_ukv")
    kv3 = kvf.reshape(t, HEADS, MLA_NOPE + MLA_V)
    xq = _pad_lanes(_heads_major(qf, MLA_QK))
    k_pe = jnp.broadcast_to(z[None, :, C_TAIL:C_TAIL + MLA_ROPE], (HEADS, t, MLA_ROPE))
    xk = _pad_lanes(jnp.concatenate([jnp.transpose(kv3[:, :, :MLA_NOPE], (1, 0, 2)), k_pe], axis=-1))
    v_mla = jnp.transpose(kv3[:, :, MLA_NOPE:], (1, 0, 2)).astype(BF16)
    gq, gk = _pad_lanes(wl["g_q_mla"].reshape(1, -1)), _pad_lanes(wl["g_k_mla"].reshape(1, -1))
    q_mla = _head_norm_fwd(xq, gq, MLA_QK, MLA_QK ** -0.5, rope)
    k_mla = _head_norm_fwd(xk, gk, MLA_QK, 1.0, rope)
    o_mla, lse_mla = _attn_fwd(q_mla, k_mla, v_mla)
    xfq = _heads_major(z[:, C_FQ:C_FQ + HEADS * FOX_DIM], FOX_DIM)
    xfk = _heads_major(z[:, C_FK:C_FK + HEADS * FOX_DIM], FOX_DIM)
    v_fox = _heads_major(z[:, C_FV:C_FV + HEADS * FOX_DIM], FOX_DIM).astype(BF16)
    bias = jnp.pad(wl["b_forget"].reshape(1, -1), ((0, 0), (TAIL_F, LANES - TAIL_F - HEADS)))
    cum = _gate_fwd(z, bias)
    neg_f = -jnp.transpose(cum[:, TAIL_F:TAIL_F + HEADS])
    q_fox = _head_norm_fwd(xfq, wl["g_q_fox"].reshape(1, -1), FOX_DIM, FOX_DIM ** -0.5)
    k_fox = _head_norm_fwd(xfk, wl["g_k_fox"].reshape(1, -1), FOX_DIM, 1.0)
    o_fox, lse_fox = _attn_fwd(q_fox, k_fox, v_fox, neg_f[:, None, :])
    cat = jnp.concatenate([_heads_minor(o_mla), _heads_minor(o_fox)], axis=1).astype(BF16)
    saved = dict(z=z, cqn=cqn, ckvn=ckvn, xq=xq, xk=xk, v_mla=v_mla, q_mla=q_mla, k_mla=k_mla, o_mla=o_mla,
                 lse_mla=lse_mla, xfq=xfq, xfk=xfk, v_fox=v_fox, q_fox=q_fox, k_fox=k_fox, bias=bias,
                 neg_f=neg_f, o_fox=o_fox, lse_fox=lse_fox, cat=cat, gq=gq, gk=gk)
    return cat, saved


def _attn_layer_bwd(dcat, hn, wl, sv, rope):
    t = hn.shape[0]
    g = {}
    do_mla = _heads_major(dcat[:, :HEADS * MLA_V], MLA_V)
    do_fox = _heads_major(dcat[:, HEADS * MLA_V:], FOX_DIM)
    as_row = lambda c: jnp.transpose(c, (0, 2, 1))
    qkv = (sv["q_fox"], sv["k_fox"], sv["v_fox"])
    neg_f = sv["neg_f"]
    dq_fox, row_sums, delta = _attn_dq(*qkv, sv["o_fox"], do_fox, sv["lse_fox"], neg_f[:, None, :])
    dk_fox, dv_fox, key_sums = _attn_dkv(*qkv, do_fox, as_row(sv["lse_fox"]), as_row(delta), neg_f[:, :, None])
    dcum = jnp.pad(jnp.transpose((row_sums - key_sums)[:, :, 0]), ((0, 0), (TAIL_F, LANES - TAIL_F - HEADS)))
    dtail_f, dbias = _gate_bwd(sv["z"], sv["bias"], dcum)
    g["b_forget"] = dbias[0, TAIL_F:TAIL_F + HEADS]
    dxfq, _, dgq = _head_norm_bwd(sv["xfq"], wl["g_q_fox"].reshape(1, -1), dq_fox, FOX_DIM, FOX_DIM ** -0.5)
    dxfk, _, dgk = _head_norm_bwd(sv["xfk"], wl["g_k_fox"].reshape(1, -1), dk_fox, FOX_DIM, 1.0)
    g["g_q_fox"], g["g_k_fox"] = dgq[0], dgk[0]
    qkv = (sv["q_mla"], sv["k_mla"], sv["v_mla"])
    dq_mla, delta = _attn_dq(*qkv, sv["o_mla"], do_mla, sv["lse_mla"])
    dk_mla, dv_mla = _attn_dkv(*qkv, do_mla, as_row(sv["lse_mla"]), as_row(delta))
    dxq, _, dgq = _head_norm_bwd(sv["xq"], sv["gq"], dq_mla, MLA_QK, MLA_QK ** -0.5, rope)
    dxk, dxk_sum, dgk = _head_norm_bwd(sv["xk"], sv["gk"], dk_mla, MLA_QK, 1.0, rope)
    g["g_q_mla"], g["g_k_mla"] = dgq[0, :MLA_QK], dgk[0, :MLA_QK]
    dqf = _heads_minor(dxq[:, :, :MLA_QK]).astype(BF16)
    dkvf = _heads_minor(jnp.concatenate([dxk[:, :, :MLA_NOPE], dv_mla], axis=-1)).astype(BF16)
    g["w_uq"] = _mm(sv["cqn"], dqf, ta=True, name="d_w_uq")
    g["w_ukv"] = _mm(sv["ckvn"], dkvf, ta=True, name="d_w_ukv")
    dcqn = _mm(dqf, wl["w_uq"], tb=True, name="d_cqn")
    dckvn = _mm(dkvf, wl["w_ukv"], tb=True, name="d_ckvn")
    z = sv["z"]
    dcq, dg_cq = _rms_bwd(z[:, C_CQ:C_CQ + Q_LORA], wl["g_cq"], dcqn)
    dckv, dg_ckv = _rms_bwd(z[:, C_CKV:C_CKV + KV_LORA], wl["g_ckv"], dckvn)
    g["g_cq"], g["g_ckv"] = dg_cq[0], dg_ckv[0]
    tail = jnp.concatenate([dxk_sum[:, MLA_NOPE:MLA_QK], dtail_f[:, TAIL_F:]], axis=1)
    dz = jnp.concatenate([dcq, dckv, _heads_minor(dxfq), _heads_minor(dxfk), _heads_minor(dv_fox), tail],
                         axis=1).astype(BF16)
    g["w_in"] = _mm(hn, dz, ta=True, name="d_w_in_attn")
    dhn = _mm(dz, wl["w_in"], tb=True, name="d_hn_attn")
    return dhn, g


def _local_step(x, target, w):
    seq = x.shape[0]
    t = seq + BLOCK
    rope = _rope_tables(t)
    h = jnp.concatenate([jnp.zeros((PAD, D_MODEL), F32), w["meta_tokens"], x], axis=0)
    tape = []
    for layer in range(DEPTH):
        j = layer // 2
        hn = _rms_fwd(h, w["g_mix"][layer])
        if layer % 2 == 0:
            wl = dict(w_in=w["w_in_attn"][j], g_cq=w["g_cq"][j], w_uq=w["w_uq"][j], g_ckv=w["g_ckv"][j],
                      w_ukv=w["w_ukv"][j], g_q_mla=w["g_q_mla"][j], g_k_mla=w["g_k_mla"][j],
                      g_q_fox=w["g_q_fox"][j], g_k_fox=w["g_k_fox"][j], b_forget=w["b_forget"][j])
            mixed, sv = _attn_layer_fwd(hn, wl, rope)
            h1 = _mm(mixed, w["w_out_attn"][j], res=h, name="attn_out")
        else:
            wl = None
            z3 = _mm(hn, w["w_in_conv"][j], out_seg=3, name="conv_in")
            mixed = _conv_fwd(z3, w["conv_w"][j])
            sv = dict(z3=z3)
            h1 = _mm(mixed, w["w_out_conv"][j], res=h, name="conv_out")
        hn2 = _rms_fwd(h1, w["g_mlp"][layer])
        u, act = _mm(hn2, w["w_mlp_up"][layer], epi="relu2", name="mlp_up")
        h2 = _mm(act, w["w_mlp_down"][layer], res=h1, name="mlp_down")
        tape.append(dict(h=h, hn=hn, wl=wl, sv=sv, mixed=mixed, h1=h1, hn2=hn2, u=u, act=act))
        h = h2

    dh, dh_b, loss = _loss_head(h, target)
    g = {n: [None] * (DEPTH if n in ("g_mix", "g_mlp", "w_mlp_up", "w_mlp_down") else DEPTH // 2)
         for n in WEIGHTS if n != "meta_tokens"}
    for layer in reversed(range(DEPTH)):
        j = layer // 2
        tp = tape[layer]
        g["w_mlp_down"][layer] = _mm(tp["act"], dh_b, ta=True, name="d_w_down")
        du = _mm(dh_b, w["w_mlp_down"][layer], tb=True, epi="relu2_bwd", aux=tp["u"], out_dtype=BF16, name="d_u")
        g["w_mlp_up"][layer] = _mm(tp["hn2"], du, ta=True, name="d_w_up")
        dhn2 = _mm(du, w["w_mlp_up"][layer], tb=True, name="d_hn2")
        dh1, dh1_b, dg = _rms_bwd(tp["h1"], w["g_mlp"][layer], dhn2, dres=dh, want_bf16=True)
        g["g_mlp"][layer] = dg[0]
        if layer % 2 == 0:
            g["w_out_attn"][j] = _mm(tp["mixed"], dh1_b, ta=True, name="d_w_out_attn")
            dcat = _mm(dh1_b, w["w_out_attn"][j], tb=True, name="d_cat")
            dhn, gl = _attn_layer_bwd(dcat, tp["hn"], tp["wl"], tp["sv"], rope)
            g["w_in_attn"][j] = _unpermute_in_attn(gl.pop("w_in"))
            for n, val in gl.items():
                g[n][j] = val
        else:
            g["w_out_conv"][j] = _mm(tp["mixed"], dh1_b, ta=True, name="d_w_out_conv")
            dyb = _mm(dh1_b, w["w_out_conv"][j], tb=True, name="d_yb")
            dz3, dcw = _conv_bwd(tp["sv"]["z3"], w["conv_w"][j], dyb)
            g["conv_w"][j] = dcw
            g["w_in_conv"][j] = _mm(tp["hn"], dz3, ta=True, name="d_w_in_conv")
            dhn = _mm(dz3, w["w_in_conv"][j], tb=True, name="d_hn_conv")
        dh, dh_b, dg = _rms_bwd(tp["h"], w["g_mix"][layer], dhn, dres=dh1, want_bf16=True)
        g["g_mix"][layer] = dg[0]
    grads = {n: jnp.stack(v) for n, v in g.items()}
    grads["meta_tokens"] = dh[PAD:BLOCK]
    return loss, dh[BLOCK:], grads


COMM_ROWS = 2048


def kernel(x, meta_tokens, g_mix, g_mlp, w_in_attn, g_cq, w_uq, g_ckv, w_ukv, g_q_mla, g_k_mla, g_q_fox, g_k_fox, b_forget, w_out_attn, w_in_conv, conv_w, w_out_conv, w_mlp_up, w_mlp_down, loss_target, m_meta_tokens, m_g_mix, m_g_mlp, m_w_in_attn, m_g_cq, m_w_uq, m_g_ckv, m_w_ukv, m_g_q_mla, m_g_k_mla, m_g_q_fox, m_g_k_fox, m_b_forget, m_w_out_attn, m_w_in_conv, m_conv_w, m_w_out_conv, m_w_mlp_up, m_w_mlp_down, v_meta_tokens, v_g_mix, v_g_mlp, v_w_in_attn, v_g_cq, v_w_uq, v_g_ckv, v_w_ukv, v_g_q_mla, v_g_k_mla, v_g_q_fox, v_g_k_fox, v_b_forget, v_w_out_attn, v_w_in_conv, v_conv_w, v_w_out_conv, v_w_mlp_up, v_w_mlp_down):
    args = dict(locals())
    local = {n: args[n] for n in WEIGHTS}
    mom = {n: args["m_" + n] for n in WEIGHTS}
    var = {n: args["v_" + n] for n in WEIGHTS}
    axis = dict(SHARDED)
    packed = [n for n, _ in SHARDED if n != "w_in_attn"]
    big = [n for n in packed if n not in F32_GATHERED]
    rows_277 = lambda a: a.reshape(-1, a.shape[-1])

    gathered_big, gathered_in, gathered_f32 = _all_gather([
        _pack([local[n] for n in big], 16, BF16), rows_277(local["w_in_attn"]).astype(BF16),
        _pack([local[n] for n in F32_GATHERED], 8, F32)])
    full = {n: local[n] for n in REPLICATED}
    for n, blocks in zip(big, _unpack(gathered_big, [local[n].shape for n in big], (N_DEV,))):
        full[n] = _from_shards(blocks, axis[n])
    for n, blocks in zip(F32_GATHERED, _unpack(gathered_f32, [local[n].shape for n in F32_GATHERED], (N_DEV,))):
        full[n] = _from_shards(blocks, axis[n])
    w_in = _from_shards(gathered_in.reshape((N_DEV,) + local["w_in_attn"].shape), axis["w_in_attn"])
    full["w_in_attn"] = jnp.stack([_permute_in_attn(w_in[j]) for j in range(DEPTH // 2)])

    loss_part, dx, grads = _local_step(x[0], loss_target[0], full)
    loss = lax.psum(loss_part[0, 0], ("x", "y", "c"))

    sent = _pack_rows([_to_shards(grads[n], axis[n]) for n in packed], COMM_ROWS, BF16)
    sent_in = _to_shards(grads["w_in_attn"], axis["w_in_attn"]).astype(BF16)
    sent_in = sent_in.reshape(N_DEV, -1, sent_in.shape[-1])
    got, got_in = _all_to_all([sent, sent_in])
    g_packed = _sum_blocks(got, COMM_ROWS)
    g_in = _sum_blocks(got_in, 256)
    rep, = _all_gather([_pack([grads[n] for n in REPLICATED], 8, F32)])
    g_rep = _sum_blocks(rep, rep.shape[1])
    g_local = dict(zip(packed, _unpack(g_packed, [local[n].shape for n in packed])))
    g_local.update(zip(REPLICATED, _unpack(g_rep, [local[n].shape for n in REPLICATED])))
    g_local["w_in_attn"] = g_in.reshape(local["w_in_attn"].shape)

    def flat(src, names, rows_multiple):
        return _pack([src[n] for n in names], rows_multiple, F32)

    upd = {}
    for names, g_flat, rows in ((packed, g_packed, COMM_ROWS), (REPLICATED, g_rep, g_rep.shape[0])):
        outs = _adamw(flat(local, names, rows), g_flat, flat(mom, names, rows), flat(var, names, rows), rows)
        for kind, buf in zip(("delta", "m", "v"), outs):
            upd.update({(kind, n): a for n, a in zip(names, _unpack(buf, [local[n].shape for n in names]))})
    outs = _adamw(rows_277(local["w_in_attn"]), g_in, rows_277(mom["w_in_attn"]), rows_277(var["w_in_attn"]), 256)
    for kind, buf in zip(("delta", "m", "v"), outs):
        upd[(kind, "w_in_attn")] = buf.reshape(local["w_in_attn"].shape)

    return (loss, dx[None], *[g_local[n] for n in WEIGHTS], *[upd[("delta", n)] for n in WEIGHTS],
            *[upd[("m", n)] for n in WEIGHTS], *[upd[("v", n)] for n in WEIGHTS])
```

```python
import functools
import math

import jax
import jax.numpy as jnp
import numpy as np
from jax import lax
from jax.experimental import pallas as pl
from jax.experimental.pallas import tpu as pltpu

F32 = jnp.float32
BF16 = jnp.bfloat16

N_DEV = 8
D_MODEL = 1024
DEPTH = 4
N_META = 16
BLOCK = 128
PAD = BLOCK - N_META
HEADS = 8
MLA_NOPE = 64
MLA_ROPE = 32
MLA_QK = MLA_NOPE + MLA_ROPE
MLA_V = 64
Q_LORA = 384
KV_LORA = 256
ROPE_BASE = 10000.0
FOX_DIM = 64
D_FF = 4 * D_MODEL
EPS = 1e-6
NEG = -1e30
LANES = 128
ATTN_IN_PAD = 2304
C_CQ, C_CKV, C_FQ, C_FK, C_FV, C_TAIL = 0, 384, 640, 1152, 1664, 2176
TAIL_F = MLA_ROPE

ADAM_LR = 0.001
ADAM_B1 = 0.9
ADAM_B2 = 0.999
ADAM_EPS = 1e-08
ADAM_WD = 0.01
ADAM_STEP = 10

VMEM_LIMIT_BYTES = 48 * 1024 * 1024
MESH = pl.DeviceIdType.MESH

SHARDED = (
    ("meta_tokens", 1), ("w_in_attn", 2), ("w_uq", 2), ("w_ukv", 2), ("w_out_attn", 1),
    ("w_in_conv", 2), ("conv_w", 2), ("w_out_conv", 1), ("w_mlp_up", 2), ("w_mlp_down", 1))
F32_GATHERED = ("meta_tokens", "conv_w")
REPLICATED = ("g_mix", "g_mlp", "g_cq", "g_ckv", "g_q_mla", "g_k_mla", "g_q_fox", "g_k_fox", "b_forget")
WEIGHTS = ("meta_tokens", "g_mix", "g_mlp", "w_in_attn", "g_cq", "w_uq", "g_ckv", "w_ukv", "g_q_mla",
           "g_k_mla", "g_q_fox", "g_k_fox", "b_forget", "w_out_attn", "w_in_conv", "conv_w",
           "w_out_conv", "w_mlp_up", "w_mlp_down")


def _params(*sem):
    return pltpu.CompilerParams(dimension_semantics=sem, vmem_limit_bytes=VMEM_LIMIT_BYTES)


def _row_tile(t):
    return 640 if (t % 640 == 0 and t > 640) else 128


def _pack(parts, rows_multiple, dtype):
    flat = jnp.concatenate([p.reshape(-1).astype(dtype) for p in parts])
    n = flat.shape[0]
    rows = -(-n // LANES)
    rows = -(-rows // rows_multiple) * rows_multiple
    return jnp.pad(flat, (0, rows * LANES - n)).reshape(rows, LANES)


def _pack_rows(parts, rows_multiple, dtype):
    flat = jnp.concatenate([p.reshape(N_DEV, -1).astype(dtype) for p in parts], axis=1)
    n = flat.shape[1]
    rows = -(-n // LANES)
    rows = -(-rows // rows_multiple) * rows_multiple
    return jnp.pad(flat, ((0, 0), (0, rows * LANES - n))).reshape(N_DEV, rows, LANES)


def _unpack(buf, shapes, lead=()):
    flat = buf.reshape(lead + (-1,))
    out, off = [], 0
    for s in shapes:
        n = math.prod(s)
        out.append(flat[..., off:off + n].reshape(lead + tuple(s)))
        off += n
    return out


def _to_shards(full, axis):
    s = full.shape
    return jnp.moveaxis(full.reshape(s[:axis] + (N_DEV, s[axis] // N_DEV) + s[axis + 1:]), axis, 0)


def _from_shards(g8, axis):
    m = jnp.moveaxis(g8, 0, axis)
    s = m.shape
    return m.reshape(s[:axis] + (s[axis] * s[axis + 1],) + s[axis + 2:])


def _comm_call(body, name, xs, out_shapes):
    n = len(xs)
    hbm = pl.BlockSpec(memory_space=pltpu.HBM)
    return pl.pallas_call(
        body, name=name, out_shape=out_shapes, in_specs=[hbm] * n, out_specs=[hbm] * n,
        scratch_shapes=[pltpu.SemaphoreType.DMA((n, 7)), pltpu.SemaphoreType.DMA((n, 7)),
                        pltpu.SemaphoreType.DMA((n,))],
    )(*xs)


def _all_gather(xs):
    n = len(xs)

    def body(*refs):
        x_refs, out_refs = refs[:n], refs[n:2 * n]
        send_sems, recv_sems, local_sems = refs[2 * n:]
        x_, y_, c = lax.axis_index("x"), lax.axis_index("y"), lax.axis_index("c")
        me, sibling = (x_, y_, c), (x_, y_, 1 - c)
        chips = [(1 - x_, y_), (x_, 1 - y_), (1 - x_, 1 - y_)]

        def rows(a, px, py, pc):
            return out_refs[a].at[4 * px + 2 * py + pc]

        def copy(a, k, block, to, src=None):
            return pltpu.make_async_remote_copy(
                src_ref=rows(a, *block) if src is None else src, dst_ref=rows(a, *block),
                send_sem=send_sems.at[a, k], recv_sem=recv_sems.at[a, k], device_id=to, device_id_type=MESH)

        mine = [pltpu.make_async_copy(x_refs[a], rows(a, *me), local_sems.at[a]) for a in range(n)]
        for cp in mine:
            cp.start()
        first = []
        for a in range(n):
            first.append(copy(a, 0, me, sibling, src=x_refs[a]))
            first += [copy(a, 1 + j, me, (*chip, c), src=x_refs[a]) for j, chip in enumerate(chips)]
        for cp in first:
            cp.start()
        passed = []
        for j, chip in enumerate(chips):
            for a in range(n):
                copy(a, 1 + j, (*chip, c), me).wait_recv()
                passed.append(copy(a, 4 + j, (*chip, c), sibling))
                passed[-1].start()
        for a in range(n):
            copy(a, 0, sibling, me).wait_recv()
            for j, chip in enumerate(chips):
                copy(a, 4 + j, (*chip, 1 - c), me).wait_recv()
        for cp in first + passed:
            cp.wait_send()
        for cp in mine:
            cp.wait()

    return _comm_call(body, "all_gather", xs, [jax.ShapeDtypeStruct((N_DEV,) + x.shape, x.dtype) for x in xs])


def _all_to_all(xs):
    n = len(xs)

    def body(*refs):
        x_refs, out_refs = refs[:n], refs[n:2 * n]
        send_sems, recv_sems, local_sems = refs[2 * n:]
        x_, y_, c = lax.axis_index("x"), lax.axis_index("y"), lax.axis_index("c")
        me = 4 * x_ + 2 * y_ + c
        mine = [pltpu.make_async_copy(x_refs[a].at[me], out_refs[a].at[me], local_sems.at[a]) for a in range(n)]
        for cp in mine:
            cp.start()

        def peer(k):
            px = 1 - x_ if k & 4 else x_
            py = 1 - y_ if k & 2 else y_
            pc = 1 - c if k & 1 else c
            return px, py, pc

        def copy(a, k):
            px, py, pc = peer(k)
            return pltpu.make_async_remote_copy(
                src_ref=x_refs[a].at[4 * px + 2 * py + pc], dst_ref=out_refs[a].at[me],
                send_sem=send_sems.at[a, k - 1], recv_sem=recv_sems.at[a, k - 1], device_id=(px, py, pc),
                device_id_type=MESH)

        def arrival(a, k):
            px, py, pc = peer(k)
            slot = 4 * px + 2 * py + pc
            return pltpu.make_async_remote_copy(
                src_ref=x_refs[a].at[slot], dst_ref=out_refs[a].at[slot],
                send_sem=send_sems.at[a, k - 1], recv_sem=recv_sems.at[a, k - 1], device_id=(px, py, pc),
                device_id_type=MESH)

        sends = [copy(a, k) for k in range(1, N_DEV) for a in range(n)]
        for cp in sends:
            cp.start()
        for k in range(1, N_DEV):
            for a in range(n):
                arrival(a, k).wait_recv()
        for cp in sends:
            cp.wait_send()
        for cp in mine:
            cp.wait()

    return _comm_call(body, "all_to_all", xs, [jax.ShapeDtypeStruct(x.shape, x.dtype) for x in xs])


def _sum_blocks(x, rows_tile):
    _, r, c_ = x.shape

    def body(x_ref, o_ref):
        acc = x_ref[0].astype(F32)
        for d in range(1, N_DEV):
            acc = acc + x_ref[d].astype(F32)
        o_ref[...] = acc

    return pl.pallas_call(
        body, name="sum_blocks", grid=(r // rows_tile,),
        in_specs=[pl.BlockSpec((N_DEV, rows_tile, c_), lambda i: (0, i, 0))],
        out_specs=pl.BlockSpec((rows_tile, c_), lambda i: (i, 0)),
        out_shape=jax.ShapeDtypeStruct((r, c_), F32),
        compiler_params=_params("parallel"),
    )(x)


def _adamw(w, g, m, v, rows_tile):
    r, c_ = w.shape
    c1 = 1.0 - ADAM_B1 ** ADAM_STEP
    c2 = 1.0 - ADAM_B2 ** ADAM_STEP

    def body(w_ref, g_ref, m_ref, v_ref, d_ref, mo_ref, vo_ref):
        g_ = g_ref[...]
        m_ = ADAM_B1 * m_ref[...] + (1.0 - ADAM_B1) * g_
        v_ = ADAM_B2 * v_ref[...] + (1.0 - ADAM_B2) * (g_ * g_)
        m_hat = m_ / c1
        v_hat = v_ / c2
        d_ref[...] = -ADAM_LR * (m_hat / (jnp.sqrt(v_hat) + ADAM_EPS) + ADAM_WD * w_ref[...])
        mo_ref[...] = m_
        vo_ref[...] = v_

    spec = pl.BlockSpec((rows_tile, c_), lambda i: (i, 0))
    shape = jax.ShapeDtypeStruct((r, c_), F32)
    return pl.pallas_call(
        body, name="adamw", grid=(r // rows_tile,), in_specs=[spec] * 4, out_specs=[spec] * 3,
        out_shape=[shape] * 3, compiler_params=_params("parallel"),
    )(w, g, m, v)


def _pick(n, prefs):
    for p in prefs:
        if n % p == 0:
            return p
    return n


def _mat_spec(arr, tr, tc, r_of, c_of):
    if arr.ndim == 2:
        return pl.BlockSpec((tr, tc), lambda i, j, k: (r_of(i, j, k), c_of(i, j, k)))
    per = arr.shape[2] // tc
    return pl.BlockSpec((None, tr, tc), lambda i, j, k: (c_of(i, j, k) // per, r_of(i, j, k), c_of(i, j, k) % per))


def _mm(a, b, *, ta=False, tb=False, out_dtype=F32, out_seg=None, res=None, epi=None, aux=None,
        tm=None, tn=None, tk=None, name="mm"):
    def dims(x):
        return (x.shape[0], x.shape[1]) if x.ndim == 2 else (x.shape[1], x.shape[0] * x.shape[2])
    ar, ac = dims(a)
    br, bc = dims(b)
    m, k = (ac, ar) if ta else (ar, ac)
    n, kb = (br, bc) if tb else (bc, br)
    assert k == kb, (a.shape, b.shape, ta, tb)
    tm = tm or _pick(m, (1024, 512, 384, 256, 128) if ta else (640, 512, 384, 256, 128))
    tn = tn or _pick(n, (1024, 768, 512, 384, 256, 128))
    tk = tk or _pick(k, ((1664,) if ta else ()) + (1024, 768, 640, 512, 384, 256, 128))
    if out_seg:
        assert (n // out_seg) % tn == 0
    for x, t in ((a, tm if ta else tk), (b, tk if tb else tn)):
        if x.ndim == 3:
            assert x.shape[2] % t == 0
    nk = k // tk
    gi, gj, gk = (lambda j, i, kk: i), (lambda j, i, kk: j), (lambda j, i, kk: kk)
    a_spec = _mat_spec(a, tk, tm, gk, gi) if ta else _mat_spec(a, tm, tk, gi, gk)
    b_spec = _mat_spec(b, tn, tk, gj, gk) if tb else _mat_spec(b, tk, tn, gk, gj)
    out_like = jnp.zeros((out_seg, 0, n // out_seg)) if out_seg else jnp.zeros((0, n))
    o_spec = _mat_spec(out_like, tm, tn, gi, gj)
    o_shape = (out_seg, m, n // out_seg) if out_seg else (m, n)
    dn = (((0 if ta else 1,), (1 if tb else 0,)), ((), ()))
    extra = [x for x in (res, aux) if x is not None]
    assert not (res is not None and aux is not None)
    n_out = 2 if epi == "relu2" else 1

    def body(*refs):
        a_ref, b_ref = refs[0], refs[1]
        x_ref = refs[2] if extra else None
        outs = refs[2 + len(extra):2 + len(extra) + n_out]
        acc_ref = refs[-1] if nk > 1 else None
        part = lax.dot_general(a_ref[...], b_ref[...], dn, preferred_element_type=F32)

        def finish(acc):
            if epi == "relu2":
                outs[0][...] = acc
                r = jnp.maximum(acc, 0.0)
                outs[1][...] = (r * r).astype(BF16)
            elif epi == "relu2_bwd":
                outs[0][...] = (acc * (2.0 * jnp.maximum(x_ref[...], 0.0))).astype(out_dtype)
            elif res is not None:
                outs[0][...] = (acc + x_ref[...]).astype(out_dtype)
            else:
                outs[0][...] = acc.astype(out_dtype)

        if nk == 1:
            finish(part)
        else:
            kk = pl.program_id(2)

            @pl.when(kk == 0)
            def _():
                acc_ref[...] = part

            @pl.when(kk > 0)
            def _():
                acc_ref[...] += part

            @pl.when(kk == nk - 1)
            def _():
                finish(acc_ref[...])

    if epi == "relu2":
        out_shape = [jax.ShapeDtypeStruct(o_shape, F32), jax.ShapeDtypeStruct(o_shape, BF16)]
        out_specs = [o_spec, o_spec]
    else:
        out_shape = jax.ShapeDtypeStruct(o_shape, out_dtype)
        out_specs = o_spec
    x_specs = [pl.BlockSpec((tm, tn), lambda j, i, kk: (i, j))] * len(extra)
    res_ = pl.pallas_call(
        body, name=name, grid=(n // tn, m // tm, nk),
        in_specs=[a_spec, b_spec] + x_specs, out_specs=out_specs, out_shape=out_shape,
        scratch_shapes=[pltpu.VMEM((tm, tn), F32)] if nk > 1 else [],
        compiler_params=_params("parallel", "parallel", "arbitrary"),
    )(a, b, *extra)
    return res_


def _rms_fwd(x, g):
    t, d = x.shape
    tm = _row_tile(t)

    def body(x_ref, g_ref, o_ref):
        x_ = x_ref[...]
        rstd = lax.rsqrt(jnp.mean(x_ * x_, axis=-1, keepdims=True) + EPS)
        o_ref[...] = (x_ * rstd * g_ref[...]).astype(BF16)

    return pl.pallas_call(
        body, name="rms_fwd", grid=(t // tm,),
        in_specs=[pl.BlockSpec((tm, d), lambda i: (i, 0)), pl.BlockSpec((1, d), lambda i: (0, 0))],
        out_specs=pl.BlockSpec((tm, d), lambda i: (i, 0)),
        out_shape=jax.ShapeDtypeStruct((t, d), BF16), compiler_params=_params("parallel"),
    )(x, g.reshape(1, d))


def _rms_bwd(x, g, dy, dres=None, want_bf16=False):
    t, d = x.shape
    tm = _row_tile(t)
    has_res = dres is not None

    def body(*refs):
        x_ref, g_ref, dy_ref = refs[:3]
        r_ref = refs[3] if has_res else None
        outs = refs[3 + has_res:]
        x_ = x_ref[...]
        rstd = lax.rsqrt(jnp.mean(x_ * x_, axis=-1, keepdims=True) + EPS)
        xh = x_ * rstd
        dy_ = dy_ref[...]
        dxh = dy_ * g_ref[...]
        dx = rstd * (dxh - xh * jnp.mean(dxh * xh, axis=-1, keepdims=True))
        if has_res:
            dx = dx + r_ref[...]
        outs[0][...] = dx
        if want_bf16:
            outs[1][...] = dx.astype(BF16)
        dg_ref = outs[-1]

        @pl.when(pl.program_id(0) == 0)
        def _():
            dg_ref[...] = jnp.zeros_like(dg_ref)

        dg_ref[...] += jnp.sum(dy_ * xh, axis=0, keepdims=True)

    row = pl.BlockSpec((tm, d), lambda i: (i, 0))
    vec = pl.BlockSpec((1, d), lambda i: (0, 0))
    out_shape = [jax.ShapeDtypeStruct((t, d), F32)] + ([jax.ShapeDtypeStruct((t, d), BF16)] if want_bf16 else []) \
        + [jax.ShapeDtypeStruct((1, d), F32)]
    out_specs = [row] + ([row] if want_bf16 else []) + [vec]
    return pl.pallas_call(
        body, name="rms_bwd", grid=(t // tm,),
        in_specs=[row, vec, row] + ([row] if has_res else []), out_specs=out_specs, out_shape=out_shape,
        compiler_params=_params("arbitrary"),
    )(x, g.reshape(1, d), dy, *([dres] if has_res else []))


def _swap_rope_halves(y):
    lane = lax.broadcasted_iota(jnp.int32, y.shape, 1)
    half = MLA_ROPE // 2
    swapped = jnp.where(lane < MLA_NOPE + half, pltpu.roll(y, LANES - half, axis=1), pltpu.roll(y, half, axis=1))
    return jnp.where((lane >= MLA_NOPE) & (lane < MLA_QK), swapped, 0.0)


def _head_norm_fwd(x, g, n_valid, scale, rope=None):
    h, t, w = x.shape
    tm = _row_tile(t)

    def body(*refs):
        x_ref, g_ref = refs[:2]
        o_ref = refs[-1]
        x_ = x_ref[...].reshape(h * tm, w)
        rstd = lax.rsqrt(jnp.sum(x_ * x_, axis=-1, keepdims=True) * (1.0 / n_valid) + EPS)
        y = x_ * rstd * (g_ref[...] * scale)
        if rope is not None:
            y3, s3 = y.reshape(h, tm, w), _swap_rope_halves(y).reshape(h, tm, w)
            y = (y3 * refs[2][...][None] + s3 * refs[3][...][None]).reshape(h * tm, w)
        o_ref[...] = y.reshape(h, tm, w).astype(BF16)

    blk = pl.BlockSpec((h, tm, w), lambda i: (0, i, 0))
    tab = pl.BlockSpec((tm, w), lambda i: (i, 0))
    return pl.pallas_call(
        body, name="head_norm_fwd", grid=(t // tm,),
        in_specs=[blk, pl.BlockSpec((1, w), lambda i: (0, 0))] + ([tab, tab] if rope is not None else []),
        out_specs=blk, out_shape=jax.ShapeDtypeStruct((h, t, w), BF16),
        compiler_params=_params("parallel"),
    )(x, g, *(rope if rope is not None else ()))


def _head_norm_bwd(x, g, dout, n_valid, scale, rope=None):
    h, t, w = x.shape
    tm = 320 if t % 320 == 0 else BLOCK

    def body(*refs):
        x_ref, g_ref, do_ref = refs[:3]
        dx_ref, dsum_ref, dg_ref = refs[-3:]
        dy = do_ref[...]
        if rope is not None:
            c_, s_ = refs[3][...], refs[4][...]
            ds = (dy * s_[None]).reshape(h * tm, w)
            dy = dy * c_[None] + _swap_rope_halves(ds).reshape(h, tm, w)
        x_ = x_ref[...]
        rstd = lax.rsqrt(jnp.sum(x_ * x_, axis=-1, keepdims=True) * (1.0 / n_valid) + EPS)
        xh = x_ * rstd
        dxh = dy * (g_ref[...] * scale)[None]
        dx = rstd * (dxh - xh * (jnp.sum(dxh * xh, axis=-1, keepdims=True) * (1.0 / n_valid)))
        dx_ref[...] = dx
        dsum_ref[...] = jnp.sum(dx, axis=0)

        @pl.when(pl.program_id(0) == 0)
        def _():
            dg_ref[...] = jnp.zeros_like(dg_ref)

        dg_ref[...] += scale * jnp.sum(jnp.sum(dy * xh, axis=0), axis=0, keepdims=True)

    blk = pl.BlockSpec((h, tm, w), lambda i: (0, i, 0))
    tab = pl.BlockSpec((tm, w), lambda i: (i, 0))
    vec = pl.BlockSpec((1, w), lambda i: (0, 0))
    return pl.pallas_call(
        body, name="head_norm_bwd", grid=(t // tm,),
        in_specs=[blk, vec, blk] + ([tab, tab] if rope is not None else []),
        out_specs=[blk, tab, vec],
        out_shape=[jax.ShapeDtypeStruct((h, t, w), F32), jax.ShapeDtypeStruct((t, w), F32),
                   jax.ShapeDtypeStruct((1, w), F32)],
        compiler_params=_params("arbitrary"),
    )(x, g, dout, *(rope if rope is not None else ()))


def _tri(n, upper):
    r = lax.broadcasted_iota(jnp.int32, (n, n), 0)
    c = lax.broadcasted_iota(jnp.int32, (n, n), 1)
    return ((r <= c) if upper else (r >= c)).astype(F32)


def _gate_mask(shape, row0):
    lane = lax.broadcasted_iota(jnp.int32, shape, 1)
    row = row0 + lax.broadcasted_iota(jnp.int32, shape, 0)
    return (lane >= TAIL_F) & (lane < TAIL_F + HEADS) & (row >= PAD)


def _gate_fwd(z, bias):
    t = z.shape[0]
    tm = BLOCK
    tail = C_TAIL // LANES

    def body(z_ref, b_ref, o_ref, carry):
        i = pl.program_id(0)

        @pl.when(i == 0)
        def _():
            carry[...] = jnp.zeros_like(carry)

        x_ = z_ref[...] + b_ref[...]
        logf = jnp.minimum(x_, 0.0) - jnp.log1p(jnp.exp(-jnp.abs(x_)))
        logf = jnp.where(_gate_mask(logf.shape, i * tm), logf, 0.0)
        cum = jnp.dot(_tri(tm, False), logf, preferred_element_type=F32, precision=lax.Precision.HIGHEST) + carry[...]
        o_ref[...] = cum
        carry[...] = cum[tm - 1:tm, :]

    return pl.pallas_call(
        body, name="gate_fwd", grid=(t // tm,),
        in_specs=[pl.BlockSpec((tm, LANES), lambda i: (i, tail)), pl.BlockSpec((1, LANES), lambda i: (0, 0))],
        out_specs=pl.BlockSpec((tm, LANES), lambda i: (i, 0)),
        out_shape=jax.ShapeDtypeStruct((t, LANES), F32),
        scratch_shapes=[pltpu.VMEM((1, LANES), F32)], compiler_params=_params("arbitrary"),
    )(z, bias)


def _gate_bwd(z, bias, dcum):
    t = z.shape[0]
    tm = BLOCK
    nb = t // tm
    tail = C_TAIL // LANES

    def body(z_ref, b_ref, d_ref, o_ref, db_ref, carry):
        i = pl.program_id(0)

        @pl.when(i == 0)
        def _():
            carry[...] = jnp.zeros_like(carry)
            db_ref[...] = jnp.zeros_like(db_ref)

        rc = jnp.dot(_tri(tm, True), d_ref[...], preferred_element_type=F32, precision=lax.Precision.HIGHEST) + carry[...]
        carry[...] = rc[0:1, :]
        x_ = z_ref[...] + b_ref[...]
        sig_neg = 1.0 / (1.0 + jnp.exp(x_))
        dl = jnp.where(_gate_mask(rc.shape, (nb - 1 - i) * tm), rc * sig_neg, 0.0)
        o_ref[...] = dl
        db_ref[...] += jnp.sum(dl, axis=0, keepdims=True)

    return pl.pallas_call(
        body, name="gate_bwd", grid=(nb,),
        in_specs=[pl.BlockSpec((tm, LANES), lambda i: (nb - 1 - i, tail)), pl.BlockSpec((1, LANES), lambda i: (0, 0)),
                  pl.BlockSpec((tm, LANES), lambda i: (nb - 1 - i, 0))],
        out_specs=[pl.BlockSpec((tm, LANES), lambda i: (nb - 1 - i, 0)), pl.BlockSpec((1, LANES), lambda i: (0, 0))],
        out_shape=[jax.ShapeDtypeStruct((t, LANES), F32), jax.ShapeDtypeStruct((1, LANES), F32)],
        scratch_shapes=[pltpu.VMEM((1, LANES), F32)], compiler_params=_params("arbitrary"),
    )(z, bias, dcum)


def _pairs(nb, by_query):
    if by_query:
        pr = [(i, j) for i in range(nb) for j in range(i + 1)]
    else:
        pr = [(i, j) for j in range(nb) for i in range(j, nb)]
    return (jnp.asarray(np.array([p[0] for p in pr], np.int32)),
            jnp.asarray(np.array([p[1] for p in pr], np.int32)))


HEADS_PER_STEP = 8


def _mask_scores(s, i, j, tile):
    qp = i * tile + lax.broadcasted_iota(jnp.int32, s.shape, 0)
    kp = j * tile + lax.broadcasted_iota(jnp.int32, s.shape, 1)
    return jnp.where((kp <= qp) & (kp >= PAD), s, NEG)


def _pipelined(n, front, back):
    nxt = front(0)
    for h in range(n):
        cur = nxt
        if h + 1 < n:
            nxt = front(h + 1)
        back(h, cur)


def _nt_dot(a, b):
    return lax.dot_general(a, b, (((1,), (1,)), ((), ())), preferred_element_type=F32)


def _tn_dot(a, b):
    return lax.dot_general(a, b, (((0,), (0,)), ((), ())), preferred_element_type=F32)


def _attn_specs(hb, tile, dk, dv):
    q_of = lambda h, p, it, jt: (h, it[p], 0)
    k_of = lambda h, p, it, jt: (h, jt[p], 0)
    return dict(
        q=pl.BlockSpec((hb, tile, dk), q_of), k=pl.BlockSpec((hb, tile, dk), k_of),
        v=pl.BlockSpec((hb, tile, dv), k_of), ov=pl.BlockSpec((hb, tile, dv), q_of),
        col=pl.BlockSpec((hb, tile, 1), q_of), colk=pl.BlockSpec((hb, tile, 1), k_of),
        fr=pl.BlockSpec((hb, 1, tile), lambda h, p, it, jt: (h, 0, jt[p])),
        rowq=pl.BlockSpec((hb, 1, tile), lambda h, p, it, jt: (h, 0, it[p])))


def _attn_fwd(q, k, v, key_bias=None):
    h, t, dk = q.shape
    dv = v.shape[-1]
    tile = _row_tile(t)
    nb = t // tile
    hb = HEADS_PER_STEP
    biased = key_bias is not None
    it, jt = _pairs(nb, True)
    sp = _attn_specs(hb, tile, dk, dv)

    def body(it_ref, jt_ref, *refs):
        q_ref, k_ref, v_ref = refs[:3]
        b_ref = refs[3] if biased else None
        o_ref, lse_ref, m_sc, l_sc, acc_sc = refs[-5:]
        p = pl.program_id(1)
        i, j = it_ref[p], jt_ref[p]

        @pl.when(j == 0)
        def _():
            m_sc[...] = jnp.full_like(m_sc, NEG)
            l_sc[...] = jnp.zeros_like(l_sc)
            acc_sc[...] = jnp.zeros_like(acc_sc)

        def step(masked):
            def front(n):
                return _nt_dot(q_ref[n], k_ref[n])

            def back(n, s):
                if biased:
                    s = s + b_ref[n]
                if masked:
                    s = _mask_scores(s, i, j, tile)
                m_prev = m_sc[n]
                m_new = jnp.maximum(m_prev, jnp.max(s, axis=-1, keepdims=True))
                alpha = jnp.exp(m_prev - m_new)
                e = jnp.exp(s - m_new)
                l_sc[n] = alpha * l_sc[n] + jnp.sum(e, axis=-1, keepdims=True)
                acc_sc[n] = alpha * acc_sc[n] + jnp.dot(e.astype(BF16), v_ref[n], preferred_element_type=F32)
                m_sc[n] = m_new

            _pipelined(hb, front, back)

        edge = (j == i) | (j == 0)
        pl.when(edge)(lambda: step(True))
        pl.when(jnp.logical_not(edge))(lambda: step(False))

        @pl.when(j == i)
        def _():
            row = i * tile + lax.broadcasted_iota(jnp.int32, (hb, tile, 1), 1)
            o_ref[...] = jnp.where(row >= PAD, acc_sc[...] / l_sc[...], 0.0)
            lse_ref[...] = m_sc[...] + jnp.log(l_sc[...])

    grid_spec = pltpu.PrefetchScalarGridSpec(
        num_scalar_prefetch=2, grid=(h // hb, int(it.shape[0])),
        in_specs=[sp["q"], sp["k"], sp["v"]] + ([sp["fr"]] if biased else []),
        out_specs=[sp["ov"], sp["col"]],
        scratch_shapes=[pltpu.VMEM((hb, tile, 1), F32), pltpu.VMEM((hb, tile, 1), F32),
                        pltpu.VMEM((hb, tile, dv), F32)])
    return pl.pallas_call(
        body, name="attn_fwd", grid_spec=grid_spec,
        out_shape=[jax.ShapeDtypeStruct((h, t, dv), F32), jax.ShapeDtypeStruct((h, t, 1), F32)],
        compiler_params=_params("parallel", "arbitrary"),
    )(it, jt, q, k, v, *((key_bias,) if biased else ()))


def _attn_delta(o, do):
    h, t, dv = o.shape
    tm = _row_tile(t)

    def body(o_ref, do_ref, d_ref):
        d_ref[...] = jnp.sum(o_ref[...] * do_ref[...], axis=-1, keepdims=True)

    blk = pl.BlockSpec((h, tm, dv), lambda i: (0, i, 0))
    return pl.pallas_call(
        body, name="attn_delta", grid=(t // tm,), in_specs=[blk, blk],
        out_specs=pl.BlockSpec((h, tm, 1), lambda i: (0, i, 0)),
        out_shape=jax.ShapeDtypeStruct((h, t, 1), F32), compiler_params=_params("parallel"),
    )(o, do)


BWD_HEADS_PER_STEP = (4, 2)


def _attn_bwd(q, k, v, do, lse_row, delta_row, key_bias=None):
    h, t, dk = q.shape
    dv = v.shape[-1]
    tile = _row_tile(t)
    nb = t // tile
    decay = key_bias is not None
    hb = BWD_HEADS_PER_STEP[int(decay)]
    it, jt = _pairs(nb, False)
    sp = _attn_specs(hb, tile, dk, dv)
    n_in = 7 if decay else 6
    n_out = 5 if decay else 3

    def body(it_ref, jt_ref, *refs):
        q_ref, k_ref, v_ref, do_ref, lse_ref, delta_ref = refs[:6]
        b_ref = refs[6] if decay else None
        dq_ref, dk_ref, dv_ref = refs[n_in:n_in + 3]
        rs_ref, ks_ref = (refs[n_in + 3], refs[n_in + 4]) if decay else (None, None)
        scratch = refs[n_in + n_out:]
        dq_sc, dk_sc, dv_sc = scratch[:3]
        ks_sc = scratch[3] if decay else None
        p = pl.program_id(1)
        i, j = it_ref[p], jt_ref[p]

        @pl.when(p == 0)
        def _():
            dq_sc[...] = jnp.zeros_like(dq_sc)
            if decay:
                rs_ref[...] = jnp.zeros_like(rs_ref)

        @pl.when(i == j)
        def _():
            dk_sc[...] = jnp.zeros_like(dk_sc)
            dv_sc[...] = jnp.zeros_like(dv_sc)
            if decay:
                ks_sc[...] = jnp.zeros_like(ks_sc)

        def step(masked):
            def front(n):
                return _nt_dot(k_ref[n], q_ref[n]), _nt_dot(v_ref[n], do_ref[n].astype(BF16))

            def back(n, s_dp):
                s, dp = s_dp
                if decay:
                    s = s + b_ref[n]
                if masked:
                    kp = j * tile + lax.broadcasted_iota(jnp.int32, s.shape, 0)
                    qp = i * tile + lax.broadcasted_iota(jnp.int32, s.shape, 1)
                    s = jnp.where((kp <= qp) & (kp >= PAD), s, NEG)
                pr = jnp.exp(s - lse_ref[n])
                ds = pr * (dp - delta_ref[n])
                ds_b = ds.astype(BF16)
                dv_sc[n] += jnp.dot(pr.astype(BF16), do_ref[n].astype(BF16), preferred_element_type=F32)
                dk_sc[n] += jnp.dot(ds_b, q_ref[n], preferred_element_type=F32)
                dq_sc[n, i] += _tn_dot(ds_b, k_ref[n])
                if decay:
                    rs_ref[n, i] += jnp.sum(ds, axis=0, keepdims=True)
                    ks_sc[n] += jnp.sum(ds, axis=-1, keepdims=True)

            _pipelined(hb, front, back)

        edge = (j == i) | (j == 0)
        pl.when(edge)(lambda: step(True))
        pl.when(jnp.logical_not(edge))(lambda: step(False))

        @pl.when(i == j)
        def _():
            dq_ref[...] = dq_sc[:, j]

        @pl.when(i == nb - 1)
        def _():
            dk_ref[...] = dk_sc[...]
            dv_ref[...] = dv_sc[...]
            if decay:
                ks_ref[...] = ks_sc[...]

    rows_out = pl.BlockSpec((hb, nb, 1, tile), lambda hh, p, it, jt: (hh, 0, 0, 0))
    grid_spec = pltpu.PrefetchScalarGridSpec(
        num_scalar_prefetch=2, grid=(h // hb, int(it.shape[0])),
        in_specs=[sp["q"], sp["k"], sp["v"], sp["ov"], sp["rowq"], sp["rowq"]] + ([sp["colk"]] if decay else []),
        out_specs=[sp["k"], sp["k"], sp["v"]] + ([rows_out, sp["colk"]] if decay else []),
        scratch_shapes=[pltpu.VMEM((hb, nb, tile, dk), F32), pltpu.VMEM((hb, tile, dk), F32),
                        pltpu.VMEM((hb, tile, dv), F32)] + ([pltpu.VMEM((hb, tile, 1), F32)] if decay else []))
    out_shape = [jax.ShapeDtypeStruct((h, t, dk), F32), jax.ShapeDtypeStruct((h, t, dk), F32),
                 jax.ShapeDtypeStruct((h, t, dv), F32)] \
        + ([jax.ShapeDtypeStruct((h, nb, 1, tile), F32), jax.ShapeDtypeStruct((h, t, 1), F32)] if decay else [])
    return pl.pallas_call(
        body, name="attn_bwd", grid_spec=grid_spec, out_shape=out_shape,
        compiler_params=_params("parallel", "arbitrary"),
    )(it, jt, q, k, v, do, lse_row, delta_row, *((key_bias,) if decay else ()))


CONV_COLS = 512


def _shift_down(g, prev, n):
    out = pltpu.roll(g, n, axis=0)
    row = lax.broadcasted_iota(jnp.int32, g.shape, 0)
    for r in range(n):
        out = jnp.where(row == r, prev[8 - n + r:8 - n + r + 1, :], out)
    return out


def _shift_up(g, nxt, n):
    tm = g.shape[0]
    out = pltpu.roll(g, tm - n, axis=0)
    row = lax.broadcasted_iota(jnp.int32, g.shape, 0)
    for r in range(n):
        out = jnp.where(row == tm - n + r, nxt[r:r + 1, :], out)
    return out


def _conv_fwd(z3, w):
    _, t, d = z3.shape
    tm = _row_tile(t)
    tc = CONV_COLS

    def body(z_ref, zp_ref, w_ref, o_ref):
        i = pl.program_id(1)
        g = z_ref[1] * z_ref[2]
        gp = jnp.where(i > 0, zp_ref[1] * zp_ref[2], 0.0)
        w_ = w_ref[...]
        y = w_[2:3] * g + w_[1:2] * _shift_down(g, gp, 1) + w_[0:1] * _shift_down(g, gp, 2)
        o_ref[...] = (z_ref[0] * y).astype(BF16)

    return pl.pallas_call(
        body, name="conv_fwd", grid=(d // tc, t // tm),
        in_specs=[pl.BlockSpec((3, tm, tc), lambda j, i: (0, i, j)),
                  pl.BlockSpec((3, 8, tc), lambda j, i: (0, jnp.maximum(i * (tm // 8) - 1, 0), j)),
                  pl.BlockSpec((3, tc), lambda j, i: (0, j))],
        out_specs=pl.BlockSpec((tm, tc), lambda j, i: (i, j)),
        out_shape=jax.ShapeDtypeStruct((t, d), BF16), compiler_params=_params("parallel", "parallel"),
    )(z3, z3, w)


def _conv_bwd(z3, w, dyb):
    _, t, d = z3.shape
    tm = _row_tile(t)
    tc = CONV_COLS
    ni = t // tm

    def body(z_ref, zp_ref, zn_ref, d_ref, dn_ref, w_ref, dz_ref, dw_ref):
        i = pl.program_id(1)
        gb, gc, u = z_ref[0], z_ref[1], z_ref[2]
        g = gc * u
        gp = jnp.where(i > 0, zp_ref[1] * zp_ref[2], 0.0)
        w_ = w_ref[...]
        g1, g2 = _shift_down(g, gp, 1), _shift_down(g, gp, 2)
        y = w_[2:3] * g + w_[1:2] * g1 + w_[0:1] * g2
        dyb_ = d_ref[...]
        dy = dyb_ * gb
        dyn = jnp.where(i < ni - 1, dn_ref[...] * zn_ref[0], 0.0)
        dg = w_[2:3] * dy + w_[1:2] * _shift_up(dy, dyn, 1) + w_[0:1] * _shift_up(dy, dyn, 2)
        dz_ref[0] = (dyb_ * y).astype(BF16)
        dz_ref[1] = (dg * u).astype(BF16)
        dz_ref[2] = (dg * gc).astype(BF16)

        @pl.when(i == 0)
        def _():
            dw_ref[...] = jnp.zeros_like(dw_ref)

        dw_ref[...] += jnp.concatenate([jnp.sum(dy * g2, axis=0, keepdims=True),
                                        jnp.sum(dy * g1, axis=0, keepdims=True),
                                        jnp.sum(dy * g, axis=0, keepdims=True)], axis=0)

    cur = pl.BlockSpec((3, tm, tc), lambda j, i: (0, i, j))
    return pl.pallas_call(
        body, name="conv_bwd", grid=(d // tc, ni),
        in_specs=[cur,
                  pl.BlockSpec((3, 8, tc), lambda j, i: (0, jnp.maximum(i * (tm // 8) - 1, 0), j)),
                  pl.BlockSpec((3, 8, tc), lambda j, i: (0, jnp.minimum((i + 1) * (tm // 8), t // 8 - 1), j)),
                  pl.BlockSpec((tm, tc), lambda j, i: (i, j)),
                  pl.BlockSpec((8, tc), lambda j, i: (jnp.minimum((i + 1) * (tm // 8), t // 8 - 1), j)),
                  pl.BlockSpec((3, tc), lambda j, i: (0, j))],
        out_specs=[cur, pl.BlockSpec((3, tc), lambda j, i: (0, j))],
        out_shape=[jax.ShapeDtypeStruct((3, t, d), BF16), jax.ShapeDtypeStruct((3, d), F32)],
        compiler_params=_params("parallel", "arbitrary"),
    )(z3, z3, z3, dyb, dyb, w)


def _loss_head(h, target):
    t, d = h.shape
    tm = BLOCK

    def body(h_ref, t_ref, dh_ref, dhb_ref, loss_ref):
        i = pl.program_id(0)

        @pl.when(i == 0)
        def _():
            loss_ref[...] = jnp.zeros_like(loss_ref)

        err = jnp.where(i > 0, h_ref[...] - t_ref[...], 0.0)
        dh = err * (1.0 / d)
        dh_ref[...] = dh
        dhb_ref[...] = dh.astype(BF16)
        loss_ref[...] += 0.5 * jnp.sum(jnp.sum(err * err, axis=-1, keepdims=True) * (1.0 / d), axis=0, keepdims=True)

    row = pl.BlockSpec((tm, d), lambda i: (i, 0))
    return pl.pallas_call(
        body, name="loss_head", grid=(t // tm,),
        in_specs=[row, pl.BlockSpec((tm, d), lambda i: (jnp.maximum(i - 1, 0), 0))],
        out_specs=[row, row, pl.BlockSpec((1, 1), lambda i: (0, 0))],
        out_shape=[jax.ShapeDtypeStruct((t, d), F32), jax.ShapeDtypeStruct((t, d), BF16),
                   jax.ShapeDtypeStruct((1, 1), F32)],
        compiler_params=_params("arbitrary"),
    )(h, target)


def _heads_major(x, width):
    t = x.shape[0]
    return jnp.transpose(x.reshape(t, HEADS, width), (1, 0, 2))


def _heads_minor(x):
    h, t, w = x.shape
    return jnp.transpose(x, (1, 0, 2)).reshape(t, h * w)


def _rope_tables(t):
    pos = jnp.arange(t, dtype=F32) - PAD
    inv_freq = ROPE_BASE ** (-jnp.arange(0, MLA_ROPE, 2, dtype=F32) / MLA_ROPE)
    ang = pos[:, None] * inv_freq[None, :]
    cos, sin = jnp.cos(ang), jnp.sin(ang)
    one, zero = jnp.ones((t, MLA_NOPE), F32), jnp.zeros((t, MLA_NOPE), F32)
    tail = jnp.zeros((t, LANES - MLA_QK), F32)
    return (jnp.concatenate([one, cos, cos, tail], axis=1), jnp.concatenate([zero, -sin, sin, tail], axis=1))


def _pad_lanes(x, width=LANES):
    return jnp.pad(x, [(0, 0)] * (x.ndim - 1) + [(0, width - x.shape[-1])])


def _permute_in_attn(w):
    return jnp.concatenate([w[:, :640], w[:, 672:2208], w[:, 640:672], w[:, 2208:2216],
                            jnp.zeros((w.shape[0], ATTN_IN_PAD - 2216), w.dtype)], axis=1)


def _unpermute_in_attn(dw):
    return jnp.concatenate([dw[:, :640], dw[:, 2176:2208], dw[:, 640:2176], dw[:, 2208:2216]], axis=1)


def _attn_layer_fwd(hn, wl, rope):
    t = hn.shape[0]
    z = _mm(hn, wl["w_in"], name="attn_in")
    cqn = _rms_fwd(z[:, C_CQ:C_CQ + Q_LORA], wl["g_cq"])
    ckvn = _rms_fwd(z[:, C_CKV:C_CKV + KV_LORA], wl["g_ckv"])
    qf = _mm(cqn, wl["w_uq"], name="mla_uq")
    kvf = _mm(ckvn, wl["w_ukv"], name="mla_ukv")
    kv3 = kvf.reshape(t, HEADS, MLA_NOPE + MLA_V)
    xq = _pad_lanes(_heads_major(qf, MLA_QK))
    k_pe = jnp.broadcast_to(z[None, :, C_TAIL:C_TAIL + MLA_ROPE], (HEADS, t, MLA_ROPE))
    xk = _pad_lanes(jnp.concatenate([jnp.transpose(kv3[:, :, :MLA_NOPE], (1, 0, 2)), k_pe], axis=-1))
    v_mla = jnp.transpose(kv3[:, :, MLA_NOPE:], (1, 0, 2)).astype(BF16)
    gq, gk = _pad_lanes(wl["g_q_mla"].reshape(1, -1)), _pad_lanes(wl["g_k_mla"].reshape(1, -1))
    q_mla = _head_norm_fwd(xq, gq, MLA_QK, MLA_QK ** -0.5, rope)
    k_mla = _head_norm_fwd(xk, gk, MLA_QK, 1.0, rope)
    o_mla, lse_mla = _attn_fwd(q_mla, k_mla, v_mla)
    xfq = _heads_major(z[:, C_FQ:C_FQ + HEADS * FOX_DIM], FOX_DIM)
    xfk = _heads_major(z[:, C_FK:C_FK + HEADS * FOX_DIM], FOX_DIM)
    v_fox = _heads_major(z[:, C_FV:C_FV + HEADS * FOX_DIM], FOX_DIM).astype(BF16)
    bias = jnp.pad(wl["b_forget"].reshape(1, -1), ((0, 0), (TAIL_F, LANES - TAIL_F - HEADS)))
    cum = _gate_fwd(z, bias)
    neg_f = -jnp.transpose(cum[:, TAIL_F:TAIL_F + HEADS])
    q_fox = _head_norm_fwd(xfq, wl["g_q_fox"].reshape(1, -1), FOX_DIM, FOX_DIM ** -0.5)
    k_fox = _head_norm_fwd(xfk, wl["g_k_fox"].reshape(1, -1), FOX_DIM, 1.0)
    o_fox, lse_fox = _attn_fwd(q_fox, k_fox, v_fox, neg_f[:, None, :])
    cat = jnp.concatenate([_heads_minor(o_mla), _heads_minor(o_fox)], axis=1).astype(BF16)
    saved = dict(z=z, cqn=cqn, ckvn=ckvn, xq=xq, xk=xk, v_mla=v_mla, q_mla=q_mla, k_mla=k_mla, o_mla=o_mla,
                 lse_mla=lse_mla, xfq=xfq, xfk=xfk, v_fox=v_fox, q_fox=q_fox, k_fox=k_fox, bias=bias,
                 neg_f=neg_f, o_fox=o_fox, lse_fox=lse_fox, cat=cat, gq=gq, gk=gk)
    return cat, saved


def _attn_layer_bwd(dcat, hn, wl, sv, rope):
    t = hn.shape[0]
    g = {}
    do_mla = _heads_major(dcat[:, :HEADS * MLA_V], MLA_V)
    do_fox = _heads_major(dcat[:, HEADS * MLA_V:], FOX_DIM)
    as_row = lambda c: jnp.transpose(c, (0, 2, 1))
    qkv = (sv["q_fox"], sv["k_fox"], sv["v_fox"])
    neg_f = sv["neg_f"]
    delta = _attn_delta(sv["o_fox"], do_fox)
    dq_fox, dk_fox, dv_fox, row_sums, key_sums = _attn_bwd(
        *qkv, do_fox, as_row(sv["lse_fox"]), as_row(delta), neg_f[:, :, None])
    dcum = jnp.pad(jnp.transpose(row_sums.reshape(HEADS, t) - key_sums[:, :, 0]),
                   ((0, 0), (TAIL_F, LANES - TAIL_F - HEADS)))
    dtail_f, dbias = _gate_bwd(sv["z"], sv["bias"], dcum)
    g["b_forget"] = dbias[0, TAIL_F:TAIL_F + HEADS]
    dxfq, _, dgq = _head_norm_bwd(sv["xfq"], wl["g_q_fox"].reshape(1, -1), dq_fox, FOX_DIM, FOX_DIM ** -0.5)
    dxfk, _, dgk = _head_norm_bwd(sv["xfk"], wl["g_k_fox"].reshape(1, -1), dk_fox, FOX_DIM, 1.0)
    g["g_q_fox"], g["g_k_fox"] = dgq[0], dgk[0]
    qkv = (sv["q_mla"], sv["k_mla"], sv["v_mla"])
    delta = _attn_delta(sv["o_mla"], do_mla)
    dq_mla, dk_mla, dv_mla = _attn_bwd(*qkv, do_mla, as_row(sv["lse_mla"]), as_row(delta))
    dxq, _, dgq = _head_norm_bwd(sv["xq"], sv["gq"], dq_mla, MLA_QK, MLA_QK ** -0.5, rope)
    dxk, dxk_sum, dgk = _head_norm_bwd(sv["xk"], sv["gk"], dk_mla, MLA_QK, 1.0, rope)
    g["g_q_mla"], g["g_k_mla"] = dgq[0, :MLA_QK], dgk[0, :MLA_QK]
    dqf = _heads_minor(dxq[:, :, :MLA_QK]).astype(BF16)
    dkvf = _heads_minor(jnp.concatenate([dxk[:, :, :MLA_NOPE], dv_mla], axis=-1)).astype(BF16)
    g["w_uq"] = _mm(sv["cqn"], dqf, ta=True, name="d_w_uq")
    g["w_ukv"] = _mm(sv["ckvn"], dkvf, ta=True, name="d_w_ukv")
    dcqn = _mm(dqf, wl["w_uq"], tb=True, name="d_cqn")
    dckvn = _mm(dkvf, wl["w_ukv"], tb=True, name="d_ckvn")
    z = sv["z"]
    dcq, dg_cq = _rms_bwd(z[:, C_CQ:C_CQ + Q_LORA], wl["g_cq"], dcqn)
    dckv, dg_ckv = _rms_bwd(z[:, C_CKV:C_CKV + KV_LORA], wl["g_ckv"], dckvn)
    g["g_cq"], g["g_ckv"] = dg_cq[0], dg_ckv[0]
    tail = jnp.concatenate([dxk_sum[:, MLA_NOPE:MLA_QK], dtail_f[:, TAIL_F:]], axis=1)
    dz = jnp.concatenate([dcq, dckv, _heads_minor(dxfq), _heads_minor(dxfk), _heads_minor(dv_fox), tail],
                         axis=1).astype(BF16)
    g["w_in"] = _mm(hn, dz, ta=True, name="d_w_in_attn")
    dhn = _mm(dz, wl["w_in"], tb=True, name="d_hn_attn")
    return dhn, g


def _local_step(x, target, w):
    seq = x.shape[0]
    t = seq + BLOCK
    rope = _rope_tables(t)
    h = jnp.concatenate([jnp.zeros((PAD, D_MODEL), F32), w["meta_tokens"], x], axis=0)
    tape = []
    for layer in range(DEPTH):
        j = layer // 2
        hn = _rms_fwd(h, w["g_mix"][layer])
        if layer % 2 == 0:
            wl = dict(w_in=w["w_in_attn"][j], g_cq=w["g_cq"][j], w_uq=w["w_uq"][j], g_ckv=w["g_ckv"][j],
                      w_ukv=w["w_ukv"][j], g_q_mla=w["g_q_mla"][j], g_k_mla=w["g_k_mla"][j],
                      g_q_fox=w["g_q_fox"][j], g_k_fox=w["g_k_fox"][j], b_forget=w["b_forget"][j])
            mixed, sv = _attn_layer_fwd(hn, wl, rope)
            h1 = _mm(mixed, w["w_out_attn"][j], res=h, name="attn_out")
        else:
            wl = None
            z3 = _mm(hn, w["w_in_conv"][j], out_seg=3, name="conv_in")
            mixed = _conv_fwd(z3, w["conv_w"][j])
            sv = dict(z3=z3)
            h1 = _mm(mixed, w["w_out_conv"][j], res=h, name="conv_out")
        hn2 = _rms_fwd(h1, w["g_mlp"][layer])
        u, act = _mm(hn2, w["w_mlp_up"][layer], epi="relu2", name="mlp_up")
        h2 = _mm(act, w["w_mlp_down"][layer], res=h1, name="mlp_down")
        tape.append(dict(h=h, hn=hn, wl=wl, sv=sv, mixed=mixed, h1=h1, hn2=hn2, u=u, act=act))
        h = h2

    dh, dh_b, loss = _loss_head(h, target)
    g = {n: [None] * (DEPTH if n in ("g_mix", "g_mlp", "w_mlp_up", "w_mlp_down") else DEPTH // 2)
         for n in WEIGHTS if n != "meta_tokens"}
    for layer in reversed(range(DEPTH)):
        j = layer // 2
        tp = tape[layer]
        g["w_mlp_down"][layer] = _mm(tp["act"], dh_b, ta=True, name="d_w_down")
        du = _mm(dh_b, w["w_mlp_down"][layer], tb=True, epi="relu2_bwd", aux=tp["u"], out_dtype=BF16, name="d_u")
        g["w_mlp_up"][layer] = _mm(tp["hn2"], du, ta=True, name="d_w_up")
        dhn2 = _mm(du, w["w_mlp_up"][layer], tb=True, name="d_hn2")
        dh1, dh1_b, dg = _rms_bwd(tp["h1"], w["g_mlp"][layer], dhn2, dres=dh, want_bf16=True)
        g["g_mlp"][layer] = dg[0]
        if layer % 2 == 0:
            g["w_out_attn"][j] = _mm(tp["mixed"], dh1_b, ta=True, name="d_w_out_attn")
            dcat = _mm(dh1_b, w["w_out_attn"][j], tb=True, name="d_cat")
            dhn, gl = _attn_layer_bwd(dcat, tp["hn"], tp["wl"], tp["sv"], rope)
            g["w_in_attn"][j] = _unpermute_in_attn(gl.pop("w_in"))
            for n, val in gl.items():
                g[n][j] = val
        else:
            g["w_out_conv"][j] = _mm(tp["mixed"], dh1_b, ta=True, name="d_w_out_conv")
            dyb = _mm(dh1_b, w["w_out_conv"][j], tb=True, name="d_yb")
            dz3, dcw = _conv_bwd(tp["sv"]["z3"], w["conv_w"][j], dyb)
            g["conv_w"][j] = dcw
            g["w_in_conv"][j] = _mm(tp["hn"], dz3, ta=True, name="d_w_in_conv")
            dhn = _mm(dz3, w["w_in_conv"][j], tb=True, name="d_hn_conv")
        dh, dh_b, dg = _rms_bwd(tp["h"], w["g_mix"][layer], dhn, dres=dh1, want_bf16=True)
        g["g_mix"][layer] = dg[0]
    grads = {n: jnp.stack(v) for n, v in g.items()}
    grads["meta_tokens"] = dh[PAD:BLOCK]
    return loss, dh[BLOCK:], grads


COMM_ROWS = 2048
ADAMW_BLOCK_BYTES = 1 << 20


def kernel(x, meta_tokens, g_mix, g_mlp, w_in_attn, g_cq, w_uq, g_ckv, w_ukv, g_q_mla, g_k_mla, g_q_fox, g_k_fox, b_forget, w_out_attn, w_in_conv, conv_w, w_out_conv, w_mlp_up, w_mlp_down, loss_target, m_meta_tokens, m_g_mix, m_g_mlp, m_w_in_attn, m_g_cq, m_w_uq, m_g_ckv, m_w_ukv, m_g_q_mla, m_g_k_mla, m_g_q_fox, m_g_k_fox, m_b_forget, m_w_out_attn, m_w_in_conv, m_conv_w, m_w_out_conv, m_w_mlp_up, m_w_mlp_down, v_meta_tokens, v_g_mix, v_g_mlp, v_w_in_attn, v_g_cq, v_w_uq, v_g_ckv, v_w_ukv, v_g_q_mla, v_g_k_mla, v_g_q_fox, v_g_k_fox, v_b_forget, v_w_out_attn, v_w_in_conv, v_conv_w, v_w_out_conv, v_w_mlp_up, v_w_mlp_down):
    args = dict(locals())
    local = {n: args[n] for n in WEIGHTS}
    mom = {n: args["m_" + n] for n in WEIGHTS}
    var = {n: args["v_" + n] for n in WEIGHTS}
    axis = dict(SHARDED)
    packed = [n for n, _ in SHARDED if n != "w_in_attn"]
    big = [n for n in packed if n not in F32_GATHERED]
    rows_277 = lambda a: a.reshape(-1, a.shape[-1])

    gathered_big, gathered_in, gathered_f32 = _all_gather([
        _pack([local[n] for n in big], 16, BF16), rows_277(local["w_in_attn"]).astype(BF16),
        _pack([local[n] for n in F32_GATHERED], 8, F32)])
    full = {n: local[n] for n in REPLICATED}
    for n, blocks in zip(big, _unpack(gathered_big, [local[n].shape for n in big], (N_DEV,))):
        full[n] = _from_shards(blocks, axis[n])
    for n, blocks in zip(F32_GATHERED, _unpack(gathered_f32, [local[n].shape for n in F32_GATHERED], (N_DEV,))):
        full[n] = _from_shards(blocks, axis[n])
    w_in = _from_shards(gathered_in.reshape((N_DEV,) + local["w_in_attn"].shape), axis["w_in_attn"])
    full["w_in_attn"] = jnp.stack([_permute_in_attn(w_in[j]) for j in range(DEPTH // 2)])

    loss_part, dx, grads = _local_step(x[0], loss_target[0], full)
    loss = lax.psum(loss_part[0, 0], ("x", "y", "c"))

    sent = _pack_rows([_to_shards(grads[n], axis[n]) for n in packed], COMM_ROWS, BF16)
    sent_in = _to_shards(grads["w_in_attn"], axis["w_in_attn"]).astype(BF16)
    sent_in = sent_in.reshape(N_DEV, -1, sent_in.shape[-1])
    got, got_in = _all_to_all([sent, sent_in])
    g_packed = _sum_blocks(got, COMM_ROWS)
    g_in = _sum_blocks(got_in, 256)
    rep, = _all_gather([_pack([grads[n] for n in REPLICATED], 8, F32)])
    g_rep = _sum_blocks(rep, rep.shape[1])
    g_local = dict(zip(packed, _unpack(g_packed, [local[n].shape for n in packed])))
    g_local.update(zip(REPLICATED, _unpack(g_rep, [local[n].shape for n in REPLICATED])))
    g_local["w_in_attn"] = g_in.reshape(local["w_in_attn"].shape)

    def flat(src, names, rows_multiple):
        return _pack([src[n] for n in names], rows_multiple, F32)

    upd = {}
    rows = g_rep.shape[0]
    outs = _adamw(flat(local, REPLICATED, rows), g_rep, flat(mom, REPLICATED, rows), flat(var, REPLICATED, rows), rows)
    for kind, buf in zip(("delta", "m", "v"), outs):
        upd.update({(kind, n): a for n, a in zip(REPLICATED, _unpack(buf, [local[n].shape for n in REPLICATED]))})
    for n, _ in SHARDED:
        as_rows = lambda a: a.reshape(-1, a.shape[-1])
        n_rows, n_cols = as_rows(local[n]).shape
        tiles = [r for r in (1024, 512, 256, 128, 64, 32, 16, 8) if r * n_cols * 4 <= ADAMW_BLOCK_BYTES]
        outs = _adamw(as_rows(local[n]), as_rows(g_local[n]), as_rows(mom[n]), as_rows(var[n]), _pick(n_rows, tiles))
        for kind, buf in zip(("delta", "m", "v"), outs):
            upd[(kind, n)] = buf.reshape(local[n].shape)

    return (loss, dx[None], *[g_local[n] for n in WEIGHTS], *[upd[("delta", n)] for n in WEIGHTS],
            *[upd[("m", n)] for n in WEIGHTS], *[upd[("v", n)] for n in WEIGHTS])
```

```python
import functools
import math

import jax
import jax.numpy as jnp
import numpy as np
from jax import lax
from jax.experimental import pallas as pl
from jax.experimental.pallas import tpu as pltpu

F32 = jnp.float32
BF16 = jnp.bfloat16

N_DEV = 8
D_MODEL = 1024
DEPTH = 4
N_META = 16
BLOCK = 128
PAD = BLOCK - N_META
HEADS = 8
MLA_NOPE = 64
MLA_ROPE = 32
MLA_QK = MLA_NOPE + MLA_ROPE
MLA_V = 64
Q_LORA = 384
KV_LORA = 256
ROPE_BASE = 10000.0
FOX_DIM = 64
D_FF = 4 * D_MODEL
EPS = 1e-6
NEG = -1e30
LANES = 128
ATTN_IN_PAD = 2304
C_CQ, C_CKV, C_FQ, C_FK, C_FV, C_TAIL = 0, 384, 640, 1152, 1664, 2176
TAIL_F = MLA_ROPE

ADAM_LR = 0.001
ADAM_B1 = 0.9
ADAM_B2 = 0.999
ADAM_EPS = 1e-08
ADAM_WD = 0.01
ADAM_STEP = 10

VMEM_LIMIT_BYTES = 48 * 1024 * 1024
MESH = pl.DeviceIdType.MESH

SHARDED = (
    ("meta_tokens", 1), ("w_in_attn", 2), ("w_uq", 2), ("w_ukv", 2), ("w_out_attn", 1),
    ("w_in_conv", 2), ("conv_w", 2), ("w_out_conv", 1), ("w_mlp_up", 2), ("w_mlp_down", 1))
F32_GATHERED = ("meta_tokens", "conv_w")
REPLICATED = ("g_mix", "g_mlp", "g_cq", "g_ckv", "g_q_mla", "g_k_mla", "g_q_fox", "g_k_fox", "b_forget")
WEIGHTS = ("meta_tokens", "g_mix", "g_mlp", "w_in_attn", "g_cq", "w_uq", "g_ckv", "w_ukv", "g_q_mla",
           "g_k_mla", "g_q_fox", "g_k_fox", "b_forget", "w_out_attn", "w_in_conv", "conv_w",
           "w_out_conv", "w_mlp_up", "w_mlp_down")


def _params(*sem):
    return pltpu.CompilerParams(dimension_semantics=sem, vmem_limit_bytes=VMEM_LIMIT_BYTES)


def _row_tile(t):
    return 640 if (t % 640 == 0 and t > 640) else 128


def _pack(parts, rows_multiple, dtype):
    flat = jnp.concatenate([p.reshape(-1).astype(dtype) for p in parts])
    n = flat.shape[0]
    rows = -(-n // LANES)
    rows = -(-rows // rows_multiple) * rows_multiple
    return jnp.pad(flat, (0, rows * LANES - n)).reshape(rows, LANES)


def _pack_rows(parts, rows_multiple, dtype):
    flat = jnp.concatenate([p.reshape(N_DEV, -1).astype(dtype) for p in parts], axis=1)
    n = flat.shape[1]
    rows = -(-n // LANES)
    rows = -(-rows // rows_multiple) * rows_multiple
    return jnp.pad(flat, ((0, 0), (0, rows * LANES - n))).reshape(N_DEV, rows, LANES)


def _unpack(buf, shapes, lead=()):
    flat = buf.reshape(lead + (-1,))
    out, off = [], 0
    for s in shapes:
        n = math.prod(s)
        out.append(flat[..., off:off + n].reshape(lead + tuple(s)))
        off += n
    return out


def _to_shards(full, axis):
    s = full.shape
    return jnp.moveaxis(full.reshape(s[:axis] + (N_DEV, s[axis] // N_DEV) + s[axis + 1:]), axis, 0)


def _from_shards(g8, axis):
    m = jnp.moveaxis(g8, 0, axis)
    s = m.shape
    return m.reshape(s[:axis] + (s[axis] * s[axis + 1],) + s[axis + 2:])


def _comm_call(body, name, xs, out_shapes):
    n = len(xs)
    hbm = pl.BlockSpec(memory_space=pltpu.HBM)
    return pl.pallas_call(
        body, name=name, out_shape=out_shapes, in_specs=[hbm] * n, out_specs=[hbm] * n,
        scratch_shapes=[pltpu.SemaphoreType.DMA((n, 7)), pltpu.SemaphoreType.DMA((n, 7)),
                        pltpu.SemaphoreType.DMA((n,))],
    )(*xs)


def _gather_ops(x_refs, out_refs, send_sems, recv_sems, local_sems):
    n = len(x_refs)
    x_, y_, c = lax.axis_index("x"), lax.axis_index("y"), lax.axis_index("c")
    me, sibling = (x_, y_, c), (x_, y_, 1 - c)
    chips = [(1 - x_, y_), (x_, 1 - y_), (1 - x_, 1 - y_)]

    def rows(a, px, py, pc):
        return out_refs[a].at[4 * px + 2 * py + pc]

    def copy(a, k, block, to, src=None):
        return pltpu.make_async_remote_copy(
            src_ref=rows(a, *block) if src is None else src, dst_ref=rows(a, *block),
            send_sem=send_sems.at[a, k], recv_sem=recv_sems.at[a, k], device_id=to, device_id_type=MESH)

    def mine():
        return [pltpu.make_async_copy(x_refs[a], rows(a, *me), local_sems.at[a]) for a in range(n)]

    def first():
        cps = []
        for a in range(n):
            cps.append(copy(a, 0, me, sibling, src=x_refs[a]))
            cps += [copy(a, 1 + j, me, (*chip, c), src=x_refs[a]) for j, chip in enumerate(chips)]
        return cps

    def start():
        for cp in mine() + first():
            cp.start()

    def finish():
        passed = []
        for j, chip in enumerate(chips):
            for a in range(n):
                copy(a, 1 + j, (*chip, c), me).wait_recv()
                passed.append(copy(a, 4 + j, (*chip, c), sibling))
                passed[-1].start()
        for a in range(n):
            copy(a, 0, sibling, me).wait_recv()
            for j, chip in enumerate(chips):
                copy(a, 4 + j, (*chip, 1 - c), me).wait_recv()
        for cp in first() + passed:
            cp.wait_send()
        for cp in mine():
            cp.wait()

    return start, finish


def _exchange_ops(x_refs, out_refs, send_sems, recv_sems, local_sems):
    n = len(x_refs)
    x_, y_, c = lax.axis_index("x"), lax.axis_index("y"), lax.axis_index("c")
    me = 4 * x_ + 2 * y_ + c

    def peer(k):
        px = 1 - x_ if k & 4 else x_
        py = 1 - y_ if k & 2 else y_
        pc = 1 - c if k & 1 else c
        return px, py, pc

    def copy(a, k):
        px, py, pc = peer(k)
        return pltpu.make_async_remote_copy(
            src_ref=x_refs[a].at[4 * px + 2 * py + pc], dst_ref=out_refs[a].at[me],
            send_sem=send_sems.at[a, k - 1], recv_sem=recv_sems.at[a, k - 1], device_id=(px, py, pc),
            device_id_type=MESH)

    def arrival(a, k):
        px, py, pc = peer(k)
        slot = 4 * px + 2 * py + pc
        return pltpu.make_async_remote_copy(
            src_ref=x_refs[a].at[slot], dst_ref=out_refs[a].at[slot],
            send_sem=send_sems.at[a, k - 1], recv_sem=recv_sems.at[a, k - 1], device_id=(px, py, pc),
            device_id_type=MESH)

    def mine():
        return [pltpu.make_async_copy(x_refs[a].at[me], out_refs[a].at[me], local_sems.at[a]) for a in range(n)]

    def sends():
        return [copy(a, k) for k in range(1, N_DEV) for a in range(n)]

    def start():
        for cp in mine() + sends():
            cp.start()

    def finish():
        for k in range(1, N_DEV):
            for a in range(n):
                arrival(a, k).wait_recv()
        for cp in sends():
            cp.wait_send()
        for cp in mine():
            cp.wait()

    return start, finish


def _comm_parts(xs, gather):
    n = len(xs)
    hbm = pl.BlockSpec(memory_space=pltpu.HBM)
    shapes = [jax.ShapeDtypeStruct(((N_DEV,) + x.shape) if gather else x.shape, x.dtype) for x in xs]
    sems = [pltpu.SemaphoreType.DMA((n, 7)), pltpu.SemaphoreType.DMA((n, 7)), pltpu.SemaphoreType.DMA((n,))]
    return [hbm] * n, [hbm] * n, shapes, sems


def _all_gather(xs):
    n = len(xs)

    def body(*refs):
        start, finish = _gather_ops(refs[:n], refs[n:2 * n], *refs[2 * n:])
        start()
        finish()

    return _comm_call(body, "all_gather", xs, [jax.ShapeDtypeStruct((N_DEV,) + x.shape, x.dtype) for x in xs])


def _all_to_all(xs):
    n = len(xs)

    def body(*refs):
        start, finish = _exchange_ops(refs[:n], refs[n:2 * n], *refs[2 * n:])
        start()
        finish()

    return _comm_call(body, "all_to_all", xs, [jax.ShapeDtypeStruct(x.shape, x.dtype) for x in xs])


def _sum_blocks(x, rows_tile):
    _, r, c_ = x.shape

    def body(x_ref, o_ref):
        acc = x_ref[0].astype(F32)
        for d in range(1, N_DEV):
            acc = acc + x_ref[d].astype(F32)
        o_ref[...] = acc

    return pl.pallas_call(
        body, name="sum_blocks", grid=(r // rows_tile,),
        in_specs=[pl.BlockSpec((N_DEV, rows_tile, c_), lambda i: (0, i, 0))],
        out_specs=pl.BlockSpec((rows_tile, c_), lambda i: (i, 0)),
        out_shape=jax.ShapeDtypeStruct((r, c_), F32),
        compiler_params=_params("parallel"),
    )(x)


def _adamw(w, g, m, v, rows_tile):
    r, c_ = w.shape
    c1 = 1.0 - ADAM_B1 ** ADAM_STEP
    c2 = 1.0 - ADAM_B2 ** ADAM_STEP

    def body(w_ref, g_ref, m_ref, v_ref, d_ref, mo_ref, vo_ref):
        g_ = g_ref[...]
        m_ = ADAM_B1 * m_ref[...] + (1.0 - ADAM_B1) * g_
        v_ = ADAM_B2 * v_ref[...] + (1.0 - ADAM_B2) * (g_ * g_)
        m_hat = m_ / c1
        v_hat = v_ / c2
        d_ref[...] = -ADAM_LR * (m_hat / (jnp.sqrt(v_hat) + ADAM_EPS) + ADAM_WD * w_ref[...])
        mo_ref[...] = m_
        vo_ref[...] = v_

    spec = pl.BlockSpec((rows_tile, c_), lambda i: (i, 0))
    shape = jax.ShapeDtypeStruct((r, c_), F32)
    return pl.pallas_call(
        body, name="adamw", grid=(r // rows_tile,), in_specs=[spec] * 4, out_specs=[spec] * 3,
        out_shape=[shape] * 3, compiler_params=_params("parallel"),
    )(w, g, m, v)


def _pick(n, prefs):
    for p in prefs:
        if n % p == 0:
            return p
    return n


def _mat_spec(arr, tr, tc, r_of, c_of):
    if arr.ndim == 2:
        return pl.BlockSpec((tr, tc), lambda i, j, k: (r_of(i, j, k), c_of(i, j, k)))
    per = arr.shape[2] // tc
    return pl.BlockSpec((None, tr, tc), lambda i, j, k: (c_of(i, j, k) // per, r_of(i, j, k), c_of(i, j, k) % per))


def _mm(a, b, *, ta=False, tb=False, out_dtype=F32, out_seg=None, res=None, epi=None, aux=None,
        tm=None, tn=None, tk=None, name="mm"):
    def dims(x):
        return (x.shape[0], x.shape[1]) if x.ndim == 2 else (x.shape[1], x.shape[0] * x.shape[2])
    ar, ac = dims(a)
    br, bc = dims(b)
    m, k = (ac, ar) if ta else (ar, ac)
    n, kb = (br, bc) if tb else (bc, br)
    assert k == kb, (a.shape, b.shape, ta, tb)
    tm = tm or _pick(m, (1024, 512, 384, 256, 128) if ta else (640, 512, 384, 256, 128))
    tn = tn or _pick(n, (1024, 768, 512, 384, 256, 128))
    tk = tk or _pick(k, ((1664,) if ta else ()) + (1024, 768, 640, 512, 384, 256, 128))
    if out_seg:
        assert (n // out_seg) % tn == 0
    for x, t in ((a, tm if ta else tk), (b, tk if tb else tn)):
        if x.ndim == 3:
            assert x.shape[2] % t == 0
    nk = k // tk
    gi, gj, gk = (lambda j, i, kk: i), (lambda j, i, kk: j), (lambda j, i, kk: kk)
    a_spec = _mat_spec(a, tk, tm, gk, gi) if ta else _mat_spec(a, tm, tk, gi, gk)
    b_spec = _mat_spec(b, tn, tk, gj, gk) if tb else _mat_spec(b, tk, tn, gk, gj)
    out_like = jnp.zeros((out_seg, 0, n // out_seg)) if out_seg else jnp.zeros((0, n))
    o_spec = _mat_spec(out_like, tm, tn, gi, gj)
    o_shape = (out_seg, m, n // out_seg) if out_seg else (m, n)
    dn = (((0 if ta else 1,), (1 if tb else 0,)), ((), ()))
    extra = [x for x in (res, aux) if x is not None]
    assert not (res is not None and aux is not None)
    n_out = 2 if epi == "relu2" else 1

    def body(*refs):
        a_ref, b_ref = refs[0], refs[1]
        x_ref = refs[2] if extra else None
        outs = refs[2 + len(extra):2 + len(extra) + n_out]
        acc_ref = refs[-1] if nk > 1 else None
        part = lax.dot_general(a_ref[...], b_ref[...], dn, preferred_element_type=F32)

        def finish(acc):
            if epi == "relu2":
                outs[0][...] = acc
                r = jnp.maximum(acc, 0.0)
                outs[1][...] = (r * r).astype(BF16)
            elif epi == "relu2_bwd":
                outs[0][...] = (acc * (2.0 * jnp.maximum(x_ref[...], 0.0))).astype(out_dtype)
            elif res is not None:
                outs[0][...] = (acc + x_ref[...]).astype(out_dtype)
            else:
                outs[0][...] = acc.astype(out_dtype)

        if nk == 1:
            finish(part)
        else:
            kk = pl.program_id(2)

            @pl.when(kk == 0)
            def _():
                acc_ref[...] = part

            @pl.when(kk > 0)
            def _():
                acc_ref[...] += part

            @pl.when(kk == nk - 1)
            def _():
                finish(acc_ref[...])

    if epi == "relu2":
        out_shape = [jax.ShapeDtypeStruct(o_shape, F32), jax.ShapeDtypeStruct(o_shape, BF16)]
        out_specs = [o_spec, o_spec]
    else:
        out_shape = jax.ShapeDtypeStruct(o_shape, out_dtype)
        out_specs = o_spec
    x_specs = [pl.BlockSpec((tm, tn), lambda j, i, kk: (i, j))] * len(extra)
    res_ = pl.pallas_call(
        body, name=name, grid=(n // tn, m // tm, nk),
        in_specs=[a_spec, b_spec] + x_specs, out_specs=out_specs, out_shape=out_shape,
        scratch_shapes=[pltpu.VMEM((tm, tn), F32)] if nk > 1 else [],
        compiler_params=_params("parallel", "parallel", "arbitrary"),
    )(a, b, *extra)
    return res_


def _rms_fwd(x, g):
    t, d = x.shape
    tm = _row_tile(t)

    def body(x_ref, g_ref, o_ref):
        x_ = x_ref[...]
        rstd = lax.rsqrt(jnp.mean(x_ * x_, axis=-1, keepdims=True) + EPS)
        o_ref[...] = (x_ * rstd * g_ref[...]).astype(BF16)

    return pl.pallas_call(
        body, name="rms_fwd", grid=(t // tm,),
        in_specs=[pl.BlockSpec((tm, d), lambda i: (i, 0)), pl.BlockSpec((1, d), lambda i: (0, 0))],
        out_specs=pl.BlockSpec((tm, d), lambda i: (i, 0)),
        out_shape=jax.ShapeDtypeStruct((t, d), BF16), compiler_params=_params("parallel"),
    )(x, g.reshape(1, d))


def _rms_bwd(x, g, dy, dres=None, want_bf16=False):
    t, d = x.shape
    tm = _row_tile(t)
    has_res = dres is not None

    def body(*refs):
        x_ref, g_ref, dy_ref = refs[:3]
        r_ref = refs[3] if has_res else None
        outs = refs[3 + has_res:]
        x_ = x_ref[...]
        rstd = lax.rsqrt(jnp.mean(x_ * x_, axis=-1, keepdims=True) + EPS)
        xh = x_ * rstd
        dy_ = dy_ref[...]
        dxh = dy_ * g_ref[...]
        dx = rstd * (dxh - xh * jnp.mean(dxh * xh, axis=-1, keepdims=True))
        if has_res:
            dx = dx + r_ref[...]
        outs[0][...] = dx
        if want_bf16:
            outs[1][...] = dx.astype(BF16)
        dg_ref = outs[-1]

        @pl.when(pl.program_id(0) == 0)
        def _():
            dg_ref[...] = jnp.zeros_like(dg_ref)

        dg_ref[...] += jnp.sum(dy_ * xh, axis=0, keepdims=True)

    row = pl.BlockSpec((tm, d), lambda i: (i, 0))
    vec = pl.BlockSpec((1, d), lambda i: (0, 0))
    out_shape = [jax.ShapeDtypeStruct((t, d), F32)] + ([jax.ShapeDtypeStruct((t, d), BF16)] if want_bf16 else []) \
        + [jax.ShapeDtypeStruct((1, d), F32)]
    out_specs = [row] + ([row] if want_bf16 else []) + [vec]
    return pl.pallas_call(
        body, name="rms_bwd", grid=(t // tm,),
        in_specs=[row, vec, row] + ([row] if has_res else []), out_specs=out_specs, out_shape=out_shape,
        compiler_params=_params("arbitrary"),
    )(x, g.reshape(1, d), dy, *([dres] if has_res else []))


def _swap_rope_halves(y):
    lane = lax.broadcasted_iota(jnp.int32, y.shape, 1)
    half = MLA_ROPE // 2
    swapped = jnp.where(lane < MLA_NOPE + half, pltpu.roll(y, LANES - half, axis=1), pltpu.roll(y, half, axis=1))
    return jnp.where((lane >= MLA_NOPE) & (lane < MLA_QK), swapped, 0.0)


def _head_norm_fwd(x, g, n_valid, scale, rope=None):
    h, t, w = x.shape
    tm = _row_tile(t)

    def body(*refs):
        x_ref, g_ref = refs[:2]
        o_ref = refs[-1]
        x_ = x_ref[...].reshape(h * tm, w)
        rstd = lax.rsqrt(jnp.sum(x_ * x_, axis=-1, keepdims=True) * (1.0 / n_valid) + EPS)
        y = x_ * rstd * (g_ref[...] * scale)
        if rope is not None:
            y3, s3 = y.reshape(h, tm, w), _swap_rope_halves(y).reshape(h, tm, w)
            y = (y3 * refs[2][...][None] + s3 * refs[3][...][None]).reshape(h * tm, w)
        o_ref[...] = y.reshape(h, tm, w).astype(BF16)

    blk = pl.BlockSpec((h, tm, w), lambda i: (0, i, 0))
    tab = pl.BlockSpec((tm, w), lambda i: (i, 0))
    return pl.pallas_call(
        body, name="head_norm_fwd", grid=(t // tm,),
        in_specs=[blk, pl.BlockSpec((1, w), lambda i: (0, 0))] + ([tab, tab] if rope is not None else []),
        out_specs=blk, out_shape=jax.ShapeDtypeStruct((h, t, w), BF16),
        compiler_params=_params("parallel"),
    )(x, g, *(rope if rope is not None else ()))


def _head_norm_bwd(x, g, dout, n_valid, scale, rope=None):
    h, t, w = x.shape
    tm = 320 if t % 320 == 0 else BLOCK

    def body(*refs):
        x_ref, g_ref, do_ref = refs[:3]
        dx_ref, dsum_ref, dg_ref = refs[-3:]
        dy = do_ref[...]
        if rope is not None:
            c_, s_ = refs[3][...], refs[4][...]
            ds = (dy * s_[None]).reshape(h * tm, w)
            dy = dy * c_[None] + _swap_rope_halves(ds).reshape(h, tm, w)
        x_ = x_ref[...]
        rstd = lax.rsqrt(jnp.sum(x_ * x_, axis=-1, keepdims=True) * (1.0 / n_valid) + EPS)
        xh = x_ * rstd
        dxh = dy * (g_ref[...] * scale)[None]
        dx = rstd * (dxh - xh * (jnp.sum(dxh * xh, axis=-1, keepdims=True) * (1.0 / n_valid)))
        dx_ref[...] = dx
        dsum_ref[...] = jnp.sum(dx, axis=0)

        @pl.when(pl.program_id(0) == 0)
        def _():
            dg_ref[...] = jnp.zeros_like(dg_ref)

        dg_ref[...] += scale * jnp.sum(jnp.sum(dy * xh, axis=0), axis=0, keepdims=True)

    blk = pl.BlockSpec((h, tm, w), lambda i: (0, i, 0))
    tab = pl.BlockSpec((tm, w), lambda i: (i, 0))
    vec = pl.BlockSpec((1, w), lambda i: (0, 0))
    return pl.pallas_call(
        body, name="head_norm_bwd", grid=(t // tm,),
        in_specs=[blk, vec, blk] + ([tab, tab] if rope is not None else []),
        out_specs=[blk, tab, vec],
        out_shape=[jax.ShapeDtypeStruct((h, t, w), F32), jax.ShapeDtypeStruct((t, w), F32),
                   jax.ShapeDtypeStruct((1, w), F32)],
        compiler_params=_params("arbitrary"),
    )(x, g, dout, *(rope if rope is not None else ()))


def _tri(n, upper):
    r = lax.broadcasted_iota(jnp.int32, (n, n), 0)
    c = lax.broadcasted_iota(jnp.int32, (n, n), 1)
    return ((r <= c) if upper else (r >= c)).astype(F32)


def _gate_mask(shape, row0):
    lane = lax.broadcasted_iota(jnp.int32, shape, 1)
    row = row0 + lax.broadcasted_iota(jnp.int32, shape, 0)
    return (lane >= TAIL_F) & (lane < TAIL_F + HEADS) & (row >= PAD)


def _gate_fwd(z, bias):
    t = z.shape[0]
    tm = BLOCK
    tail = C_TAIL // LANES

    def body(z_ref, b_ref, o_ref, carry):
        i = pl.program_id(0)

        @pl.when(i == 0)
        def _():
            carry[...] = jnp.zeros_like(carry)

        x_ = z_ref[...] + b_ref[...]
        logf = jnp.minimum(x_, 0.0) - jnp.log1p(jnp.exp(-jnp.abs(x_)))
        logf = jnp.where(_gate_mask(logf.shape, i * tm), logf, 0.0)
        cum = jnp.dot(_tri(tm, False), logf, preferred_element_type=F32, precision=lax.Precision.HIGHEST) + carry[...]
        o_ref[...] = cum
        carry[...] = cum[tm - 1:tm, :]

    return pl.pallas_call(
        body, name="gate_fwd", grid=(t // tm,),
        in_specs=[pl.BlockSpec((tm, LANES), lambda i: (i, tail)), pl.BlockSpec((1, LANES), lambda i: (0, 0))],
        out_specs=pl.BlockSpec((tm, LANES), lambda i: (i, 0)),
        out_shape=jax.ShapeDtypeStruct((t, LANES), F32),
        scratch_shapes=[pltpu.VMEM((1, LANES), F32)], compiler_params=_params("arbitrary"),
    )(z, bias)


def _gate_bwd(z, bias, dcum):
    t = z.shape[0]
    tm = BLOCK
    nb = t // tm
    tail = C_TAIL // LANES

    def body(z_ref, b_ref, d_ref, o_ref, db_ref, carry):
        i = pl.program_id(0)

        @pl.when(i == 0)
        def _():
            carry[...] = jnp.zeros_like(carry)
            db_ref[...] = jnp.zeros_like(db_ref)

        rc = jnp.dot(_tri(tm, True), d_ref[...], preferred_element_type=F32, precision=lax.Precision.HIGHEST) + carry[...]
        carry[...] = rc[0:1, :]
        x_ = z_ref[...] + b_ref[...]
        sig_neg = 1.0 / (1.0 + jnp.exp(x_))
        dl = jnp.where(_gate_mask(rc.shape, (nb - 1 - i) * tm), rc * sig_neg, 0.0)
        o_ref[...] = dl
        db_ref[...] += jnp.sum(dl, axis=0, keepdims=True)

    return pl.pallas_call(
        body, name="gate_bwd", grid=(nb,),
        in_specs=[pl.BlockSpec((tm, LANES), lambda i: (nb - 1 - i, tail)), pl.BlockSpec((1, LANES), lambda i: (0, 0)),
                  pl.BlockSpec((tm, LANES), lambda i: (nb - 1 - i, 0))],
        out_specs=[pl.BlockSpec((tm, LANES), lambda i: (nb - 1 - i, 0)), pl.BlockSpec((1, LANES), lambda i: (0, 0))],
        out_shape=[jax.ShapeDtypeStruct((t, LANES), F32), jax.ShapeDtypeStruct((1, LANES), F32)],
        scratch_shapes=[pltpu.VMEM((1, LANES), F32)], compiler_params=_params("arbitrary"),
    )(z, bias, dcum)


def _pairs(nb, by_query):
    if by_query:
        pr = [(i, j) for i in range(nb) for j in range(i + 1)]
    else:
        pr = [(i, j) for j in range(nb) for i in range(j, nb)]
    return (jnp.asarray(np.array([p[0] for p in pr], np.int32)),
            jnp.asarray(np.array([p[1] for p in pr], np.int32)))


HEADS_PER_STEP = 8


def _mask_scores(s, i, j, tile):
    qp = i * tile + lax.broadcasted_iota(jnp.int32, s.shape, 0)
    kp = j * tile + lax.broadcasted_iota(jnp.int32, s.shape, 1)
    return jnp.where((kp <= qp) & (kp >= PAD), s, NEG)


def _pipelined(n, front, back):
    nxt = front(0)
    for h in range(n):
        cur = nxt
        if h + 1 < n:
            nxt = front(h + 1)
        back(h, cur)


def _nt_dot(a, b):
    return lax.dot_general(a, b, (((1,), (1,)), ((), ())), preferred_element_type=F32)


def _tn_dot(a, b):
    return lax.dot_general(a, b, (((0,), (0,)), ((), ())), preferred_element_type=F32)


def _attn_specs(hb, tile, dk, dv):
    q_of = lambda h, p, it, jt: (h, it[p], 0)
    k_of = lambda h, p, it, jt: (h, jt[p], 0)
    return dict(
        q=pl.BlockSpec((hb, tile, dk), q_of), k=pl.BlockSpec((hb, tile, dk), k_of),
        v=pl.BlockSpec((hb, tile, dv), k_of), ov=pl.BlockSpec((hb, tile, dv), q_of),
        col=pl.BlockSpec((hb, tile, 1), q_of), colk=pl.BlockSpec((hb, tile, 1), k_of),
        fr=pl.BlockSpec((hb, 1, tile), lambda h, p, it, jt: (h, 0, jt[p])),
        rowq=pl.BlockSpec((hb, 1, tile), lambda h, p, it, jt: (h, 0, it[p])))


def _attn_fwd(q, k, v, key_bias=None, gather=()):
    h, t, dk = q.shape
    dv = v.shape[-1]
    tile = _row_tile(t)
    nb = t // tile
    hb = HEADS_PER_STEP
    biased = key_bias is not None
    it, jt = _pairs(nb, True)
    sp = _attn_specs(hb, tile, dk, dv)

    n_in = 4 if biased else 3
    n_g = len(gather)
    g_in, g_out, g_shapes, g_sems = _comm_parts(gather, True)
    n_steps = int(it.shape[0])

    def body(it_ref, jt_ref, *refs):
        q_ref, k_ref, v_ref = refs[:3]
        b_ref = refs[3] if biased else None
        o_ref, lse_ref = refs[n_in + 2 * n_g:n_in + 2 * n_g + 2]
        m_sc, l_sc, acc_sc = refs[n_in + 2 * n_g + 2:n_in + 2 * n_g + 5]
        p = pl.program_id(1)
        i, j = it_ref[p], jt_ref[p]
        if n_g:
            g_start, g_finish = _gather_ops(refs[n_in:n_in + n_g], refs[n_in + n_g:n_in + 2 * n_g],
                                            *refs[n_in + 2 * n_g + 5:])
            first = (pl.program_id(0) == 0) & (p == 0)
            last = (pl.program_id(0) == h // hb - 1) & (p == n_steps - 1)
            pl.when(first)(g_start)

        @pl.when(j == 0)
        def _():
            m_sc[...] = jnp.full_like(m_sc, NEG)
            l_sc[...] = jnp.zeros_like(l_sc)
            acc_sc[...] = jnp.zeros_like(acc_sc)

        def step(masked):
            def front(n):
                return _nt_dot(q_ref[n], k_ref[n])

            def back(n, s):
                if biased:
                    s = s + b_ref[n]
                if masked:
                    s = _mask_scores(s, i, j, tile)
                m_prev = m_sc[n]
                m_new = jnp.maximum(m_prev, jnp.max(s, axis=-1, keepdims=True))
                alpha = jnp.exp(m_prev - m_new)
                e = jnp.exp(s - m_new)
                l_sc[n] = alpha * l_sc[n] + jnp.sum(e, axis=-1, keepdims=True)
                acc_sc[n] = alpha * acc_sc[n] + jnp.dot(e.astype(BF16), v_ref[n], preferred_element_type=F32)
                m_sc[n] = m_new

            _pipelined(hb, front, back)

        edge = (j == i) | (j == 0)
        pl.when(edge)(lambda: step(True))
        pl.when(jnp.logical_not(edge))(lambda: step(False))

        @pl.when(j == i)
        def _():
            row = i * tile + lax.broadcasted_iota(jnp.int32, (hb, tile, 1), 1)
            o_ref[...] = jnp.where(row >= PAD, acc_sc[...] / l_sc[...], 0.0)
            lse_ref[...] = m_sc[...] + jnp.log(l_sc[...])

        if n_g:
            pl.when(last)(g_finish)

    grid_spec = pltpu.PrefetchScalarGridSpec(
        num_scalar_prefetch=2, grid=(h // hb, n_steps),
        in_specs=[sp["q"], sp["k"], sp["v"]] + ([sp["fr"]] if biased else []) + g_in,
        out_specs=g_out + [sp["ov"], sp["col"]],
        scratch_shapes=[pltpu.VMEM((hb, tile, 1), F32), pltpu.VMEM((hb, tile, 1), F32),
                        pltpu.VMEM((hb, tile, dv), F32)] + (g_sems if n_g else []))
    outs = pl.pallas_call(
        body, name="attn_fwd_gather" if n_g else "attn_fwd", grid_spec=grid_spec,
        out_shape=g_shapes + [jax.ShapeDtypeStruct((h, t, dv), F32), jax.ShapeDtypeStruct((h, t, 1), F32)],
        compiler_params=_params("arbitrary" if n_g else "parallel", "arbitrary"),
    )(it, jt, q, k, v, *((key_bias,) if biased else ()), *gather)
    return (outs[n_g], outs[n_g + 1], outs[:n_g]) if n_g else tuple(outs)


def _attn_delta(o, do):
    h, t, dv = o.shape
    tm = _row_tile(t)

    def body(o_ref, do_ref, d_ref):
        d_ref[...] = jnp.sum(o_ref[...] * do_ref[...], axis=-1, keepdims=True)

    blk = pl.BlockSpec((h, tm, dv), lambda i: (0, i, 0))
    return pl.pallas_call(
        body, name="attn_delta", grid=(t // tm,), in_specs=[blk, blk],
        out_specs=pl.BlockSpec((h, tm, 1), lambda i: (0, i, 0)),
        out_shape=jax.ShapeDtypeStruct((h, t, 1), F32), compiler_params=_params("parallel"),
    )(o, do)


BWD_HEADS_PER_STEP = (4, 2)


def _attn_bwd(q, k, v, do, lse_row, delta_row, key_bias=None, exchange=()):
    h, t, dk = q.shape
    dv = v.shape[-1]
    tile = _row_tile(t)
    nb = t // tile
    decay = key_bias is not None
    hb = BWD_HEADS_PER_STEP[int(decay)]
    it, jt = _pairs(nb, False)
    sp = _attn_specs(hb, tile, dk, dv)
    n_in = 7 if decay else 6
    n_out = 5 if decay else 3
    n_x = len(exchange)
    x_in, x_out, x_shapes, x_sems = _comm_parts(exchange, False)
    n_steps = int(it.shape[0])

    def body(it_ref, jt_ref, *refs):
        q_ref, k_ref, v_ref, do_ref, lse_ref, delta_ref = refs[:6]
        b_ref = refs[6] if decay else None
        outs = refs[n_in + n_x:]
        dq_ref, dk_ref, dv_ref = outs[:3]
        rs_ref, ks_ref = (outs[3], outs[4]) if decay else (None, None)
        scratch = outs[n_out + n_x:]
        dq_sc, dk_sc, dv_sc = scratch[:3]
        ks_sc = scratch[3] if decay else None
        p = pl.program_id(1)
        i, j = it_ref[p], jt_ref[p]
        if n_x:
            x_start, x_finish = _exchange_ops(refs[n_in:n_in + n_x], outs[n_out:n_out + n_x],
                                              *scratch[4 if decay else 3:])
            first = (pl.program_id(0) == 0) & (p == 0)
            last = (pl.program_id(0) == h // hb - 1) & (p == n_steps - 1)
            pl.when(first)(x_start)

        @pl.when(p == 0)
        def _():
            dq_sc[...] = jnp.zeros_like(dq_sc)
            if decay:
                rs_ref[...] = jnp.zeros_like(rs_ref)

        @pl.when(i == j)
        def _():
            dk_sc[...] = jnp.zeros_like(dk_sc)
            dv_sc[...] = jnp.zeros_like(dv_sc)
            if decay:
                ks_sc[...] = jnp.zeros_like(ks_sc)

        def step(masked):
            def front(n):
                return _nt_dot(k_ref[n], q_ref[n]), _nt_dot(v_ref[n], do_ref[n].astype(BF16))

            def back(n, s_dp):
                s, dp = s_dp
                if decay:
                    s = s + b_ref[n]
                if masked:
                    kp = j * tile + lax.broadcasted_iota(jnp.int32, s.shape, 0)
                    qp = i * tile + lax.broadcasted_iota(jnp.int32, s.shape, 1)
                    s = jnp.where((kp <= qp) & (kp >= PAD), s, NEG)
                pr = jnp.exp(s - lse_ref[n])
                ds = pr * (dp - delta_ref[n])
                ds_b = ds.astype(BF16)
                dv_sc[n] += jnp.dot(pr.astype(BF16), do_ref[n].astype(BF16), preferred_element_type=F32)
                dk_sc[n] += jnp.dot(ds_b, q_ref[n], preferred_element_type=F32)
                dq_sc[n, i] += _tn_dot(ds_b, k_ref[n])
                if decay:
                    rs_ref[n, i] += jnp.sum(ds, axis=0, keepdims=True)
                    ks_sc[n] += jnp.sum(ds, axis=-1, keepdims=True)

            _pipelined(hb, front, back)

        edge = (j == i) | (j == 0)
        pl.when(edge)(lambda: step(True))
        pl.when(jnp.logical_not(edge))(lambda: step(False))

        @pl.when(i == j)
        def _():
            dq_ref[...] = dq_sc[:, j]

        @pl.when(i == nb - 1)
        def _():
            dk_ref[...] = dk_sc[...]
            dv_ref[...] = dv_sc[...]
            if decay:
                ks_ref[...] = ks_sc[...]

        if n_x:
            pl.when(last)(x_finish)

    rows_out = pl.BlockSpec((hb, nb, 1, tile), lambda hh, p, it, jt: (hh, 0, 0, 0))
    grid_spec = pltpu.PrefetchScalarGridSpec(
        num_scalar_prefetch=2, grid=(h // hb, n_steps),
        in_specs=[sp["q"], sp["k"], sp["v"], sp["ov"], sp["rowq"], sp["rowq"]] + ([sp["colk"]] if decay else [])
        + x_in,
        out_specs=[sp["k"], sp["k"], sp["v"]] + ([rows_out, sp["colk"]] if decay else []) + x_out,
        scratch_shapes=[pltpu.VMEM((hb, nb, tile, dk), F32), pltpu.VMEM((hb, tile, dk), F32),
                        pltpu.VMEM((hb, tile, dv), F32)] + ([pltpu.VMEM((hb, tile, 1), F32)] if decay else [])
        + (x_sems if n_x else []))
    out_shape = [jax.ShapeDtypeStruct((h, t, dk), F32), jax.ShapeDtypeStruct((h, t, dk), F32),
                 jax.ShapeDtypeStruct((h, t, dv), F32)] \
        + ([jax.ShapeDtypeStruct((h, nb, 1, tile), F32), jax.ShapeDtypeStruct((h, t, 1), F32)] if decay else []) \
        + x_shapes
    outs = pl.pallas_call(
        body, name="attn_bwd_exchange" if n_x else "attn_bwd", grid_spec=grid_spec, out_shape=out_shape,
        compiler_params=_params("arbitrary" if n_x else "parallel", "arbitrary"),
    )(it, jt, q, k, v, do, lse_row, delta_row, *((key_bias,) if decay else ()), *exchange)
    return tuple(outs[:n_out]) + ((list(outs[n_out:]),) if n_x else ())


CONV_COLS = 512


def _shift_down(g, prev, n):
    out = pltpu.roll(g, n, axis=0)
    row = lax.broadcasted_iota(jnp.int32, g.shape, 0)
    for r in range(n):
        out = jnp.where(row == r, prev[8 - n + r:8 - n + r + 1, :], out)
    return out


def _shift_up(g, nxt, n):
    tm = g.shape[0]
    out = pltpu.roll(g, tm - n, axis=0)
    row = lax.broadcasted_iota(jnp.int32, g.shape, 0)
    for r in range(n):
        out = jnp.where(row == tm - n + r, nxt[r:r + 1, :], out)
    return out


def _conv_fwd(z3, w):
    _, t, d = z3.shape
    tm = _row_tile(t)
    tc = CONV_COLS

    def body(z_ref, zp_ref, w_ref, o_ref):
        i = pl.program_id(1)
        g = z_ref[1] * z_ref[2]
        gp = jnp.where(i > 0, zp_ref[1] * zp_ref[2], 0.0)
        w_ = w_ref[...]
        y = w_[2:3] * g + w_[1:2] * _shift_down(g, gp, 1) + w_[0:1] * _shift_down(g, gp, 2)
        o_ref[...] = (z_ref[0] * y).astype(BF16)

    return pl.pallas_call(
        body, name="conv_fwd", grid=(d // tc, t // tm),
        in_specs=[pl.BlockSpec((3, tm, tc), lambda j, i: (0, i, j)),
                  pl.BlockSpec((3, 8, tc), lambda j, i: (0, jnp.maximum(i * (tm // 8) - 1, 0), j)),
                  pl.BlockSpec((3, tc), lambda j, i: (0, j))],
        out_specs=pl.BlockSpec((tm, tc), lambda j, i: (i, j)),
        out_shape=jax.ShapeDtypeStruct((t, d), BF16), compiler_params=_params("parallel", "parallel"),
    )(z3, z3, w)


def _conv_bwd(z3, w, dyb):
    _, t, d = z3.shape
    tm = _row_tile(t)
    tc = CONV_COLS
    ni = t // tm

    def body(z_ref, zp_ref, zn_ref, d_ref, dn_ref, w_ref, dz_ref, dw_ref):
        i = pl.program_id(1)
        gb, gc, u = z_ref[0], z_ref[1], z_ref[2]
        g = gc * u
        gp = jnp.where(i > 0, zp_ref[1] * zp_ref[2], 0.0)
        w_ = w_ref[...]
        g1, g2 = _shift_down(g, gp, 1), _shift_down(g, gp, 2)
        y = w_[2:3] * g + w_[1:2] * g1 + w_[0:1] * g2
        dyb_ = d_ref[...]
        dy = dyb_ * gb
        dyn = jnp.where(i < ni - 1, dn_ref[...] * zn_ref[0], 0.0)
        dg = w_[2:3] * dy + w_[1:2] * _shift_up(dy, dyn, 1) + w_[0:1] * _shift_up(dy, dyn, 2)
        dz_ref[0] = (dyb_ * y).astype(BF16)
        dz_ref[1] = (dg * u).astype(BF16)
        dz_ref[2] = (dg * gc).astype(BF16)

        @pl.when(i == 0)
        def _():
            dw_ref[...] = jnp.zeros_like(dw_ref)

        dw_ref[...] += jnp.concatenate([jnp.sum(dy * g2, axis=0, keepdims=True),
                                        jnp.sum(dy * g1, axis=0, keepdims=True),
                                        jnp.sum(dy * g, axis=0, keepdims=True)], axis=0)

    cur = pl.BlockSpec((3, tm, tc), lambda j, i: (0, i, j))
    return pl.pallas_call(
        body, name="conv_bwd", grid=(d // tc, ni),
        in_specs=[cur,
                  pl.BlockSpec((3, 8, tc), lambda j, i: (0, jnp.maximum(i * (tm // 8) - 1, 0), j)),
                  pl.BlockSpec((3, 8, tc), lambda j, i: (0, jnp.minimum((i + 1) * (tm // 8), t // 8 - 1), j)),
                  pl.BlockSpec((tm, tc), lambda j, i: (i, j)),
                  pl.BlockSpec((8, tc), lambda j, i: (jnp.minimum((i + 1) * (tm // 8), t // 8 - 1), j)),
                  pl.BlockSpec((3, tc), lambda j, i: (0, j))],
        out_specs=[cur, pl.BlockSpec((3, tc), lambda j, i: (0, j))],
        out_shape=[jax.ShapeDtypeStruct((3, t, d), BF16), jax.ShapeDtypeStruct((3, d), F32)],
        compiler_params=_params("parallel", "arbitrary"),
    )(z3, z3, z3, dyb, dyb, w)


def _loss_head(h, target):
    t, d = h.shape
    tm = BLOCK

    def body(h_ref, t_ref, dh_ref, dhb_ref, loss_ref):
        i = pl.program_id(0)

        @pl.when(i == 0)
        def _():
            loss_ref[...] = jnp.zeros_like(loss_ref)

        err = jnp.where(i > 0, h_ref[...] - t_ref[...], 0.0)
        dh = err * (1.0 / d)
        dh_ref[...] = dh
        dhb_ref[...] = dh.astype(BF16)
        loss_ref[...] += 0.5 * jnp.sum(jnp.sum(err * err, axis=-1, keepdims=True) * (1.0 / d), axis=0, keepdims=True)

    row = pl.BlockSpec((tm, d), lambda i: (i, 0))
    return pl.pallas_call(
        body, name="loss_head", grid=(t // tm,),
        in_specs=[row, pl.BlockSpec((tm, d), lambda i: (jnp.maximum(i - 1, 0), 0))],
        out_specs=[row, row, pl.BlockSpec((1, 1), lambda i: (0, 0))],
        out_shape=[jax.ShapeDtypeStruct((t, d), F32), jax.ShapeDtypeStruct((t, d), BF16),
                   jax.ShapeDtypeStruct((1, 1), F32)],
        compiler_params=_params("arbitrary"),
    )(h, target)


def _heads_major(x, width):
    t = x.shape[0]
    return jnp.transpose(x.reshape(t, HEADS, width), (1, 0, 2))


def _heads_minor(x):
    h, t, w = x.shape
    return jnp.transpose(x, (1, 0, 2)).reshape(t, h * w)


def _rope_tables(t):
    pos = jnp.arange(t, dtype=F32) - PAD
    inv_freq = ROPE_BASE ** (-jnp.arange(0, MLA_ROPE, 2, dtype=F32) / MLA_ROPE)
    ang = pos[:, None] * inv_freq[None, :]
    cos, sin = jnp.cos(ang), jnp.sin(ang)
    one, zero = jnp.ones((t, MLA_NOPE), F32), jnp.zeros((t, MLA_NOPE), F32)
    tail = jnp.zeros((t, LANES - MLA_QK), F32)
    return (jnp.concatenate([one, cos, cos, tail], axis=1), jnp.concatenate([zero, -sin, sin, tail], axis=1))


def _pad_lanes(x, width=LANES):
    return jnp.pad(x, [(0, 0)] * (x.ndim - 1) + [(0, width - x.shape[-1])])


def _permute_in_attn(w):
    return jnp.concatenate([w[:, :640], w[:, 672:2208], w[:, 640:672], w[:, 2208:2216],
                            jnp.zeros((w.shape[0], ATTN_IN_PAD - 2216), w.dtype)], axis=1)


def _unpermute_in_attn(dw):
    return jnp.concatenate([dw[:, :640], dw[:, 2176:2208], dw[:, 640:2176], dw[:, 2208:2216]], axis=1)


def _attn_layer_fwd(hn, wl, rope, gather=()):
    t = hn.shape[0]
    z = _mm(hn, wl["w_in"], name="attn_in")
    cqn = _rms_fwd(z[:, C_CQ:C_CQ + Q_LORA], wl["g_cq"])
    ckvn = _rms_fwd(z[:, C_CKV:C_CKV + KV_LORA], wl["g_ckv"])
    qf = _mm(cqn, wl["w_uq"], name="mla_uq")
    kvf = _mm(ckvn, wl["w_ukv"], name="mla_ukv")
    kv3 = kvf.reshape(t, HEADS, MLA_NOPE + MLA_V)
    xq = _pad_lanes(_heads_major(qf, MLA_QK))
    k_pe = jnp.broadcast_to(z[None, :, C_TAIL:C_TAIL + MLA_ROPE], (HEADS, t, MLA_ROPE))
    xk = _pad_lanes(jnp.concatenate([jnp.transpose(kv3[:, :, :MLA_NOPE], (1, 0, 2)), k_pe], axis=-1))
    v_mla = jnp.transpose(kv3[:, :, MLA_NOPE:], (1, 0, 2)).astype(BF16)
    gq, gk = _pad_lanes(wl["g_q_mla"].reshape(1, -1)), _pad_lanes(wl["g_k_mla"].reshape(1, -1))
    q_mla = _head_norm_fwd(xq, gq, MLA_QK, MLA_QK ** -0.5, rope)
    k_mla = _head_norm_fwd(xk, gk, MLA_QK, 1.0, rope)
    o_mla, lse_mla, *gathered = _attn_fwd(q_mla, k_mla, v_mla, gather=gather)
    xfq = _heads_major(z[:, C_FQ:C_FQ + HEADS * FOX_DIM], FOX_DIM)
    xfk = _heads_major(z[:, C_FK:C_FK + HEADS * FOX_DIM], FOX_DIM)
    v_fox = _heads_major(z[:, C_FV:C_FV + HEADS * FOX_DIM], FOX_DIM).astype(BF16)
    bias = jnp.pad(wl["b_forget"].reshape(1, -1), ((0, 0), (TAIL_F, LANES - TAIL_F - HEADS)))
    cum = _gate_fwd(z, bias)
    neg_f = -jnp.transpose(cum[:, TAIL_F:TAIL_F + HEADS])
    q_fox = _head_norm_fwd(xfq, wl["g_q_fox"].reshape(1, -1), FOX_DIM, FOX_DIM ** -0.5)
    k_fox = _head_norm_fwd(xfk, wl["g_k_fox"].reshape(1, -1), FOX_DIM, 1.0)
    o_fox, lse_fox = _attn_fwd(q_fox, k_fox, v_fox, neg_f[:, None, :])
    cat = jnp.concatenate([_heads_minor(o_mla), _heads_minor(o_fox)], axis=1).astype(BF16)
    saved = dict(z=z, cqn=cqn, ckvn=ckvn, xq=xq, xk=xk, v_mla=v_mla, q_mla=q_mla, k_mla=k_mla, o_mla=o_mla,
                 lse_mla=lse_mla, xfq=xfq, xfk=xfk, v_fox=v_fox, q_fox=q_fox, k_fox=k_fox, bias=bias,
                 neg_f=neg_f, o_fox=o_fox, lse_fox=lse_fox, cat=cat, gq=gq, gk=gk)
    return cat, saved, (gathered[0] if gathered else ())


def _attn_layer_bwd(dcat, hn, wl, sv, rope, exchange=()):
    t = hn.shape[0]
    g = {}
    do_mla = _heads_major(dcat[:, :HEADS * MLA_V], MLA_V)
    do_fox = _heads_major(dcat[:, HEADS * MLA_V:], FOX_DIM)
    as_row = lambda c: jnp.transpose(c, (0, 2, 1))
    qkv = (sv["q_fox"], sv["k_fox"], sv["v_fox"])
    neg_f = sv["neg_f"]
    delta = _attn_delta(sv["o_fox"], do_fox)
    dq_fox, dk_fox, dv_fox, row_sums, key_sums, *exchanged = _attn_bwd(
        *qkv, do_fox, as_row(sv["lse_fox"]), as_row(delta), neg_f[:, :, None], exchange=exchange)
    dcum = jnp.pad(jnp.transpose(row_sums.reshape(HEADS, t) - key_sums[:, :, 0]),
                   ((0, 0), (TAIL_F, LANES - TAIL_F - HEADS)))
    dtail_f, dbias = _gate_bwd(sv["z"], sv["bias"], dcum)
    g["b_forget"] = dbias[0, TAIL_F:TAIL_F + HEADS]
    dxfq, _, dgq = _head_norm_bwd(sv["xfq"], wl["g_q_fox"].reshape(1, -1), dq_fox, FOX_DIM, FOX_DIM ** -0.5)
    dxfk, _, dgk = _head_norm_bwd(sv["xfk"], wl["g_k_fox"].reshape(1, -1), dk_fox, FOX_DIM, 1.0)
    g["g_q_fox"], g["g_k_fox"] = dgq[0], dgk[0]
    qkv = (sv["q_mla"], sv["k_mla"], sv["v_mla"])
    delta = _attn_delta(sv["o_mla"], do_mla)
    dq_mla, dk_mla, dv_mla = _attn_bwd(*qkv, do_mla, as_row(sv["lse_mla"]), as_row(delta))
    dxq, _, dgq = _head_norm_bwd(sv["xq"], sv["gq"], dq_mla, MLA_QK, MLA_QK ** -0.5, rope)
    dxk, dxk_sum, dgk = _head_norm_bwd(sv["xk"], sv["gk"], dk_mla, MLA_QK, 1.0, rope)
    g["g_q_mla"], g["g_k_mla"] = dgq[0, :MLA_QK], dgk[0, :MLA_QK]
    dqf = _heads_minor(dxq[:, :, :MLA_QK]).astype(BF16)
    dkvf = _heads_minor(jnp.concatenate([dxk[:, :, :MLA_NOPE], dv_mla], axis=-1)).astype(BF16)
    g["w_uq"] = _mm(sv["cqn"], dqf, ta=True, name="d_w_uq")
    g["w_ukv"] = _mm(sv["ckvn"], dkvf, ta=True, name="d_w_ukv")
    dcqn = _mm(dqf, wl["w_uq"], tb=True, name="d_cqn")
    dckvn = _mm(dkvf, wl["w_ukv"], tb=True, name="d_ckvn")
    z = sv["z"]
    dcq, dg_cq = _rms_bwd(z[:, C_CQ:C_CQ + Q_LORA], wl["g_cq"], dcqn)
    dckv, dg_ckv = _rms_bwd(z[:, C_CKV:C_CKV + KV_LORA], wl["g_ckv"], dckvn)
    g["g_cq"], g["g_ckv"] = dg_cq[0], dg_ckv[0]
    tail = jnp.concatenate([dxk_sum[:, MLA_NOPE:MLA_QK], dtail_f[:, TAIL_F:]], axis=1)
    dz = jnp.concatenate([dcq, dckv, _heads_minor(dxfq), _heads_minor(dxfk), _heads_minor(dv_fox), tail],
                         axis=1).astype(BF16)
    g["w_in"] = _mm(hn, dz, ta=True, name="d_w_in_attn")
    dhn = _mm(dz, wl["w_in"], tb=True, name="d_hn_attn")
    return dhn, g, (exchanged[0] if exchanged else ())


def _local_step(x, target, w, gather_late=None, exchange_early=None):
    seq = x.shape[0]
    t = seq + BLOCK
    rope = _rope_tables(t)
    h = jnp.concatenate([jnp.zeros((PAD, D_MODEL), F32), w["meta_tokens"], x], axis=0)
    tape = []
    for layer in range(DEPTH):
        j = layer // 2
        hn = _rms_fwd(h, w["g_mix"][layer])
        if layer % 2 == 0:
            wl = dict(w_in=w["w_in_attn"][j], g_cq=w["g_cq"][j], w_uq=w["w_uq"][j], g_ckv=w["g_ckv"][j],
                      w_ukv=w["w_ukv"][j], g_q_mla=w["g_q_mla"][j], g_k_mla=w["g_k_mla"][j],
                      g_q_fox=w["g_q_fox"][j], g_k_fox=w["g_k_fox"][j], b_forget=w["b_forget"][j])
            hosted = gather_late is not None and layer == 0
            mixed, sv, gathered = _attn_layer_fwd(hn, wl, rope, gather_late[0] if hosted else ())
            if hosted:
                gather_late[1](w, gathered)
            h1 = _mm(mixed, w["w_out_attn"][j], res=h, name="attn_out")
        else:
            wl = None
            z3 = _mm(hn, w["w_in_conv"][j], out_seg=3, name="conv_in")
            mixed = _conv_fwd(z3, w["conv_w"][j])
            sv = dict(z3=z3)
            h1 = _mm(mixed, w["w_out_conv"][j], res=h, name="conv_out")
        hn2 = _rms_fwd(h1, w["g_mlp"][layer])
        u, act = _mm(hn2, w["w_mlp_up"][layer], epi="relu2", name="mlp_up")
        h2 = _mm(act, w["w_mlp_down"][layer], res=h1, name="mlp_down")
        tape.append(dict(h=h, hn=hn, wl=wl, sv=sv, mixed=mixed, h1=h1, hn2=hn2, u=u, act=act))
        h = h2

    dh, dh_b, loss = _loss_head(h, target)
    exchanged = ()
    g = {n: [None] * (DEPTH if n in ("g_mix", "g_mlp", "w_mlp_up", "w_mlp_down") else DEPTH // 2)
         for n in WEIGHTS if n != "meta_tokens"}
    for layer in reversed(range(DEPTH)):
        j = layer // 2
        tp = tape[layer]
        g["w_mlp_down"][layer] = _mm(tp["act"], dh_b, ta=True, name="d_w_down")
        du = _mm(dh_b, w["w_mlp_down"][layer], tb=True, epi="relu2_bwd", aux=tp["u"], out_dtype=BF16, name="d_u")
        g["w_mlp_up"][layer] = _mm(tp["hn2"], du, ta=True, name="d_w_up")
        dhn2 = _mm(du, w["w_mlp_up"][layer], tb=True, name="d_hn2")
        dh1, dh1_b, dg = _rms_bwd(tp["h1"], w["g_mlp"][layer], dhn2, dres=dh, want_bf16=True)
        g["g_mlp"][layer] = dg[0]
        if layer % 2 == 0:
            g["w_out_attn"][j] = _mm(tp["mixed"], dh1_b, ta=True, name="d_w_out_attn")
            dcat = _mm(dh1_b, w["w_out_attn"][j], tb=True, name="d_cat")
            hosted = exchange_early is not None and layer == 0
            dhn, gl, got = _attn_layer_bwd(dcat, tp["hn"], tp["wl"], tp["sv"], rope,
                                           exchange_early(g) if hosted else ())
            if hosted:
                exchanged = got
            g["w_in_attn"][j] = _unpermute_in_attn(gl.pop("w_in"))
            for n, val in gl.items():
                g[n][j] = val
        else:
            g["w_out_conv"][j] = _mm(tp["mixed"], dh1_b, ta=True, name="d_w_out_conv")
            dyb = _mm(dh1_b, w["w_out_conv"][j], tb=True, name="d_yb")
            dz3, dcw = _conv_bwd(tp["sv"]["z3"], w["conv_w"][j], dyb)
            g["conv_w"][j] = dcw
            g["w_in_conv"][j] = _mm(tp["hn"], dz3, ta=True, name="d_w_in_conv")
            dhn = _mm(dz3, w["w_in_conv"][j], tb=True, name="d_hn_conv")
        dh, dh_b, dg = _rms_bwd(tp["h"], w["g_mix"][layer], dhn, dres=dh1, want_bf16=True)
        g["g_mix"][layer] = dg[0]
    g["meta_tokens"] = [dh[PAD:BLOCK]]
    return loss, dh[BLOCK:], g, exchanged


COMM_ROWS = 2048
ADAMW_BLOCK_BYTES = 1 << 20


def kernel(x, meta_tokens, g_mix, g_mlp, w_in_attn, g_cq, w_uq, g_ckv, w_ukv, g_q_mla, g_k_mla, g_q_fox, g_k_fox, b_forget, w_out_attn, w_in_conv, conv_w, w_out_conv, w_mlp_up, w_mlp_down, loss_target, m_meta_tokens, m_g_mix, m_g_mlp, m_w_in_attn, m_g_cq, m_w_uq, m_g_ckv, m_w_ukv, m_g_q_mla, m_g_k_mla, m_g_q_fox, m_g_k_fox, m_b_forget, m_w_out_attn, m_w_in_conv, m_conv_w, m_w_out_conv, m_w_mlp_up, m_w_mlp_down, v_meta_tokens, v_g_mix, v_g_mlp, v_w_in_attn, v_g_cq, v_w_uq, v_g_ckv, v_w_ukv, v_g_q_mla, v_g_k_mla, v_g_q_fox, v_g_k_fox, v_b_forget, v_w_out_attn, v_w_in_conv, v_conv_w, v_w_out_conv, v_w_mlp_up, v_w_mlp_down):
    args = dict(locals())
    local = {n: args[n] for n in WEIGHTS}
    mom = {n: args["m_" + n] for n in WEIGHTS}
    var = {n: args["v_" + n] for n in WEIGHTS}
    axis = dict(SHARDED)
    count = {n: local[n].shape[0] for n, _ in SHARDED if n != "meta_tokens"}
    piece = lambda src, p: src[p[0]] if p[1] is None else src[p[0]][p[1]]
    piece_axis = lambda p: axis[p[0]] - (0 if p[1] is None else 1)
    shape_of = lambda p: piece(local, p).shape
    layers = lambda n, ls: [(n, l) for l in ls]
    first_bf = [("w_uq", 0), ("w_ukv", 0), ("w_out_attn", 0), ("w_mlp_up", 0), ("w_mlp_down", 0)]
    first_f32 = [("meta_tokens", None), ("conv_w", 0), ("conv_w", 1)]
    late_bf = ([("w_uq", 1), ("w_ukv", 1), ("w_out_attn", 1)] + layers("w_in_conv", (0, 1))
               + layers("w_out_conv", (0, 1)) + layers("w_mlp_up", (1, 2, 3)) + layers("w_mlp_down", (1, 2, 3)))
    early_bf = ([("w_uq", 1), ("w_ukv", 1)] + layers("w_out_attn", (0, 1)) + layers("w_in_conv", (0, 1))
                + layers("conv_w", (0, 1)) + layers("w_out_conv", (0, 1)) + layers("w_mlp_up", range(DEPTH))
                + layers("w_mlp_down", range(DEPTH)))
    last_bf = [("w_uq", 0), ("w_ukv", 0), ("meta_tokens", None)]

    w = {n: local[n] for n in REPLICATED}
    w.update({n: [None] * c for n, c in count.items()})

    def install(w, pieces, gathered, in_layer, gathered_in):
        for p, blocks in zip(pieces, _unpack(gathered, [shape_of(p) for p in pieces], (N_DEV,))):
            value = _from_shards(blocks, piece_axis(p))
            if p[1] is None:
                w[p[0]] = value
            else:
                w[p[0]][p[1]] = value
        w["w_in_attn"][in_layer] = _permute_in_attn(_from_shards(gathered_in, piece_axis(("w_in_attn", 0))))

    got_bf, got_in, got_f32 = _all_gather([
        _pack([piece(local, p) for p in first_bf], 16, BF16), local["w_in_attn"][0].astype(BF16),
        _pack([piece(local, p) for p in first_f32], 8, F32)])
    install(w, first_bf, got_bf, 0, got_in)
    for p, blocks in zip(first_f32, _unpack(got_f32, [shape_of(p) for p in first_f32], (N_DEV,))):
        if p[1] is None:
            w[p[0]] = _from_shards(blocks, piece_axis(p))
        else:
            w[p[0]][p[1]] = _from_shards(blocks, piece_axis(p))
    gather_late = ([_pack([piece(local, p) for p in late_bf], 16, BF16), local["w_in_attn"][1].astype(BF16)],
                   lambda w_, got: install(w_, late_bf, got[0], 1, got[1]))

    def pack_grads(g, pieces, in_layer):
        sent = _pack_rows([_to_shards(piece(g, p), piece_axis(p)) for p in pieces], COMM_ROWS, BF16)
        return [sent, _to_shards(g["w_in_attn"][in_layer], piece_axis(("w_in_attn", 0))).astype(BF16)]

    loss_part, dx, grads, early = _local_step(x[0], loss_target[0], w, gather_late,
                                              lambda g: pack_grads(g, early_bf, 1))
    loss = lax.psum(loss_part[0, 0], ("x", "y", "c"))

    grads["meta_tokens"] = grads["meta_tokens"][0]
    last = _all_to_all(pack_grads(grads, last_bf, 0))
    g_piece = {}
    for pieces, (got, got_in), in_layer in ((early_bf, early, 1), (last_bf, last, 0)):
        summed = _unpack(_sum_blocks(got, COMM_ROWS), [shape_of(p) for p in pieces])
        g_piece.update(zip(pieces, summed))
        g_piece[("w_in_attn", in_layer)] = _sum_blocks(got_in, 256)
    g_local = {n: jnp.stack([g_piece[(n, l)] for l in range(c)]) for n, c in count.items()}
    g_local["meta_tokens"] = g_piece[("meta_tokens", None)]
    rep, = _all_gather([_pack([jnp.stack(grads[n]) for n in REPLICATED], 8, F32)])
    g_rep = _sum_blocks(rep, rep.shape[1])
    g_local.update(zip(REPLICATED, _unpack(g_rep, [local[n].shape for n in REPLICATED])))

    def flat(src, names, rows_multiple):
        return _pack([src[n] for n in names], rows_multiple, F32)

    upd = {}
    rows = g_rep.shape[0]
    outs = _adamw(flat(local, REPLICATED, rows), g_rep, flat(mom, REPLICATED, rows), flat(var, REPLICATED, rows), rows)
    for kind, buf in zip(("delta", "m", "v"), outs):
        upd.update({(kind, n): a for n, a in zip(REPLICATED, _unpack(buf, [local[n].shape for n in REPLICATED]))})
    for n, _ in SHARDED:
        as_rows = lambda a: a.reshape(-1, a.shape[-1])
        n_rows, n_cols = as_rows(local[n]).shape
        tiles = [r for r in (1024, 512, 256, 128, 64, 32, 16, 8) if r * n_cols * 4 <= ADAMW_BLOCK_BYTES]
        outs = _adamw(as_rows(local[n]), as_rows(g_local[n]), as_rows(mom[n]), as_rows(var[n]), _pick(n_rows, tiles))
        for kind, buf in zip(("delta", "m", "v"), outs):
            upd[(kind, n)] = buf.reshape(local[n].shape)

    return (loss, dx[None], *[g_local[n] for n in WEIGHTS], *[upd[("delta", n)] for n in WEIGHTS],
            *[upd[("m", n)] for n in WEIGHTS], *[upd[("v", n)] for n in WEIGHTS])
```

```python
import functools
import math

import jax
import jax.numpy as jnp
import numpy as np
from jax import lax
from jax.experimental import pallas as pl
from jax.experimental.pallas import tpu as pltpu

F32 = jnp.float32
BF16 = jnp.bfloat16

N_DEV = 8
D_MODEL = 1024
DEPTH = 4
N_META = 16
BLOCK = 128
PAD = BLOCK - N_META
HEADS = 8
MLA_NOPE = 64
MLA_ROPE = 32
MLA_QK = MLA_NOPE + MLA_ROPE
MLA_V = 64
Q_LORA = 384
KV_LORA = 256
ROPE_BASE = 10000.0
FOX_DIM = 64
D_FF = 4 * D_MODEL
EPS = 1e-6
NEG = -1e30
LANES = 128
ATTN_IN_PAD = 2304
C_CQ, C_CKV, C_FQ, C_FK, C_FV, C_TAIL = 0, 384, 640, 1152, 1664, 2176
TAIL_F = MLA_ROPE

ADAM_LR = 0.001
ADAM_B1 = 0.9
ADAM_B2 = 0.999
ADAM_EPS = 1e-08
ADAM_WD = 0.01
ADAM_STEP = 10

VMEM_LIMIT_BYTES = 48 * 1024 * 1024
MESH = pl.DeviceIdType.MESH

SHARDED = (
    ("meta_tokens", 1), ("w_in_attn", 2), ("w_uq", 2), ("w_ukv", 2), ("w_out_attn", 1),
    ("w_in_conv", 2), ("conv_w", 2), ("w_out_conv", 1), ("w_mlp_up", 2), ("w_mlp_down", 1))
F32_GATHERED = ("meta_tokens", "conv_w")
REPLICATED = ("g_mix", "g_mlp", "g_cq", "g_ckv", "g_q_mla", "g_k_mla", "g_q_fox", "g_k_fox", "b_forget")
WEIGHTS = ("meta_tokens", "g_mix", "g_mlp", "w_in_attn", "g_cq", "w_uq", "g_ckv", "w_ukv", "g_q_mla",
           "g_k_mla", "g_q_fox", "g_k_fox", "b_forget", "w_out_attn", "w_in_conv", "conv_w",
           "w_out_conv", "w_mlp_up", "w_mlp_down")


def _params(*sem):
    return pltpu.CompilerParams(dimension_semantics=sem, vmem_limit_bytes=VMEM_LIMIT_BYTES)


def _row_tile(t):
    return 640 if (t % 640 == 0 and t > 640) else 128


def _pack(parts, rows_multiple, dtype):
    flat = jnp.concatenate([p.reshape(-1).astype(dtype) for p in parts])
    n = flat.shape[0]
    rows = -(-n // LANES)
    rows = -(-rows // rows_multiple) * rows_multiple
    return jnp.pad(flat, (0, rows * LANES - n)).reshape(rows, LANES)


def _pack_rows(parts, rows_multiple, dtype):
    flat = jnp.concatenate([p.reshape(N_DEV, -1).astype(dtype) for p in parts], axis=1)
    n = flat.shape[1]
    rows = -(-n // LANES)
    rows = -(-rows // rows_multiple) * rows_multiple
    return jnp.pad(flat, ((0, 0), (0, rows * LANES - n))).reshape(N_DEV, rows, LANES)


def _unpack(buf, shapes, lead=()):
    flat = buf.reshape(lead + (-1,))
    out, off = [], 0
    for s in shapes:
        n = math.prod(s)
        out.append(flat[..., off:off + n].reshape(lead + tuple(s)))
        off += n
    return out


def _to_shards(full, axis):
    s = full.shape
    return jnp.moveaxis(full.reshape(s[:axis] + (N_DEV, s[axis] // N_DEV) + s[axis + 1:]), axis, 0)


def _from_shards(g8, axis):
    m = jnp.moveaxis(g8, 0, axis)
    s = m.shape
    return m.reshape(s[:axis] + (s[axis] * s[axis + 1],) + s[axis + 2:])


def _comm_call(body, name, xs, out_shapes):
    n = len(xs)
    hbm = pl.BlockSpec(memory_space=pltpu.HBM)
    return pl.pallas_call(
        body, name=name, out_shape=out_shapes, in_specs=[hbm] * n, out_specs=[hbm] * n,
        scratch_shapes=[pltpu.SemaphoreType.DMA((n, 7)), pltpu.SemaphoreType.DMA((n, 7)),
                        pltpu.SemaphoreType.DMA((n,))],
    )(*xs)


def _gather_ops(x_refs, out_refs, send_sems, recv_sems, local_sems):
    n = len(x_refs)
    x_, y_, c = lax.axis_index("x"), lax.axis_index("y"), lax.axis_index("c")
    me, sibling = (x_, y_, c), (x_, y_, 1 - c)
    chips = [(1 - x_, y_), (x_, 1 - y_), (1 - x_, 1 - y_)]

    def rows(a, px, py, pc):
        return out_refs[a].at[4 * px + 2 * py + pc]

    def copy(a, k, block, to, src=None):
        return pltpu.make_async_remote_copy(
            src_ref=rows(a, *block) if src is None else src, dst_ref=rows(a, *block),
            send_sem=send_sems.at[a, k], recv_sem=recv_sems.at[a, k], device_id=to, device_id_type=MESH)

    def mine():
        return [pltpu.make_async_copy(x_refs[a], rows(a, *me), local_sems.at[a]) for a in range(n)]

    def first():
        cps = []
        for a in range(n):
            cps.append(copy(a, 0, me, sibling, src=x_refs[a]))
            cps += [copy(a, 1 + j, me, (*chip, c), src=x_refs[a]) for j, chip in enumerate(chips)]
        return cps

    def start():
        for cp in mine() + first():
            cp.start()

    def finish():
        passed = []
        for j, chip in enumerate(chips):
            for a in range(n):
                copy(a, 1 + j, (*chip, c), me).wait_recv()
                passed.append(copy(a, 4 + j, (*chip, c), sibling))
                passed[-1].start()
        for a in range(n):
            copy(a, 0, sibling, me).wait_recv()
            for j, chip in enumerate(chips):
                copy(a, 4 + j, (*chip, 1 - c), me).wait_recv()
        for cp in first() + passed:
            cp.wait_send()
        for cp in mine():
            cp.wait()

    return start, finish


def _exchange_ops(x_refs, out_refs, send_sems, recv_sems, local_sems):
    n = len(x_refs)
    x_, y_, c = lax.axis_index("x"), lax.axis_index("y"), lax.axis_index("c")
    me = 4 * x_ + 2 * y_ + c

    def peer(k):
        px = 1 - x_ if k & 4 else x_
        py = 1 - y_ if k & 2 else y_
        pc = 1 - c if k & 1 else c
        return px, py, pc

    def copy(a, k):
        px, py, pc = peer(k)
        return pltpu.make_async_remote_copy(
            src_ref=x_refs[a].at[4 * px + 2 * py + pc], dst_ref=out_refs[a].at[me],
            send_sem=send_sems.at[a, k - 1], recv_sem=recv_sems.at[a, k - 1], device_id=(px, py, pc),
            device_id_type=MESH)

    def arrival(a, k):
        px, py, pc = peer(k)
        slot = 4 * px + 2 * py + pc
        return pltpu.make_async_remote_copy(
            src_ref=x_refs[a].at[slot], dst_ref=out_refs[a].at[slot],
            send_sem=send_sems.at[a, k - 1], recv_sem=recv_sems.at[a, k - 1], device_id=(px, py, pc),
            device_id_type=MESH)

    def mine():
        return [pltpu.make_async_copy(x_refs[a].at[me], out_refs[a].at[me], local_sems.at[a]) for a in range(n)]

    def sends():
        return [copy(a, k) for k in range(1, N_DEV) for a in range(n)]

    def start():
        for cp in mine() + sends():
            cp.start()

    def finish():
        for k in range(1, N_DEV):
            for a in range(n):
                arrival(a, k).wait_recv()
        for cp in sends():
            cp.wait_send()
        for cp in mine():
            cp.wait()

    return start, finish


def _comm_parts(xs, gather):
    n = len(xs)
    hbm = pl.BlockSpec(memory_space=pltpu.HBM)
    shapes = [jax.ShapeDtypeStruct(((N_DEV,) + x.shape) if gather else x.shape, x.dtype) for x in xs]
    sems = [pltpu.SemaphoreType.DMA((n, 7)), pltpu.SemaphoreType.DMA((n, 7)), pltpu.SemaphoreType.DMA((n,))]
    return [hbm] * n, [hbm] * n, shapes, sems


def _all_gather(xs):
    n = len(xs)

    def body(*refs):
        start, finish = _gather_ops(refs[:n], refs[n:2 * n], *refs[2 * n:])
        start()
        finish()

    return _comm_call(body, "all_gather", xs, [jax.ShapeDtypeStruct((N_DEV,) + x.shape, x.dtype) for x in xs])


def _all_to_all(xs):
    n = len(xs)

    def body(*refs):
        start, finish = _exchange_ops(refs[:n], refs[n:2 * n], *refs[2 * n:])
        start()
        finish()

    return _comm_call(body, "all_to_all", xs, [jax.ShapeDtypeStruct(x.shape, x.dtype) for x in xs])


def _sum_blocks(x, rows_tile):
    _, r, c_ = x.shape

    def body(x_ref, o_ref):
        acc = x_ref[0].astype(F32)
        for d in range(1, N_DEV):
            acc = acc + x_ref[d].astype(F32)
        o_ref[...] = acc

    return pl.pallas_call(
        body, name="sum_blocks", grid=(r // rows_tile,),
        in_specs=[pl.BlockSpec((N_DEV, rows_tile, c_), lambda i: (0, i, 0))],
        out_specs=pl.BlockSpec((rows_tile, c_), lambda i: (i, 0)),
        out_shape=jax.ShapeDtypeStruct((r, c_), F32),
        compiler_params=_params("parallel"),
    )(x)


def _adamw(w, g, m, v, rows_tile):
    r, c_ = w.shape
    c1 = 1.0 - ADAM_B1 ** ADAM_STEP
    c2 = 1.0 - ADAM_B2 ** ADAM_STEP

    def body(w_ref, g_ref, m_ref, v_ref, d_ref, mo_ref, vo_ref):
        g_ = g_ref[...]
        m_ = ADAM_B1 * m_ref[...] + (1.0 - ADAM_B1) * g_
        v_ = ADAM_B2 * v_ref[...] + (1.0 - ADAM_B2) * (g_ * g_)
        m_hat = m_ / c1
        v_hat = v_ / c2
        d_ref[...] = -ADAM_LR * (m_hat / (jnp.sqrt(v_hat) + ADAM_EPS) + ADAM_WD * w_ref[...])
        mo_ref[...] = m_
        vo_ref[...] = v_

    spec = pl.BlockSpec((rows_tile, c_), lambda i: (i, 0))
    shape = jax.ShapeDtypeStruct((r, c_), F32)
    return pl.pallas_call(
        body, name="adamw", grid=(r // rows_tile,), in_specs=[spec] * 4, out_specs=[spec] * 3,
        out_shape=[shape] * 3, compiler_params=_params("parallel"),
    )(w, g, m, v)


def _pick(n, prefs):
    for p in prefs:
        if n % p == 0:
            return p
    return n


def _mat_spec(arr, tr, tc, r_of, c_of):
    if arr.ndim == 2:
        return pl.BlockSpec((tr, tc), lambda i, j, k: (r_of(i, j, k), c_of(i, j, k)))
    per = arr.shape[2] // tc
    return pl.BlockSpec((None, tr, tc), lambda i, j, k: (c_of(i, j, k) // per, r_of(i, j, k), c_of(i, j, k) % per))


def _mm(a, b, *, ta=False, tb=False, out_dtype=F32, out_seg=None, res=None, epi=None, aux=None,
        tm=None, tn=None, tk=None, name="mm"):
    def dims(x):
        return (x.shape[0], x.shape[1]) if x.ndim == 2 else (x.shape[1], x.shape[0] * x.shape[2])
    ar, ac = dims(a)
    br, bc = dims(b)
    m, k = (ac, ar) if ta else (ar, ac)
    n, kb = (br, bc) if tb else (bc, br)
    assert k == kb, (a.shape, b.shape, ta, tb)
    tm = tm or _pick(m, (1024, 512, 384, 256, 128) if ta else (640, 512, 384, 256, 128))
    tn = tn or _pick(n, (1024, 768, 512, 384, 256, 128))
    tk = tk or _pick(k, ((1664,) if ta else (2048,)) + (1024, 768, 640, 512, 384, 256, 128))
    if out_seg:
        assert (n // out_seg) % tn == 0
    for x, t in ((a, tm if ta else tk), (b, tk if tb else tn)):
        if x.ndim == 3:
            assert x.shape[2] % t == 0
    nk = k // tk
    gi, gj, gk = (lambda j, i, kk: i), (lambda j, i, kk: j), (lambda j, i, kk: kk)
    a_spec = _mat_spec(a, tk, tm, gk, gi) if ta else _mat_spec(a, tm, tk, gi, gk)
    b_spec = _mat_spec(b, tn, tk, gj, gk) if tb else _mat_spec(b, tk, tn, gk, gj)
    out_like = jnp.zeros((out_seg, 0, n // out_seg)) if out_seg else jnp.zeros((0, n))
    o_spec = _mat_spec(out_like, tm, tn, gi, gj)
    o_shape = (out_seg, m, n // out_seg) if out_seg else (m, n)
    dn = (((0 if ta else 1,), (1 if tb else 0,)), ((), ()))
    extra = [x for x in (res, aux) if x is not None]
    assert not (res is not None and aux is not None)
    n_out = 2 if epi == "relu2" else 1

    def body(*refs):
        a_ref, b_ref = refs[0], refs[1]
        x_ref = refs[2] if extra else None
        outs = refs[2 + len(extra):2 + len(extra) + n_out]
        acc_ref = refs[-1] if nk > 1 else None
        part = lax.dot_general(a_ref[...], b_ref[...], dn, preferred_element_type=F32)

        def finish(acc):
            if epi == "relu2":
                outs[0][...] = acc
                r = jnp.maximum(acc, 0.0)
                outs[1][...] = (r * r).astype(BF16)
            elif epi == "relu2_bwd":
                outs[0][...] = (acc * (2.0 * jnp.maximum(x_ref[...], 0.0))).astype(out_dtype)
            elif res is not None:
                outs[0][...] = (acc + x_ref[...]).astype(out_dtype)
            else:
                outs[0][...] = acc.astype(out_dtype)

        if nk == 1:
            finish(part)
        else:
            kk = pl.program_id(2)

            @pl.when(kk == 0)
            def _():
                acc_ref[...] = part

            @pl.when(kk > 0)
            def _():
                acc_ref[...] += part

            @pl.when(kk == nk - 1)
            def _():
                finish(acc_ref[...])

    if epi == "relu2":
        out_shape = [jax.ShapeDtypeStruct(o_shape, F32), jax.ShapeDtypeStruct(o_shape, BF16)]
        out_specs = [o_spec, o_spec]
    else:
        out_shape = jax.ShapeDtypeStruct(o_shape, out_dtype)
        out_specs = o_spec
    x_specs = [pl.BlockSpec((tm, tn), lambda j, i, kk: (i, j))] * len(extra)
    res_ = pl.pallas_call(
        body, name=name, grid=(n // tn, m // tm, nk),
        in_specs=[a_spec, b_spec] + x_specs, out_specs=out_specs, out_shape=out_shape,
        scratch_shapes=[pltpu.VMEM((tm, tn), F32)] if nk > 1 else [],
        compiler_params=_params("parallel", "parallel", "arbitrary"),
    )(a, b, *extra)
    return res_


def _rms_fwd(x, g):
    t, d = x.shape
    tm = _row_tile(t)

    def body(x_ref, g_ref, o_ref):
        x_ = x_ref[...]
        rstd = lax.rsqrt(jnp.mean(x_ * x_, axis=-1, keepdims=True) + EPS)
        o_ref[...] = (x_ * rstd * g_ref[...]).astype(BF16)

    return pl.pallas_call(
        body, name="rms_fwd", grid=(t // tm,),
        in_specs=[pl.BlockSpec((tm, d), lambda i: (i, 0)), pl.BlockSpec((1, d), lambda i: (0, 0))],
        out_specs=pl.BlockSpec((tm, d), lambda i: (i, 0)),
        out_shape=jax.ShapeDtypeStruct((t, d), BF16), compiler_params=_params("parallel"),
    )(x, g.reshape(1, d))


def _rms_bwd(x, g, dy, dres=None, want_bf16=False):
    t, d = x.shape
    tm = _row_tile(t)
    has_res = dres is not None

    def body(*refs):
        x_ref, g_ref, dy_ref = refs[:3]
        r_ref = refs[3] if has_res else None
        outs = refs[3 + has_res:]
        x_ = x_ref[...]
        rstd = lax.rsqrt(jnp.mean(x_ * x_, axis=-1, keepdims=True) + EPS)
        xh = x_ * rstd
        dy_ = dy_ref[...]
        dxh = dy_ * g_ref[...]
        dx = rstd * (dxh - xh * jnp.mean(dxh * xh, axis=-1, keepdims=True))
        if has_res:
            dx = dx + r_ref[...]
        outs[0][...] = dx
        if want_bf16:
            outs[1][...] = dx.astype(BF16)
        dg_ref = outs[-1]

        @pl.when(pl.program_id(0) == 0)
        def _():
            dg_ref[...] = jnp.zeros_like(dg_ref)

        dg_ref[...] += jnp.sum(dy_ * xh, axis=0, keepdims=True)

    row = pl.BlockSpec((tm, d), lambda i: (i, 0))
    vec = pl.BlockSpec((1, d), lambda i: (0, 0))
    out_shape = [jax.ShapeDtypeStruct((t, d), F32)] + ([jax.ShapeDtypeStruct((t, d), BF16)] if want_bf16 else []) \
        + [jax.ShapeDtypeStruct((1, d), F32)]
    out_specs = [row] + ([row] if want_bf16 else []) + [vec]
    return pl.pallas_call(
        body, name="rms_bwd", grid=(t // tm,),
        in_specs=[row, vec, row] + ([row] if has_res else []), out_specs=out_specs, out_shape=out_shape,
        compiler_params=_params("arbitrary"),
    )(x, g.reshape(1, d), dy, *([dres] if has_res else []))


def _swap_rope_halves(y):
    lane = lax.broadcasted_iota(jnp.int32, y.shape, 1)
    half = MLA_ROPE // 2
    swapped = jnp.where(lane < MLA_NOPE + half, pltpu.roll(y, LANES - half, axis=1), pltpu.roll(y, half, axis=1))
    return jnp.where((lane >= MLA_NOPE) & (lane < MLA_QK), swapped, 0.0)


def _head_norm_fwd(x, g, n_valid, scale, rope=None):
    h, t, w = x.shape
    tm = _row_tile(t)

    def body(*refs):
        x_ref, g_ref = refs[:2]
        o_ref = refs[-1]
        x_ = x_ref[...].reshape(h * tm, w)
        rstd = lax.rsqrt(jnp.sum(x_ * x_, axis=-1, keepdims=True) * (1.0 / n_valid) + EPS)
        y = x_ * rstd * (g_ref[...] * scale)
        if rope is not None:
            y3, s3 = y.reshape(h, tm, w), _swap_rope_halves(y).reshape(h, tm, w)
            y = (y3 * refs[2][...][None] + s3 * refs[3][...][None]).reshape(h * tm, w)
        o_ref[...] = y.reshape(h, tm, w).astype(BF16)

    blk = pl.BlockSpec((h, tm, w), lambda i: (0, i, 0))
    tab = pl.BlockSpec((tm, w), lambda i: (i, 0))
    return pl.pallas_call(
        body, name="head_norm_fwd", grid=(t // tm,),
        in_specs=[blk, pl.BlockSpec((1, w), lambda i: (0, 0))] + ([tab, tab] if rope is not None else []),
        out_specs=blk, out_shape=jax.ShapeDtypeStruct((h, t, w), BF16),
        compiler_params=_params("parallel"),
    )(x, g, *(rope if rope is not None else ()))


def _head_norm_bwd(x, g, dout, n_valid, scale, rope=None):
    h, t, w = x.shape
    tm = 320 if t % 320 == 0 else BLOCK

    def body(*refs):
        x_ref, g_ref, do_ref = refs[:3]
        dx_ref, dsum_ref, dg_ref = refs[-3:]
        dy = do_ref[...]
        if rope is not None:
            c_, s_ = refs[3][...], refs[4][...]
            ds = (dy * s_[None]).reshape(h * tm, w)
            dy = dy * c_[None] + _swap_rope_halves(ds).reshape(h, tm, w)
        x_ = x_ref[...]
        rstd = lax.rsqrt(jnp.sum(x_ * x_, axis=-1, keepdims=True) * (1.0 / n_valid) + EPS)
        xh = x_ * rstd
        dxh = dy * (g_ref[...] * scale)[None]
        dx = rstd * (dxh - xh * (jnp.sum(dxh * xh, axis=-1, keepdims=True) * (1.0 / n_valid)))
        dx_ref[...] = dx
        dsum_ref[...] = jnp.sum(dx, axis=0)

        @pl.when(pl.program_id(0) == 0)
        def _():
            dg_ref[...] = jnp.zeros_like(dg_ref)

        dg_ref[...] += scale * jnp.sum(jnp.sum(dy * xh, axis=0), axis=0, keepdims=True)

    blk = pl.BlockSpec((h, tm, w), lambda i: (0, i, 0))
    tab = pl.BlockSpec((tm, w), lambda i: (i, 0))
    vec = pl.BlockSpec((1, w), lambda i: (0, 0))
    return pl.pallas_call(
        body, name="head_norm_bwd", grid=(t // tm,),
        in_specs=[blk, vec, blk] + ([tab, tab] if rope is not None else []),
        out_specs=[blk, tab, vec],
        out_shape=[jax.ShapeDtypeStruct((h, t, w), F32), jax.ShapeDtypeStruct((t, w), F32),
                   jax.ShapeDtypeStruct((1, w), F32)],
        compiler_params=_params("arbitrary"),
    )(x, g, dout, *(rope if rope is not None else ()))


def _tri(n, upper):
    r = lax.broadcasted_iota(jnp.int32, (n, n), 0)
    c = lax.broadcasted_iota(jnp.int32, (n, n), 1)
    return ((r <= c) if upper else (r >= c)).astype(F32)


def _gate_mask(shape, row0):
    lane = lax.broadcasted_iota(jnp.int32, shape, 1)
    row = row0 + lax.broadcasted_iota(jnp.int32, shape, 0)
    return (lane >= TAIL_F) & (lane < TAIL_F + HEADS) & (row >= PAD)


def _gate_fwd(z, bias):
    t = z.shape[0]
    tm = BLOCK
    tail = C_TAIL // LANES

    def body(z_ref, b_ref, o_ref, carry):
        i = pl.program_id(0)

        @pl.when(i == 0)
        def _():
            carry[...] = jnp.zeros_like(carry)

        x_ = z_ref[...] + b_ref[...]
        logf = jnp.minimum(x_, 0.0) - jnp.log1p(jnp.exp(-jnp.abs(x_)))
        logf = jnp.where(_gate_mask(logf.shape, i * tm), logf, 0.0)
        cum = jnp.dot(_tri(tm, False), logf, preferred_element_type=F32, precision=lax.Precision.HIGHEST) + carry[...]
        o_ref[...] = cum
        carry[...] = cum[tm - 1:tm, :]

    return pl.pallas_call(
        body, name="gate_fwd", grid=(t // tm,),
        in_specs=[pl.BlockSpec((tm, LANES), lambda i: (i, tail)), pl.BlockSpec((1, LANES), lambda i: (0, 0))],
        out_specs=pl.BlockSpec((tm, LANES), lambda i: (i, 0)),
        out_shape=jax.ShapeDtypeStruct((t, LANES), F32),
        scratch_shapes=[pltpu.VMEM((1, LANES), F32)], compiler_params=_params("arbitrary"),
    )(z, bias)


def _gate_bwd(z, bias, dcum):
    t = z.shape[0]
    tm = BLOCK
    nb = t // tm
    tail = C_TAIL // LANES

    def body(z_ref, b_ref, d_ref, o_ref, db_ref, carry):
        i = pl.program_id(0)

        @pl.when(i == 0)
        def _():
            carry[...] = jnp.zeros_like(carry)
            db_ref[...] = jnp.zeros_like(db_ref)

        rc = jnp.dot(_tri(tm, True), d_ref[...], preferred_element_type=F32, precision=lax.Precision.HIGHEST) + carry[...]
        carry[...] = rc[0:1, :]
        x_ = z_ref[...] + b_ref[...]
        sig_neg = 1.0 / (1.0 + jnp.exp(x_))
        dl = jnp.where(_gate_mask(rc.shape, (nb - 1 - i) * tm), rc * sig_neg, 0.0)
        o_ref[...] = dl
        db_ref[...] += jnp.sum(dl, axis=0, keepdims=True)

    return pl.pallas_call(
        body, name="gate_bwd", grid=(nb,),
        in_specs=[pl.BlockSpec((tm, LANES), lambda i: (nb - 1 - i, tail)), pl.BlockSpec((1, LANES), lambda i: (0, 0)),
                  pl.BlockSpec((tm, LANES), lambda i: (nb - 1 - i, 0))],
        out_specs=[pl.BlockSpec((tm, LANES), lambda i: (nb - 1 - i, 0)), pl.BlockSpec((1, LANES), lambda i: (0, 0))],
        out_shape=[jax.ShapeDtypeStruct((t, LANES), F32), jax.ShapeDtypeStruct((1, LANES), F32)],
        scratch_shapes=[pltpu.VMEM((1, LANES), F32)], compiler_params=_params("arbitrary"),
    )(z, bias, dcum)


def _pairs(nb, by_query):
    if by_query:
        pr = [(i, j) for i in range(nb) for j in range(i + 1)]
    else:
        pr = [(i, j) for j in range(nb) for i in range(j, nb)]
    return (jnp.asarray(np.array([p[0] for p in pr], np.int32)),
            jnp.asarray(np.array([p[1] for p in pr], np.int32)))


HEADS_PER_STEP = 8


def _mask_scores(s, i, j, tile):
    qp = i * tile + lax.broadcasted_iota(jnp.int32, s.shape, 0)
    kp = j * tile + lax.broadcasted_iota(jnp.int32, s.shape, 1)
    return jnp.where((kp <= qp) & (kp >= PAD), s, NEG)


def _pipelined(n, front, back):
    nxt = front(0)
    for h in range(n):
        cur = nxt
        if h + 1 < n:
            nxt = front(h + 1)
        back(h, cur)


def _nt_dot(a, b):
    return lax.dot_general(a, b, (((1,), (1,)), ((), ())), preferred_element_type=F32)


def _tn_dot(a, b):
    return lax.dot_general(a, b, (((0,), (0,)), ((), ())), preferred_element_type=F32)


def _attn_specs(hb, tile, dk, dv):
    q_of = lambda h, p, it, jt: (h, it[p], 0)
    k_of = lambda h, p, it, jt: (h, jt[p], 0)
    return dict(
        q=pl.BlockSpec((hb, tile, dk), q_of), k=pl.BlockSpec((hb, tile, dk), k_of),
        v=pl.BlockSpec((hb, tile, dv), k_of), ov=pl.BlockSpec((hb, tile, dv), q_of),
        col=pl.BlockSpec((hb, tile, 1), q_of), colk=pl.BlockSpec((hb, tile, 1), k_of),
        fr=pl.BlockSpec((hb, 1, tile), lambda h, p, it, jt: (h, 0, jt[p])),
        rowq=pl.BlockSpec((hb, 1, tile), lambda h, p, it, jt: (h, 0, it[p])))


def _attn_fwd(q, k, v, key_bias=None, gather=()):
    h, t, dk = q.shape
    dv = v.shape[-1]
    tile = _row_tile(t)
    nb = t // tile
    hb = HEADS_PER_STEP
    biased = key_bias is not None
    it, jt = _pairs(nb, True)
    sp = _attn_specs(hb, tile, dk, dv)

    n_in = 4 if biased else 3
    n_g = len(gather)
    g_in, g_out, g_shapes, g_sems = _comm_parts(gather, True)
    n_steps = int(it.shape[0])

    def body(it_ref, jt_ref, *refs):
        q_ref, k_ref, v_ref = refs[:3]
        b_ref = refs[3] if biased else None
        o_ref, lse_ref = refs[n_in + 2 * n_g:n_in + 2 * n_g + 2]
        m_sc, l_sc, acc_sc = refs[n_in + 2 * n_g + 2:n_in + 2 * n_g + 5]
        p = pl.program_id(1)
        i, j = it_ref[p], jt_ref[p]
        if n_g:
            g_start, g_finish = _gather_ops(refs[n_in:n_in + n_g], refs[n_in + n_g:n_in + 2 * n_g],
                                            *refs[n_in + 2 * n_g + 5:])
            first = (pl.program_id(0) == 0) & (p == 0)
            last = (pl.program_id(0) == h // hb - 1) & (p == n_steps - 1)
            pl.when(first)(g_start)

        @pl.when(j == 0)
        def _():
            m_sc[...] = jnp.full_like(m_sc, NEG)
            l_sc[...] = jnp.zeros_like(l_sc)
            acc_sc[...] = jnp.zeros_like(acc_sc)

        def step(masked):
            def front(n):
                return _nt_dot(q_ref[n], k_ref[n])

            def back(n, s):
                if biased:
                    s = s + b_ref[n]
                if masked:
                    s = _mask_scores(s, i, j, tile)
                m_prev = m_sc[n]
                m_new = jnp.maximum(m_prev, jnp.max(s, axis=-1, keepdims=True))
                alpha = jnp.exp2(m_prev - m_new)
                e = jnp.exp2(s - m_new)
                l_sc[n] = alpha * l_sc[n] + jnp.sum(e, axis=-1, keepdims=True)
                acc_sc[n] = alpha * acc_sc[n] + jnp.dot(e.astype(BF16), v_ref[n], preferred_element_type=F32)
                m_sc[n] = m_new

            _pipelined(hb, front, back)

        edge = (j == i) | (j == 0)
        pl.when(edge)(lambda: step(True))
        pl.when(jnp.logical_not(edge))(lambda: step(False))

        @pl.when(j == i)
        def _():
            row = i * tile + lax.broadcasted_iota(jnp.int32, (hb, tile, 1), 1)
            o_ref[...] = jnp.where(row >= PAD, acc_sc[...] / l_sc[...], 0.0)
            for n in range(hb):
                lse_ref[n] = jnp.transpose(m_sc[n] + jnp.log2(l_sc[n]))

        if n_g:
            pl.when(last)(g_finish)

    grid_spec = pltpu.PrefetchScalarGridSpec(
        num_scalar_prefetch=2, grid=(h // hb, n_steps),
        in_specs=[sp["q"], sp["k"], sp["v"]] + ([sp["fr"]] if biased else []) + g_in,
        out_specs=g_out + [sp["ov"], sp["rowq"]],
        scratch_shapes=[pltpu.VMEM((hb, tile, 1), F32), pltpu.VMEM((hb, tile, 1), F32),
                        pltpu.VMEM((hb, tile, dv), F32)] + (g_sems if n_g else []))
    outs = pl.pallas_call(
        body, name="attn_fwd_gather" if n_g else "attn_fwd", grid_spec=grid_spec,
        out_shape=g_shapes + [jax.ShapeDtypeStruct((h, t, dv), F32), jax.ShapeDtypeStruct((h, 1, t), F32)],
        compiler_params=_params("arbitrary" if n_g else "parallel", "arbitrary"),
    )(it, jt, q, k, v, *((key_bias,) if biased else ()), *gather)
    return (outs[n_g], outs[n_g + 1], outs[:n_g]) if n_g else tuple(outs)


LOG2E = 1.4426950408889634
LN2 = 0.6931471805599453
MLA_SCALE = MLA_QK ** -0.5 * LOG2E
FOX_SCALE = FOX_DIM ** -0.5 * LOG2E


def _attn_delta(o, do):
    h, t, dv = o.shape
    tm = _row_tile(t)

    def body(o_ref, do_ref, d_ref):
        for n in range(h):
            d_ref[n] = jnp.transpose(jnp.sum(o_ref[n] * do_ref[n], axis=-1, keepdims=True) * LN2)

    blk = pl.BlockSpec((h, tm, dv), lambda i: (0, i, 0))
    return pl.pallas_call(
        body, name="attn_delta", grid=(t // tm,), in_specs=[blk, blk],
        out_specs=pl.BlockSpec((h, 1, tm), lambda i: (0, 0, i)),
        out_shape=jax.ShapeDtypeStruct((h, 1, t), F32), compiler_params=_params("parallel"),
    )(o, do)


BWD_HEADS_PER_STEP = (4, 2)


def _attn_bwd(q, k, v, do, lse_row, delta_row, key_bias=None, exchange=()):
    h, t, dk = q.shape
    dv = v.shape[-1]
    tile = _row_tile(t)
    nb = t // tile
    decay = key_bias is not None
    hb = BWD_HEADS_PER_STEP[int(decay)]
    it, jt = _pairs(nb, False)
    sp = _attn_specs(hb, tile, dk, dv)
    n_in = 7 if decay else 6
    n_out = 5 if decay else 3
    n_x = len(exchange)
    x_in, x_out, x_shapes, x_sems = _comm_parts(exchange, False)
    n_steps = int(it.shape[0])

    def body(it_ref, jt_ref, *refs):
        q_ref, k_ref, v_ref, do_ref, lse_ref, delta_ref = refs[:6]
        b_ref = refs[6] if decay else None
        outs = refs[n_in + n_x:]
        dq_ref, dk_ref, dv_ref = outs[:3]
        rs_ref, ks_ref = (outs[3], outs[4]) if decay else (None, None)
        scratch = outs[n_out + n_x:]
        dq_sc, dk_sc, dv_sc = scratch[:3]
        ks_sc, b_sc = (scratch[3], scratch[4]) if decay else (None, None)
        p = pl.program_id(1)
        i, j = it_ref[p], jt_ref[p]
        if n_x:
            x_start, x_finish = _exchange_ops(refs[n_in:n_in + n_x], outs[n_out:n_out + n_x],
                                              *scratch[5 if decay else 3:])
            first = (pl.program_id(0) == 0) & (p == 0)
            last = (pl.program_id(0) == h // hb - 1) & (p == n_steps - 1)
            pl.when(first)(x_start)

        @pl.when(p == 0)
        def _():
            dq_sc[...] = jnp.zeros_like(dq_sc)
            if decay:
                rs_ref[...] = jnp.zeros_like(rs_ref)

        @pl.when(i == j)
        def _():
            dk_sc[...] = jnp.zeros_like(dk_sc)
            dv_sc[...] = jnp.zeros_like(dv_sc)
            if decay:
                ks_sc[...] = jnp.zeros_like(ks_sc)
                for n in range(hb):
                    b_sc[n] = jnp.transpose(b_ref[n])

        def step(masked):
            def front(n):
                return _nt_dot(k_ref[n], q_ref[n]), _nt_dot(v_ref[n], (do_ref[n] * LN2).astype(BF16))

            def back(n, s_dp):
                s, dp = s_dp
                if decay:
                    s = s + b_sc[n]
                if masked:
                    kp = j * tile + lax.broadcasted_iota(jnp.int32, s.shape, 0)
                    qp = i * tile + lax.broadcasted_iota(jnp.int32, s.shape, 1)
                    s = jnp.where((kp <= qp) & (kp >= PAD), s, NEG)
                pr = jnp.exp2(s - lse_ref[n])
                ds = pr * (dp - delta_ref[n])
                ds_b = ds.astype(BF16)
                dv_sc[n] += jnp.dot(pr.astype(BF16), do_ref[n].astype(BF16), preferred_element_type=F32)
                dk_sc[n] += jnp.dot(ds_b, q_ref[n], preferred_element_type=F32)
                dq_sc[n, i] += _tn_dot(ds_b, k_ref[n])
                if decay:
                    rs_ref[n, i] += jnp.sum(ds, axis=0, keepdims=True)
                    ks_sc[n] += jnp.sum(ds, axis=-1, keepdims=True)

            _pipelined(hb, front, back)

        edge = (j == i) | (j == 0)
        pl.when(edge)(lambda: step(True))
        pl.when(jnp.logical_not(edge))(lambda: step(False))

        @pl.when(i == j)
        def _():
            dq_ref[...] = dq_sc[:, j]

        @pl.when(i == nb - 1)
        def _():
            dk_ref[...] = dk_sc[...]
            dv_ref[...] = dv_sc[...]
            if decay:
                for n in range(hb):
                    ks_ref[n] = jnp.transpose(ks_sc[n])

        if n_x:
            pl.when(last)(x_finish)

    rows_out = pl.BlockSpec((hb, nb, 1, tile), lambda hh, p, it, jt: (hh, 0, 0, 0))
    grid_spec = pltpu.PrefetchScalarGridSpec(
        num_scalar_prefetch=2, grid=(h // hb, n_steps),
        in_specs=[sp["q"], sp["k"], sp["v"], sp["ov"], sp["rowq"], sp["rowq"]] + ([sp["fr"]] if decay else [])
        + x_in,
        out_specs=[sp["k"], sp["k"], sp["v"]] + ([rows_out, sp["fr"]] if decay else []) + x_out,
        scratch_shapes=[pltpu.VMEM((hb, nb, tile, dk), F32), pltpu.VMEM((hb, tile, dk), F32),
                        pltpu.VMEM((hb, tile, dv), F32)] + ([pltpu.VMEM((hb, tile, 1), F32)] * 2 if decay else [])
        + (x_sems if n_x else []))
    out_shape = [jax.ShapeDtypeStruct((h, t, dk), F32), jax.ShapeDtypeStruct((h, t, dk), F32),
                 jax.ShapeDtypeStruct((h, t, dv), F32)] \
        + ([jax.ShapeDtypeStruct((h, nb, 1, tile), F32), jax.ShapeDtypeStruct((h, 1, t), F32)] if decay else []) \
        + x_shapes
    outs = pl.pallas_call(
        body, name="attn_bwd_exchange" if n_x else "attn_bwd", grid_spec=grid_spec, out_shape=out_shape,
        compiler_params=_params("arbitrary" if n_x else "parallel", "arbitrary"),
    )(it, jt, q, k, v, do, lse_row, delta_row, *((key_bias,) if decay else ()), *exchange)
    return tuple(outs[:n_out]) + ((list(outs[n_out:]),) if n_x else ())


CONV_COLS = 512


def _shift_down(g, prev, n):
    out = pltpu.roll(g, n, axis=0)
    row = lax.broadcasted_iota(jnp.int32, g.shape, 0)
    for r in range(n):
        out = jnp.where(row == r, prev[8 - n + r:8 - n + r + 1, :], out)
    return out


def _shift_up(g, nxt, n):
    tm = g.shape[0]
    out = pltpu.roll(g, tm - n, axis=0)
    row = lax.broadcasted_iota(jnp.int32, g.shape, 0)
    for r in range(n):
        out = jnp.where(row == tm - n + r, nxt[r:r + 1, :], out)
    return out


def _conv_fwd(z3, w):
    _, t, d = z3.shape
    tm = _row_tile(t)
    tc = CONV_COLS

    def body(z_ref, zp_ref, w_ref, o_ref):
        i = pl.program_id(1)
        g = z_ref[1] * z_ref[2]
        gp = jnp.where(i > 0, zp_ref[1] * zp_ref[2], 0.0)
        w_ = w_ref[...]
        y = w_[2:3] * g + w_[1:2] * _shift_down(g, gp, 1) + w_[0:1] * _shift_down(g, gp, 2)
        o_ref[...] = (z_ref[0] * y).astype(BF16)

    return pl.pallas_call(
        body, name="conv_fwd", grid=(d // tc, t // tm),
        in_specs=[pl.BlockSpec((3, tm, tc), lambda j, i: (0, i, j)),
                  pl.BlockSpec((3, 8, tc), lambda j, i: (0, jnp.maximum(i * (tm // 8) - 1, 0), j)),
                  pl.BlockSpec((3, tc), lambda j, i: (0, j))],
        out_specs=pl.BlockSpec((tm, tc), lambda j, i: (i, j)),
        out_shape=jax.ShapeDtypeStruct((t, d), BF16), compiler_params=_params("parallel", "parallel"),
    )(z3, z3, w)


def _conv_bwd(z3, w, dyb):
    _, t, d = z3.shape
    tm = _row_tile(t)
    tc = CONV_COLS
    ni = t // tm

    def body(z_ref, zp_ref, zn_ref, d_ref, dn_ref, w_ref, dz_ref, dw_ref):
        i = pl.program_id(1)
        gb, gc, u = z_ref[0], z_ref[1], z_ref[2]
        g = gc * u
        gp = jnp.where(i > 0, zp_ref[1] * zp_ref[2], 0.0)
        w_ = w_ref[...]
        g1, g2 = _shift_down(g, gp, 1), _shift_down(g, gp, 2)
        y = w_[2:3] * g + w_[1:2] * g1 + w_[0:1] * g2
        dyb_ = d_ref[...]
        dy = dyb_ * gb
        dyn = jnp.where(i < ni - 1, dn_ref[...] * zn_ref[0], 0.0)
        dg = w_[2:3] * dy + w_[1:2] * _shift_up(dy, dyn, 1) + w_[0:1] * _shift_up(dy, dyn, 2)
        dz_ref[0] = (dyb_ * y).astype(BF16)
        dz_ref[1] = (dg * u).astype(BF16)
        dz_ref[2] = (dg * gc).astype(BF16)

        @pl.when(i == 0)
        def _():
            dw_ref[...] = jnp.zeros_like(dw_ref)

        dw_ref[...] += jnp.concatenate([jnp.sum(dy * g2, axis=0, keepdims=True),
                                        jnp.sum(dy * g1, axis=0, keepdims=True),
                                        jnp.sum(dy * g, axis=0, keepdims=True)], axis=0)

    cur = pl.BlockSpec((3, tm, tc), lambda j, i: (0, i, j))
    return pl.pallas_call(
        body, name="conv_bwd", grid=(d // tc, ni),
        in_specs=[cur,
                  pl.BlockSpec((3, 8, tc), lambda j, i: (0, jnp.maximum(i * (tm // 8) - 1, 0), j)),
                  pl.BlockSpec((3, 8, tc), lambda j, i: (0, jnp.minimum((i + 1) * (tm // 8), t // 8 - 1), j)),
                  pl.BlockSpec((tm, tc), lambda j, i: (i, j)),
                  pl.BlockSpec((8, tc), lambda j, i: (jnp.minimum((i + 1) * (tm // 8), t // 8 - 1), j)),
                  pl.BlockSpec((3, tc), lambda j, i: (0, j))],
        out_specs=[cur, pl.BlockSpec((3, tc), lambda j, i: (0, j))],
        out_shape=[jax.ShapeDtypeStruct((3, t, d), BF16), jax.ShapeDtypeStruct((3, d), F32)],
        compiler_params=_params("parallel", "arbitrary"),
    )(z3, z3, z3, dyb, dyb, w)


def _loss_head(h, target):
    t, d = h.shape
    tm = BLOCK

    def body(h_ref, t_ref, dh_ref, dhb_ref, loss_ref):
        i = pl.program_id(0)

        @pl.when(i == 0)
        def _():
            loss_ref[...] = jnp.zeros_like(loss_ref)

        err = jnp.where(i > 0, h_ref[...] - t_ref[...], 0.0)
        dh = err * (1.0 / d)
        dh_ref[...] = dh
        dhb_ref[...] = dh.astype(BF16)
        loss_ref[...] += 0.5 * jnp.sum(jnp.sum(err * err, axis=-1, keepdims=True) * (1.0 / d), axis=0, keepdims=True)

    row = pl.BlockSpec((tm, d), lambda i: (i, 0))
    return pl.pallas_call(
        body, name="loss_head", grid=(t // tm,),
        in_specs=[row, pl.BlockSpec((tm, d), lambda i: (jnp.maximum(i - 1, 0), 0))],
        out_specs=[row, row, pl.BlockSpec((1, 1), lambda i: (0, 0))],
        out_shape=[jax.ShapeDtypeStruct((t, d), F32), jax.ShapeDtypeStruct((t, d), BF16),
                   jax.ShapeDtypeStruct((1, 1), F32)],
        compiler_params=_params("arbitrary"),
    )(h, target)


def _heads_major(x, width):
    t = x.shape[0]
    return jnp.transpose(x.reshape(t, HEADS, width), (1, 0, 2))


def _heads_minor(x):
    h, t, w = x.shape
    return jnp.transpose(x, (1, 0, 2)).reshape(t, h * w)


def _rope_tables(t):
    pos = jnp.arange(t, dtype=F32) - PAD
    inv_freq = ROPE_BASE ** (-jnp.arange(0, MLA_ROPE, 2, dtype=F32) / MLA_ROPE)
    ang = pos[:, None] * inv_freq[None, :]
    cos, sin = jnp.cos(ang), jnp.sin(ang)
    one, zero = jnp.ones((t, MLA_NOPE), F32), jnp.zeros((t, MLA_NOPE), F32)
    tail = jnp.zeros((t, LANES - MLA_QK), F32)
    return (jnp.concatenate([one, cos, cos, tail], axis=1), jnp.concatenate([zero, -sin, sin, tail], axis=1))


def _pad_lanes(x, width=LANES):
    return jnp.pad(x, [(0, 0)] * (x.ndim - 1) + [(0, width - x.shape[-1])])


def _permute_in_attn(w):
    return jnp.concatenate([w[:, :640], w[:, 672:2208], w[:, 640:672], w[:, 2208:2216],
                            jnp.zeros((w.shape[0], ATTN_IN_PAD - 2216), w.dtype)], axis=1)


def _unpermute_in_attn(dw):
    return jnp.concatenate([dw[:, :640], dw[:, 2176:2208], dw[:, 640:2176], dw[:, 2208:2216]], axis=1)


def _attn_layer_fwd(hn, wl, rope, gather=()):
    t = hn.shape[0]
    z = _mm(hn, wl["w_in"], name="attn_in")
    cqn = _rms_fwd(z[:, C_CQ:C_CQ + Q_LORA], wl["g_cq"])
    ckvn = _rms_fwd(z[:, C_CKV:C_CKV + KV_LORA], wl["g_ckv"])
    qf = _mm(cqn, wl["w_uq"], name="mla_uq")
    kvf = _mm(ckvn, wl["w_ukv"], name="mla_ukv")
    kv3 = kvf.reshape(t, HEADS, MLA_NOPE + MLA_V)
    xq = _pad_lanes(_heads_major(qf, MLA_QK))
    k_pe = jnp.broadcast_to(z[None, :, C_TAIL:C_TAIL + MLA_ROPE], (HEADS, t, MLA_ROPE))
    xk = _pad_lanes(jnp.concatenate([jnp.transpose(kv3[:, :, :MLA_NOPE], (1, 0, 2)), k_pe], axis=-1))
    v_mla = jnp.transpose(kv3[:, :, MLA_NOPE:], (1, 0, 2)).astype(BF16)
    gq, gk = _pad_lanes(wl["g_q_mla"].reshape(1, -1)), _pad_lanes(wl["g_k_mla"].reshape(1, -1))
    q_mla = _head_norm_fwd(xq, gq, MLA_QK, MLA_SCALE, rope)
    k_mla = _head_norm_fwd(xk, gk, MLA_QK, 1.0, rope)
    o_mla, lse_mla, *gathered = _attn_fwd(q_mla, k_mla, v_mla, gather=gather)
    xfq = _heads_major(z[:, C_FQ:C_FQ + HEADS * FOX_DIM], FOX_DIM)
    xfk = _heads_major(z[:, C_FK:C_FK + HEADS * FOX_DIM], FOX_DIM)
    v_fox = _heads_major(z[:, C_FV:C_FV + HEADS * FOX_DIM], FOX_DIM).astype(BF16)
    bias = jnp.pad(wl["b_forget"].reshape(1, -1), ((0, 0), (TAIL_F, LANES - TAIL_F - HEADS)))
    cum = _gate_fwd(z, bias)
    neg_f = (-LOG2E * jnp.transpose(cum[:, TAIL_F:TAIL_F + HEADS]))[:, None, :]
    q_fox = _head_norm_fwd(xfq, wl["g_q_fox"].reshape(1, -1), FOX_DIM, FOX_SCALE)
    k_fox = _head_norm_fwd(xfk, wl["g_k_fox"].reshape(1, -1), FOX_DIM, 1.0)
    o_fox, lse_fox = _attn_fwd(q_fox, k_fox, v_fox, neg_f)
    cat = jnp.concatenate([_heads_minor(o_mla), _heads_minor(o_fox)], axis=1).astype(BF16)
    saved = dict(z=z, cqn=cqn, ckvn=ckvn, xq=xq, xk=xk, v_mla=v_mla, q_mla=q_mla, k_mla=k_mla, o_mla=o_mla,
                 lse_mla=lse_mla, xfq=xfq, xfk=xfk, v_fox=v_fox, q_fox=q_fox, k_fox=k_fox, bias=bias,
                 neg_f=neg_f, o_fox=o_fox, lse_fox=lse_fox, cat=cat, gq=gq, gk=gk)
    return cat, saved, (gathered[0] if gathered else ())


def _attn_layer_bwd(dcat, hn, wl, sv, rope, exchange=()):
    t = hn.shape[0]
    g = {}
    do_mla = _heads_major(dcat[:, :HEADS * MLA_V], MLA_V)
    do_fox = _heads_major(dcat[:, HEADS * MLA_V:], FOX_DIM)
    qkv = (sv["q_fox"], sv["k_fox"], sv["v_fox"])
    delta = _attn_delta(sv["o_fox"], do_fox)
    dq_fox, dk_fox, dv_fox, row_sums, key_sums, *exchanged = _attn_bwd(
        *qkv, do_fox, sv["lse_fox"], delta, sv["neg_f"], exchange=exchange)
    dcum = jnp.pad(jnp.transpose(LOG2E * (row_sums.reshape(HEADS, t) - key_sums.reshape(HEADS, t))),
                   ((0, 0), (TAIL_F, LANES - TAIL_F - HEADS)))
    dtail_f, dbias = _gate_bwd(sv["z"], sv["bias"], dcum)
    g["b_forget"] = dbias[0, TAIL_F:TAIL_F + HEADS]
    dxfq, _, dgq = _head_norm_bwd(sv["xfq"], wl["g_q_fox"].reshape(1, -1), dq_fox, FOX_DIM, FOX_SCALE)
    dxfk, _, dgk = _head_norm_bwd(sv["xfk"], wl["g_k_fox"].reshape(1, -1), dk_fox, FOX_DIM, 1.0)
    g["g_q_fox"], g["g_k_fox"] = dgq[0], dgk[0]
    qkv = (sv["q_mla"], sv["k_mla"], sv["v_mla"])
    delta = _attn_delta(sv["o_mla"], do_mla)
    dq_mla, dk_mla, dv_mla = _attn_bwd(*qkv, do_mla, sv["lse_mla"], delta)
    dxq, _, dgq = _head_norm_bwd(sv["xq"], sv["gq"], dq_mla, MLA_QK, MLA_SCALE, rope)
    dxk, dxk_sum, dgk = _head_norm_bwd(sv["xk"], sv["gk"], dk_mla, MLA_QK, 1.0, rope)
    g["g_q_mla"], g["g_k_mla"] = dgq[0, :MLA_QK], dgk[0, :MLA_QK]
    dqf = _heads_minor(dxq[:, :, :MLA_QK]).astype(BF16)
    dkvf = _heads_minor(jnp.concatenate([dxk[:, :, :MLA_NOPE], dv_mla], axis=-1)).astype(BF16)
    g["w_uq"] = _mm(sv["cqn"], dqf, ta=True, name="d_w_uq")
    g["w_ukv"] = _mm(sv["ckvn"], dkvf, ta=True, name="d_w_ukv")
    dcqn = _mm(dqf, wl["w_uq"], tb=True, name="d_cqn")
    dckvn = _mm(dkvf, wl["w_ukv"], tb=True, name="d_ckvn")
    z = sv["z"]
    dcq, dg_cq = _rms_bwd(z[:, C_CQ:C_CQ + Q_LORA], wl["g_cq"], dcqn)
    dckv, dg_ckv = _rms_bwd(z[:, C_CKV:C_CKV + KV_LORA], wl["g_ckv"], dckvn)
    g["g_cq"], g["g_ckv"] = dg_cq[0], dg_ckv[0]
    tail = jnp.concatenate([dxk_sum[:, MLA_NOPE:MLA_QK], dtail_f[:, TAIL_F:]], axis=1)
    dz = jnp.concatenate([dcq, dckv, _heads_minor(dxfq), _heads_minor(dxfk), _heads_minor(dv_fox), tail],
                         axis=1).astype(BF16)
    g["w_in"] = _mm(hn, dz, ta=True, name="d_w_in_attn")
    dhn = _mm(dz, wl["w_in"], tb=True, name="d_hn_attn")
    return dhn, g, (exchanged[0] if exchanged else ())


def _local_step(x, target, w, gather_late=None, exchange_early=None):
    seq = x.shape[0]
    t = seq + BLOCK
    rope = _rope_tables(t)
    h = jnp.concatenate([jnp.zeros((PAD, D_MODEL), F32), w["meta_tokens"], x], axis=0)
    tape = []
    for layer in range(DEPTH):
        j = layer // 2
        hn = _rms_fwd(h, w["g_mix"][layer])
        if layer % 2 == 0:
            wl = dict(w_in=w["w_in_attn"][j], g_cq=w["g_cq"][j], w_uq=w["w_uq"][j], g_ckv=w["g_ckv"][j],
                      w_ukv=w["w_ukv"][j], g_q_mla=w["g_q_mla"][j], g_k_mla=w["g_k_mla"][j],
                      g_q_fox=w["g_q_fox"][j], g_k_fox=w["g_k_fox"][j], b_forget=w["b_forget"][j])
            hosted = gather_late is not None and layer == 0
            mixed, sv, gathered = _attn_layer_fwd(hn, wl, rope, gather_late[0] if hosted else ())
            if hosted:
                gather_late[1](w, gathered)
            h1 = _mm(mixed, w["w_out_attn"][j], res=h, name="attn_out")
        else:
            wl = None
            z3 = _mm(hn, w["w_in_conv"][j], out_seg=3, name="conv_in")
            mixed = _conv_fwd(z3, w["conv_w"][j])
            sv = dict(z3=z3)
            h1 = _mm(mixed, w["w_out_conv"][j], res=h, name="conv_out")
        hn2 = _rms_fwd(h1, w["g_mlp"][layer])
        u, act = _mm(hn2, w["w_mlp_up"][layer], epi="relu2", name="mlp_up")
        h2 = _mm(act, w["w_mlp_down"][layer], res=h1, name="mlp_down")
        tape.append(dict(h=h, hn=hn, wl=wl, sv=sv, mixed=mixed, h1=h1, hn2=hn2, u=u, act=act))
        h = h2

    dh, dh_b, loss = _loss_head(h, target)
    exchanged = ()
    g = {n: [None] * (DEPTH if n in ("g_mix", "g_mlp", "w_mlp_up", "w_mlp_down") else DEPTH // 2)
         for n in WEIGHTS if n != "meta_tokens"}
    for layer in reversed(range(DEPTH)):
        j = layer // 2
        tp = tape[layer]
        g["w_mlp_down"][layer] = _mm(tp["act"], dh_b, ta=True, name="d_w_down")
        du = _mm(dh_b, w["w_mlp_down"][layer], tb=True, epi="relu2_bwd", aux=tp["u"], out_dtype=BF16, name="d_u")
        g["w_mlp_up"][layer] = _mm(tp["hn2"], du, ta=True, name="d_w_up")
        dhn2 = _mm(du, w["w_mlp_up"][layer], tb=True, name="d_hn2")
        dh1, dh1_b, dg = _rms_bwd(tp["h1"], w["g_mlp"][layer], dhn2, dres=dh, want_bf16=True)
        g["g_mlp"][layer] = dg[0]
        if layer % 2 == 0:
            g["w_out_attn"][j] = _mm(tp["mixed"], dh1_b, ta=True, name="d_w_out_attn")
            dcat = _mm(dh1_b, w["w_out_attn"][j], tb=True, name="d_cat")
            hosted = exchange_early is not None and layer == 0
            dhn, gl, got = _attn_layer_bwd(dcat, tp["hn"], tp["wl"], tp["sv"], rope,
                                           exchange_early(g) if hosted else ())
            if hosted:
                exchanged = got
            g["w_in_attn"][j] = _unpermute_in_attn(gl.pop("w_in"))
            for n, val in gl.items():
                g[n][j] = val
        else:
            g["w_out_conv"][j] = _mm(tp["mixed"], dh1_b, ta=True, name="d_w_out_conv")
            dyb = _mm(dh1_b, w["w_out_conv"][j], tb=True, name="d_yb")
            dz3, dcw = _conv_bwd(tp["sv"]["z3"], w["conv_w"][j], dyb)
            g["conv_w"][j] = dcw
            g["w_in_conv"][j] = _mm(tp["hn"], dz3, ta=True, name="d_w_in_conv")
            dhn = _mm(dz3, w["w_in_conv"][j], tb=True, name="d_hn_conv")
        dh, dh_b, dg = _rms_bwd(tp["h"], w["g_mix"][layer], dhn, dres=dh1, want_bf16=True)
        g["g_mix"][layer] = dg[0]
    g["meta_tokens"] = [dh[PAD:BLOCK]]
    return loss, dh[BLOCK:], g, exchanged


COMM_ROWS = 2048
ADAMW_BLOCK_BYTES = 1 << 20


def kernel(x, meta_tokens, g_mix, g_mlp, w_in_attn, g_cq, w_uq, g_ckv, w_ukv, g_q_mla, g_k_mla, g_q_fox, g_k_fox, b_forget, w_out_attn, w_in_conv, conv_w, w_out_conv, w_mlp_up, w_mlp_down, loss_target, m_meta_tokens, m_g_mix, m_g_mlp, m_w_in_attn, m_g_cq, m_w_uq, m_g_ckv, m_w_ukv, m_g_q_mla, m_g_k_mla, m_g_q_fox, m_g_k_fox, m_b_forget, m_w_out_attn, m_w_in_conv, m_conv_w, m_w_out_conv, m_w_mlp_up, m_w_mlp_down, v_meta_tokens, v_g_mix, v_g_mlp, v_w_in_attn, v_g_cq, v_w_uq, v_g_ckv, v_w_ukv, v_g_q_mla, v_g_k_mla, v_g_q_fox, v_g_k_fox, v_b_forget, v_w_out_attn, v_w_in_conv, v_conv_w, v_w_out_conv, v_w_mlp_up, v_w_mlp_down):
    args = dict(locals())
    local = {n: args[n] for n in WEIGHTS}
    mom = {n: args["m_" + n] for n in WEIGHTS}
    var = {n: args["v_" + n] for n in WEIGHTS}
    axis = dict(SHARDED)
    count = {n: local[n].shape[0] for n, _ in SHARDED if n != "meta_tokens"}
    piece = lambda src, p: src[p[0]] if p[1] is None else src[p[0]][p[1]]
    piece_axis = lambda p: axis[p[0]] - (0 if p[1] is None else 1)
    shape_of = lambda p: piece(local, p).shape
    layers = lambda n, ls: [(n, l) for l in ls]
    first_bf = [("w_uq", 0), ("w_ukv", 0), ("w_out_attn", 0), ("w_mlp_up", 0), ("w_mlp_down", 0)]
    first_f32 = [("meta_tokens", None), ("conv_w", 0), ("conv_w", 1)]
    late_bf = ([("w_uq", 1), ("w_ukv", 1), ("w_out_attn", 1)] + layers("w_in_conv", (0, 1))
               + layers("w_out_conv", (0, 1)) + layers("w_mlp_up", (1, 2, 3)) + layers("w_mlp_down", (1, 2, 3)))
    early_bf = ([("w_uq", 1), ("w_ukv", 1)] + layers("w_out_attn", (0, 1)) + layers("w_in_conv", (0, 1))
                + layers("conv_w", (0, 1)) + layers("w_out_conv", (0, 1)) + layers("w_mlp_up", range(DEPTH))
                + layers("w_mlp_down", range(DEPTH)))
    last_bf = [("w_uq", 0), ("w_ukv", 0), ("meta_tokens", None)]

    w = {n: local[n] for n in REPLICATED}
    w.update({n: [None] * c for n, c in count.items()})

    def install(w, pieces, gathered, in_layer, gathered_in):
        for p, blocks in zip(pieces, _unpack(gathered, [shape_of(p) for p in pieces], (N_DEV,))):
            value = _from_shards(blocks, piece_axis(p))
            if p[1] is None:
                w[p[0]] = value
            else:
                w[p[0]][p[1]] = value
        w["w_in_attn"][in_layer] = _permute_in_attn(_from_shards(gathered_in, piece_axis(("w_in_attn", 0))))

    got_bf, got_in, got_f32 = _all_gather([
        _pack([piece(local, p) for p in first_bf], 16, BF16), local["w_in_attn"][0].astype(BF16),
        _pack([piece(local, p) for p in first_f32], 8, F32)])
    install(w, first_bf, got_bf, 0, got_in)
    for p, blocks in zip(first_f32, _unpack(got_f32, [shape_of(p) for p in first_f32], (N_DEV,))):
        if p[1] is None:
            w[p[0]] = _from_shards(blocks, piece_axis(p))
        else:
            w[p[0]][p[1]] = _from_shards(blocks, piece_axis(p))
    gather_late = ([_pack([piece(local, p) for p in late_bf], 16, BF16), local["w_in_attn"][1].astype(BF16)],
                   lambda w_, got: install(w_, late_bf, got[0], 1, got[1]))

    def pack_grads(g, pieces, in_layer):
        sent = _pack_rows([_to_shards(piece(g, p), piece_axis(p)) for p in pieces], COMM_ROWS, BF16)
        return [sent, _to_shards(g["w_in_attn"][in_layer], piece_axis(("w_in_attn", 0))).astype(BF16)]

    loss_part, dx, grads, early = _local_step(x[0], loss_target[0], w, gather_late,
                                              lambda g: pack_grads(g, early_bf, 1))
    loss = lax.psum(loss_part[0, 0], ("x", "y", "c"))

    grads["meta_tokens"] = grads["meta_tokens"][0]
    last = _all_to_all(pack_grads(grads, last_bf, 0))
    g_piece = {}
    for pieces, (got, got_in), in_layer in ((early_bf, early, 1), (last_bf, last, 0)):
        summed = _unpack(_sum_blocks(got, COMM_ROWS), [shape_of(p) for p in pieces])
        g_piece.update(zip(pieces, summed))
        g_piece[("w_in_attn", in_layer)] = _sum_blocks(got_in, 256)
    g_local = {n: jnp.stack([g_piece[(n, l)] for l in range(c)]) for n, c in count.items()}
    g_local["meta_tokens"] = g_piece[("meta_tokens", None)]
    rep, = _all_gather([_pack([jnp.stack(grads[n]) for n in REPLICATED], 8, F32)])
    g_rep = _sum_blocks(rep, rep.shape[1])
    g_local.update(zip(REPLICATED, _unpack(g_rep, [local[n].shape for n in REPLICATED])))

    def flat(src, names, rows_multiple):
        return _pack([src[n] for n in names], rows_multiple, F32)

    upd = {}
    rows = g_rep.shape[0]
    outs = _adamw(flat(local, REPLICATED, rows), g_rep, flat(mom, REPLICATED, rows), flat(var, REPLICATED, rows), rows)
    for kind, buf in zip(("delta", "m", "v"), outs):
        upd.update({(kind, n): a for n, a in zip(REPLICATED, _unpack(buf, [local[n].shape for n in REPLICATED]))})
    for n, _ in SHARDED:
        as_rows = lambda a: a.reshape(-1, a.shape[-1])
        n_rows, n_cols = as_rows(local[n]).shape
        tiles = [r for r in (1024, 512, 256, 128, 64, 32, 16, 8) if r * n_cols * 4 <= ADAMW_BLOCK_BYTES]
        outs = _adamw(as_rows(local[n]), as_rows(g_local[n]), as_rows(mom[n]), as_rows(var[n]), _pick(n_rows, tiles))
        for kind, buf in zip(("delta", "m", "v"), outs):
            upd[(kind, n)] = buf.reshape(local[n].shape)

    return (loss, dx[None], *[g_local[n] for n in WEIGHTS], *[upd[("delta", n)] for n in WEIGHTS],
            *[upd[("m", n)] for n in WEIGHTS], *[upd[("v", n)] for n in WEIGHTS])
```

```python
import functools
import math

import jax
import jax.numpy as jnp
import numpy as np
from jax import lax
from jax.experimental import pallas as pl
from jax.experimental.pallas import tpu as pltpu

F32 = jnp.float32
BF16 = jnp.bfloat16

N_DEV = 8
D_MODEL = 1024
DEPTH = 4
N_META = 16
BLOCK = 128
PAD = BLOCK - N_META
HEADS = 8
MLA_NOPE = 64
MLA_ROPE = 32
MLA_QK = MLA_NOPE + MLA_ROPE
MLA_V = 64
Q_LORA = 384
KV_LORA = 256
ROPE_BASE = 10000.0
FOX_DIM = 64
D_FF = 4 * D_MODEL
EPS = 1e-6
NEG = -1e30
LANES = 128
ATTN_IN_PAD = 2304
C_CQ, C_CKV, C_FQ, C_FK, C_FV, C_TAIL = 0, 384, 640, 1152, 1664, 2176
TAIL_F = MLA_ROPE

ADAM_LR = 0.001
ADAM_B1 = 0.9
ADAM_B2 = 0.999
ADAM_EPS = 1e-08
ADAM_WD = 0.01
ADAM_STEP = 10

VMEM_LIMIT_BYTES = 48 * 1024 * 1024
MESH = pl.DeviceIdType.MESH

SHARDED = (
    ("meta_tokens", 1), ("w_in_attn", 2), ("w_uq", 2), ("w_ukv", 2), ("w_out_attn", 1),
    ("w_in_conv", 2), ("conv_w", 2), ("w_out_conv", 1), ("w_mlp_up", 2), ("w_mlp_down", 1))
F32_GATHERED = ("meta_tokens", "conv_w")
REPLICATED = ("g_mix", "g_mlp", "g_cq", "g_ckv", "g_q_mla", "g_k_mla", "g_q_fox", "g_k_fox", "b_forget")
WEIGHTS = ("meta_tokens", "g_mix", "g_mlp", "w_in_attn", "g_cq", "w_uq", "g_ckv", "w_ukv", "g_q_mla",
           "g_k_mla", "g_q_fox", "g_k_fox", "b_forget", "w_out_attn", "w_in_conv", "conv_w",
           "w_out_conv", "w_mlp_up", "w_mlp_down")


def _params(*sem):
    return pltpu.CompilerParams(dimension_semantics=sem, vmem_limit_bytes=VMEM_LIMIT_BYTES)


def _row_tile(t):
    return 640 if (t % 640 == 0 and t > 640) else 128


def _pack(parts, rows_multiple, dtype):
    flat = jnp.concatenate([p.reshape(-1).astype(dtype) for p in parts])
    n = flat.shape[0]
    rows = -(-n // LANES)
    rows = -(-rows // rows_multiple) * rows_multiple
    return jnp.pad(flat, (0, rows * LANES - n)).reshape(rows, LANES)


def _pack_rows(parts, rows_multiple, dtype):
    flat = jnp.concatenate([p.reshape(N_DEV, -1).astype(dtype) for p in parts], axis=1)
    n = flat.shape[1]
    rows = -(-n // LANES)
    rows = -(-rows // rows_multiple) * rows_multiple
    return jnp.pad(flat, ((0, 0), (0, rows * LANES - n))).reshape(N_DEV, rows, LANES)


def _unpack(buf, shapes, lead=()):
    flat = buf.reshape(lead + (-1,))
    out, off = [], 0
    for s in shapes:
        n = math.prod(s)
        out.append(flat[..., off:off + n].reshape(lead + tuple(s)))
        off += n
    return out


def _to_shards(full, axis):
    s = full.shape
    return jnp.moveaxis(full.reshape(s[:axis] + (N_DEV, s[axis] // N_DEV) + s[axis + 1:]), axis, 0)


def _from_shards(g8, axis):
    m = jnp.moveaxis(g8, 0, axis)
    s = m.shape
    return m.reshape(s[:axis] + (s[axis] * s[axis + 1],) + s[axis + 2:])


def _comm_call(body, name, xs, out_shapes):
    n = len(xs)
    hbm = pl.BlockSpec(memory_space=pltpu.HBM)
    return pl.pallas_call(
        body, name=name, out_shape=out_shapes, in_specs=[hbm] * n, out_specs=[hbm] * n,
        scratch_shapes=[pltpu.SemaphoreType.DMA((n, 7)), pltpu.SemaphoreType.DMA((n, 7)),
                        pltpu.SemaphoreType.DMA((n,))],
    )(*xs)


def _gather_ops(x_refs, out_refs, send_sems, recv_sems, local_sems):
    n = len(x_refs)
    x_, y_, c = lax.axis_index("x"), lax.axis_index("y"), lax.axis_index("c")
    me, sibling = (x_, y_, c), (x_, y_, 1 - c)
    chips = [(1 - x_, y_), (x_, 1 - y_), (1 - x_, 1 - y_)]

    def rows(a, px, py, pc):
        return out_refs[a].at[4 * px + 2 * py + pc]

    def copy(a, k, block, to, src=None):
        return pltpu.make_async_remote_copy(
            src_ref=rows(a, *block) if src is None else src, dst_ref=rows(a, *block),
            send_sem=send_sems.at[a, k], recv_sem=recv_sems.at[a, k], device_id=to, device_id_type=MESH)

    def mine():
        return [pltpu.make_async_copy(x_refs[a], rows(a, *me), local_sems.at[a]) for a in range(n)]

    def first():
        cps = []
        for a in range(n):
            cps.append(copy(a, 0, me, sibling, src=x_refs[a]))
            cps += [copy(a, 1 + j, me, (*chip, c), src=x_refs[a]) for j, chip in enumerate(chips)]
        return cps

    def start():
        for cp in mine() + first():
            cp.start()

    def finish():
        passed = []
        for j, chip in enumerate(chips):
            for a in range(n):
                copy(a, 1 + j, (*chip, c), me).wait_recv()
                passed.append(copy(a, 4 + j, (*chip, c), sibling))
                passed[-1].start()
        for a in range(n):
            copy(a, 0, sibling, me).wait_recv()
            for j, chip in enumerate(chips):
                copy(a, 4 + j, (*chip, 1 - c), me).wait_recv()
        for cp in first() + passed:
            cp.wait_send()
        for cp in mine():
            cp.wait()

    return start, finish


def _exchange_ops(x_refs, out_refs, send_sems, recv_sems, local_sems):
    n = len(x_refs)
    x_, y_, c = lax.axis_index("x"), lax.axis_index("y"), lax.axis_index("c")
    me = 4 * x_ + 2 * y_ + c

    def peer(k):
        px = 1 - x_ if k & 4 else x_
        py = 1 - y_ if k & 2 else y_
        pc = 1 - c if k & 1 else c
        return px, py, pc

    def copy(a, k):
        px, py, pc = peer(k)
        return pltpu.make_async_remote_copy(
            src_ref=x_refs[a].at[4 * px + 2 * py + pc], dst_ref=out_refs[a].at[me],
            send_sem=send_sems.at[a, k - 1], recv_sem=recv_sems.at[a, k - 1], device_id=(px, py, pc),
            device_id_type=MESH)

    def arrival(a, k):
        px, py, pc = peer(k)
        slot = 4 * px + 2 * py + pc
        return pltpu.make_async_remote_copy(
            src_ref=x_refs[a].at[slot], dst_ref=out_refs[a].at[slot],
            send_sem=send_sems.at[a, k - 1], recv_sem=recv_sems.at[a, k - 1], device_id=(px, py, pc),
            device_id_type=MESH)

    def mine():
        return [pltpu.make_async_copy(x_refs[a].at[me], out_refs[a].at[me], local_sems.at[a]) for a in range(n)]

    def sends():
        return [copy(a, k) for k in range(1, N_DEV) for a in range(n)]

    def start():
        for cp in mine() + sends():
            cp.start()

    def finish():
        for k in range(1, N_DEV):
            for a in range(n):
                arrival(a, k).wait_recv()
        for cp in sends():
            cp.wait_send()
        for cp in mine():
            cp.wait()

    return start, finish


def _comm_parts(xs, gather):
    n = len(xs)
    hbm = pl.BlockSpec(memory_space=pltpu.HBM)
    shapes = [jax.ShapeDtypeStruct(((N_DEV,) + x.shape) if gather else x.shape, x.dtype) for x in xs]
    sems = [pltpu.SemaphoreType.DMA((n, 7)), pltpu.SemaphoreType.DMA((n, 7)), pltpu.SemaphoreType.DMA((n,))]
    return [hbm] * n, [hbm] * n, shapes, sems


def _all_gather(xs):
    n = len(xs)

    def body(*refs):
        start, finish = _gather_ops(refs[:n], refs[n:2 * n], *refs[2 * n:])
        start()
        finish()

    return _comm_call(body, "all_gather", xs, [jax.ShapeDtypeStruct((N_DEV,) + x.shape, x.dtype) for x in xs])


def _all_to_all(xs):
    n = len(xs)

    def body(*refs):
        start, finish = _exchange_ops(refs[:n], refs[n:2 * n], *refs[2 * n:])
        start()
        finish()

    return _comm_call(body, "all_to_all", xs, [jax.ShapeDtypeStruct(x.shape, x.dtype) for x in xs])


def _sum_blocks(x, rows_tile):
    _, r, c_ = x.shape

    def body(x_ref, o_ref):
        acc = x_ref[0].astype(F32)
        for d in range(1, N_DEV):
            acc = acc + x_ref[d].astype(F32)
        o_ref[...] = acc

    return pl.pallas_call(
        body, name="sum_blocks", grid=(r // rows_tile,),
        in_specs=[pl.BlockSpec((N_DEV, rows_tile, c_), lambda i: (0, i, 0))],
        out_specs=pl.BlockSpec((rows_tile, c_), lambda i: (i, 0)),
        out_shape=jax.ShapeDtypeStruct((r, c_), F32),
        compiler_params=_params("parallel"),
    )(x)


def _adamw(w, g, m, v, rows_tile):
    r, c_ = w.shape
    c1 = 1.0 - ADAM_B1 ** ADAM_STEP
    c2 = 1.0 - ADAM_B2 ** ADAM_STEP

    def body(w_ref, g_ref, m_ref, v_ref, d_ref, mo_ref, vo_ref):
        g_ = g_ref[...]
        m_ = ADAM_B1 * m_ref[...] + (1.0 - ADAM_B1) * g_
        v_ = ADAM_B2 * v_ref[...] + (1.0 - ADAM_B2) * (g_ * g_)
        m_hat = m_ / c1
        v_hat = v_ / c2
        d_ref[...] = -ADAM_LR * (m_hat / (jnp.sqrt(v_hat) + ADAM_EPS) + ADAM_WD * w_ref[...])
        mo_ref[...] = m_
        vo_ref[...] = v_

    spec = pl.BlockSpec((rows_tile, c_), lambda i: (i, 0))
    shape = jax.ShapeDtypeStruct((r, c_), F32)
    return pl.pallas_call(
        body, name="adamw", grid=(r // rows_tile,), in_specs=[spec] * 4, out_specs=[spec] * 3,
        out_shape=[shape] * 3, compiler_params=_params("parallel"),
    )(w, g, m, v)


def _pick(n, prefs):
    for p in prefs:
        if n % p == 0:
            return p
    return n


def _mat_spec(arr, tr, tc, r_of, c_of):
    if arr.ndim == 2:
        return pl.BlockSpec((tr, tc), lambda i, j, k: (r_of(i, j, k), c_of(i, j, k)))
    per = arr.shape[2] // tc
    return pl.BlockSpec((None, tr, tc), lambda i, j, k: (c_of(i, j, k) // per, r_of(i, j, k), c_of(i, j, k) % per))


def _mm(a, b, *, ta=False, tb=False, out_dtype=F32, out_seg=None, res=None, epi=None, aux=None,
        tm=None, tn=None, tk=None, name="mm"):
    def dims(x):
        return (x.shape[0], x.shape[1]) if x.ndim == 2 else (x.shape[1], x.shape[0] * x.shape[2])
    ar, ac = dims(a)
    br, bc = dims(b)
    m, k = (ac, ar) if ta else (ar, ac)
    n, kb = (br, bc) if tb else (bc, br)
    assert k == kb, (a.shape, b.shape, ta, tb)
    tm = tm or _pick(m, (1024, 512, 384, 256, 128) if ta else (640, 512, 384, 256, 128))
    tn = tn or _pick(n, (1024, 768, 512, 384, 256, 128))
    tk = tk or _pick(k, ((1664,) if ta else (2048,)) + (1024, 768, 640, 512, 384, 256, 128))
    if out_seg:
        assert (n // out_seg) % tn == 0
    for x, t in ((a, tm if ta else tk), (b, tk if tb else tn)):
        if x.ndim == 3:
            assert x.shape[2] % t == 0
    nk = k // tk
    gi, gj, gk = (lambda j, i, kk: i), (lambda j, i, kk: j), (lambda j, i, kk: kk)
    a_spec = _mat_spec(a, tk, tm, gk, gi) if ta else _mat_spec(a, tm, tk, gi, gk)
    b_spec = _mat_spec(b, tn, tk, gj, gk) if tb else _mat_spec(b, tk, tn, gk, gj)
    out_like = jnp.zeros((out_seg, 0, n // out_seg)) if out_seg else jnp.zeros((0, n))
    o_spec = _mat_spec(out_like, tm, tn, gi, gj)
    o_shape = (out_seg, m, n // out_seg) if out_seg else (m, n)
    dn = (((0 if ta else 1,), (1 if tb else 0,)), ((), ()))
    extra = [x for x in (res, aux) if x is not None]
    assert not (res is not None and aux is not None)
    n_out = 2 if epi == "relu2" else 1

    def body(*refs):
        a_ref, b_ref = refs[0], refs[1]
        x_ref = refs[2] if extra else None
        outs = refs[2 + len(extra):2 + len(extra) + n_out]
        acc_ref = refs[-1] if nk > 1 else None
        part = lax.dot_general(a_ref[...], b_ref[...], dn, preferred_element_type=F32)

        def finish(acc):
            if epi == "relu2":
                outs[0][...] = acc
                r = jnp.maximum(acc, 0.0)
                outs[1][...] = (r * r).astype(BF16)
            elif epi == "relu2_bwd":
                outs[0][...] = (acc * (2.0 * jnp.maximum(x_ref[...], 0.0))).astype(out_dtype)
            elif res is not None:
                outs[0][...] = (acc + x_ref[...]).astype(out_dtype)
            else:
                outs[0][...] = acc.astype(out_dtype)

        if nk == 1:
            finish(part)
        else:
            kk = pl.program_id(2)

            @pl.when(kk == 0)
            def _():
                acc_ref[...] = part

            @pl.when(kk > 0)
            def _():
                acc_ref[...] += part

            @pl.when(kk == nk - 1)
            def _():
                finish(acc_ref[...])

    if epi == "relu2":
        out_shape = [jax.ShapeDtypeStruct(o_shape, F32), jax.ShapeDtypeStruct(o_shape, BF16)]
        out_specs = [o_spec, o_spec]
    else:
        out_shape = jax.ShapeDtypeStruct(o_shape, out_dtype)
        out_specs = o_spec
    x_specs = [pl.BlockSpec((tm, tn), lambda j, i, kk: (i, j))] * len(extra)
    res_ = pl.pallas_call(
        body, name=name, grid=(n // tn, m // tm, nk),
        in_specs=[a_spec, b_spec] + x_specs, out_specs=out_specs, out_shape=out_shape,
        scratch_shapes=[pltpu.VMEM((tm, tn), F32)] if nk > 1 else [],
        compiler_params=_params("parallel", "parallel", "arbitrary"),
    )(a, b, *extra)
    return res_


def _rms_fwd(x, g):
    t, d = x.shape
    tm = _row_tile(t)

    def body(x_ref, g_ref, o_ref):
        x_ = x_ref[...]
        rstd = lax.rsqrt(jnp.mean(x_ * x_, axis=-1, keepdims=True) + EPS)
        o_ref[...] = (x_ * rstd * g_ref[...]).astype(BF16)

    return pl.pallas_call(
        body, name="rms_fwd", grid=(t // tm,),
        in_specs=[pl.BlockSpec((tm, d), lambda i: (i, 0)), pl.BlockSpec((1, d), lambda i: (0, 0))],
        out_specs=pl.BlockSpec((tm, d), lambda i: (i, 0)),
        out_shape=jax.ShapeDtypeStruct((t, d), BF16), compiler_params=_params("parallel"),
    )(x, g.reshape(1, d))


def _rms_bwd(x, g, dy, dres=None, want_bf16=False):
    t, d = x.shape
    tm = _row_tile(t)
    has_res = dres is not None

    def body(*refs):
        x_ref, g_ref, dy_ref = refs[:3]
        r_ref = refs[3] if has_res else None
        outs = refs[3 + has_res:]
        x_ = x_ref[...]
        rstd = lax.rsqrt(jnp.mean(x_ * x_, axis=-1, keepdims=True) + EPS)
        xh = x_ * rstd
        dy_ = dy_ref[...]
        dxh = dy_ * g_ref[...]
        dx = rstd * (dxh - xh * jnp.mean(dxh * xh, axis=-1, keepdims=True))
        if has_res:
            dx = dx + r_ref[...]
        outs[0][...] = dx
        if want_bf16:
            outs[1][...] = dx.astype(BF16)
        dg_ref = outs[-1]

        @pl.when(pl.program_id(0) == 0)
        def _():
            dg_ref[...] = jnp.zeros_like(dg_ref)

        dg_ref[...] += jnp.sum(dy_ * xh, axis=0, keepdims=True)

    row = pl.BlockSpec((tm, d), lambda i: (i, 0))
    vec = pl.BlockSpec((1, d), lambda i: (0, 0))
    out_shape = [jax.ShapeDtypeStruct((t, d), F32)] + ([jax.ShapeDtypeStruct((t, d), BF16)] if want_bf16 else []) \
        + [jax.ShapeDtypeStruct((1, d), F32)]
    out_specs = [row] + ([row] if want_bf16 else []) + [vec]
    return pl.pallas_call(
        body, name="rms_bwd", grid=(t // tm,),
        in_specs=[row, vec, row] + ([row] if has_res else []), out_specs=out_specs, out_shape=out_shape,
        compiler_params=_params("arbitrary"),
    )(x, g.reshape(1, d), dy, *([dres] if has_res else []))


def _swap_rope_halves(y):
    lane = lax.broadcasted_iota(jnp.int32, y.shape, 1)
    half = MLA_ROPE // 2
    swapped = jnp.where(lane < MLA_NOPE + half, pltpu.roll(y, LANES - half, axis=1), pltpu.roll(y, half, axis=1))
    return jnp.where((lane >= MLA_NOPE) & (lane < MLA_QK), swapped, 0.0)


def _head_norm_fwd(x, g, n_valid, scale, rope=None):
    t, hw = x.shape
    w = g.shape[1]
    tm = _row_tile(t)

    def body(*refs):
        x_ref, g_ref = refs[:2]
        o_ref = refs[-1]
        gain = g_ref[...] * scale
        for n in range(hw // w):
            cols = slice(n * w, (n + 1) * w)
            x_ = x_ref[:, cols]
            rstd = lax.rsqrt(jnp.sum(x_ * x_, axis=-1, keepdims=True) * (1.0 / n_valid) + EPS)
            y = x_ * rstd * gain
            if rope is not None:
                y = y * refs[2][...] + _swap_rope_halves(y) * refs[3][...]
            o_ref[:, cols] = y.astype(BF16)

    row = pl.BlockSpec((tm, hw), lambda i: (i, 0))
    tab = pl.BlockSpec((tm, w), lambda i: (i, 0))
    return pl.pallas_call(
        body, name="head_norm_fwd", grid=(t // tm,),
        in_specs=[row, pl.BlockSpec((1, w), lambda i: (0, 0))] + ([tab, tab] if rope is not None else []),
        out_specs=row, out_shape=jax.ShapeDtypeStruct((t, hw), BF16),
        compiler_params=_params("parallel"),
    )(x, g, *(rope if rope is not None else ()))


def _head_norm_bwd(x, g, dout, n_valid, scale, rope=None):
    t, hw = x.shape
    w = g.shape[1]
    tm = _row_tile(t)

    def body(*refs):
        x_ref, g_ref, do_ref = refs[:3]
        dx_ref, dsum_ref, dg_ref = refs[-3:]
        gain = g_ref[...] * scale
        dsum = jnp.zeros((tm, w), F32)
        dg = jnp.zeros((1, w), F32)
        for n in range(hw // w):
            cols = slice(n * w, (n + 1) * w)
            dy = do_ref[:, cols]
            if rope is not None:
                dy = dy * refs[3][...] + _swap_rope_halves(dy * refs[4][...])
            x_ = x_ref[:, cols]
            rstd = lax.rsqrt(jnp.sum(x_ * x_, axis=-1, keepdims=True) * (1.0 / n_valid) + EPS)
            xh = x_ * rstd
            dxh = dy * gain
            dx = rstd * (dxh - xh * (jnp.sum(dxh * xh, axis=-1, keepdims=True) * (1.0 / n_valid)))
            dx_ref[:, cols] = dx
            dsum = dsum + dx
            dg = dg + jnp.sum(dy * xh, axis=0, keepdims=True)
        dsum_ref[...] = dsum

        @pl.when(pl.program_id(0) == 0)
        def _():
            dg_ref[...] = jnp.zeros_like(dg_ref)

        dg_ref[...] += scale * dg

    row = pl.BlockSpec((tm, hw), lambda i: (i, 0))
    tab = pl.BlockSpec((tm, w), lambda i: (i, 0))
    vec = pl.BlockSpec((1, w), lambda i: (0, 0))
    return pl.pallas_call(
        body, name="head_norm_bwd", grid=(t // tm,),
        in_specs=[row, vec, row] + ([tab, tab] if rope is not None else []),
        out_specs=[row, tab, vec],
        out_shape=[jax.ShapeDtypeStruct((t, hw), F32), jax.ShapeDtypeStruct((t, w), F32),
                   jax.ShapeDtypeStruct((1, w), F32)],
        compiler_params=_params("arbitrary"),
    )(x, g, dout, *(rope if rope is not None else ()))


def _tri(n, upper):
    r = lax.broadcasted_iota(jnp.int32, (n, n), 0)
    c = lax.broadcasted_iota(jnp.int32, (n, n), 1)
    return ((r <= c) if upper else (r >= c)).astype(F32)


def _gate_mask(shape, row0):
    lane = lax.broadcasted_iota(jnp.int32, shape, 1)
    row = row0 + lax.broadcasted_iota(jnp.int32, shape, 0)
    return (lane >= TAIL_F) & (lane < TAIL_F + HEADS) & (row >= PAD)


def _gate_fwd(z, bias):
    t = z.shape[0]
    tm = BLOCK
    tail = C_TAIL // LANES

    def body(z_ref, b_ref, o_ref, carry):
        i = pl.program_id(0)

        @pl.when(i == 0)
        def _():
            carry[...] = jnp.zeros_like(carry)

        x_ = z_ref[...] + b_ref[...]
        logf = jnp.minimum(x_, 0.0) - jnp.log1p(jnp.exp(-jnp.abs(x_)))
        logf = jnp.where(_gate_mask(logf.shape, i * tm), logf, 0.0)
        cum = jnp.dot(_tri(tm, False), logf, preferred_element_type=F32, precision=lax.Precision.HIGHEST) + carry[...]
        o_ref[...] = cum
        carry[...] = cum[tm - 1:tm, :]

    return pl.pallas_call(
        body, name="gate_fwd", grid=(t // tm,),
        in_specs=[pl.BlockSpec((tm, LANES), lambda i: (i, tail)), pl.BlockSpec((1, LANES), lambda i: (0, 0))],
        out_specs=pl.BlockSpec((tm, LANES), lambda i: (i, 0)),
        out_shape=jax.ShapeDtypeStruct((t, LANES), F32),
        scratch_shapes=[pltpu.VMEM((1, LANES), F32)], compiler_params=_params("arbitrary"),
    )(z, bias)


def _gate_bwd(z, bias, dcum):
    t = z.shape[0]
    tm = BLOCK
    nb = t // tm
    tail = C_TAIL // LANES

    def body(z_ref, b_ref, d_ref, o_ref, db_ref, carry):
        i = pl.program_id(0)

        @pl.when(i == 0)
        def _():
            carry[...] = jnp.zeros_like(carry)
            db_ref[...] = jnp.zeros_like(db_ref)

        rc = jnp.dot(_tri(tm, True), d_ref[...], preferred_element_type=F32, precision=lax.Precision.HIGHEST) + carry[...]
        carry[...] = rc[0:1, :]
        x_ = z_ref[...] + b_ref[...]
        sig_neg = 1.0 / (1.0 + jnp.exp(x_))
        dl = jnp.where(_gate_mask(rc.shape, (nb - 1 - i) * tm), rc * sig_neg, 0.0)
        o_ref[...] = dl
        db_ref[...] += jnp.sum(dl, axis=0, keepdims=True)

    return pl.pallas_call(
        body, name="gate_bwd", grid=(nb,),
        in_specs=[pl.BlockSpec((tm, LANES), lambda i: (nb - 1 - i, tail)), pl.BlockSpec((1, LANES), lambda i: (0, 0)),
                  pl.BlockSpec((tm, LANES), lambda i: (nb - 1 - i, 0))],
        out_specs=[pl.BlockSpec((tm, LANES), lambda i: (nb - 1 - i, 0)), pl.BlockSpec((1, LANES), lambda i: (0, 0))],
        out_shape=[jax.ShapeDtypeStruct((t, LANES), F32), jax.ShapeDtypeStruct((1, LANES), F32)],
        scratch_shapes=[pltpu.VMEM((1, LANES), F32)], compiler_params=_params("arbitrary"),
    )(z, bias, dcum)


def _pairs(nb, by_query):
    if by_query:
        pr = [(i, j) for i in range(nb) for j in range(i + 1)]
    else:
        pr = [(i, j) for j in range(nb) for i in range(j, nb)]
    return (jnp.asarray(np.array([p[0] for p in pr], np.int32)),
            jnp.asarray(np.array([p[1] for p in pr], np.int32)))


HEADS_PER_STEP = 8


def _mask_scores(s, i, j, tile):
    qp = i * tile + lax.broadcasted_iota(jnp.int32, s.shape, 0)
    kp = j * tile + lax.broadcasted_iota(jnp.int32, s.shape, 1)
    return jnp.where((kp <= qp) & (kp >= PAD), s, NEG)


def _pipelined(n, front, back):
    nxt = front(0)
    for h in range(n):
        cur = nxt
        if h + 1 < n:
            nxt = front(h + 1)
        back(h, cur)


def _nt_dot(a, b):
    return lax.dot_general(a, b, (((1,), (1,)), ((), ())), preferred_element_type=F32)


def _tn_dot(a, b):
    return lax.dot_general(a, b, (((0,), (0,)), ((), ())), preferred_element_type=F32)


def _attn_specs(hb, tile, dk, dv):
    q_of = lambda g, p, it, jt: (it[p], g)
    k_of = lambda g, p, it, jt: (jt[p], g)
    return dict(
        q=pl.BlockSpec((tile, hb * dk), q_of), k=pl.BlockSpec((tile, hb * dk), k_of),
        v=pl.BlockSpec((tile, hb * dv), k_of), ov=pl.BlockSpec((tile, hb * dv), q_of),
        fr=pl.BlockSpec((hb, 1, tile), lambda g, p, it, jt: (g, 0, jt[p])),
        rowq=pl.BlockSpec((hb, 1, tile), lambda g, p, it, jt: (g, 0, it[p])))


def _head(ref, n, d):
    return ref[:, n * d:(n + 1) * d]


def _attn_fwd(q, k, v, key_bias=None, gather=()):
    h = HEADS
    t, dk, dv = q.shape[0], q.shape[1] // h, v.shape[1] // h
    tile = _row_tile(t)
    nb = t // tile
    hb = HEADS_PER_STEP
    biased = key_bias is not None
    it, jt = _pairs(nb, True)
    sp = _attn_specs(hb, tile, dk, dv)

    n_in = 4 if biased else 3
    n_g = len(gather)
    g_in, g_out, g_shapes, g_sems = _comm_parts(gather, True)
    n_steps = int(it.shape[0])

    def body(it_ref, jt_ref, *refs):
        q_ref, k_ref, v_ref = refs[:3]
        b_ref = refs[3] if biased else None
        o_ref, lse_ref = refs[n_in + 2 * n_g:n_in + 2 * n_g + 2]
        m_sc, l_sc, acc_sc = refs[n_in + 2 * n_g + 2:n_in + 2 * n_g + 5]
        p = pl.program_id(1)
        i, j = it_ref[p], jt_ref[p]
        if n_g:
            g_start, g_finish = _gather_ops(refs[n_in:n_in + n_g], refs[n_in + n_g:n_in + 2 * n_g],
                                            *refs[n_in + 2 * n_g + 5:])
            first = (pl.program_id(0) == 0) & (p == 0)
            last = (pl.program_id(0) == h // hb - 1) & (p == n_steps - 1)
            pl.when(first)(g_start)

        @pl.when(j == 0)
        def _():
            m_sc[...] = jnp.full_like(m_sc, NEG)
            l_sc[...] = jnp.zeros_like(l_sc)
            acc_sc[...] = jnp.zeros_like(acc_sc)

        def step(masked):
            def front(n):
                return _nt_dot(_head(q_ref, n, dk), _head(k_ref, n, dk))

            def back(n, s):
                if biased:
                    s = s + b_ref[n]
                if masked:
                    s = _mask_scores(s, i, j, tile)
                m_prev = m_sc[n]
                m_new = jnp.maximum(m_prev, jnp.max(s, axis=-1, keepdims=True))
                alpha = jnp.exp2(m_prev - m_new)
                e = jnp.exp2(s - m_new)
                l_sc[n] = alpha * l_sc[n] + jnp.sum(e, axis=-1, keepdims=True)
                acc_sc[n] = alpha * acc_sc[n] + jnp.dot(e.astype(BF16), _head(v_ref, n, dv),
                                                        preferred_element_type=F32)
                m_sc[n] = m_new

            _pipelined(hb, front, back)

        edge = (j == i) | (j == 0)
        pl.when(edge)(lambda: step(True))
        pl.when(jnp.logical_not(edge))(lambda: step(False))

        @pl.when(j == i)
        def _():
            row = i * tile + lax.broadcasted_iota(jnp.int32, (tile, 1), 0)
            for n in range(hb):
                o_ref[:, n * dv:(n + 1) * dv] = jnp.where(row >= PAD, acc_sc[n] / l_sc[n], 0.0)
                lse_ref[n] = jnp.transpose(m_sc[n] + jnp.log2(l_sc[n]))

        if n_g:
            pl.when(last)(g_finish)

    grid_spec = pltpu.PrefetchScalarGridSpec(
        num_scalar_prefetch=2, grid=(h // hb, n_steps),
        in_specs=[sp["q"], sp["k"], sp["v"]] + ([sp["fr"]] if biased else []) + g_in,
        out_specs=g_out + [sp["ov"], sp["rowq"]],
        scratch_shapes=[pltpu.VMEM((hb, tile, 1), F32), pltpu.VMEM((hb, tile, 1), F32),
                        pltpu.VMEM((hb, tile, dv), F32)] + (g_sems if n_g else []))
    outs = pl.pallas_call(
        body, name="attn_fwd_gather" if n_g else "attn_fwd", grid_spec=grid_spec,
        out_shape=g_shapes + [jax.ShapeDtypeStruct((t, h * dv), F32), jax.ShapeDtypeStruct((h, 1, t), F32)],
        compiler_params=_params("arbitrary" if n_g else "parallel", "arbitrary"),
    )(it, jt, q, k, v, *((key_bias,) if biased else ()), *gather)
    return (outs[n_g], outs[n_g + 1], outs[:n_g]) if n_g else tuple(outs)


LOG2E = 1.4426950408889634
LN2 = 0.6931471805599453
MLA_SCALE = MLA_QK ** -0.5 * LOG2E
FOX_SCALE = FOX_DIM ** -0.5 * LOG2E


def _attn_delta(o, do):
    h = HEADS
    t, dv = o.shape[0], o.shape[1] // h
    tm = _row_tile(t)

    def body(o_ref, do_ref, d_ref):
        prod = o_ref[...] * do_ref[...]
        for n in range(h):
            d_ref[n] = jnp.transpose(jnp.sum(prod[:, n * dv:(n + 1) * dv], axis=-1, keepdims=True) * LN2)

    blk = pl.BlockSpec((tm, h * dv), lambda i: (i, 0))
    return pl.pallas_call(
        body, name="attn_delta", grid=(t // tm,), in_specs=[blk, blk],
        out_specs=pl.BlockSpec((h, 1, tm), lambda i: (0, 0, i)),
        out_shape=jax.ShapeDtypeStruct((h, 1, t), F32), compiler_params=_params("parallel"),
    )(o, do)


BWD_HEADS_PER_STEP = (4, 4)


def _attn_bwd(q, k, v, do, lse_row, delta_row, key_bias=None, exchange=()):
    h = HEADS
    t, dk, dv = q.shape[0], q.shape[1] // h, v.shape[1] // h
    tile = _row_tile(t)
    nb = t // tile
    decay = key_bias is not None
    hb = BWD_HEADS_PER_STEP[int(decay)]
    it, jt = _pairs(nb, False)
    sp = _attn_specs(hb, tile, dk, dv)
    n_in = 7 if decay else 6
    n_out = 5 if decay else 3
    n_x = len(exchange)
    x_in, x_out, x_shapes, x_sems = _comm_parts(exchange, False)
    n_steps = int(it.shape[0])

    def body(it_ref, jt_ref, *refs):
        q_ref, k_ref, v_ref, do_ref, lse_ref, delta_ref = refs[:6]
        b_ref = refs[6] if decay else None
        outs = refs[n_in + n_x:]
        dq_ref, dk_ref, dv_ref = outs[:3]
        rs_ref, ks_ref = (outs[3], outs[4]) if decay else (None, None)
        scratch = outs[n_out + n_x:]
        dq_sc, dk_sc, dv_sc = scratch[:3]
        ks_sc, b_sc = (scratch[3], scratch[4]) if decay else (None, None)
        p = pl.program_id(1)
        i, j = it_ref[p], jt_ref[p]
        if n_x:
            x_start, x_finish = _exchange_ops(refs[n_in:n_in + n_x], outs[n_out:n_out + n_x],
                                              *scratch[5 if decay else 3:])
            first = (pl.program_id(0) == 0) & (p == 0)
            last = (pl.program_id(0) == h // hb - 1) & (p == n_steps - 1)
            pl.when(first)(x_start)

        @pl.when(p == 0)
        def _():
            dq_sc[...] = jnp.zeros_like(dq_sc)
            if decay:
                rs_ref[...] = jnp.zeros_like(rs_ref)

        @pl.when(i == j)
        def _():
            dk_sc[...] = jnp.zeros_like(dk_sc)
            dv_sc[...] = jnp.zeros_like(dv_sc)
            if decay:
                ks_sc[...] = jnp.zeros_like(ks_sc)
                for n in range(hb):
                    b_sc[n] = jnp.transpose(b_ref[n])

        def step(masked):
            def front(n):
                return (_nt_dot(_head(k_ref, n, dk), _head(q_ref, n, dk)),
                        _nt_dot(_head(v_ref, n, dv), (_head(do_ref, n, dv) * LN2).astype(BF16)))

            def back(n, s_dp):
                s, dp = s_dp
                if decay:
                    s = s + b_sc[n]
                if masked:
                    kp = j * tile + lax.broadcasted_iota(jnp.int32, s.shape, 0)
                    qp = i * tile + lax.broadcasted_iota(jnp.int32, s.shape, 1)
                    s = jnp.where((kp <= qp) & (kp >= PAD), s, NEG)
                pr = jnp.exp2(s - lse_ref[n])
                ds = pr * (dp - delta_ref[n])
                ds_b = ds.astype(BF16)
                dv_sc[n] += jnp.dot(pr.astype(BF16), _head(do_ref, n, dv).astype(BF16), preferred_element_type=F32)
                dk_sc[n] += jnp.dot(ds_b, _head(q_ref, n, dk), preferred_element_type=F32)
                dq_sc[n, i] += _tn_dot(ds_b, _head(k_ref, n, dk))
                if decay:
                    rs_ref[n, i] += jnp.sum(ds, axis=0, keepdims=True)
                    ks_sc[n] += jnp.sum(ds, axis=-1, keepdims=True)

            _pipelined(hb, front, back)

        edge = (j == i) | (j == 0)
        pl.when(edge)(lambda: step(True))
        pl.when(jnp.logical_not(edge))(lambda: step(False))

        @pl.when(i == j)
        def _():
            for n in range(hb):
                dq_ref[:, n * dk:(n + 1) * dk] = dq_sc[n, j]

        @pl.when(i == nb - 1)
        def _():
            for n in range(hb):
                dk_ref[:, n * dk:(n + 1) * dk] = dk_sc[n]
                dv_ref[:, n * dv:(n + 1) * dv] = dv_sc[n]
                if decay:
                    ks_ref[n] = jnp.transpose(ks_sc[n])

        if n_x:
            pl.when(last)(x_finish)

    rows_out = pl.BlockSpec((hb, nb, 1, tile), lambda hh, p, it, jt: (hh, 0, 0, 0))
    grid_spec = pltpu.PrefetchScalarGridSpec(
        num_scalar_prefetch=2, grid=(h // hb, n_steps),
        in_specs=[sp["q"], sp["k"], sp["v"], sp["ov"], sp["rowq"], sp["rowq"]] + ([sp["fr"]] if decay else [])
        + x_in,
        out_specs=[sp["k"], sp["k"], sp["v"]] + ([rows_out, sp["fr"]] if decay else []) + x_out,
        scratch_shapes=[pltpu.VMEM((hb, nb, tile, dk), F32), pltpu.VMEM((hb, tile, dk), F32),
                        pltpu.VMEM((hb, tile, dv), F32)] + ([pltpu.VMEM((hb, tile, 1), F32)] * 2 if decay else [])
        + (x_sems if n_x else []))
    out_shape = [jax.ShapeDtypeStruct((t, h * dk), F32), jax.ShapeDtypeStruct((t, h * dk), F32),
                 jax.ShapeDtypeStruct((t, h * dv), F32)] \
        + ([jax.ShapeDtypeStruct((h, nb, 1, tile), F32), jax.ShapeDtypeStruct((h, 1, t), F32)] if decay else []) \
        + x_shapes
    outs = pl.pallas_call(
        body, name="attn_bwd_exchange" if n_x else "attn_bwd", grid_spec=grid_spec, out_shape=out_shape,
        compiler_params=_params("arbitrary" if n_x else "parallel", "arbitrary"),
    )(it, jt, q, k, v, do, lse_row, delta_row, *((key_bias,) if decay else ()), *exchange)
    return tuple(outs[:n_out]) + ((list(outs[n_out:]),) if n_x else ())


CONV_COLS = 512


def _shift_down(g, prev, n):
    out = pltpu.roll(g, n, axis=0)
    row = lax.broadcasted_iota(jnp.int32, g.shape, 0)
    for r in range(n):
        out = jnp.where(row == r, prev[8 - n + r:8 - n + r + 1, :], out)
    return out


def _shift_up(g, nxt, n):
    tm = g.shape[0]
    out = pltpu.roll(g, tm - n, axis=0)
    row = lax.broadcasted_iota(jnp.int32, g.shape, 0)
    for r in range(n):
        out = jnp.where(row == tm - n + r, nxt[r:r + 1, :], out)
    return out


def _conv_fwd(z3, w):
    _, t, d = z3.shape
    tm = _row_tile(t)
    tc = CONV_COLS

    def body(z_ref, zp_ref, w_ref, o_ref):
        i = pl.program_id(1)
        g = z_ref[1] * z_ref[2]
        gp = jnp.where(i > 0, zp_ref[1] * zp_ref[2], 0.0)
        w_ = w_ref[...]
        y = w_[2:3] * g + w_[1:2] * _shift_down(g, gp, 1) + w_[0:1] * _shift_down(g, gp, 2)
        o_ref[...] = (z_ref[0] * y).astype(BF16)

    return pl.pallas_call(
        body, name="conv_fwd", grid=(d // tc, t // tm),
        in_specs=[pl.BlockSpec((3, tm, tc), lambda j, i: (0, i, j)),
                  pl.BlockSpec((3, 8, tc), lambda j, i: (0, jnp.maximum(i * (tm // 8) - 1, 0), j)),
                  pl.BlockSpec((3, tc), lambda j, i: (0, j))],
        out_specs=pl.BlockSpec((tm, tc), lambda j, i: (i, j)),
        out_shape=jax.ShapeDtypeStruct((t, d), BF16), compiler_params=_params("parallel", "parallel"),
    )(z3, z3, w)


def _conv_bwd(z3, w, dyb):
    _, t, d = z3.shape
    tm = _row_tile(t)
    tc = CONV_COLS
    ni = t // tm

    def body(z_ref, zp_ref, zn_ref, d_ref, dn_ref, w_ref, dz_ref, dw_ref):
        i = pl.program_id(1)
        gb, gc, u = z_ref[0], z_ref[1], z_ref[2]
        g = gc * u
        gp = jnp.where(i > 0, zp_ref[1] * zp_ref[2], 0.0)
        w_ = w_ref[...]
        g1, g2 = _shift_down(g, gp, 1), _shift_down(g, gp, 2)
        y = w_[2:3] * g + w_[1:2] * g1 + w_[0:1] * g2
        dyb_ = d_ref[...]
        dy = dyb_ * gb
        dyn = jnp.where(i < ni - 1, dn_ref[...] * zn_ref[0], 0.0)
        dg = w_[2:3] * dy + w_[1:2] * _shift_up(dy, dyn, 1) + w_[0:1] * _shift_up(dy, dyn, 2)
        dz_ref[0] = (dyb_ * y).astype(BF16)
        dz_ref[1] = (dg * u).astype(BF16)
        dz_ref[2] = (dg * gc).astype(BF16)

        @pl.when(i == 0)
        def _():
            dw_ref[...] = jnp.zeros_like(dw_ref)

        dw_ref[...] += jnp.concatenate([jnp.sum(dy * g2, axis=0, keepdims=True),
                                        jnp.sum(dy * g1, axis=0, keepdims=True),
                                        jnp.sum(dy * g, axis=0, keepdims=True)], axis=0)

    cur = pl.BlockSpec((3, tm, tc), lambda j, i: (0, i, j))
    return pl.pallas_call(
        body, name="conv_bwd", grid=(d // tc, ni),
        in_specs=[cur,
                  pl.BlockSpec((3, 8, tc), lambda j, i: (0, jnp.maximum(i * (tm // 8) - 1, 0), j)),
                  pl.BlockSpec((3, 8, tc), lambda j, i: (0, jnp.minimum((i + 1) * (tm // 8), t // 8 - 1), j)),
                  pl.BlockSpec((tm, tc), lambda j, i: (i, j)),
                  pl.BlockSpec((8, tc), lambda j, i: (jnp.minimum((i + 1) * (tm // 8), t // 8 - 1), j)),
                  pl.BlockSpec((3, tc), lambda j, i: (0, j))],
        out_specs=[cur, pl.BlockSpec((3, tc), lambda j, i: (0, j))],
        out_shape=[jax.ShapeDtypeStruct((3, t, d), BF16), jax.ShapeDtypeStruct((3, d), F32)],
        compiler_params=_params("parallel", "arbitrary"),
    )(z3, z3, z3, dyb, dyb, w)


def _loss_head(h, target):
    t, d = h.shape
    tm = BLOCK

    def body(h_ref, t_ref, dh_ref, dhb_ref, loss_ref):
        i = pl.program_id(0)

        @pl.when(i == 0)
        def _():
            loss_ref[...] = jnp.zeros_like(loss_ref)

        err = jnp.where(i > 0, h_ref[...] - t_ref[...], 0.0)
        dh = err * (1.0 / d)
        dh_ref[...] = dh
        dhb_ref[...] = dh.astype(BF16)
        loss_ref[...] += 0.5 * jnp.sum(jnp.sum(err * err, axis=-1, keepdims=True) * (1.0 / d), axis=0, keepdims=True)

    row = pl.BlockSpec((tm, d), lambda i: (i, 0))
    return pl.pallas_call(
        body, name="loss_head", grid=(t // tm,),
        in_specs=[row, pl.BlockSpec((tm, d), lambda i: (jnp.maximum(i - 1, 0), 0))],
        out_specs=[row, row, pl.BlockSpec((1, 1), lambda i: (0, 0))],
        out_shape=[jax.ShapeDtypeStruct((t, d), F32), jax.ShapeDtypeStruct((t, d), BF16),
                   jax.ShapeDtypeStruct((1, 1), F32)],
        compiler_params=_params("arbitrary"),
    )(h, target)


def _rope_tables(t):
    pos = jnp.arange(t, dtype=F32) - PAD
    inv_freq = ROPE_BASE ** (-jnp.arange(0, MLA_ROPE, 2, dtype=F32) / MLA_ROPE)
    ang = pos[:, None] * inv_freq[None, :]
    cos, sin = jnp.cos(ang), jnp.sin(ang)
    one, zero = jnp.ones((t, MLA_NOPE), F32), jnp.zeros((t, MLA_NOPE), F32)
    tail = jnp.zeros((t, LANES - MLA_QK), F32)
    return (jnp.concatenate([one, cos, cos, tail], axis=1), jnp.concatenate([zero, -sin, sin, tail], axis=1))


def _pad_lanes(x, width=LANES):
    return jnp.pad(x, [(0, 0)] * (x.ndim - 1) + [(0, width - x.shape[-1])])


def _permute_in_attn(w):
    return jnp.concatenate([w[:, :640], w[:, 672:2208], w[:, 640:672], w[:, 2208:2216],
                            jnp.zeros((w.shape[0], ATTN_IN_PAD - 2216), w.dtype)], axis=1)


def _unpermute_in_attn(dw):
    return jnp.concatenate([dw[:, :640], dw[:, 2176:2208], dw[:, 640:2176], dw[:, 2208:2216]], axis=1)


def _attn_layer_fwd(hn, wl, rope, gather=()):
    t = hn.shape[0]
    z = _mm(hn, wl["w_in"], name="attn_in")
    cqn = _rms_fwd(z[:, C_CQ:C_CQ + Q_LORA], wl["g_cq"])
    ckvn = _rms_fwd(z[:, C_CKV:C_CKV + KV_LORA], wl["g_ckv"])
    qf = _mm(cqn, wl["w_uq"], name="mla_uq")
    kvf = _mm(ckvn, wl["w_ukv"], name="mla_ukv")
    kv3 = kvf.reshape(t, HEADS, MLA_NOPE + MLA_V)
    xq = _pad_lanes(qf.reshape(t, HEADS, MLA_QK)).reshape(t, HEADS * LANES)
    k_pe = jnp.broadcast_to(z[:, None, C_TAIL:C_TAIL + MLA_ROPE], (t, HEADS, MLA_ROPE))
    xk = _pad_lanes(jnp.concatenate([kv3[:, :, :MLA_NOPE], k_pe], axis=-1)).reshape(t, HEADS * LANES)
    v_mla = kv3[:, :, MLA_NOPE:].reshape(t, HEADS * MLA_V).astype(BF16)
    gq, gk = _pad_lanes(wl["g_q_mla"].reshape(1, -1)), _pad_lanes(wl["g_k_mla"].reshape(1, -1))
    q_mla = _head_norm_fwd(xq, gq, MLA_QK, MLA_SCALE, rope)
    k_mla = _head_norm_fwd(xk, gk, MLA_QK, 1.0, rope)
    o_mla, lse_mla, *gathered = _attn_fwd(q_mla, k_mla, v_mla, gather=gather)
    xfq = z[:, C_FQ:C_FQ + HEADS * FOX_DIM]
    xfk = z[:, C_FK:C_FK + HEADS * FOX_DIM]
    v_fox = z[:, C_FV:C_FV + HEADS * FOX_DIM].astype(BF16)
    bias = jnp.pad(wl["b_forget"].reshape(1, -1), ((0, 0), (TAIL_F, LANES - TAIL_F - HEADS)))
    cum = _gate_fwd(z, bias)
    neg_f = (-LOG2E * jnp.transpose(cum[:, TAIL_F:TAIL_F + HEADS]))[:, None, :]
    q_fox = _head_norm_fwd(xfq, wl["g_q_fox"].reshape(1, -1), FOX_DIM, FOX_SCALE)
    k_fox = _head_norm_fwd(xfk, wl["g_k_fox"].reshape(1, -1), FOX_DIM, 1.0)
    o_fox, lse_fox = _attn_fwd(q_fox, k_fox, v_fox, neg_f)
    cat = jnp.concatenate([o_mla, o_fox], axis=1).astype(BF16)
    saved = dict(z=z, cqn=cqn, ckvn=ckvn, xq=xq, xk=xk, v_mla=v_mla, q_mla=q_mla, k_mla=k_mla, o_mla=o_mla,
                 lse_mla=lse_mla, xfq=xfq, xfk=xfk, v_fox=v_fox, q_fox=q_fox, k_fox=k_fox, bias=bias,
                 neg_f=neg_f, o_fox=o_fox, lse_fox=lse_fox, cat=cat, gq=gq, gk=gk)
    return cat, saved, (gathered[0] if gathered else ())


def _attn_layer_bwd(dcat, hn, wl, sv, rope, exchange=()):
    t = hn.shape[0]
    g = {}
    do_mla = dcat[:, :HEADS * MLA_V]
    do_fox = dcat[:, HEADS * MLA_V:]
    qkv = (sv["q_fox"], sv["k_fox"], sv["v_fox"])
    delta = _attn_delta(sv["o_fox"], do_fox)
    dq_fox, dk_fox, dv_fox, row_sums, key_sums, *exchanged = _attn_bwd(
        *qkv, do_fox, sv["lse_fox"], delta, sv["neg_f"], exchange=exchange)
    dcum = jnp.pad(jnp.transpose(LOG2E * (row_sums.reshape(HEADS, t) - key_sums.reshape(HEADS, t))),
                   ((0, 0), (TAIL_F, LANES - TAIL_F - HEADS)))
    dtail_f, dbias = _gate_bwd(sv["z"], sv["bias"], dcum)
    g["b_forget"] = dbias[0, TAIL_F:TAIL_F + HEADS]
    dxfq, _, dgq = _head_norm_bwd(sv["xfq"], wl["g_q_fox"].reshape(1, -1), dq_fox, FOX_DIM, FOX_SCALE)
    dxfk, _, dgk = _head_norm_bwd(sv["xfk"], wl["g_k_fox"].reshape(1, -1), dk_fox, FOX_DIM, 1.0)
    g["g_q_fox"], g["g_k_fox"] = dgq[0], dgk[0]
    qkv = (sv["q_mla"], sv["k_mla"], sv["v_mla"])
    delta = _attn_delta(sv["o_mla"], do_mla)
    dq_mla, dk_mla, dv_mla = _attn_bwd(*qkv, do_mla, sv["lse_mla"], delta)
    dxq, _, dgq = _head_norm_bwd(sv["xq"], sv["gq"], dq_mla, MLA_QK, MLA_SCALE, rope)
    dxk, dxk_sum, dgk = _head_norm_bwd(sv["xk"], sv["gk"], dk_mla, MLA_QK, 1.0, rope)
    g["g_q_mla"], g["g_k_mla"] = dgq[0, :MLA_QK], dgk[0, :MLA_QK]
    dqf = dxq.reshape(t, HEADS, LANES)[:, :, :MLA_QK].reshape(t, HEADS * MLA_QK).astype(BF16)
    dkvf = jnp.concatenate([dxk.reshape(t, HEADS, LANES)[:, :, :MLA_NOPE], dv_mla.reshape(t, HEADS, MLA_V)],
                           axis=-1).reshape(t, HEADS * (MLA_NOPE + MLA_V)).astype(BF16)
    g["w_uq"] = _mm(sv["cqn"], dqf, ta=True, name="d_w_uq")
    g["w_ukv"] = _mm(sv["ckvn"], dkvf, ta=True, name="d_w_ukv")
    dcqn = _mm(dqf, wl["w_uq"], tb=True, name="d_cqn")
    dckvn = _mm(dkvf, wl["w_ukv"], tb=True, name="d_ckvn")
    z = sv["z"]
    dcq, dg_cq = _rms_bwd(z[:, C_CQ:C_CQ + Q_LORA], wl["g_cq"], dcqn)
    dckv, dg_ckv = _rms_bwd(z[:, C_CKV:C_CKV + KV_LORA], wl["g_ckv"], dckvn)
    g["g_cq"], g["g_ckv"] = dg_cq[0], dg_ckv[0]
    tail = jnp.concatenate([dxk_sum[:, MLA_NOPE:MLA_QK], dtail_f[:, TAIL_F:]], axis=1)
    dz = jnp.concatenate([dcq, dckv, dxfq, dxfk, dv_fox, tail], axis=1).astype(BF16)
    g["w_in"] = _mm(hn, dz, ta=True, name="d_w_in_attn")
    dhn = _mm(dz, wl["w_in"], tb=True, name="d_hn_attn")
    return dhn, g, (exchanged[0] if exchanged else ())


def _local_step(x, target, w, gather_late=None, exchange_early=None):
    seq = x.shape[0]
    t = seq + BLOCK
    rope = _rope_tables(t)
    h = jnp.concatenate([jnp.zeros((PAD, D_MODEL), F32), w["meta_tokens"], x], axis=0)
    tape = []
    for layer in range(DEPTH):
        j = layer // 2
        hn = _rms_fwd(h, w["g_mix"][layer])
        if layer % 2 == 0:
            wl = dict(w_in=w["w_in_attn"][j], g_cq=w["g_cq"][j], w_uq=w["w_uq"][j], g_ckv=w["g_ckv"][j],
                      w_ukv=w["w_ukv"][j], g_q_mla=w["g_q_mla"][j], g_k_mla=w["g_k_mla"][j],
                      g_q_fox=w["g_q_fox"][j], g_k_fox=w["g_k_fox"][j], b_forget=w["b_forget"][j])
            hosted = gather_late is not None and layer == 0
            mixed, sv, gathered = _attn_layer_fwd(hn, wl, rope, gather_late[0] if hosted else ())
            if hosted:
                gather_late[1](w, gathered)
            h1 = _mm(mixed, w["w_out_attn"][j], res=h, name="attn_out")
        else:
            wl = None
            z3 = _mm(hn, w["w_in_conv"][j], out_seg=3, name="conv_in")
            mixed = _conv_fwd(z3, w["conv_w"][j])
            sv = dict(z3=z3)
            h1 = _mm(mixed, w["w_out_conv"][j], res=h, name="conv_out")
        hn2 = _rms_fwd(h1, w["g_mlp"][layer])
        u, act = _mm(hn2, w["w_mlp_up"][layer], epi="relu2", name="mlp_up")
        h2 = _mm(act, w["w_mlp_down"][layer], res=h1, name="mlp_down")
        tape.append(dict(h=h, hn=hn, wl=wl, sv=sv, mixed=mixed, h1=h1, hn2=hn2, u=u, act=act))
        h = h2

    dh, dh_b, loss = _loss_head(h, target)
    exchanged = ()
    g = {n: [None] * (DEPTH if n in ("g_mix", "g_mlp", "w_mlp_up", "w_mlp_down") else DEPTH // 2)
         for n in WEIGHTS if n != "meta_tokens"}
    for layer in reversed(range(DEPTH)):
        j = layer // 2
        tp = tape[layer]
        g["w_mlp_down"][layer] = _mm(tp["act"], dh_b, ta=True, name="d_w_down")
        du = _mm(dh_b, w["w_mlp_down"][layer], tb=True, epi="relu2_bwd", aux=tp["u"], out_dtype=BF16, name="d_u")
        g["w_mlp_up"][layer] = _mm(tp["hn2"], du, ta=True, name="d_w_up")
        dhn2 = _mm(du, w["w_mlp_up"][layer], tb=True, name="d_hn2")
        dh1, dh1_b, dg = _rms_bwd(tp["h1"], w["g_mlp"][layer], dhn2, dres=dh, want_bf16=True)
        g["g_mlp"][layer] = dg[0]
        if layer % 2 == 0:
            g["w_out_attn"][j] = _mm(tp["mixed"], dh1_b, ta=True, name="d_w_out_attn")
            dcat = _mm(dh1_b, w["w_out_attn"][j], tb=True, name="d_cat")
            hosted = exchange_early is not None and layer == 0
            dhn, gl, got = _attn_layer_bwd(dcat, tp["hn"], tp["wl"], tp["sv"], rope,
                                           exchange_early(g) if hosted else ())
            if hosted:
                exchanged = got
            g["w_in_attn"][j] = _unpermute_in_attn(gl.pop("w_in"))
            for n, val in gl.items():
                g[n][j] = val
        else:
            g["w_out_conv"][j] = _mm(tp["mixed"], dh1_b, ta=True, name="d_w_out_conv")
            dyb = _mm(dh1_b, w["w_out_conv"][j], tb=True, name="d_yb")
            dz3, dcw = _conv_bwd(tp["sv"]["z3"], w["conv_w"][j], dyb)
            g["conv_w"][j] = dcw
            g["w_in_conv"][j] = _mm(tp["hn"], dz3, ta=True, name="d_w_in_conv")
            dhn = _mm(dz3, w["w_in_conv"][j], tb=True, name="d_hn_conv")
        dh, dh_b, dg = _rms_bwd(tp["h"], w["g_mix"][layer], dhn, dres=dh1, want_bf16=True)
        g["g_mix"][layer] = dg[0]
    g["meta_tokens"] = [dh[PAD:BLOCK]]
    return loss, dh[BLOCK:], g, exchanged


COMM_ROWS = 2048
ADAMW_BLOCK_BYTES = 1 << 20


def kernel(x, meta_tokens, g_mix, g_mlp, w_in_attn, g_cq, w_uq, g_ckv, w_ukv, g_q_mla, g_k_mla, g_q_fox, g_k_fox, b_forget, w_out_attn, w_in_conv, conv_w, w_out_conv, w_mlp_up, w_mlp_down, loss_target, m_meta_tokens, m_g_mix, m_g_mlp, m_w_in_attn, m_g_cq, m_w_uq, m_g_ckv, m_w_ukv, m_g_q_mla, m_g_k_mla, m_g_q_fox, m_g_k_fox, m_b_forget, m_w_out_attn, m_w_in_conv, m_conv_w, m_w_out_conv, m_w_mlp_up, m_w_mlp_down, v_meta_tokens, v_g_mix, v_g_mlp, v_w_in_attn, v_g_cq, v_w_uq, v_g_ckv, v_w_ukv, v_g_q_mla, v_g_k_mla, v_g_q_fox, v_g_k_fox, v_b_forget, v_w_out_attn, v_w_in_conv, v_conv_w, v_w_out_conv, v_w_mlp_up, v_w_mlp_down):
    args = dict(locals())
    local = {n: args[n] for n in WEIGHTS}
    mom = {n: args["m_" + n] for n in WEIGHTS}
    var = {n: args["v_" + n] for n in WEIGHTS}
    axis = dict(SHARDED)
    count = {n: local[n].shape[0] for n, _ in SHARDED if n != "meta_tokens"}
    piece = lambda src, p: src[p[0]] if p[1] is None else src[p[0]][p[1]]
    piece_axis = lambda p: axis[p[0]] - (0 if p[1] is None else 1)
    shape_of = lambda p: piece(local, p).shape
    layers = lambda n, ls: [(n, l) for l in ls]
    first_bf = [("w_uq", 0), ("w_ukv", 0), ("w_out_attn", 0), ("w_mlp_up", 0), ("w_mlp_down", 0)]
    first_f32 = [("meta_tokens", None), ("conv_w", 0), ("conv_w", 1)]
    late_bf = ([("w_uq", 1), ("w_ukv", 1), ("w_out_attn", 1)] + layers("w_in_conv", (0, 1))
               + layers("w_out_conv", (0, 1)) + layers("w_mlp_up", (1, 2, 3)) + layers("w_mlp_down", (1, 2, 3)))
    early_bf = ([("w_uq", 1), ("w_ukv", 1)] + layers("w_out_attn", (0, 1)) + layers("w_in_conv", (0, 1))
                + layers("conv_w", (0, 1)) + layers("w_out_conv", (0, 1)) + layers("w_mlp_up", range(DEPTH))
                + layers("w_mlp_down", range(DEPTH)))
    last_bf = [("w_uq", 0), ("w_ukv", 0), ("meta_tokens", None)]

    w = {n: local[n] for n in REPLICATED}
    w.update({n: [None] * c for n, c in count.items()})

    def install(w, pieces, gathered, in_layer, gathered_in):
        for p, blocks in zip(pieces, _unpack(gathered, [shape_of(p) for p in pieces], (N_DEV,))):
            value = _from_shards(blocks, piece_axis(p))
            if p[1] is None:
                w[p[0]] = value
            else:
                w[p[0]][p[1]] = value
        w["w_in_attn"][in_layer] = _permute_in_attn(_from_shards(gathered_in, piece_axis(("w_in_attn", 0))))

    got_bf, got_in, got_f32 = _all_gather([
        _pack([piece(local, p) for p in first_bf], 16, BF16), local["w_in_attn"][0].astype(BF16),
        _pack([piece(local, p) for p in first_f32], 8, F32)])
    install(w, first_bf, got_bf, 0, got_in)
    for p, blocks in zip(first_f32, _unpack(got_f32, [shape_of(p) for p in first_f32], (N_DEV,))):
        if p[1] is None:
            w[p[0]] = _from_shards(blocks, piece_axis(p))
        else:
            w[p[0]][p[1]] = _from_shards(blocks, piece_axis(p))
    gather_late = ([_pack([piece(local, p) for p in late_bf], 16, BF16), local["w_in_attn"][1].astype(BF16)],
                   lambda w_, got: install(w_, late_bf, got[0], 1, got[1]))

    def pack_grads(g, pieces, in_layer):
        sent = _pack_rows([_to_shards(piece(g, p), piece_axis(p)) for p in pieces], COMM_ROWS, BF16)
        return [sent, _to_shards(g["w_in_attn"][in_layer], piece_axis(("w_in_attn", 0))).astype(BF16)]

    loss_part, dx, grads, early = _local_step(x[0], loss_target[0], w, gather_late,
                                              lambda g: pack_grads(g, early_bf, 1))
    loss = lax.psum(loss_part[0, 0], ("x", "y", "c"))

    grads["meta_tokens"] = grads["meta_tokens"][0]
    last = _all_to_all(pack_grads(grads, last_bf, 0))
    g_piece = {}
    for pieces, (got, got_in), in_layer in ((early_bf, early, 1), (last_bf, last, 0)):
        summed = _unpack(_sum_blocks(got, COMM_ROWS), [shape_of(p) for p in pieces])
        g_piece.update(zip(pieces, summed))
        g_piece[("w_in_attn", in_layer)] = _sum_blocks(got_in, 256)
    g_local = {n: jnp.stack([g_piece[(n, l)] for l in range(c)]) for n, c in count.items()}
    g_local["meta_tokens"] = g_piece[("meta_tokens", None)]
    rep, = _all_gather([_pack([jnp.stack(grads[n]) for n in REPLICATED], 8, F32)])
    g_rep = _sum_blocks(rep, rep.shape[1])
    g_local.update(zip(REPLICATED, _unpack(g_rep, [local[n].shape for n in REPLICATED])))

    def flat(src, names, rows_multiple):
        return _pack([src[n] for n in names], rows_multiple, F32)

    upd = {}
    rows = g_rep.shape[0]
    outs = _adamw(flat(local, REPLICATED, rows), g_rep, flat(mom, REPLICATED, rows), flat(var, REPLICATED, rows), rows)
    for kind, buf in zip(("delta", "m", "v"), outs):
        upd.update({(kind, n): a for n, a in zip(REPLICATED, _unpack(buf, [local[n].shape for n in REPLICATED]))})
    for n, _ in SHARDED:
        as_rows = lambda a: a.reshape(-1, a.shape[-1])
        n_rows, n_cols = as_rows(local[n]).shape
        tiles = [r for r in (1024, 512, 256, 128, 64, 32, 16, 8) if r * n_cols * 4 <= ADAMW_BLOCK_BYTES]
        outs = _adamw(as_rows(local[n]), as_rows(g_local[n]), as_rows(mom[n]), as_rows(var[n]), _pick(n_rows, tiles))
        for kind, buf in zip(("delta", "m", "v"), outs):
            upd[(kind, n)] = buf.reshape(local[n].shape)

    return (loss, dx[None], *[g_local[n] for n in WEIGHTS], *[upd[("delta", n)] for n in WEIGHTS],
            *[upd[("m", n)] for n in WEIGHTS], *[upd[("v", n)] for n in WEIGHTS])
```

```python
import functools
import math

import jax
import jax.numpy as jnp
import numpy as np
from jax import lax
from jax.experimental import pallas as pl
from jax.experimental.pallas import tpu as pltpu

F32 = jnp.float32
BF16 = jnp.bfloat16

N_DEV = 8
D_MODEL = 1024
DEPTH = 4
N_META = 16
BLOCK = 128
PAD = BLOCK - N_META
HEADS = 8
MLA_NOPE = 64
MLA_ROPE = 32
MLA_QK = MLA_NOPE + MLA_ROPE
MLA_V = 64
Q_LORA = 384
KV_LORA = 256
ROPE_BASE = 10000.0
FOX_DIM = 64
D_FF = 4 * D_MODEL
EPS = 1e-6
NEG = -1e30
LANES = 128
ATTN_IN_PAD = 2304
C_CQ, C_CKV, C_FQ, C_FK, C_FV, C_TAIL = 0, 384, 640, 1152, 1664, 2176
TAIL_F = MLA_ROPE

ADAM_LR = 0.001
ADAM_B1 = 0.9
ADAM_B2 = 0.999
ADAM_EPS = 1e-08
ADAM_WD = 0.01
ADAM_STEP = 10

VMEM_LIMIT_BYTES = 48 * 1024 * 1024
MESH = pl.DeviceIdType.MESH

SHARDED = (
    ("meta_tokens", 1), ("w_in_attn", 2), ("w_uq", 2), ("w_ukv", 2), ("w_out_attn", 1),
    ("w_in_conv", 2), ("conv_w", 2), ("w_out_conv", 1), ("w_mlp_up", 2), ("w_mlp_down", 1))
F32_GATHERED = ("meta_tokens", "conv_w")
REPLICATED = ("g_mix", "g_mlp", "g_cq", "g_ckv", "g_q_mla", "g_k_mla", "g_q_fox", "g_k_fox", "b_forget")
WEIGHTS = ("meta_tokens", "g_mix", "g_mlp", "w_in_attn", "g_cq", "w_uq", "g_ckv", "w_ukv", "g_q_mla",
           "g_k_mla", "g_q_fox", "g_k_fox", "b_forget", "w_out_attn", "w_in_conv", "conv_w",
           "w_out_conv", "w_mlp_up", "w_mlp_down")


def _params(*sem):
    return pltpu.CompilerParams(dimension_semantics=sem, vmem_limit_bytes=VMEM_LIMIT_BYTES)


def _row_tile(t):
    return 640 if (t % 640 == 0 and t > 640) else 128


def _pack(parts, rows_multiple, dtype):
    flat = jnp.concatenate([p.reshape(-1).astype(dtype) for p in parts])
    n = flat.shape[0]
    rows = -(-n // LANES)
    rows = -(-rows // rows_multiple) * rows_multiple
    return jnp.pad(flat, (0, rows * LANES - n)).reshape(rows, LANES)


def _pack_rows(parts, rows_multiple, dtype):
    flat = jnp.concatenate([p.reshape(N_DEV, -1).astype(dtype) for p in parts], axis=1)
    n = flat.shape[1]
    rows = -(-n // LANES)
    rows = -(-rows // rows_multiple) * rows_multiple
    return jnp.pad(flat, ((0, 0), (0, rows * LANES - n))).reshape(N_DEV, rows, LANES)


def _unpack(buf, shapes, lead=()):
    flat = buf.reshape(lead + (-1,))
    out, off = [], 0
    for s in shapes:
        n = math.prod(s)
        out.append(flat[..., off:off + n].reshape(lead + tuple(s)))
        off += n
    return out


def _to_shards(full, axis):
    s = full.shape
    return jnp.moveaxis(full.reshape(s[:axis] + (N_DEV, s[axis] // N_DEV) + s[axis + 1:]), axis, 0)


def _from_shards(g8, axis):
    m = jnp.moveaxis(g8, 0, axis)
    s = m.shape
    return m.reshape(s[:axis] + (s[axis] * s[axis + 1],) + s[axis + 2:])


def _comm_call(body, name, xs, out_shapes):
    n = len(xs)
    hbm = pl.BlockSpec(memory_space=pltpu.HBM)
    return pl.pallas_call(
        body, name=name, out_shape=out_shapes, in_specs=[hbm] * n, out_specs=[hbm] * n,
        scratch_shapes=[pltpu.SemaphoreType.DMA((n, 7)), pltpu.SemaphoreType.DMA((n, 7)),
                        pltpu.SemaphoreType.DMA((n,))],
    )(*xs)


def _gather_ops(x_refs, out_refs, send_sems, recv_sems, local_sems):
    n = len(x_refs)
    x_, y_, c = lax.axis_index("x"), lax.axis_index("y"), lax.axis_index("c")
    me, sibling = (x_, y_, c), (x_, y_, 1 - c)
    chips = [(1 - x_, y_), (x_, 1 - y_), (1 - x_, 1 - y_)]

    def rows(a, px, py, pc):
        return out_refs[a].at[4 * px + 2 * py + pc]

    def copy(a, k, block, to, src=None):
        return pltpu.make_async_remote_copy(
            src_ref=rows(a, *block) if src is None else src, dst_ref=rows(a, *block),
            send_sem=send_sems.at[a, k], recv_sem=recv_sems.at[a, k], device_id=to, device_id_type=MESH)

    def mine():
        return [pltpu.make_async_copy(x_refs[a], rows(a, *me), local_sems.at[a]) for a in range(n)]

    def first():
        cps = []
        for a in range(n):
            cps.append(copy(a, 0, me, sibling, src=x_refs[a]))
            cps += [copy(a, 1 + j, me, (*chip, c), src=x_refs[a]) for j, chip in enumerate(chips)]
        return cps

    def start():
        for cp in mine() + first():
            cp.start()

    def finish():
        passed = []
        for j, chip in enumerate(chips):
            for a in range(n):
                copy(a, 1 + j, (*chip, c), me).wait_recv()
                passed.append(copy(a, 4 + j, (*chip, c), sibling))
                passed[-1].start()
        for a in range(n):
            copy(a, 0, sibling, me).wait_recv()
            for j, chip in enumerate(chips):
                copy(a, 4 + j, (*chip, 1 - c), me).wait_recv()
        for cp in first() + passed:
            cp.wait_send()
        for cp in mine():
            cp.wait()

    return start, finish


def _exchange_ops(x_refs, out_refs, send_sems, recv_sems, local_sems):
    n = len(x_refs)
    x_, y_, c = lax.axis_index("x"), lax.axis_index("y"), lax.axis_index("c")
    me = 4 * x_ + 2 * y_ + c

    def peer(k):
        px = 1 - x_ if k & 4 else x_
        py = 1 - y_ if k & 2 else y_
        pc = 1 - c if k & 1 else c
        return px, py, pc

    def copy(a, k):
        px, py, pc = peer(k)
        return pltpu.make_async_remote_copy(
            src_ref=x_refs[a].at[4 * px + 2 * py + pc], dst_ref=out_refs[a].at[me],
            send_sem=send_sems.at[a, k - 1], recv_sem=recv_sems.at[a, k - 1], device_id=(px, py, pc),
            device_id_type=MESH)

    def arrival(a, k):
        px, py, pc = peer(k)
        slot = 4 * px + 2 * py + pc
        return pltpu.make_async_remote_copy(
            src_ref=x_refs[a].at[slot], dst_ref=out_refs[a].at[slot],
            send_sem=send_sems.at[a, k - 1], recv_sem=recv_sems.at[a, k - 1], device_id=(px, py, pc),
            device_id_type=MESH)

    def mine():
        return [pltpu.make_async_copy(x_refs[a].at[me], out_refs[a].at[me], local_sems.at[a]) for a in range(n)]

    def sends():
        return [copy(a, k) for k in range(1, N_DEV) for a in range(n)]

    def start():
        for cp in mine() + sends():
            cp.start()

    def finish():
        for k in range(1, N_DEV):
            for a in range(n):
                arrival(a, k).wait_recv()
        for cp in sends():
            cp.wait_send()
        for cp in mine():
            cp.wait()

    return start, finish


def _comm_parts(xs, gather):
    n = len(xs)
    hbm = pl.BlockSpec(memory_space=pltpu.HBM)
    shapes = [jax.ShapeDtypeStruct(((N_DEV,) + x.shape) if gather else x.shape, x.dtype) for x in xs]
    sems = [pltpu.SemaphoreType.DMA((n, 7)), pltpu.SemaphoreType.DMA((n, 7)), pltpu.SemaphoreType.DMA((n,))]
    return [hbm] * n, [hbm] * n, shapes, sems


def _all_gather(xs):
    n = len(xs)

    def body(*refs):
        start, finish = _gather_ops(refs[:n], refs[n:2 * n], *refs[2 * n:])
        start()
        finish()

    return _comm_call(body, "all_gather", xs, [jax.ShapeDtypeStruct((N_DEV,) + x.shape, x.dtype) for x in xs])


def _all_to_all(xs):
    n = len(xs)

    def body(*refs):
        start, finish = _exchange_ops(refs[:n], refs[n:2 * n], *refs[2 * n:])
        start()
        finish()

    return _comm_call(body, "all_to_all", xs, [jax.ShapeDtypeStruct(x.shape, x.dtype) for x in xs])


def _sum_blocks(x, rows_tile):
    _, r, c_ = x.shape

    def body(x_ref, o_ref):
        acc = x_ref[0].astype(F32)
        for d in range(1, N_DEV):
            acc = acc + x_ref[d].astype(F32)
        o_ref[...] = acc

    return pl.pallas_call(
        body, name="sum_blocks", grid=(r // rows_tile,),
        in_specs=[pl.BlockSpec((N_DEV, rows_tile, c_), lambda i: (0, i, 0))],
        out_specs=pl.BlockSpec((rows_tile, c_), lambda i: (i, 0)),
        out_shape=jax.ShapeDtypeStruct((r, c_), F32),
        compiler_params=_params("parallel"),
    )(x)


def _adamw(w, g, m, v, rows_tile):
    r, c_ = w.shape
    c1 = 1.0 - ADAM_B1 ** ADAM_STEP
    c2 = 1.0 - ADAM_B2 ** ADAM_STEP

    def body(w_ref, g_ref, m_ref, v_ref, d_ref, mo_ref, vo_ref):
        g_ = g_ref[...]
        m_ = ADAM_B1 * m_ref[...] + (1.0 - ADAM_B1) * g_
        v_ = ADAM_B2 * v_ref[...] + (1.0 - ADAM_B2) * (g_ * g_)
        m_hat = m_ / c1
        v_hat = v_ / c2
        d_ref[...] = -ADAM_LR * (m_hat / (jnp.sqrt(v_hat) + ADAM_EPS) + ADAM_WD * w_ref[...])
        mo_ref[...] = m_
        vo_ref[...] = v_

    spec = pl.BlockSpec((rows_tile, c_), lambda i: (i, 0))
    shape = jax.ShapeDtypeStruct((r, c_), F32)
    return pl.pallas_call(
        body, name="adamw", grid=(r // rows_tile,), in_specs=[spec] * 4, out_specs=[spec] * 3,
        out_shape=[shape] * 3, compiler_params=_params("parallel"),
    )(w, g, m, v)


def _pick(n, prefs):
    for p in prefs:
        if n % p == 0:
            return p
    return n


def _mat_spec(arr, tr, tc, r_of, c_of):
    if arr.ndim == 2:
        return pl.BlockSpec((tr, tc), lambda i, j, k: (r_of(i, j, k), c_of(i, j, k)))
    per = arr.shape[2] // tc
    return pl.BlockSpec((None, tr, tc), lambda i, j, k: (c_of(i, j, k) // per, r_of(i, j, k), c_of(i, j, k) % per))


def _mm(a, b, *, ta=False, tb=False, out_dtype=F32, out_seg=None, res=None, epi=None, aux=None,
        tm=None, tn=None, tk=None, name="mm"):
    def dims(x):
        return (x.shape[0], x.shape[1]) if x.ndim == 2 else (x.shape[1], x.shape[0] * x.shape[2])
    ar, ac = dims(a)
    br, bc = dims(b)
    m, k = (ac, ar) if ta else (ar, ac)
    n, kb = (br, bc) if tb else (bc, br)
    assert k == kb, (a.shape, b.shape, ta, tb)
    tm = tm or _pick(m, (1024, 512, 384, 256, 128) if ta else (640, 512, 384, 256, 128))
    tn = tn or _pick(n, (1024, 768, 512, 384, 256, 128))
    tk = tk or _pick(k, ((1664,) if ta else (2048,)) + (1024, 768, 640, 512, 384, 256, 128))
    if out_seg:
        assert (n // out_seg) % tn == 0
    for x, t in ((a, tm if ta else tk), (b, tk if tb else tn)):
        if x.ndim == 3:
            assert x.shape[2] % t == 0
    nk = k // tk
    gi, gj, gk = (lambda j, i, kk: i), (lambda j, i, kk: j), (lambda j, i, kk: kk)
    a_spec = _mat_spec(a, tk, tm, gk, gi) if ta else _mat_spec(a, tm, tk, gi, gk)
    b_spec = _mat_spec(b, tn, tk, gj, gk) if tb else _mat_spec(b, tk, tn, gk, gj)
    out_like = jnp.zeros((out_seg, 0, n // out_seg)) if out_seg else jnp.zeros((0, n))
    o_spec = _mat_spec(out_like, tm, tn, gi, gj)
    o_shape = (out_seg, m, n // out_seg) if out_seg else (m, n)
    dn = (((0 if ta else 1,), (1 if tb else 0,)), ((), ()))
    extra = [x for x in (res, aux) if x is not None]
    assert not (res is not None and aux is not None)
    n_out = 2 if epi == "relu2" else 1

    def body(*refs):
        a_ref, b_ref = refs[0], refs[1]
        x_ref = refs[2] if extra else None
        outs = refs[2 + len(extra):2 + len(extra) + n_out]
        acc_ref = refs[-1] if nk > 1 else None
        part = lax.dot_general(a_ref[...], b_ref[...], dn, preferred_element_type=F32)

        def finish(acc):
            if epi == "relu2":
                r = jnp.maximum(acc, 0.0)
                outs[0][...] = r.astype(BF16)
                outs[1][...] = (r * r).astype(BF16)
            elif epi == "relu2_bwd":
                outs[0][...] = (acc * (2.0 * x_ref[...].astype(F32))).astype(out_dtype)
            elif res is not None:
                outs[0][...] = (acc + x_ref[...]).astype(out_dtype)
            else:
                outs[0][...] = acc.astype(out_dtype)

        if nk == 1:
            finish(part)
        else:
            kk = pl.program_id(2)

            @pl.when(kk == 0)
            def _():
                acc_ref[...] = part

            @pl.when(kk > 0)
            def _():
                acc_ref[...] += part

            @pl.when(kk == nk - 1)
            def _():
                finish(acc_ref[...])

    if epi == "relu2":
        out_shape = [jax.ShapeDtypeStruct(o_shape, BF16), jax.ShapeDtypeStruct(o_shape, BF16)]
        out_specs = [o_spec, o_spec]
    else:
        out_shape = jax.ShapeDtypeStruct(o_shape, out_dtype)
        out_specs = o_spec
    x_specs = [pl.BlockSpec((tm, tn), lambda j, i, kk: (i, j))] * len(extra)
    res_ = pl.pallas_call(
        body, name=name, grid=(n // tn, m // tm, nk),
        in_specs=[a_spec, b_spec] + x_specs, out_specs=out_specs, out_shape=out_shape,
        scratch_shapes=[pltpu.VMEM((tm, tn), F32)] if nk > 1 else [],
        compiler_params=_params("parallel", "parallel", "arbitrary"),
    )(a, b, *extra)
    return res_


def _rms_fwd(x, g):
    t, d = x.shape
    tm = _row_tile(t)

    def body(x_ref, g_ref, o_ref):
        x_ = x_ref[...]
        rstd = lax.rsqrt(jnp.mean(x_ * x_, axis=-1, keepdims=True) + EPS)
        o_ref[...] = (x_ * rstd * g_ref[...]).astype(BF16)

    return pl.pallas_call(
        body, name="rms_fwd", grid=(t // tm,),
        in_specs=[pl.BlockSpec((tm, d), lambda i: (i, 0)), pl.BlockSpec((1, d), lambda i: (0, 0))],
        out_specs=pl.BlockSpec((tm, d), lambda i: (i, 0)),
        out_shape=jax.ShapeDtypeStruct((t, d), BF16), compiler_params=_params("parallel"),
    )(x, g.reshape(1, d))


def _rms_bwd(x, g, dy, dres=None, want_bf16=False):
    t, d = x.shape
    tm = _row_tile(t)
    has_res = dres is not None

    def body(*refs):
        x_ref, g_ref, dy_ref = refs[:3]
        r_ref = refs[3] if has_res else None
        outs = refs[3 + has_res:]
        x_ = x_ref[...]
        rstd = lax.rsqrt(jnp.mean(x_ * x_, axis=-1, keepdims=True) + EPS)
        xh = x_ * rstd
        dy_ = dy_ref[...]
        dxh = dy_ * g_ref[...]
        dx = rstd * (dxh - xh * jnp.mean(dxh * xh, axis=-1, keepdims=True))
        if has_res:
            dx = dx + r_ref[...]
        outs[0][...] = dx
        if want_bf16:
            outs[1][...] = dx.astype(BF16)
        dg_ref = outs[-1]

        @pl.when(pl.program_id(0) == 0)
        def _():
            dg_ref[...] = jnp.zeros_like(dg_ref)

        dg_ref[...] += jnp.sum(dy_ * xh, axis=0, keepdims=True)

    row = pl.BlockSpec((tm, d), lambda i: (i, 0))
    vec = pl.BlockSpec((1, d), lambda i: (0, 0))
    out_shape = [jax.ShapeDtypeStruct((t, d), F32)] + ([jax.ShapeDtypeStruct((t, d), BF16)] if want_bf16 else []) \
        + [jax.ShapeDtypeStruct((1, d), F32)]
    out_specs = [row] + ([row] if want_bf16 else []) + [vec]
    return pl.pallas_call(
        body, name="rms_bwd", grid=(t // tm,),
        in_specs=[row, vec, row] + ([row] if has_res else []), out_specs=out_specs, out_shape=out_shape,
        compiler_params=_params("arbitrary"),
    )(x, g.reshape(1, d), dy, *([dres] if has_res else []))


def _swap_rope_halves(y):
    lane = lax.broadcasted_iota(jnp.int32, y.shape, 1)
    half = MLA_ROPE // 2
    swapped = jnp.where(lane < MLA_NOPE + half, pltpu.roll(y, LANES - half, axis=1), pltpu.roll(y, half, axis=1))
    return jnp.where((lane >= MLA_NOPE) & (lane < MLA_QK), swapped, 0.0)


def _head_norm_fwd(x, g, n_valid, scale, rope=None):
    t, hw = x.shape
    w = g.shape[1]
    tm = _row_tile(t)

    def body(*refs):
        x_ref, g_ref = refs[:2]
        o_ref = refs[-1]
        gain = g_ref[...] * scale
        for n in range(hw // w):
            cols = slice(n * w, (n + 1) * w)
            x_ = x_ref[:, cols]
            rstd = lax.rsqrt(jnp.sum(x_ * x_, axis=-1, keepdims=True) * (1.0 / n_valid) + EPS)
            y = x_ * rstd * gain
            if rope is not None:
                y = y * refs[2][...] + _swap_rope_halves(y) * refs[3][...]
            o_ref[:, cols] = y.astype(BF16)

    row = pl.BlockSpec((tm, hw), lambda i: (i, 0))
    tab = pl.BlockSpec((tm, w), lambda i: (i, 0))
    return pl.pallas_call(
        body, name="head_norm_fwd", grid=(t // tm,),
        in_specs=[row, pl.BlockSpec((1, w), lambda i: (0, 0))] + ([tab, tab] if rope is not None else []),
        out_specs=row, out_shape=jax.ShapeDtypeStruct((t, hw), BF16),
        compiler_params=_params("parallel"),
    )(x, g, *(rope if rope is not None else ()))


def _head_norm_bwd(x, g, dout, n_valid, scale, rope=None):
    t, hw = x.shape
    w = g.shape[1]
    tm = _row_tile(t)

    def body(*refs):
        x_ref, g_ref, do_ref = refs[:3]
        dx_ref, dsum_ref, dg_ref = refs[-3:]
        gain = g_ref[...] * scale
        dsum = jnp.zeros((tm, w), F32)
        dg = jnp.zeros((1, w), F32)
        for n in range(hw // w):
            cols = slice(n * w, (n + 1) * w)
            dy = do_ref[:, cols]
            if rope is not None:
                dy = dy * refs[3][...] + _swap_rope_halves(dy * refs[4][...])
            x_ = x_ref[:, cols]
            rstd = lax.rsqrt(jnp.sum(x_ * x_, axis=-1, keepdims=True) * (1.0 / n_valid) + EPS)
            xh = x_ * rstd
            dxh = dy * gain
            dx = rstd * (dxh - xh * (jnp.sum(dxh * xh, axis=-1, keepdims=True) * (1.0 / n_valid)))
            dx_ref[:, cols] = dx
            dsum = dsum + dx
            dg = dg + jnp.sum(dy * xh, axis=0, keepdims=True)
        dsum_ref[...] = dsum

        @pl.when(pl.program_id(0) == 0)
        def _():
            dg_ref[...] = jnp.zeros_like(dg_ref)

        dg_ref[...] += scale * dg

    row = pl.BlockSpec((tm, hw), lambda i: (i, 0))
    tab = pl.BlockSpec((tm, w), lambda i: (i, 0))
    vec = pl.BlockSpec((1, w), lambda i: (0, 0))
    return pl.pallas_call(
        body, name="head_norm_bwd", grid=(t // tm,),
        in_specs=[row, vec, row] + ([tab, tab] if rope is not None else []),
        out_specs=[row, tab, vec],
        out_shape=[jax.ShapeDtypeStruct((t, hw), F32), jax.ShapeDtypeStruct((t, w), F32),
                   jax.ShapeDtypeStruct((1, w), F32)],
        compiler_params=_params("arbitrary"),
    )(x, g, dout, *(rope if rope is not None else ()))


def _tri(n, upper):
    r = lax.broadcasted_iota(jnp.int32, (n, n), 0)
    c = lax.broadcasted_iota(jnp.int32, (n, n), 1)
    return ((r <= c) if upper else (r >= c)).astype(F32)


def _gate_mask(shape, row0):
    lane = lax.broadcasted_iota(jnp.int32, shape, 1)
    row = row0 + lax.broadcasted_iota(jnp.int32, shape, 0)
    return (lane >= TAIL_F) & (lane < TAIL_F + HEADS) & (row >= PAD)


def _gate_fwd(z, bias):
    t = z.shape[0]
    tm = BLOCK
    tail = C_TAIL // LANES

    def body(z_ref, b_ref, o_ref, carry):
        i = pl.program_id(0)

        @pl.when(i == 0)
        def _():
            carry[...] = jnp.zeros_like(carry)

        x_ = z_ref[...] + b_ref[...]
        logf = jnp.minimum(x_, 0.0) - jnp.log1p(jnp.exp(-jnp.abs(x_)))
        logf = jnp.where(_gate_mask(logf.shape, i * tm), logf, 0.0)
        cum = jnp.dot(_tri(tm, False), logf, preferred_element_type=F32, precision=lax.Precision.HIGHEST) + carry[...]
        o_ref[...] = cum
        carry[...] = cum[tm - 1:tm, :]

    return pl.pallas_call(
        body, name="gate_fwd", grid=(t // tm,),
        in_specs=[pl.BlockSpec((tm, LANES), lambda i: (i, tail)), pl.BlockSpec((1, LANES), lambda i: (0, 0))],
        out_specs=pl.BlockSpec((tm, LANES), lambda i: (i, 0)),
        out_shape=jax.ShapeDtypeStruct((t, LANES), F32),
        scratch_shapes=[pltpu.VMEM((1, LANES), F32)], compiler_params=_params("arbitrary"),
    )(z, bias)


def _gate_bwd(z, bias, dcum):
    t = z.shape[0]
    tm = BLOCK
    nb = t // tm
    tail = C_TAIL // LANES

    def body(z_ref, b_ref, d_ref, o_ref, db_ref, carry):
        i = pl.program_id(0)

        @pl.when(i == 0)
        def _():
            carry[...] = jnp.zeros_like(carry)
            db_ref[...] = jnp.zeros_like(db_ref)

        rc = jnp.dot(_tri(tm, True), d_ref[...], preferred_element_type=F32, precision=lax.Precision.HIGHEST) + carry[...]
        carry[...] = rc[0:1, :]
        x_ = z_ref[...] + b_ref[...]
        sig_neg = 1.0 / (1.0 + jnp.exp(x_))
        dl = jnp.where(_gate_mask(rc.shape, (nb - 1 - i) * tm), rc * sig_neg, 0.0)
        o_ref[...] = dl
        db_ref[...] += jnp.sum(dl, axis=0, keepdims=True)

    return pl.pallas_call(
        body, name="gate_bwd", grid=(nb,),
        in_specs=[pl.BlockSpec((tm, LANES), lambda i: (nb - 1 - i, tail)), pl.BlockSpec((1, LANES), lambda i: (0, 0)),
                  pl.BlockSpec((tm, LANES), lambda i: (nb - 1 - i, 0))],
        out_specs=[pl.BlockSpec((tm, LANES), lambda i: (nb - 1 - i, 0)), pl.BlockSpec((1, LANES), lambda i: (0, 0))],
        out_shape=[jax.ShapeDtypeStruct((t, LANES), F32), jax.ShapeDtypeStruct((1, LANES), F32)],
        scratch_shapes=[pltpu.VMEM((1, LANES), F32)], compiler_params=_params("arbitrary"),
    )(z, bias, dcum)


def _pairs(nb, by_query):
    if by_query:
        pr = [(i, j) for i in range(nb) for j in range(i + 1)]
    else:
        pr = [(i, j) for j in range(nb) for i in range(j, nb)]
    return (jnp.asarray(np.array([p[0] for p in pr], np.int32)),
            jnp.asarray(np.array([p[1] for p in pr], np.int32)))


HEADS_PER_STEP = 8


def _mask_scores(s, i, j, tile):
    qp = i * tile + lax.broadcasted_iota(jnp.int32, s.shape, 0)
    kp = j * tile + lax.broadcasted_iota(jnp.int32, s.shape, 1)
    return jnp.where((kp <= qp) & (kp >= PAD), s, NEG)


def _pipelined(n, front, back):
    nxt = front(0)
    for h in range(n):
        cur = nxt
        if h + 1 < n:
            nxt = front(h + 1)
        back(h, cur)


def _nt_dot(a, b):
    return lax.dot_general(a, b, (((1,), (1,)), ((), ())), preferred_element_type=F32)


def _tn_dot(a, b):
    return lax.dot_general(a, b, (((0,), (0,)), ((), ())), preferred_element_type=F32)


def _attn_specs(hb, tile, dk, dv):
    q_of = lambda g, p, it, jt: (it[p], g)
    k_of = lambda g, p, it, jt: (jt[p], g)
    return dict(
        q=pl.BlockSpec((tile, hb * dk), q_of), k=pl.BlockSpec((tile, hb * dk), k_of),
        v=pl.BlockSpec((tile, hb * dv), k_of), ov=pl.BlockSpec((tile, hb * dv), q_of),
        fr=pl.BlockSpec((hb, 1, tile), lambda g, p, it, jt: (g, 0, jt[p])),
        rowq=pl.BlockSpec((hb, 1, tile), lambda g, p, it, jt: (g, 0, it[p])))


def _head(ref, n, d):
    return ref[:, n * d:(n + 1) * d]


def _attn_fwd(q, k, v, key_bias=None, gather=()):
    h = HEADS
    t, dk, dv = q.shape[0], q.shape[1] // h, v.shape[1] // h
    tile = _row_tile(t)
    nb = t // tile
    hb = HEADS_PER_STEP
    biased = key_bias is not None
    it, jt = _pairs(nb, True)
    sp = _attn_specs(hb, tile, dk, dv)

    n_in = 4 if biased else 3
    n_g = len(gather)
    g_in, g_out, g_shapes, g_sems = _comm_parts(gather, True)
    n_steps = int(it.shape[0])

    def body(it_ref, jt_ref, *refs):
        q_ref, k_ref, v_ref = refs[:3]
        b_ref = refs[3] if biased else None
        o_ref, lse_ref = refs[n_in + 2 * n_g:n_in + 2 * n_g + 2]
        m_sc, l_sc, acc_sc = refs[n_in + 2 * n_g + 2:n_in + 2 * n_g + 5]
        p = pl.program_id(1)
        i, j = it_ref[p], jt_ref[p]
        if n_g:
            g_start, g_finish = _gather_ops(refs[n_in:n_in + n_g], refs[n_in + n_g:n_in + 2 * n_g],
                                            *refs[n_in + 2 * n_g + 5:])
            first = (pl.program_id(0) == 0) & (p == 0)
            last = (pl.program_id(0) == h // hb - 1) & (p == n_steps - 1)
            pl.when(first)(g_start)

        @pl.when(j == 0)
        def _():
            m_sc[...] = jnp.full_like(m_sc, NEG)
            l_sc[...] = jnp.zeros_like(l_sc)
            acc_sc[...] = jnp.zeros_like(acc_sc)

        def step(masked):
            def front(n):
                return _nt_dot(_head(q_ref, n, dk), _head(k_ref, n, dk))

            def back(n, s):
                if biased:
                    s = s + b_ref[n]
                if masked:
                    s = _mask_scores(s, i, j, tile)
                m_prev = m_sc[n]
                m_new = jnp.maximum(m_prev, jnp.max(s, axis=-1, keepdims=True))
                alpha = jnp.exp2(m_prev - m_new)
                e = jnp.exp2(s - m_new)
                l_sc[n] = alpha * l_sc[n] + jnp.sum(e, axis=-1, keepdims=True)
                acc_sc[n] = alpha * acc_sc[n] + jnp.dot(e.astype(BF16), _head(v_ref, n, dv),
                                                        preferred_element_type=F32)
                m_sc[n] = m_new

            _pipelined(hb, front, back)

        edge = (j == i) | (j == 0)
        pl.when(edge)(lambda: step(True))
        pl.when(jnp.logical_not(edge))(lambda: step(False))

        @pl.when(j == i)
        def _():
            row = i * tile + lax.broadcasted_iota(jnp.int32, (tile, 1), 0)
            for n in range(hb):
                o_ref[:, n * dv:(n + 1) * dv] = jnp.where(row >= PAD, acc_sc[n] / l_sc[n], 0.0)
                lse_ref[n] = jnp.transpose(m_sc[n] + jnp.log2(l_sc[n]))

        if n_g:
            pl.when(last)(g_finish)

    grid_spec = pltpu.PrefetchScalarGridSpec(
        num_scalar_prefetch=2, grid=(h // hb, n_steps),
        in_specs=[sp["q"], sp["k"], sp["v"]] + ([sp["fr"]] if biased else []) + g_in,
        out_specs=g_out + [sp["ov"], sp["rowq"]],
        scratch_shapes=[pltpu.VMEM((hb, tile, 1), F32), pltpu.VMEM((hb, tile, 1), F32),
                        pltpu.VMEM((hb, tile, dv), F32)] + (g_sems if n_g else []))
    outs = pl.pallas_call(
        body, name="attn_fwd_gather" if n_g else "attn_fwd", grid_spec=grid_spec,
        out_shape=g_shapes + [jax.ShapeDtypeStruct((t, h * dv), F32), jax.ShapeDtypeStruct((h, 1, t), F32)],
        compiler_params=_params("arbitrary" if n_g else "parallel", "arbitrary"),
    )(it, jt, q, k, v, *((key_bias,) if biased else ()), *gather)
    return (outs[n_g], outs[n_g + 1], outs[:n_g]) if n_g else tuple(outs)


LOG2E = 1.4426950408889634
LN2 = 0.6931471805599453
MLA_SCALE = MLA_QK ** -0.5 * LOG2E
FOX_SCALE = FOX_DIM ** -0.5 * LOG2E


def _attn_delta(o, do):
    h = HEADS
    t, dv = o.shape[0], o.shape[1] // h
    tm = _row_tile(t)

    def body(o_ref, do_ref, d_ref):
        prod = o_ref[...] * do_ref[...]
        for n in range(h):
            d_ref[n] = jnp.transpose(jnp.sum(prod[:, n * dv:(n + 1) * dv], axis=-1, keepdims=True) * LN2)

    blk = pl.BlockSpec((tm, h * dv), lambda i: (i, 0))
    return pl.pallas_call(
        body, name="attn_delta", grid=(t // tm,), in_specs=[blk, blk],
        out_specs=pl.BlockSpec((h, 1, tm), lambda i: (0, 0, i)),
        out_shape=jax.ShapeDtypeStruct((h, 1, t), F32), compiler_params=_params("parallel"),
    )(o, do)


BWD_HEADS_PER_STEP = (4, 4)


def _attn_bwd(q, k, v, do, lse_row, delta_row, key_bias=None, exchange=()):
    h = HEADS
    t, dk, dv = q.shape[0], q.shape[1] // h, v.shape[1] // h
    tile = _row_tile(t)
    nb = t // tile
    decay = key_bias is not None
    hb = BWD_HEADS_PER_STEP[int(decay)]
    it, jt = _pairs(nb, False)
    sp = _attn_specs(hb, tile, dk, dv)
    n_in = 7 if decay else 6
    n_out = 5 if decay else 3
    n_x = len(exchange)
    x_in, x_out, x_shapes, x_sems = _comm_parts(exchange, False)
    n_steps = int(it.shape[0])

    def body(it_ref, jt_ref, *refs):
        q_ref, k_ref, v_ref, do_ref, lse_ref, delta_ref = refs[:6]
        b_ref = refs[6] if decay else None
        outs = refs[n_in + n_x:]
        dq_ref, dk_ref, dv_ref = outs[:3]
        rs_ref, ks_ref = (outs[3], outs[4]) if decay else (None, None)
        scratch = outs[n_out + n_x:]
        dq_sc, dk_sc, dv_sc = scratch[:3]
        ks_sc, b_sc = (scratch[3], scratch[4]) if decay else (None, None)
        p = pl.program_id(1)
        i, j = it_ref[p], jt_ref[p]
        if n_x:
            x_start, x_finish = _exchange_ops(refs[n_in:n_in + n_x], outs[n_out:n_out + n_x],
                                              *scratch[5 if decay else 3:])
            first = (pl.program_id(0) == 0) & (p == 0)
            last = (pl.program_id(0) == h // hb - 1) & (p == n_steps - 1)
            pl.when(first)(x_start)

        @pl.when(p == 0)
        def _():
            dq_sc[...] = jnp.zeros_like(dq_sc)
            if decay:
                rs_ref[...] = jnp.zeros_like(rs_ref)

        @pl.when(i == j)
        def _():
            dk_sc[...] = jnp.zeros_like(dk_sc)
            dv_sc[...] = jnp.zeros_like(dv_sc)
            if decay:
                ks_sc[...] = jnp.zeros_like(ks_sc)
                for n in range(hb):
                    b_sc[n] = jnp.transpose(b_ref[n])

        def step(masked):
            def front(n):
                return (_nt_dot(_head(k_ref, n, dk), _head(q_ref, n, dk)),
                        _nt_dot(_head(v_ref, n, dv), (_head(do_ref, n, dv) * LN2).astype(BF16)))

            def back(n, s_dp):
                s, dp = s_dp
                if decay:
                    s = s + b_sc[n]
                if masked:
                    kp = j * tile + lax.broadcasted_iota(jnp.int32, s.shape, 0)
                    qp = i * tile + lax.broadcasted_iota(jnp.int32, s.shape, 1)
                    s = jnp.where((kp <= qp) & (kp >= PAD), s, NEG)
                pr = jnp.exp2(s - lse_ref[n])
                ds = pr * (dp - delta_ref[n])
                ds_b = ds.astype(BF16)
                dv_sc[n] += jnp.dot(pr.astype(BF16), _head(do_ref, n, dv).astype(BF16), preferred_element_type=F32)
                dk_sc[n] += jnp.dot(ds_b, _head(q_ref, n, dk), preferred_element_type=F32)
                dq_sc[n, i] += _tn_dot(ds_b, _head(k_ref, n, dk))
                if decay:
                    rs_ref[n, i] += jnp.sum(ds, axis=0, keepdims=True)
                    ks_sc[n] += jnp.sum(ds, axis=-1, keepdims=True)

            _pipelined(hb, front, back)

        edge = (j == i) | (j == 0)
        pl.when(edge)(lambda: step(True))
        pl.when(jnp.logical_not(edge))(lambda: step(False))

        @pl.when(i == j)
        def _():
            for n in range(hb):
                dq_ref[:, n * dk:(n + 1) * dk] = dq_sc[n, j]

        @pl.when(i == nb - 1)
        def _():
            for n in range(hb):
                dk_ref[:, n * dk:(n + 1) * dk] = dk_sc[n]
                dv_ref[:, n * dv:(n + 1) * dv] = dv_sc[n]
                if decay:
                    ks_ref[n] = jnp.transpose(ks_sc[n])

        if n_x:
            pl.when(last)(x_finish)

    rows_out = pl.BlockSpec((hb, nb, 1, tile), lambda hh, p, it, jt: (hh, 0, 0, 0))
    grid_spec = pltpu.PrefetchScalarGridSpec(
        num_scalar_prefetch=2, grid=(h // hb, n_steps),
        in_specs=[sp["q"], sp["k"], sp["v"], sp["ov"], sp["rowq"], sp["rowq"]] + ([sp["fr"]] if decay else [])
        + x_in,
        out_specs=[sp["k"], sp["k"], sp["v"]] + ([rows_out, sp["fr"]] if decay else []) + x_out,
        scratch_shapes=[pltpu.VMEM((hb, nb, tile, dk), F32), pltpu.VMEM((hb, tile, dk), F32),
                        pltpu.VMEM((hb, tile, dv), F32)] + ([pltpu.VMEM((hb, tile, 1), F32)] * 2 if decay else [])
        + (x_sems if n_x else []))
    out_shape = [jax.ShapeDtypeStruct((t, h * dk), F32), jax.ShapeDtypeStruct((t, h * dk), F32),
                 jax.ShapeDtypeStruct((t, h * dv), F32)] \
        + ([jax.ShapeDtypeStruct((h, nb, 1, tile), F32), jax.ShapeDtypeStruct((h, 1, t), F32)] if decay else []) \
        + x_shapes
    outs = pl.pallas_call(
        body, name="attn_bwd_exchange" if n_x else "attn_bwd", grid_spec=grid_spec, out_shape=out_shape,
        compiler_params=_params("arbitrary" if n_x else "parallel", "arbitrary"),
    )(it, jt, q, k, v, do, lse_row, delta_row, *((key_bias,) if decay else ()), *exchange)
    return tuple(outs[:n_out]) + ((list(outs[n_out:]),) if n_x else ())


CONV_COLS = 512


def _shift_down(g, prev, n):
    out = pltpu.roll(g, n, axis=0)
    row = lax.broadcasted_iota(jnp.int32, g.shape, 0)
    for r in range(n):
        out = jnp.where(row == r, prev[8 - n + r:8 - n + r + 1, :], out)
    return out


def _shift_up(g, nxt, n):
    tm = g.shape[0]
    out = pltpu.roll(g, tm - n, axis=0)
    row = lax.broadcasted_iota(jnp.int32, g.shape, 0)
    for r in range(n):
        out = jnp.where(row == tm - n + r, nxt[r:r + 1, :], out)
    return out


def _conv_fwd(z3, w):
    _, t, d = z3.shape
    tm = _row_tile(t)
    tc = CONV_COLS

    def body(z_ref, zp_ref, w_ref, o_ref):
        i = pl.program_id(1)
        g = z_ref[1] * z_ref[2]
        gp = jnp.where(i > 0, zp_ref[1] * zp_ref[2], 0.0)
        w_ = w_ref[...]
        y = w_[2:3] * g + w_[1:2] * _shift_down(g, gp, 1) + w_[0:1] * _shift_down(g, gp, 2)
        o_ref[...] = (z_ref[0] * y).astype(BF16)

    return pl.pallas_call(
        body, name="conv_fwd", grid=(d // tc, t // tm),
        in_specs=[pl.BlockSpec((3, tm, tc), lambda j, i: (0, i, j)),
                  pl.BlockSpec((3, 8, tc), lambda j, i: (0, jnp.maximum(i * (tm // 8) - 1, 0), j)),
                  pl.BlockSpec((3, tc), lambda j, i: (0, j))],
        out_specs=pl.BlockSpec((tm, tc), lambda j, i: (i, j)),
        out_shape=jax.ShapeDtypeStruct((t, d), BF16), compiler_params=_params("parallel", "parallel"),
    )(z3, z3, w)


def _conv_bwd(z3, w, dyb):
    _, t, d = z3.shape
    tm = _row_tile(t)
    tc = CONV_COLS
    ni = t // tm

    def body(z_ref, zp_ref, zn_ref, d_ref, dn_ref, w_ref, dz_ref, dw_ref):
        i = pl.program_id(1)
        gb, gc, u = z_ref[0], z_ref[1], z_ref[2]
        g = gc * u
        gp = jnp.where(i > 0, zp_ref[1] * zp_ref[2], 0.0)
        w_ = w_ref[...]
        g1, g2 = _shift_down(g, gp, 1), _shift_down(g, gp, 2)
        y = w_[2:3] * g + w_[1:2] * g1 + w_[0:1] * g2
        dyb_ = d_ref[...]
        dy = dyb_ * gb
        dyn = jnp.where(i < ni - 1, dn_ref[...] * zn_ref[0], 0.0)
        dg = w_[2:3] * dy + w_[1:2] * _shift_up(dy, dyn, 1) + w_[0:1] * _shift_up(dy, dyn, 2)
        dz_ref[0] = (dyb_ * y).astype(BF16)
        dz_ref[1] = (dg * u).astype(BF16)
        dz_ref[2] = (dg * gc).astype(BF16)

        @pl.when(i == 0)
        def _():
            dw_ref[...] = jnp.zeros_like(dw_ref)

        dw_ref[...] += jnp.concatenate([jnp.sum(dy * g2, axis=0, keepdims=True),
                                        jnp.sum(dy * g1, axis=0, keepdims=True),
                                        jnp.sum(dy * g, axis=0, keepdims=True)], axis=0)

    cur = pl.BlockSpec((3, tm, tc), lambda j, i: (0, i, j))
    return pl.pallas_call(
        body, name="conv_bwd", grid=(d // tc, ni),
        in_specs=[cur,
                  pl.BlockSpec((3, 8, tc), lambda j, i: (0, jnp.maximum(i * (tm // 8) - 1, 0), j)),
                  pl.BlockSpec((3, 8, tc), lambda j, i: (0, jnp.minimum((i + 1) * (tm // 8), t // 8 - 1), j)),
                  pl.BlockSpec((tm, tc), lambda j, i: (i, j)),
                  pl.BlockSpec((8, tc), lambda j, i: (jnp.minimum((i + 1) * (tm // 8), t // 8 - 1), j)),
                  pl.BlockSpec((3, tc), lambda j, i: (0, j))],
        out_specs=[cur, pl.BlockSpec((3, tc), lambda j, i: (0, j))],
        out_shape=[jax.ShapeDtypeStruct((3, t, d), BF16), jax.ShapeDtypeStruct((3, d), F32)],
        compiler_params=_params("parallel", "arbitrary"),
    )(z3, z3, z3, dyb, dyb, w)


def _loss_head(h, target):
    t, d = h.shape
    tm = BLOCK

    def body(h_ref, t_ref, dh_ref, dhb_ref, loss_ref):
        i = pl.program_id(0)

        @pl.when(i == 0)
        def _():
            loss_ref[...] = jnp.zeros_like(loss_ref)

        err = jnp.where(i > 0, h_ref[...] - t_ref[...], 0.0)
        dh = err * (1.0 / d)
        dh_ref[...] = dh
        dhb_ref[...] = dh.astype(BF16)
        loss_ref[...] += 0.5 * jnp.sum(jnp.sum(err * err, axis=-1, keepdims=True) * (1.0 / d), axis=0, keepdims=True)

    row = pl.BlockSpec((tm, d), lambda i: (i, 0))
    return pl.pallas_call(
        body, name="loss_head", grid=(t // tm,),
        in_specs=[row, pl.BlockSpec((tm, d), lambda i: (jnp.maximum(i - 1, 0), 0))],
        out_specs=[row, row, pl.BlockSpec((1, 1), lambda i: (0, 0))],
        out_shape=[jax.ShapeDtypeStruct((t, d), F32), jax.ShapeDtypeStruct((t, d), BF16),
                   jax.ShapeDtypeStruct((1, 1), F32)],
        compiler_params=_params("arbitrary"),
    )(h, target)


def _rope_tables(t):
    pos = jnp.arange(t, dtype=F32) - PAD
    inv_freq = ROPE_BASE ** (-jnp.arange(0, MLA_ROPE, 2, dtype=F32) / MLA_ROPE)
    ang = pos[:, None] * inv_freq[None, :]
    cos, sin = jnp.cos(ang), jnp.sin(ang)
    one, zero = jnp.ones((t, MLA_NOPE), F32), jnp.zeros((t, MLA_NOPE), F32)
    tail = jnp.zeros((t, LANES - MLA_QK), F32)
    return (jnp.concatenate([one, cos, cos, tail], axis=1), jnp.concatenate([zero, -sin, sin, tail], axis=1))


def _pad_lanes(x, width=LANES):
    return jnp.pad(x, [(0, 0)] * (x.ndim - 1) + [(0, width - x.shape[-1])])


def _permute_in_attn(w):
    return jnp.concatenate([w[:, :640], w[:, 672:2208], w[:, 640:672], w[:, 2208:2216],
                            jnp.zeros((w.shape[0], ATTN_IN_PAD - 2216), w.dtype)], axis=1)


def _unpermute_in_attn(dw):
    return jnp.concatenate([dw[:, :640], dw[:, 2176:2208], dw[:, 640:2176], dw[:, 2208:2216]], axis=1)


def _attn_layer_fwd(hn, wl, rope, gather=()):
    t = hn.shape[0]
    z = _mm(hn, wl["w_in"], name="attn_in")
    cqn = _rms_fwd(z[:, C_CQ:C_CQ + Q_LORA], wl["g_cq"])
    ckvn = _rms_fwd(z[:, C_CKV:C_CKV + KV_LORA], wl["g_ckv"])
    qf = _mm(cqn, wl["w_uq"], name="mla_uq")
    kvf = _mm(ckvn, wl["w_ukv"], name="mla_ukv")
    kv3 = kvf.reshape(t, HEADS, MLA_NOPE + MLA_V)
    xq = _pad_lanes(qf.reshape(t, HEADS, MLA_QK)).reshape(t, HEADS * LANES)
    k_pe = jnp.broadcast_to(z[:, None, C_TAIL:C_TAIL + MLA_ROPE], (t, HEADS, MLA_ROPE))
    xk = _pad_lanes(jnp.concatenate([kv3[:, :, :MLA_NOPE], k_pe], axis=-1)).reshape(t, HEADS * LANES)
    v_mla = kv3[:, :, MLA_NOPE:].reshape(t, HEADS * MLA_V).astype(BF16)
    gq, gk = _pad_lanes(wl["g_q_mla"].reshape(1, -1)), _pad_lanes(wl["g_k_mla"].reshape(1, -1))
    q_mla = _head_norm_fwd(xq, gq, MLA_QK, MLA_SCALE, rope)
    k_mla = _head_norm_fwd(xk, gk, MLA_QK, 1.0, rope)
    o_mla, lse_mla, *gathered = _attn_fwd(q_mla, k_mla, v_mla, gather=gather)
    xfq = z[:, C_FQ:C_FQ + HEADS * FOX_DIM]
    xfk = z[:, C_FK:C_FK + HEADS * FOX_DIM]
    v_fox = z[:, C_FV:C_FV + HEADS * FOX_DIM].astype(BF16)
    bias = jnp.pad(wl["b_forget"].reshape(1, -1), ((0, 0), (TAIL_F, LANES - TAIL_F - HEADS)))
    cum = _gate_fwd(z, bias)
    neg_f = (-LOG2E * jnp.transpose(cum[:, TAIL_F:TAIL_F + HEADS]))[:, None, :]
    q_fox = _head_norm_fwd(xfq, wl["g_q_fox"].reshape(1, -1), FOX_DIM, FOX_SCALE)
    k_fox = _head_norm_fwd(xfk, wl["g_k_fox"].reshape(1, -1), FOX_DIM, 1.0)
    o_fox, lse_fox = _attn_fwd(q_fox, k_fox, v_fox, neg_f)
    cat = jnp.concatenate([o_mla, o_fox], axis=1).astype(BF16)
    saved = dict(z=z, cqn=cqn, ckvn=ckvn, xq=xq, xk=xk, v_mla=v_mla, q_mla=q_mla, k_mla=k_mla, o_mla=o_mla,
                 lse_mla=lse_mla, xfq=xfq, xfk=xfk, v_fox=v_fox, q_fox=q_fox, k_fox=k_fox, bias=bias,
                 neg_f=neg_f, o_fox=o_fox, lse_fox=lse_fox, cat=cat, gq=gq, gk=gk)
    return cat, saved, (gathered[0] if gathered else ())


def _attn_layer_bwd(dcat, hn, wl, sv, rope, exchange=()):
    t = hn.shape[0]
    g = {}
    do_mla = dcat[:, :HEADS * MLA_V]
    do_fox = dcat[:, HEADS * MLA_V:]
    qkv = (sv["q_fox"], sv["k_fox"], sv["v_fox"])
    delta = _attn_delta(sv["o_fox"], do_fox)
    dq_fox, dk_fox, dv_fox, row_sums, key_sums, *exchanged = _attn_bwd(
        *qkv, do_fox, sv["lse_fox"], delta, sv["neg_f"], exchange=exchange)
    dcum = jnp.pad(jnp.transpose(LOG2E * (row_sums.reshape(HEADS, t) - key_sums.reshape(HEADS, t))),
                   ((0, 0), (TAIL_F, LANES - TAIL_F - HEADS)))
    dtail_f, dbias = _gate_bwd(sv["z"], sv["bias"], dcum)
    g["b_forget"] = dbias[0, TAIL_F:TAIL_F + HEADS]
    dxfq, _, dgq = _head_norm_bwd(sv["xfq"], wl["g_q_fox"].reshape(1, -1), dq_fox, FOX_DIM, FOX_SCALE)
    dxfk, _, dgk = _head_norm_bwd(sv["xfk"], wl["g_k_fox"].reshape(1, -1), dk_fox, FOX_DIM, 1.0)
    g["g_q_fox"], g["g_k_fox"] = dgq[0], dgk[0]
    qkv = (sv["q_mla"], sv["k_mla"], sv["v_mla"])
    delta = _attn_delta(sv["o_mla"], do_mla)
    dq_mla, dk_mla, dv_mla = _attn_bwd(*qkv, do_mla, sv["lse_mla"], delta)
    dxq, _, dgq = _head_norm_bwd(sv["xq"], sv["gq"], dq_mla, MLA_QK, MLA_SCALE, rope)
    dxk, dxk_sum, dgk = _head_norm_bwd(sv["xk"], sv["gk"], dk_mla, MLA_QK, 1.0, rope)
    g["g_q_mla"], g["g_k_mla"] = dgq[0, :MLA_QK], dgk[0, :MLA_QK]
    dqf = dxq.reshape(t, HEADS, LANES)[:, :, :MLA_QK].reshape(t, HEADS * MLA_QK).astype(BF16)
    dkvf = jnp.concatenate([dxk.reshape(t, HEADS, LANES)[:, :, :MLA_NOPE], dv_mla.reshape(t, HEADS, MLA_V)],
                           axis=-1).reshape(t, HEADS * (MLA_NOPE + MLA_V)).astype(BF16)
    g["w_uq"] = _mm(sv["cqn"], dqf, ta=True, name="d_w_uq")
    g["w_ukv"] = _mm(sv["ckvn"], dkvf, ta=True, name="d_w_ukv")
    dcqn = _mm(dqf, wl["w_uq"], tb=True, name="d_cqn")
    dckvn = _mm(dkvf, wl["w_ukv"], tb=True, name="d_ckvn")
    z = sv["z"]
    dcq, dg_cq = _rms_bwd(z[:, C_CQ:C_CQ + Q_LORA], wl["g_cq"], dcqn)
    dckv, dg_ckv = _rms_bwd(z[:, C_CKV:C_CKV + KV_LORA], wl["g_ckv"], dckvn)
    g["g_cq"], g["g_ckv"] = dg_cq[0], dg_ckv[0]
    tail = jnp.concatenate([dxk_sum[:, MLA_NOPE:MLA_QK], dtail_f[:, TAIL_F:]], axis=1)
    dz = jnp.concatenate([dcq, dckv, dxfq, dxfk, dv_fox, tail], axis=1).astype(BF16)
    g["w_in"] = _mm(hn, dz, ta=True, name="d_w_in_attn")
    dhn = _mm(dz, wl["w_in"], tb=True, name="d_hn_attn")
    return dhn, g, (exchanged[0] if exchanged else ())


def _local_step(x, target, w, gather_late=None, exchange_early=None):
    seq = x.shape[0]
    t = seq + BLOCK
    rope = _rope_tables(t)
    h = jnp.concatenate([jnp.zeros((PAD, D_MODEL), F32), w["meta_tokens"], x], axis=0)
    tape = []
    for layer in range(DEPTH):
        j = layer // 2
        hn = _rms_fwd(h, w["g_mix"][layer])
        if layer % 2 == 0:
            wl = dict(w_in=w["w_in_attn"][j], g_cq=w["g_cq"][j], w_uq=w["w_uq"][j], g_ckv=w["g_ckv"][j],
                      w_ukv=w["w_ukv"][j], g_q_mla=w["g_q_mla"][j], g_k_mla=w["g_k_mla"][j],
                      g_q_fox=w["g_q_fox"][j], g_k_fox=w["g_k_fox"][j], b_forget=w["b_forget"][j])
            hosted = gather_late is not None and layer == 0
            mixed, sv, gathered = _attn_layer_fwd(hn, wl, rope, gather_late[0] if hosted else ())
            if hosted:
                gather_late[1](w, gathered)
            h1 = _mm(mixed, w["w_out_attn"][j], res=h, name="attn_out")
        else:
            wl = None
            z3 = _mm(hn, w["w_in_conv"][j], out_seg=3, name="conv_in")
            mixed = _conv_fwd(z3, w["conv_w"][j])
            sv = dict(z3=z3)
            h1 = _mm(mixed, w["w_out_conv"][j], res=h, name="conv_out")
        hn2 = _rms_fwd(h1, w["g_mlp"][layer])
        u, act = _mm(hn2, w["w_mlp_up"][layer], epi="relu2", name="mlp_up")
        h2 = _mm(act, w["w_mlp_down"][layer], res=h1, name="mlp_down")
        tape.append(dict(h=h, hn=hn, wl=wl, sv=sv, mixed=mixed, h1=h1, hn2=hn2, u=u, act=act))
        h = h2

    dh, dh_b, loss = _loss_head(h, target)
    exchanged = ()
    g = {n: [None] * (DEPTH if n in ("g_mix", "g_mlp", "w_mlp_up", "w_mlp_down") else DEPTH // 2)
         for n in WEIGHTS if n != "meta_tokens"}
    for layer in reversed(range(DEPTH)):
        j = layer // 2
        tp = tape[layer]
        g["w_mlp_down"][layer] = _mm(tp["act"], dh_b, ta=True, name="d_w_down")
        du = _mm(dh_b, w["w_mlp_down"][layer], tb=True, epi="relu2_bwd", aux=tp["u"], out_dtype=BF16, name="d_u")
        g["w_mlp_up"][layer] = _mm(tp["hn2"], du, ta=True, name="d_w_up")
        dhn2 = _mm(du, w["w_mlp_up"][layer], tb=True, name="d_hn2")
        dh1, dh1_b, dg = _rms_bwd(tp["h1"], w["g_mlp"][layer], dhn2, dres=dh, want_bf16=True)
        g["g_mlp"][layer] = dg[0]
        if layer % 2 == 0:
            g["w_out_attn"][j] = _mm(tp["mixed"], dh1_b, ta=True, name="d_w_out_attn")
            dcat = _mm(dh1_b, w["w_out_attn"][j], tb=True, name="d_cat")
            hosted = exchange_early is not None and layer == 0
            dhn, gl, got = _attn_layer_bwd(dcat, tp["hn"], tp["wl"], tp["sv"], rope,
                                           exchange_early(g) if hosted else ())
            if hosted:
                exchanged = got
            g["w_in_attn"][j] = _unpermute_in_attn(gl.pop("w_in"))
            for n, val in gl.items():
                g[n][j] = val
        else:
            g["w_out_conv"][j] = _mm(tp["mixed"], dh1_b, ta=True, name="d_w_out_conv")
            dyb = _mm(dh1_b, w["w_out_conv"][j], tb=True, name="d_yb")
            dz3, dcw = _conv_bwd(tp["sv"]["z3"], w["conv_w"][j], dyb)
            g["conv_w"][j] = dcw
            g["w_in_conv"][j] = _mm(tp["hn"], dz3, ta=True, name="d_w_in_conv")
            dhn = _mm(dz3, w["w_in_conv"][j], tb=True, name="d_hn_conv")
        dh, dh_b, dg = _rms_bwd(tp["h"], w["g_mix"][layer], dhn, dres=dh1, want_bf16=True)
        g["g_mix"][layer] = dg[0]
    g["meta_tokens"] = [dh[PAD:BLOCK]]
    return loss, dh[BLOCK:], g, exchanged


COMM_ROWS = 2048
ADAMW_BLOCK_BYTES = 1 << 20


def kernel(x, meta_tokens, g_mix, g_mlp, w_in_attn, g_cq, w_uq, g_ckv, w_ukv, g_q_mla, g_k_mla, g_q_fox, g_k_fox, b_forget, w_out_attn, w_in_conv, conv_w, w_out_conv, w_mlp_up, w_mlp_down, loss_target, m_meta_tokens, m_g_mix, m_g_mlp, m_w_in_attn, m_g_cq, m_w_uq, m_g_ckv, m_w_ukv, m_g_q_mla, m_g_k_mla, m_g_q_fox, m_g_k_fox, m_b_forget, m_w_out_attn, m_w_in_conv, m_conv_w, m_w_out_conv, m_w_mlp_up, m_w_mlp_down, v_meta_tokens, v_g_mix, v_g_mlp, v_w_in_attn, v_g_cq, v_w_uq, v_g_ckv, v_w_ukv, v_g_q_mla, v_g_k_mla, v_g_q_fox, v_g_k_fox, v_b_forget, v_w_out_attn, v_w_in_conv, v_conv_w, v_w_out_conv, v_w_mlp_up, v_w_mlp_down):
    args = dict(locals())
    local = {n: args[n] for n in WEIGHTS}
    mom = {n: args["m_" + n] for n in WEIGHTS}
    var = {n: args["v_" + n] for n in WEIGHTS}
    axis = dict(SHARDED)
    count = {n: local[n].shape[0] for n, _ in SHARDED if n != "meta_tokens"}
    piece = lambda src, p: src[p[0]] if p[1] is None else src[p[0]][p[1]]
    piece_axis = lambda p: axis[p[0]] - (0 if p[1] is None else 1)
    shape_of = lambda p: piece(local, p).shape
    layers = lambda n, ls: [(n, l) for l in ls]
    first_bf = [("w_uq", 0), ("w_ukv", 0), ("w_out_attn", 0), ("w_mlp_up", 0), ("w_mlp_down", 0)]
    first_f32 = [("meta_tokens", None), ("conv_w", 0), ("conv_w", 1)]
    late_bf = ([("w_uq", 1), ("w_ukv", 1), ("w_out_attn", 1)] + layers("w_in_conv", (0, 1))
               + layers("w_out_conv", (0, 1)) + layers("w_mlp_up", (1, 2, 3)) + layers("w_mlp_down", (1, 2, 3)))
    early_bf = ([("w_uq", 1), ("w_ukv", 1)] + layers("w_out_attn", (0, 1)) + layers("w_in_conv", (0, 1))
                + layers("conv_w", (0, 1)) + layers("w_out_conv", (0, 1)) + layers("w_mlp_up", range(DEPTH))
                + layers("w_mlp_down", range(DEPTH)))
    last_bf = [("w_uq", 0), ("w_ukv", 0), ("meta_tokens", None)]

    w = {n: local[n] for n in REPLICATED}
    w.update({n: [None] * c for n, c in count.items()})

    def install(w, pieces, gathered, in_layer, gathered_in):
        for p, blocks in zip(pieces, _unpack(gathered, [shape_of(p) for p in pieces], (N_DEV,))):
            value = _from_shards(blocks, piece_axis(p))
            if p[1] is None:
                w[p[0]] = value
            else:
                w[p[0]][p[1]] = value
        w["w_in_attn"][in_layer] = _permute_in_attn(_from_shards(gathered_in, piece_axis(("w_in_attn", 0))))

    got_bf, got_in, got_f32 = _all_gather([
        _pack([piece(local, p) for p in first_bf], 16, BF16), local["w_in_attn"][0].astype(BF16),
        _pack([piece(local, p) for p in first_f32], 8, F32)])
    install(w, first_bf, got_bf, 0, got_in)
    for p, blocks in zip(first_f32, _unpack(got_f32, [shape_of(p) for p in first_f32], (N_DEV,))):
        if p[1] is None:
            w[p[0]] = _from_shards(blocks, piece_axis(p))
        else:
            w[p[0]][p[1]] = _from_shards(blocks, piece_axis(p))
    gather_late = ([_pack([piece(local, p) for p in late_bf], 16, BF16), local["w_in_attn"][1].astype(BF16)],
                   lambda w_, got: install(w_, late_bf, got[0], 1, got[1]))

    def pack_grads(g, pieces, in_layer):
        sent = _pack_rows([_to_shards(piece(g, p), piece_axis(p)) for p in pieces], COMM_ROWS, BF16)
        return [sent, _to_shards(g["w_in_attn"][in_layer], piece_axis(("w_in_attn", 0))).astype(BF16)]

    loss_part, dx, grads, early = _local_step(x[0], loss_target[0], w, gather_late,
                                              lambda g: pack_grads(g, early_bf, 1))

    grads["meta_tokens"] = grads["meta_tokens"][0]
    last = _all_to_all(pack_grads(grads, last_bf, 0))
    g_piece = {}
    for pieces, (got, got_in), in_layer in ((early_bf, early, 1), (last_bf, last, 0)):
        summed = _unpack(_sum_blocks(got, COMM_ROWS), [shape_of(p) for p in pieces])
        g_piece.update(zip(pieces, summed))
        g_piece[("w_in_attn", in_layer)] = _sum_blocks(got_in, 256)
    g_local = {n: jnp.stack([g_piece[(n, l)] for l in range(c)]) for n, c in count.items()}
    g_local["meta_tokens"] = g_piece[("meta_tokens", None)]
    rep, = _all_gather([_pack([jnp.stack(grads[n]) for n in REPLICATED] + [loss_part], 8, F32)])
    g_rep = _sum_blocks(rep, rep.shape[1])
    *g_reps, loss = _unpack(g_rep, [local[n].shape for n in REPLICATED] + [()])
    g_local.update(zip(REPLICATED, g_reps))

    def flat(src, names, rows_multiple):
        return _pack([src[n] for n in names], rows_multiple, F32)

    upd = {}
    rows = g_rep.shape[0]
    outs = _adamw(flat(local, REPLICATED, rows), g_rep, flat(mom, REPLICATED, rows), flat(var, REPLICATED, rows), rows)
    for kind, buf in zip(("delta", "m", "v"), outs):
        upd.update({(kind, n): a for n, a in zip(REPLICATED, _unpack(buf, [local[n].shape for n in REPLICATED]))})
    for n, _ in SHARDED:
        as_rows = lambda a: a.reshape(-1, a.shape[-1])
        n_rows, n_cols = as_rows(local[n]).shape
        tiles = [r for r in (1024, 512, 256, 128, 64, 32, 16, 8) if r * n_cols * 4 <= ADAMW_BLOCK_BYTES]
        outs = _adamw(as_rows(local[n]), as_rows(g_local[n]), as_rows(mom[n]), as_rows(var[n]), _pick(n_rows, tiles))
        for kind, buf in zip(("delta", "m", "v"), outs):
            upd[(kind, n)] = buf.reshape(local[n].shape)

    return (loss, dx[None], *[g_local[n] for n in WEIGHTS], *[upd[("delta", n)] for n in WEIGHTS],
            *[upd[("m", n)] for n in WEIGHTS], *[upd[("v", n)] for n in WEIGHTS])
```

```python
import functools
import math

import jax
import jax.numpy as jnp
import numpy as np
from jax import lax
from jax.experimental import pallas as pl
from jax.experimental.pallas import tpu as pltpu

F32 = jnp.float32
BF16 = jnp.bfloat16

N_DEV = 8
D_MODEL = 1024
DEPTH = 4
N_META = 16
BLOCK = 128
PAD = BLOCK - N_META
HEADS = 8
MLA_NOPE = 64
MLA_ROPE = 32
MLA_QK = MLA_NOPE + MLA_ROPE
MLA_V = 64
Q_LORA = 384
KV_LORA = 256
ROPE_BASE = 10000.0
FOX_DIM = 64
D_FF = 4 * D_MODEL
EPS = 1e-6
NEG = -1e30
LANES = 128
ATTN_IN_PAD = 2304
C_CQ, C_CKV, C_FQ, C_FK, C_FV, C_TAIL = 0, 384, 640, 1152, 1664, 2176
TAIL_F = MLA_ROPE

ADAM_LR = 0.001
ADAM_B1 = 0.9
ADAM_B2 = 0.999
ADAM_EPS = 1e-08
ADAM_WD = 0.01
ADAM_STEP = 10

VMEM_LIMIT_BYTES = 48 * 1024 * 1024
MM_VMEM_BUDGET_BYTES = 28 * 1024 * 1024
MESH = pl.DeviceIdType.MESH

SHARDED = (
    ("meta_tokens", 1), ("w_in_attn", 2), ("w_uq", 2), ("w_ukv", 2), ("w_out_attn", 1),
    ("w_in_conv", 2), ("conv_w", 2), ("w_out_conv", 1), ("w_mlp_up", 2), ("w_mlp_down", 1))
F32_GATHERED = ("meta_tokens", "conv_w")
REPLICATED = ("g_mix", "g_mlp", "g_cq", "g_ckv", "g_q_mla", "g_k_mla", "g_q_fox", "g_k_fox", "b_forget")
WEIGHTS = ("meta_tokens", "g_mix", "g_mlp", "w_in_attn", "g_cq", "w_uq", "g_ckv", "w_ukv", "g_q_mla",
           "g_k_mla", "g_q_fox", "g_k_fox", "b_forget", "w_out_attn", "w_in_conv", "conv_w",
           "w_out_conv", "w_mlp_up", "w_mlp_down")


def _params(*sem):
    return pltpu.CompilerParams(dimension_semantics=sem, vmem_limit_bytes=VMEM_LIMIT_BYTES)


def _row_tile(t):
    return 640 if (t % 640 == 0 and t > 640) else 128


def _padded_rows(n, rows_multiple):
    rows = -(-n // LANES)
    return -(-rows // rows_multiple) * rows_multiple


def _pack(parts, rows_multiple, dtype):
    n = sum(p.size for p in parts)
    rows = _padded_rows(n, rows_multiple)
    fill = [jnp.zeros((rows * LANES - n,), dtype)]
    return jnp.concatenate([p.reshape(-1).astype(dtype) for p in parts] + fill).reshape(rows, LANES)


def _pack_rows(parts, rows_multiple, dtype):
    n = sum(p.size for p in parts) // N_DEV
    rows = _padded_rows(n, rows_multiple)
    fill = [jnp.zeros((N_DEV, rows * LANES - n), dtype)]
    flat = jnp.concatenate([p.reshape(N_DEV, -1).astype(dtype) for p in parts] + fill, axis=1)
    return flat.reshape(N_DEV, rows, LANES)


def _unpack(buf, shapes, lead=()):
    flat = buf.reshape(lead + (-1,))
    out, off = [], 0
    for s in shapes:
        n = math.prod(s)
        out.append(flat[..., off:off + n].reshape(lead + tuple(s)))
        off += n
    return out


def _to_shards(full, axis):
    s = full.shape
    return jnp.moveaxis(full.reshape(s[:axis] + (N_DEV, s[axis] // N_DEV) + s[axis + 1:]), axis, 0)


def _from_shards(g8, axis):
    m = jnp.moveaxis(g8, 0, axis)
    s = m.shape
    return m.reshape(s[:axis] + (s[axis] * s[axis + 1],) + s[axis + 2:])


def _comm_call(body, name, xs, out_shapes):
    n = len(xs)
    hbm = pl.BlockSpec(memory_space=pltpu.HBM)
    return pl.pallas_call(
        body, name=name, out_shape=out_shapes, in_specs=[hbm] * n, out_specs=[hbm] * n,
        scratch_shapes=[pltpu.SemaphoreType.DMA((n, 7)), pltpu.SemaphoreType.DMA((n, 7)),
                        pltpu.SemaphoreType.DMA((n,))],
    )(*xs)


def _gather_ops(x_refs, out_refs, send_sems, recv_sems, local_sems):
    n = len(x_refs)
    x_, y_, c = lax.axis_index("x"), lax.axis_index("y"), lax.axis_index("c")
    me, sibling = (x_, y_, c), (x_, y_, 1 - c)
    chips = [(1 - x_, y_), (x_, 1 - y_), (1 - x_, 1 - y_)]

    def rows(a, px, py, pc):
        return out_refs[a].at[4 * px + 2 * py + pc]

    def copy(a, k, block, to, src=None):
        return pltpu.make_async_remote_copy(
            src_ref=rows(a, *block) if src is None else src, dst_ref=rows(a, *block),
            send_sem=send_sems.at[a, k], recv_sem=recv_sems.at[a, k], device_id=to, device_id_type=MESH)

    def mine():
        return [pltpu.make_async_copy(x_refs[a], rows(a, *me), local_sems.at[a]) for a in range(n)]

    def first():
        cps = []
        for a in range(n):
            cps.append(copy(a, 0, me, sibling, src=x_refs[a]))
            cps += [copy(a, 1 + j, me, (*chip, c), src=x_refs[a]) for j, chip in enumerate(chips)]
        return cps

    def start():
        for cp in mine() + first():
            cp.start()

    def finish():
        passed = []
        for j, chip in enumerate(chips):
            for a in range(n):
                copy(a, 1 + j, (*chip, c), me).wait_recv()
                passed.append(copy(a, 4 + j, (*chip, c), sibling))
                passed[-1].start()
        for a in range(n):
            copy(a, 0, sibling, me).wait_recv()
            for j, chip in enumerate(chips):
                copy(a, 4 + j, (*chip, 1 - c), me).wait_recv()
        for cp in first() + passed:
            cp.wait_send()
        for cp in mine():
            cp.wait()

    return start, finish


def _exchange_ops(x_refs, out_refs, send_sems, recv_sems, local_sems):
    n = len(x_refs)
    x_, y_, c = lax.axis_index("x"), lax.axis_index("y"), lax.axis_index("c")
    me = 4 * x_ + 2 * y_ + c

    def peer(k):
        px = 1 - x_ if k & 4 else x_
        py = 1 - y_ if k & 2 else y_
        pc = 1 - c if k & 1 else c
        return px, py, pc

    def copy(a, k):
        px, py, pc = peer(k)
        return pltpu.make_async_remote_copy(
            src_ref=x_refs[a].at[4 * px + 2 * py + pc], dst_ref=out_refs[a].at[me],
            send_sem=send_sems.at[a, k - 1], recv_sem=recv_sems.at[a, k - 1], device_id=(px, py, pc),
            device_id_type=MESH)

    def arrival(a, k):
        px, py, pc = peer(k)
        slot = 4 * px + 2 * py + pc
        return pltpu.make_async_remote_copy(
            src_ref=x_refs[a].at[slot], dst_ref=out_refs[a].at[slot],
            send_sem=send_sems.at[a, k - 1], recv_sem=recv_sems.at[a, k - 1], device_id=(px, py, pc),
            device_id_type=MESH)

    def mine():
        return [pltpu.make_async_copy(x_refs[a].at[me], out_refs[a].at[me], local_sems.at[a]) for a in range(n)]

    def sends():
        return [copy(a, k) for k in range(1, N_DEV) for a in range(n)]

    def start():
        for cp in mine() + sends():
            cp.start()

    def finish():
        for k in range(1, N_DEV):
            for a in range(n):
                arrival(a, k).wait_recv()
        for cp in sends():
            cp.wait_send()
        for cp in mine():
            cp.wait()

    return start, finish


def _comm_parts(xs, gather):
    n = len(xs)
    hbm = pl.BlockSpec(memory_space=pltpu.HBM)
    shapes = [jax.ShapeDtypeStruct(((N_DEV,) + x.shape) if gather else x.shape, x.dtype) for x in xs]
    sems = [pltpu.SemaphoreType.DMA((n, 7)), pltpu.SemaphoreType.DMA((n, 7)), pltpu.SemaphoreType.DMA((n,))]
    return [hbm] * n, [hbm] * n, shapes, sems


def _all_gather(xs):
    n = len(xs)

    def body(*refs):
        start, finish = _gather_ops(refs[:n], refs[n:2 * n], *refs[2 * n:])
        start()
        finish()

    return _comm_call(body, "all_gather", xs, [jax.ShapeDtypeStruct((N_DEV,) + x.shape, x.dtype) for x in xs])


def _all_to_all(xs):
    n = len(xs)

    def body(*refs):
        start, finish = _exchange_ops(refs[:n], refs[n:2 * n], *refs[2 * n:])
        start()
        finish()

    return _comm_call(body, "all_to_all", xs, [jax.ShapeDtypeStruct(x.shape, x.dtype) for x in xs])


def _sum_blocks(x, rows_tile):
    _, r, c_ = x.shape

    def body(x_ref, o_ref):
        acc = x_ref[0].astype(F32)
        for d in range(1, N_DEV):
            acc = acc + x_ref[d].astype(F32)
        o_ref[...] = acc

    return pl.pallas_call(
        body, name="sum_blocks", grid=(r // rows_tile,),
        in_specs=[pl.BlockSpec((N_DEV, rows_tile, c_), lambda i: (0, i, 0))],
        out_specs=pl.BlockSpec((rows_tile, c_), lambda i: (i, 0)),
        out_shape=jax.ShapeDtypeStruct((r, c_), F32),
        compiler_params=_params("parallel"),
    )(x)


def _adamw(w, g, m, v, rows_tile):
    r, c_ = w.shape
    c1 = 1.0 - ADAM_B1 ** ADAM_STEP
    c2 = 1.0 - ADAM_B2 ** ADAM_STEP

    def body(w_ref, g_ref, m_ref, v_ref, d_ref, mo_ref, vo_ref):
        g_ = g_ref[...]
        m_ = ADAM_B1 * m_ref[...] + (1.0 - ADAM_B1) * g_
        v_ = ADAM_B2 * v_ref[...] + (1.0 - ADAM_B2) * (g_ * g_)
        m_hat = m_ / c1
        v_hat = v_ / c2
        d_ref[...] = -ADAM_LR * (m_hat / (jnp.sqrt(v_hat) + ADAM_EPS) + ADAM_WD * w_ref[...])
        mo_ref[...] = m_
        vo_ref[...] = v_

    spec = pl.BlockSpec((rows_tile, c_), lambda i: (i, 0))
    shape = jax.ShapeDtypeStruct((r, c_), F32)
    return pl.pallas_call(
        body, name="adamw", grid=(r // rows_tile,), in_specs=[spec] * 4, out_specs=[spec] * 3,
        out_shape=[shape] * 3, compiler_params=_params("parallel"),
    )(w, g, m, v)


def _pick(n, prefs):
    for p in prefs:
        if n % p == 0:
            return p
    return n


def _mat_spec(arr, tr, tc, r_of, c_of):
    if arr.ndim == 2:
        return pl.BlockSpec((tr, tc), lambda i, j, k: (r_of(i, j, k), c_of(i, j, k)))
    per = arr.shape[2] // tc
    return pl.BlockSpec((None, tr, tc), lambda i, j, k: (c_of(i, j, k) // per, r_of(i, j, k), c_of(i, j, k) % per))


def _mm(a, b, *, ta=False, tb=False, out_dtype=F32, out_seg=None, res=None, epi=None, aux=None,
        tm=None, tn=None, tk=None, name="mm"):
    def dims(x):
        return (x.shape[0], x.shape[1]) if x.ndim == 2 else (x.shape[1], x.shape[0] * x.shape[2])
    ar, ac = dims(a)
    br, bc = dims(b)
    m, k = (ac, ar) if ta else (ar, ac)
    n, kb = (br, bc) if tb else (bc, br)
    assert k == kb, (a.shape, b.shape, ta, tb)
    tn = tn or _pick(n, (1024, 768, 512, 384, 256, 128))
    tk = tk or _pick(k, ((1664,) if ta else (2048,)) + (1024, 768, 640, 512, 384, 256, 128))

    def vmem_bytes(rows):
        out_bytes = 2 * 2 if epi == "relu2" else jnp.dtype(out_dtype).itemsize
        x_bytes = sum(x.dtype.itemsize for x in (res, aux) if x is not None)
        return 2 * (rows * tk * 2 + tk * tn * 2 + rows * tn * (out_bytes + x_bytes)) + (rows * tn * 4 if k > tk else 0)

    tall = (1664,) if (not ta and m % 1664 == 0 and vmem_bytes(1664) <= MM_VMEM_BUDGET_BYTES) else ()
    tm = tm or _pick(m, (1024, 512, 384, 256, 128) if ta else tall + (640, 512, 384, 256, 128))
    if out_seg:
        assert (n // out_seg) % tn == 0
    for x, t in ((a, tm if ta else tk), (b, tk if tb else tn)):
        if x.ndim == 3:
            assert x.shape[2] % t == 0
    nk = k // tk
    gi, gj, gk = (lambda j, i, kk: i), (lambda j, i, kk: j), (lambda j, i, kk: kk)
    a_spec = _mat_spec(a, tk, tm, gk, gi) if ta else _mat_spec(a, tm, tk, gi, gk)
    b_spec = _mat_spec(b, tn, tk, gj, gk) if tb else _mat_spec(b, tk, tn, gk, gj)
    out_like = jax.ShapeDtypeStruct((out_seg, m, n // out_seg) if out_seg else (m, n), out_dtype)
    o_spec = _mat_spec(out_like, tm, tn, gi, gj)
    o_shape = (out_seg, m, n // out_seg) if out_seg else (m, n)
    dn = (((0 if ta else 1,), (1 if tb else 0,)), ((), ()))
    extra = [x for x in (res, aux) if x is not None]
    assert not (res is not None and aux is not None)
    n_out = 2 if epi == "relu2" else 1

    def body(*refs):
        a_ref, b_ref = refs[0], refs[1]
        x_ref = refs[2] if extra else None
        outs = refs[2 + len(extra):2 + len(extra) + n_out]
        acc_ref = refs[-1] if nk > 1 else None
        part = lax.dot_general(a_ref[...], b_ref[...], dn, preferred_element_type=F32)

        def finish(acc):
            if epi == "relu2":
                r = jnp.maximum(acc, 0.0)
                outs[0][...] = r.astype(BF16)
                outs[1][...] = (r * r).astype(BF16)
            elif epi == "relu2_bwd":
                outs[0][...] = (acc * (2.0 * x_ref[...].astype(F32))).astype(out_dtype)
            elif res is not None:
                outs[0][...] = (acc + x_ref[...]).astype(out_dtype)
            else:
                outs[0][...] = acc.astype(out_dtype)

        if nk == 1:
            finish(part)
        else:
            kk = pl.program_id(2)

            @pl.when(kk == 0)
            def _():
                acc_ref[...] = part

            @pl.when(kk > 0)
            def _():
                acc_ref[...] += part

            @pl.when(kk == nk - 1)
            def _():
                finish(acc_ref[...])

    if epi == "relu2":
        out_shape = [jax.ShapeDtypeStruct(o_shape, BF16), jax.ShapeDtypeStruct(o_shape, BF16)]
        out_specs = [o_spec, o_spec]
    else:
        out_shape = jax.ShapeDtypeStruct(o_shape, out_dtype)
        out_specs = o_spec
    x_specs = [pl.BlockSpec((tm, tn), lambda j, i, kk: (i, j))] * len(extra)
    res_ = pl.pallas_call(
        body, name=name, grid=(n // tn, m // tm, nk),
        in_specs=[a_spec, b_spec] + x_specs, out_specs=out_specs, out_shape=out_shape,
        scratch_shapes=[pltpu.VMEM((tm, tn), F32)] if nk > 1 else [],
        compiler_params=_params("parallel", "parallel", "arbitrary"),
    )(a, b, *extra)
    return res_


def _rms_fwd(x, g):
    t, d = x.shape
    tm = _row_tile(t)

    def body(x_ref, g_ref, o_ref):
        x_ = x_ref[...]
        rstd = lax.rsqrt(jnp.mean(x_ * x_, axis=-1, keepdims=True) + EPS)
        o_ref[...] = (x_ * rstd * g_ref[...]).astype(BF16)

    return pl.pallas_call(
        body, name="rms_fwd", grid=(t // tm,),
        in_specs=[pl.BlockSpec((tm, d), lambda i: (i, 0)), pl.BlockSpec((1, d), lambda i: (0, 0))],
        out_specs=pl.BlockSpec((tm, d), lambda i: (i, 0)),
        out_shape=jax.ShapeDtypeStruct((t, d), BF16), compiler_params=_params("parallel"),
    )(x, g.reshape(1, d))


def _rms_bwd(x, g, dy, dres=None, want_bf16=False):
    t, d = x.shape
    tm = _row_tile(t)
    has_res = dres is not None

    def body(*refs):
        x_ref, g_ref, dy_ref = refs[:3]
        r_ref = refs[3] if has_res else None
        outs = refs[3 + has_res:]
        x_ = x_ref[...]
        rstd = lax.rsqrt(jnp.mean(x_ * x_, axis=-1, keepdims=True) + EPS)
        xh = x_ * rstd
        dy_ = dy_ref[...]
        dxh = dy_ * g_ref[...]
        dx = rstd * (dxh - xh * jnp.mean(dxh * xh, axis=-1, keepdims=True))
        if has_res:
            dx = dx + r_ref[...]
        outs[0][...] = dx
        if want_bf16:
            outs[1][...] = dx.astype(BF16)
        dg_ref = outs[-1]

        @pl.when(pl.program_id(0) == 0)
        def _():
            dg_ref[...] = jnp.zeros_like(dg_ref)

        dg_ref[...] += jnp.sum(dy_ * xh, axis=0, keepdims=True)

    row = pl.BlockSpec((tm, d), lambda i: (i, 0))
    vec = pl.BlockSpec((1, d), lambda i: (0, 0))
    out_shape = [jax.ShapeDtypeStruct((t, d), F32)] + ([jax.ShapeDtypeStruct((t, d), BF16)] if want_bf16 else []) \
        + [jax.ShapeDtypeStruct((1, d), F32)]
    out_specs = [row] + ([row] if want_bf16 else []) + [vec]
    return pl.pallas_call(
        body, name="rms_bwd", grid=(t // tm,),
        in_specs=[row, vec, row] + ([row] if has_res else []), out_specs=out_specs, out_shape=out_shape,
        compiler_params=_params("arbitrary"),
    )(x, g.reshape(1, d), dy, *([dres] if has_res else []))


def _swap_rope_halves(y):
    lane = lax.broadcasted_iota(jnp.int32, y.shape, 1)
    half = MLA_ROPE // 2
    swapped = jnp.where(lane < MLA_NOPE + half, pltpu.roll(y, LANES - half, axis=1), pltpu.roll(y, half, axis=1))
    return jnp.where((lane >= MLA_NOPE) & (lane < MLA_QK), swapped, 0.0)


def _head_norm_fwd(x, g, n_valid, scale, rope=None):
    t, hw = x.shape
    w = g.shape[1]
    tm = _row_tile(t)

    def body(*refs):
        x_ref, g_ref = refs[:2]
        o_ref = refs[-1]
        gain = g_ref[...] * scale
        for n in range(hw // w):
            cols = slice(n * w, (n + 1) * w)
            x_ = x_ref[:, cols]
            rstd = lax.rsqrt(jnp.sum(x_ * x_, axis=-1, keepdims=True) * (1.0 / n_valid) + EPS)
            y = x_ * rstd * gain
            if rope is not None:
                y = y * refs[2][...] + _swap_rope_halves(y) * refs[3][...]
            o_ref[:, cols] = y.astype(BF16)

    row = pl.BlockSpec((tm, hw), lambda i: (i, 0))
    tab = pl.BlockSpec((tm, w), lambda i: (i, 0))
    return pl.pallas_call(
        body, name="head_norm_fwd", grid=(t // tm,),
        in_specs=[row, pl.BlockSpec((1, w), lambda i: (0, 0))] + ([tab, tab] if rope is not None else []),
        out_specs=row, out_shape=jax.ShapeDtypeStruct((t, hw), BF16),
        compiler_params=_params("parallel"),
    )(x, g, *(rope if rope is not None else ()))


def _head_norm_bwd(x, g, dout, n_valid, scale, rope=None):
    t, hw = x.shape
    w = g.shape[1]
    tm = _row_tile(t)

    def body(*refs):
        x_ref, g_ref, do_ref = refs[:3]
        dx_ref, dsum_ref, dg_ref = refs[-3:]
        gain = g_ref[...] * scale
        dsum = jnp.zeros((tm, w), F32)
        dg = jnp.zeros((1, w), F32)
        for n in range(hw // w):
            cols = slice(n * w, (n + 1) * w)
            dy = do_ref[:, cols]
            if rope is not None:
                dy = dy * refs[3][...] + _swap_rope_halves(dy * refs[4][...])
            x_ = x_ref[:, cols]
            rstd = lax.rsqrt(jnp.sum(x_ * x_, axis=-1, keepdims=True) * (1.0 / n_valid) + EPS)
            xh = x_ * rstd
            dxh = dy * gain
            dx = rstd * (dxh - xh * (jnp.sum(dxh * xh, axis=-1, keepdims=True) * (1.0 / n_valid)))
            dx_ref[:, cols] = dx
            dsum = dsum + dx
            dg = dg + jnp.sum(dy * xh, axis=0, keepdims=True)
        dsum_ref[...] = dsum

        @pl.when(pl.program_id(0) == 0)
        def _():
            dg_ref[...] = jnp.zeros_like(dg_ref)

        dg_ref[...] += scale * dg

    row = pl.BlockSpec((tm, hw), lambda i: (i, 0))
    tab = pl.BlockSpec((tm, w), lambda i: (i, 0))
    vec = pl.BlockSpec((1, w), lambda i: (0, 0))
    return pl.pallas_call(
        body, name="head_norm_bwd", grid=(t // tm,),
        in_specs=[row, vec, row] + ([tab, tab] if rope is not None else []),
        out_specs=[row, tab, vec],
        out_shape=[jax.ShapeDtypeStruct((t, hw), F32), jax.ShapeDtypeStruct((t, w), F32),
                   jax.ShapeDtypeStruct((1, w), F32)],
        compiler_params=_params("arbitrary"),
    )(x, g, dout, *(rope if rope is not None else ()))


def _tri(n, upper):
    r = lax.broadcasted_iota(jnp.int32, (n, n), 0)
    c = lax.broadcasted_iota(jnp.int32, (n, n), 1)
    return ((r <= c) if upper else (r >= c)).astype(F32)


def _gate_mask(shape, row0):
    lane = lax.broadcasted_iota(jnp.int32, shape, 1)
    row = row0 + lax.broadcasted_iota(jnp.int32, shape, 0)
    return (lane >= TAIL_F) & (lane < TAIL_F + HEADS) & (row >= PAD)


def _gate_fwd(z, bias):
    t = z.shape[0]
    tm = BLOCK
    tail = C_TAIL // LANES

    def body(z_ref, b_ref, o_ref, carry):
        i = pl.program_id(0)

        @pl.when(i == 0)
        def _():
            carry[...] = jnp.zeros_like(carry)

        x_ = z_ref[...] + b_ref[...]
        logf = jnp.minimum(x_, 0.0) - jnp.log1p(jnp.exp(-jnp.abs(x_)))
        logf = jnp.where(_gate_mask(logf.shape, i * tm), logf, 0.0)
        cum = jnp.dot(_tri(tm, False), logf, preferred_element_type=F32, precision=lax.Precision.HIGHEST) + carry[...]
        o_ref[...] = cum
        carry[...] = cum[tm - 1:tm, :]

    return pl.pallas_call(
        body, name="gate_fwd", grid=(t // tm,),
        in_specs=[pl.BlockSpec((tm, LANES), lambda i: (i, tail)), pl.BlockSpec((1, LANES), lambda i: (0, 0))],
        out_specs=pl.BlockSpec((tm, LANES), lambda i: (i, 0)),
        out_shape=jax.ShapeDtypeStruct((t, LANES), F32),
        scratch_shapes=[pltpu.VMEM((1, LANES), F32)], compiler_params=_params("arbitrary"),
    )(z, bias)


def _gate_bwd(z, bias, dcum):
    t = z.shape[0]
    tm = BLOCK
    nb = t // tm
    tail = C_TAIL // LANES

    def body(z_ref, b_ref, d_ref, o_ref, db_ref, carry):
        i = pl.program_id(0)

        @pl.when(i == 0)
        def _():
            carry[...] = jnp.zeros_like(carry)
            db_ref[...] = jnp.zeros_like(db_ref)

        rc = jnp.dot(_tri(tm, True), d_ref[...], preferred_element_type=F32, precision=lax.Precision.HIGHEST) + carry[...]
        carry[...] = rc[0:1, :]
        x_ = z_ref[...] + b_ref[...]
        sig_neg = 1.0 / (1.0 + jnp.exp(x_))
        dl = jnp.where(_gate_mask(rc.shape, (nb - 1 - i) * tm), rc * sig_neg, 0.0)
        o_ref[...] = dl
        db_ref[...] += jnp.sum(dl, axis=0, keepdims=True)

    return pl.pallas_call(
        body, name="gate_bwd", grid=(nb,),
        in_specs=[pl.BlockSpec((tm, LANES), lambda i: (nb - 1 - i, tail)), pl.BlockSpec((1, LANES), lambda i: (0, 0)),
                  pl.BlockSpec((tm, LANES), lambda i: (nb - 1 - i, 0))],
        out_specs=[pl.BlockSpec((tm, LANES), lambda i: (nb - 1 - i, 0)), pl.BlockSpec((1, LANES), lambda i: (0, 0))],
        out_shape=[jax.ShapeDtypeStruct((t, LANES), F32), jax.ShapeDtypeStruct((1, LANES), F32)],
        scratch_shapes=[pltpu.VMEM((1, LANES), F32)], compiler_params=_params("arbitrary"),
    )(z, bias, dcum)


def _pairs(nb, by_query):
    if by_query:
        pr = [(i, j) for i in range(nb) for j in range(i + 1)]
    else:
        pr = [(i, j) for j in range(nb) for i in range(j, nb)]
    return (jnp.asarray(np.array([p[0] for p in pr], np.int32)),
            jnp.asarray(np.array([p[1] for p in pr], np.int32)))


HEADS_PER_STEP = 8


def _mask_scores(s, i, j, tile):
    qp = i * tile + lax.broadcasted_iota(jnp.int32, s.shape, 0)
    kp = j * tile + lax.broadcasted_iota(jnp.int32, s.shape, 1)
    return jnp.where((kp <= qp) & (kp >= PAD), s, NEG)


def _pipelined(n, front, back):
    nxt = front(0)
    for h in range(n):
        cur = nxt
        if h + 1 < n:
            nxt = front(h + 1)
        back(h, cur)


def _nt_dot(a, b):
    return lax.dot_general(a, b, (((1,), (1,)), ((), ())), preferred_element_type=F32)


def _tn_dot(a, b):
    return lax.dot_general(a, b, (((0,), (0,)), ((), ())), preferred_element_type=F32)


def _attn_specs(hb, tile, dk, dv):
    q_of = lambda g, p, it, jt: (it[p], g)
    k_of = lambda g, p, it, jt: (jt[p], g)
    return dict(
        q=pl.BlockSpec((tile, hb * dk), q_of), k=pl.BlockSpec((tile, hb * dk), k_of),
        v=pl.BlockSpec((tile, hb * dv), k_of), ov=pl.BlockSpec((tile, hb * dv), q_of),
        fr=pl.BlockSpec((hb, 1, tile), lambda g, p, it, jt: (g, 0, jt[p])),
        rowq=pl.BlockSpec((hb, 1, tile), lambda g, p, it, jt: (g, 0, it[p])))


def _head(ref, n, d):
    return ref[:, n * d:(n + 1) * d]


def _attn_fwd(q, k, v, key_bias=None, gather=()):
    h = HEADS
    t, dk, dv = q.shape[0], q.shape[1] // h, v.shape[1] // h
    tile = _row_tile(t)
    nb = t // tile
    hb = HEADS_PER_STEP
    biased = key_bias is not None
    it, jt = _pairs(nb, True)
    sp = _attn_specs(hb, tile, dk, dv)

    n_in = 4 if biased else 3
    n_g = len(gather)
    g_in, g_out, g_shapes, g_sems = _comm_parts(gather, True)
    n_steps = int(it.shape[0])

    def body(it_ref, jt_ref, *refs):
        q_ref, k_ref, v_ref = refs[:3]
        b_ref = refs[3] if biased else None
        o_ref, lse_ref = refs[n_in + 2 * n_g:n_in + 2 * n_g + 2]
        m_sc, l_sc, acc_sc = refs[n_in + 2 * n_g + 2:n_in + 2 * n_g + 5]
        p = pl.program_id(1)
        i, j = it_ref[p], jt_ref[p]
        if n_g:
            g_start, g_finish = _gather_ops(refs[n_in:n_in + n_g], refs[n_in + n_g:n_in + 2 * n_g],
                                            *refs[n_in + 2 * n_g + 5:])
            first = (pl.program_id(0) == 0) & (p == 0)
            last = (pl.program_id(0) == h // hb - 1) & (p == n_steps - 1)
            pl.when(first)(g_start)

        @pl.when(j == 0)
        def _():
            m_sc[...] = jnp.full_like(m_sc, NEG)
            l_sc[...] = jnp.zeros_like(l_sc)
            acc_sc[...] = jnp.zeros_like(acc_sc)

        def step(masked):
            def front(n):
                return _nt_dot(_head(q_ref, n, dk), _head(k_ref, n, dk))

            def back(n, s):
                if biased:
                    s = s + b_ref[n]
                if masked:
                    s = _mask_scores(s, i, j, tile)
                m_prev = m_sc[n]
                m_new = jnp.maximum(m_prev, jnp.max(s, axis=-1, keepdims=True))
                alpha = jnp.exp2(m_prev - m_new)
                e = jnp.exp2(s - m_new)
                l_sc[n] = alpha * l_sc[n] + jnp.sum(e, axis=-1, keepdims=True)
                acc_sc[n] = alpha * acc_sc[n] + jnp.dot(e.astype(BF16), _head(v_ref, n, dv),
                                                        preferred_element_type=F32)
                m_sc[n] = m_new

            _pipelined(hb, front, back)

        edge = (j == i) | (j == 0)
        pl.when(edge)(lambda: step(True))
        pl.when(jnp.logical_not(edge))(lambda: step(False))

        @pl.when(j == i)
        def _():
            row = i * tile + lax.broadcasted_iota(jnp.int32, (tile, 1), 0)
            for n in range(hb):
                o_ref[:, n * dv:(n + 1) * dv] = jnp.where(row >= PAD, acc_sc[n] / l_sc[n], 0.0)
                lse_ref[n] = jnp.transpose(m_sc[n] + jnp.log2(l_sc[n]))

        if n_g:
            pl.when(last)(g_finish)

    grid_spec = pltpu.PrefetchScalarGridSpec(
        num_scalar_prefetch=2, grid=(h // hb, n_steps),
        in_specs=[sp["q"], sp["k"], sp["v"]] + ([sp["fr"]] if biased else []) + g_in,
        out_specs=g_out + [sp["ov"], sp["rowq"]],
        scratch_shapes=[pltpu.VMEM((hb, tile, 1), F32), pltpu.VMEM((hb, tile, 1), F32),
                        pltpu.VMEM((hb, tile, dv), F32)] + (g_sems if n_g else []))
    outs = pl.pallas_call(
        body, name="attn_fwd_gather" if n_g else "attn_fwd", grid_spec=grid_spec,
        out_shape=g_shapes + [jax.ShapeDtypeStruct((t, h * dv), F32), jax.ShapeDtypeStruct((h, 1, t), F32)],
        compiler_params=_params("arbitrary" if n_g else "parallel", "arbitrary"),
    )(it, jt, q, k, v, *((key_bias,) if biased else ()), *gather)
    return (outs[n_g], outs[n_g + 1], outs[:n_g]) if n_g else tuple(outs)


LOG2E = 1.4426950408889634
LN2 = 0.6931471805599453
MLA_SCALE = MLA_QK ** -0.5 * LOG2E
FOX_SCALE = FOX_DIM ** -0.5 * LOG2E


def _attn_delta(o, do):
    h = HEADS
    t, dv = o.shape[0], o.shape[1] // h
    tm = _row_tile(t)

    def body(o_ref, do_ref, d_ref):
        prod = o_ref[...] * do_ref[...]
        for n in range(h):
            d_ref[n] = jnp.transpose(jnp.sum(prod[:, n * dv:(n + 1) * dv], axis=-1, keepdims=True) * LN2)

    blk = pl.BlockSpec((tm, h * dv), lambda i: (i, 0))
    return pl.pallas_call(
        body, name="attn_delta", grid=(t // tm,), in_specs=[blk, blk],
        out_specs=pl.BlockSpec((h, 1, tm), lambda i: (0, 0, i)),
        out_shape=jax.ShapeDtypeStruct((h, 1, t), F32), compiler_params=_params("parallel"),
    )(o, do)


BWD_HEADS_PER_STEP = (4, 4)


def _attn_bwd(q, k, v, do, lse_row, delta_row, key_bias=None, exchange=()):
    h = HEADS
    t, dk, dv = q.shape[0], q.shape[1] // h, v.shape[1] // h
    tile = _row_tile(t)
    nb = t // tile
    decay = key_bias is not None
    hb = BWD_HEADS_PER_STEP[int(decay)]
    it, jt = _pairs(nb, False)
    sp = _attn_specs(hb, tile, dk, dv)
    n_in = 7 if decay else 6
    n_out = 5 if decay else 3
    n_x = len(exchange)
    x_in, x_out, x_shapes, x_sems = _comm_parts(exchange, False)
    n_steps = int(it.shape[0])

    def body(it_ref, jt_ref, *refs):
        q_ref, k_ref, v_ref, do_ref, lse_ref, delta_ref = refs[:6]
        b_ref = refs[6] if decay else None
        outs = refs[n_in + n_x:]
        dq_ref, dk_ref, dv_ref = outs[:3]
        rs_ref, ks_ref = (outs[3], outs[4]) if decay else (None, None)
        scratch = outs[n_out + n_x:]
        dq_sc, dk_sc, dv_sc = scratch[:3]
        ks_sc, b_sc = (scratch[3], scratch[4]) if decay else (None, None)
        p = pl.program_id(1)
        i, j = it_ref[p], jt_ref[p]
        if n_x:
            x_start, x_finish = _exchange_ops(refs[n_in:n_in + n_x], outs[n_out:n_out + n_x],
                                              *scratch[5 if decay else 3:])
            first = (pl.program_id(0) == 0) & (p == 0)
            last = (pl.program_id(0) == h // hb - 1) & (p == n_steps - 1)
            pl.when(first)(x_start)

        @pl.when(p == 0)
        def _():
            dq_sc[...] = jnp.zeros_like(dq_sc)
            if decay:
                rs_ref[...] = jnp.zeros_like(rs_ref)

        @pl.when(i == j)
        def _():
            dk_sc[...] = jnp.zeros_like(dk_sc)
            dv_sc[...] = jnp.zeros_like(dv_sc)
            if decay:
                ks_sc[...] = jnp.zeros_like(ks_sc)
                for n in range(hb):
                    b_sc[n] = jnp.transpose(b_ref[n])

        def step(masked):
            def front(n):
                return (_nt_dot(_head(k_ref, n, dk), _head(q_ref, n, dk)),
                        _nt_dot(_head(v_ref, n, dv), (_head(do_ref, n, dv) * LN2).astype(BF16)))

            def back(n, s_dp):
                s, dp = s_dp
                if decay:
                    s = s + b_sc[n]
                if masked:
                    kp = j * tile + lax.broadcasted_iota(jnp.int32, s.shape, 0)
                    qp = i * tile + lax.broadcasted_iota(jnp.int32, s.shape, 1)
                    s = jnp.where((kp <= qp) & (kp >= PAD), s, NEG)
                pr = jnp.exp2(s - lse_ref[n])
                ds = pr * (dp - delta_ref[n])
                ds_b = ds.astype(BF16)
                dv_sc[n] += jnp.dot(pr.astype(BF16), _head(do_ref, n, dv).astype(BF16), preferred_element_type=F32)
                dk_sc[n] += jnp.dot(ds_b, _head(q_ref, n, dk), preferred_element_type=F32)
                dq_sc[n, i] += _tn_dot(ds_b, _head(k_ref, n, dk))
                if decay:
                    rs_ref[n, i] += jnp.sum(ds, axis=0, keepdims=True)
                    ks_sc[n] += jnp.sum(ds, axis=-1, keepdims=True)

            _pipelined(hb, front, back)

        edge = (j == i) | (j == 0)
        pl.when(edge)(lambda: step(True))
        pl.when(jnp.logical_not(edge))(lambda: step(False))

        @pl.when(i == j)
        def _():
            for n in range(hb):
                dq_ref[:, n * dk:(n + 1) * dk] = dq_sc[n, j]

        @pl.when(i == nb - 1)
        def _():
            for n in range(hb):
                dk_ref[:, n * dk:(n + 1) * dk] = dk_sc[n]
                dv_ref[:, n * dv:(n + 1) * dv] = dv_sc[n]
                if decay:
                    ks_ref[n] = jnp.transpose(ks_sc[n])

        if n_x:
            pl.when(last)(x_finish)

    rows_out = pl.BlockSpec((hb, nb, 1, tile), lambda hh, p, it, jt: (hh, 0, 0, 0))
    grid_spec = pltpu.PrefetchScalarGridSpec(
        num_scalar_prefetch=2, grid=(h // hb, n_steps),
        in_specs=[sp["q"], sp["k"], sp["v"], sp["ov"], sp["rowq"], sp["rowq"]] + ([sp["fr"]] if decay else [])
        + x_in,
        out_specs=[sp["k"], sp["k"], sp["v"]] + ([rows_out, sp["fr"]] if decay else []) + x_out,
        scratch_shapes=[pltpu.VMEM((hb, nb, tile, dk), F32), pltpu.VMEM((hb, tile, dk), F32),
                        pltpu.VMEM((hb, tile, dv), F32)] + ([pltpu.VMEM((hb, tile, 1), F32)] * 2 if decay else [])
        + (x_sems if n_x else []))
    out_shape = [jax.ShapeDtypeStruct((t, h * dk), F32), jax.ShapeDtypeStruct((t, h * dk), F32),
                 jax.ShapeDtypeStruct((t, h * dv), F32)] \
        + ([jax.ShapeDtypeStruct((h, nb, 1, tile), F32), jax.ShapeDtypeStruct((h, 1, t), F32)] if decay else []) \
        + x_shapes
    outs = pl.pallas_call(
        body, name="attn_bwd_exchange" if n_x else "attn_bwd", grid_spec=grid_spec, out_shape=out_shape,
        compiler_params=_params("arbitrary" if n_x else "parallel", "arbitrary"),
    )(it, jt, q, k, v, do, lse_row, delta_row, *((key_bias,) if decay else ()), *exchange)
    return tuple(outs[:n_out]) + ((list(outs[n_out:]),) if n_x else ())


CONV_COLS = 512


def _shift_down(g, prev, n):
    out = pltpu.roll(g, n, axis=0)
    row = lax.broadcasted_iota(jnp.int32, g.shape, 0)
    for r in range(n):
        out = jnp.where(row == r, prev[8 - n + r:8 - n + r + 1, :], out)
    return out


def _shift_up(g, nxt, n):
    tm = g.shape[0]
    out = pltpu.roll(g, tm - n, axis=0)
    row = lax.broadcasted_iota(jnp.int32, g.shape, 0)
    for r in range(n):
        out = jnp.where(row == tm - n + r, nxt[r:r + 1, :], out)
    return out


def _conv_fwd(z3, w):
    _, t, d = z3.shape
    tm = _row_tile(t)
    tc = CONV_COLS

    def body(z_ref, zp_ref, w_ref, o_ref):
        i = pl.program_id(1)
        g = z_ref[1] * z_ref[2]
        gp = jnp.where(i > 0, zp_ref[1] * zp_ref[2], 0.0)
        w_ = w_ref[...]
        y = w_[2:3] * g + w_[1:2] * _shift_down(g, gp, 1) + w_[0:1] * _shift_down(g, gp, 2)
        o_ref[...] = (z_ref[0] * y).astype(BF16)

    return pl.pallas_call(
        body, name="conv_fwd", grid=(d // tc, t // tm),
        in_specs=[pl.BlockSpec((3, tm, tc), lambda j, i: (0, i, j)),
                  pl.BlockSpec((3, 8, tc), lambda j, i: (0, jnp.maximum(i * (tm // 8) - 1, 0), j)),
                  pl.BlockSpec((3, tc), lambda j, i: (0, j))],
        out_specs=pl.BlockSpec((tm, tc), lambda j, i: (i, j)),
        out_shape=jax.ShapeDtypeStruct((t, d), BF16), compiler_params=_params("parallel", "parallel"),
    )(z3, z3, w)


def _conv_bwd(z3, w, dyb):
    _, t, d = z3.shape
    tm = _row_tile(t)
    tc = CONV_COLS
    ni = t // tm

    def body(z_ref, zp_ref, zn_ref, d_ref, dn_ref, w_ref, dz_ref, dw_ref):
        i = pl.program_id(1)
        gb, gc, u = z_ref[0], z_ref[1], z_ref[2]
        g = gc * u
        gp = jnp.where(i > 0, zp_ref[1] * zp_ref[2], 0.0)
        w_ = w_ref[...]
        g1, g2 = _shift_down(g, gp, 1), _shift_down(g, gp, 2)
        y = w_[2:3] * g + w_[1:2] * g1 + w_[0:1] * g2
        dyb_ = d_ref[...]
        dy = dyb_ * gb
        dyn = jnp.where(i < ni - 1, dn_ref[...] * zn_ref[0], 0.0)
        dg = w_[2:3] * dy + w_[1:2] * _shift_up(dy, dyn, 1) + w_[0:1] * _shift_up(dy, dyn, 2)
        dz_ref[0] = (dyb_ * y).astype(BF16)
        dz_ref[1] = (dg * u).astype(BF16)
        dz_ref[2] = (dg * gc).astype(BF16)

        @pl.when(i == 0)
        def _():
            dw_ref[...] = jnp.zeros_like(dw_ref)

        dw_ref[...] += jnp.concatenate([jnp.sum(dy * g2, axis=0, keepdims=True),
                                        jnp.sum(dy * g1, axis=0, keepdims=True),
                                        jnp.sum(dy * g, axis=0, keepdims=True)], axis=0)

    cur = pl.BlockSpec((3, tm, tc), lambda j, i: (0, i, j))
    return pl.pallas_call(
        body, name="conv_bwd", grid=(d // tc, ni),
        in_specs=[cur,
                  pl.BlockSpec((3, 8, tc), lambda j, i: (0, jnp.maximum(i * (tm // 8) - 1, 0), j)),
                  pl.BlockSpec((3, 8, tc), lambda j, i: (0, jnp.minimum((i + 1) * (tm // 8), t // 8 - 1), j)),
                  pl.BlockSpec((tm, tc), lambda j, i: (i, j)),
                  pl.BlockSpec((8, tc), lambda j, i: (jnp.minimum((i + 1) * (tm // 8), t // 8 - 1), j)),
                  pl.BlockSpec((3, tc), lambda j, i: (0, j))],
        out_specs=[cur, pl.BlockSpec((3, tc), lambda j, i: (0, j))],
        out_shape=[jax.ShapeDtypeStruct((3, t, d), BF16), jax.ShapeDtypeStruct((3, d), F32)],
        compiler_params=_params("parallel", "arbitrary"),
    )(z3, z3, z3, dyb, dyb, w)


def _loss_head(h, target):
    t, d = h.shape
    tm = BLOCK

    def body(h_ref, t_ref, dh_ref, dhb_ref, loss_ref):
        i = pl.program_id(0)

        @pl.when(i == 0)
        def _():
            loss_ref[...] = jnp.zeros_like(loss_ref)

        err = jnp.where(i > 0, h_ref[...] - t_ref[...], 0.0)
        dh = err * (1.0 / d)
        dh_ref[...] = dh
        dhb_ref[...] = dh.astype(BF16)
        loss_ref[...] += 0.5 * jnp.sum(jnp.sum(err * err, axis=-1, keepdims=True) * (1.0 / d), axis=0, keepdims=True)

    row = pl.BlockSpec((tm, d), lambda i: (i, 0))
    return pl.pallas_call(
        body, name="loss_head", grid=(t // tm,),
        in_specs=[row, pl.BlockSpec((tm, d), lambda i: (jnp.maximum(i - 1, 0), 0))],
        out_specs=[row, row, pl.BlockSpec((1, 1), lambda i: (0, 0))],
        out_shape=[jax.ShapeDtypeStruct((t, d), F32), jax.ShapeDtypeStruct((t, d), BF16),
                   jax.ShapeDtypeStruct((1, 1), F32)],
        compiler_params=_params("arbitrary"),
    )(h, target)


def _rope_tables(t):
    pos = jnp.arange(t, dtype=F32) - PAD
    inv_freq = ROPE_BASE ** (-jnp.arange(0, MLA_ROPE, 2, dtype=F32) / MLA_ROPE)
    ang = pos[:, None] * inv_freq[None, :]
    cos, sin = jnp.cos(ang), jnp.sin(ang)
    one, zero = jnp.ones((t, MLA_NOPE), F32), jnp.zeros((t, MLA_NOPE), F32)
    tail = jnp.zeros((t, LANES - MLA_QK), F32)
    return (jnp.concatenate([one, cos, cos, tail], axis=1), jnp.concatenate([zero, -sin, sin, tail], axis=1))


def _pad_lanes(x, width=LANES):
    return jnp.pad(x, [(0, 0)] * (x.ndim - 1) + [(0, width - x.shape[-1])])


def _permute_in_attn(w):
    return jnp.concatenate([w[:, :640], w[:, 672:2208], w[:, 640:672], w[:, 2208:2216],
                            jnp.zeros((w.shape[0], ATTN_IN_PAD - 2216), w.dtype)], axis=1)


def _unpermute_in_attn(dw):
    return jnp.concatenate([dw[:, :640], dw[:, 2176:2208], dw[:, 640:2176], dw[:, 2208:2216]], axis=1)


def _attn_layer_fwd(hn, wl, rope, gather=()):
    t = hn.shape[0]
    z = _mm(hn, wl["w_in"], name="attn_in")
    cqn = _rms_fwd(z[:, C_CQ:C_CQ + Q_LORA], wl["g_cq"])
    ckvn = _rms_fwd(z[:, C_CKV:C_CKV + KV_LORA], wl["g_ckv"])
    w_uq = _pad_lanes(wl["w_uq"].reshape(Q_LORA, HEADS, MLA_QK)).reshape(Q_LORA, HEADS * LANES)
    xq = _mm(cqn, w_uq, name="mla_uq")
    kvf = _mm(ckvn, wl["w_ukv"], name="mla_ukv")
    kv3 = kvf.reshape(t, HEADS, MLA_NOPE + MLA_V)
    k_pe = jnp.broadcast_to(z[:, None, C_TAIL:C_TAIL + MLA_ROPE], (t, HEADS, MLA_ROPE))
    xk = _pad_lanes(jnp.concatenate([kv3[:, :, :MLA_NOPE], k_pe], axis=-1)).reshape(t, HEADS * LANES)
    v_mla = kv3[:, :, MLA_NOPE:].reshape(t, HEADS * MLA_V).astype(BF16)
    gq, gk = _pad_lanes(wl["g_q_mla"].reshape(1, -1)), _pad_lanes(wl["g_k_mla"].reshape(1, -1))
    q_mla = _head_norm_fwd(xq, gq, MLA_QK, MLA_SCALE, rope)
    k_mla = _head_norm_fwd(xk, gk, MLA_QK, 1.0, rope)
    o_mla, lse_mla, *gathered = _attn_fwd(q_mla, k_mla, v_mla, gather=gather)
    xfq = z[:, C_FQ:C_FQ + HEADS * FOX_DIM]
    xfk = z[:, C_FK:C_FK + HEADS * FOX_DIM]
    v_fox = z[:, C_FV:C_FV + HEADS * FOX_DIM].astype(BF16)
    bias = jnp.pad(wl["b_forget"].reshape(1, -1), ((0, 0), (TAIL_F, LANES - TAIL_F - HEADS)))
    cum = _gate_fwd(z, bias)
    neg_f = (-LOG2E * jnp.transpose(cum[:, TAIL_F:TAIL_F + HEADS]))[:, None, :]
    q_fox = _head_norm_fwd(xfq, wl["g_q_fox"].reshape(1, -1), FOX_DIM, FOX_SCALE)
    k_fox = _head_norm_fwd(xfk, wl["g_k_fox"].reshape(1, -1), FOX_DIM, 1.0)
    o_fox, lse_fox = _attn_fwd(q_fox, k_fox, v_fox, neg_f)
    cat = jnp.concatenate([o_mla, o_fox], axis=1).astype(BF16)
    saved = dict(z=z, cqn=cqn, ckvn=ckvn, xq=xq, xk=xk, v_mla=v_mla, q_mla=q_mla, k_mla=k_mla, o_mla=o_mla,
                 lse_mla=lse_mla, xfq=xfq, xfk=xfk, v_fox=v_fox, q_fox=q_fox, k_fox=k_fox, bias=bias,
                 neg_f=neg_f, o_fox=o_fox, lse_fox=lse_fox, cat=cat, gq=gq, gk=gk, w_uq=w_uq)
    return cat, saved, (gathered[0] if gathered else ())


def _attn_layer_bwd(dcat, hn, wl, sv, rope, exchange=()):
    t = hn.shape[0]
    g = {}
    do_mla = dcat[:, :HEADS * MLA_V]
    do_fox = dcat[:, HEADS * MLA_V:]
    qkv = (sv["q_fox"], sv["k_fox"], sv["v_fox"])
    delta = _attn_delta(sv["o_fox"], do_fox)
    dq_fox, dk_fox, dv_fox, row_sums, key_sums, *exchanged = _attn_bwd(
        *qkv, do_fox, sv["lse_fox"], delta, sv["neg_f"], exchange=exchange)
    dcum = jnp.pad(jnp.transpose(LOG2E * (row_sums.reshape(HEADS, t) - key_sums.reshape(HEADS, t))),
                   ((0, 0), (TAIL_F, LANES - TAIL_F - HEADS)))
    dtail_f, dbias = _gate_bwd(sv["z"], sv["bias"], dcum)
    g["b_forget"] = dbias[0, TAIL_F:TAIL_F + HEADS]
    dxfq, _, dgq = _head_norm_bwd(sv["xfq"], wl["g_q_fox"].reshape(1, -1), dq_fox, FOX_DIM, FOX_SCALE)
    dxfk, _, dgk = _head_norm_bwd(sv["xfk"], wl["g_k_fox"].reshape(1, -1), dk_fox, FOX_DIM, 1.0)
    g["g_q_fox"], g["g_k_fox"] = dgq[0], dgk[0]
    qkv = (sv["q_mla"], sv["k_mla"], sv["v_mla"])
    delta = _attn_delta(sv["o_mla"], do_mla)
    dq_mla, dk_mla, dv_mla = _attn_bwd(*qkv, do_mla, sv["lse_mla"], delta)
    dxq, _, dgq = _head_norm_bwd(sv["xq"], sv["gq"], dq_mla, MLA_QK, MLA_SCALE, rope)
    dxk, dxk_sum, dgk = _head_norm_bwd(sv["xk"], sv["gk"], dk_mla, MLA_QK, 1.0, rope)
    g["g_q_mla"], g["g_k_mla"] = dgq[0, :MLA_QK], dgk[0, :MLA_QK]
    dqf = dxq.astype(BF16)
    dkvf = jnp.concatenate([dxk.reshape(t, HEADS, LANES)[:, :, :MLA_NOPE], dv_mla.reshape(t, HEADS, MLA_V)],
                           axis=-1).reshape(t, HEADS * (MLA_NOPE + MLA_V)).astype(BF16)
    g["w_uq"] = _mm(sv["cqn"], dqf, ta=True, name="d_w_uq").reshape(Q_LORA, HEADS, LANES)[:, :, :MLA_QK].reshape(
        Q_LORA, HEADS * MLA_QK)
    g["w_ukv"] = _mm(sv["ckvn"], dkvf, ta=True, name="d_w_ukv")
    dcqn = _mm(dqf, sv["w_uq"], tb=True, name="d_cqn")
    dckvn = _mm(dkvf, wl["w_ukv"], tb=True, name="d_ckvn")
    z = sv["z"]
    dcq, dg_cq = _rms_bwd(z[:, C_CQ:C_CQ + Q_LORA], wl["g_cq"], dcqn)
    dckv, dg_ckv = _rms_bwd(z[:, C_CKV:C_CKV + KV_LORA], wl["g_ckv"], dckvn)
    g["g_cq"], g["g_ckv"] = dg_cq[0], dg_ckv[0]
    tail = jnp.concatenate([dxk_sum[:, MLA_NOPE:MLA_QK], dtail_f[:, TAIL_F:]], axis=1)
    dz = jnp.concatenate([dcq, dckv, dxfq, dxfk, dv_fox, tail], axis=1).astype(BF16)
    g["w_in"] = _mm(hn, dz, ta=True, name="d_w_in_attn")
    dhn = _mm(dz, wl["w_in"], tb=True, name="d_hn_attn")
    return dhn, g, (exchanged[0] if exchanged else ())


def _local_step(x, target, w, gather_late=None, exchange_early=None):
    seq = x.shape[0]
    t = seq + BLOCK
    rope = _rope_tables(t)
    h = jnp.concatenate([jnp.zeros((PAD, D_MODEL), F32), w["meta_tokens"], x], axis=0)
    tape = []
    for layer in range(DEPTH):
        j = layer // 2
        hn = _rms_fwd(h, w["g_mix"][layer])
        if layer % 2 == 0:
            wl = dict(w_in=w["w_in_attn"][j], g_cq=w["g_cq"][j], w_uq=w["w_uq"][j], g_ckv=w["g_ckv"][j],
                      w_ukv=w["w_ukv"][j], g_q_mla=w["g_q_mla"][j], g_k_mla=w["g_k_mla"][j],
                      g_q_fox=w["g_q_fox"][j], g_k_fox=w["g_k_fox"][j], b_forget=w["b_forget"][j])
            hosted = gather_late is not None and layer == 0
            mixed, sv, gathered = _attn_layer_fwd(hn, wl, rope, gather_late[0] if hosted else ())
            if hosted:
                gather_late[1](w, gathered)
            h1 = _mm(mixed, w["w_out_attn"][j], res=h, name="attn_out")
        else:
            wl = None
            z3 = _mm(hn, w["w_in_conv"][j], out_seg=3, name="conv_in")
            mixed = _conv_fwd(z3, w["conv_w"][j])
            sv = dict(z3=z3)
            h1 = _mm(mixed, w["w_out_conv"][j], res=h, name="conv_out")
        hn2 = _rms_fwd(h1, w["g_mlp"][layer])
        u, act = _mm(hn2, w["w_mlp_up"][layer], epi="relu2", name="mlp_up")
        h2 = _mm(act, w["w_mlp_down"][layer], res=h1, name="mlp_down")
        tape.append(dict(h=h, hn=hn, wl=wl, sv=sv, mixed=mixed, h1=h1, hn2=hn2, u=u, act=act))
        h = h2

    dh, dh_b, loss = _loss_head(h, target)
    exchanged = ()
    g = {n: [None] * (DEPTH if n in ("g_mix", "g_mlp", "w_mlp_up", "w_mlp_down") else DEPTH // 2)
         for n in WEIGHTS if n != "meta_tokens"}
    for layer in reversed(range(DEPTH)):
        j = layer // 2
        tp = tape[layer]
        g["w_mlp_down"][layer] = _mm(tp["act"], dh_b, ta=True, name="d_w_down")
        du = _mm(dh_b, w["w_mlp_down"][layer], tb=True, epi="relu2_bwd", aux=tp["u"], out_dtype=BF16, name="d_u")
        g["w_mlp_up"][layer] = _mm(tp["hn2"], du, ta=True, name="d_w_up")
        dhn2 = _mm(du, w["w_mlp_up"][layer], tb=True, name="d_hn2")
        dh1, dh1_b, dg = _rms_bwd(tp["h1"], w["g_mlp"][layer], dhn2, dres=dh, want_bf16=True)
        g["g_mlp"][layer] = dg[0]
        if layer % 2 == 0:
            g["w_out_attn"][j] = _mm(tp["mixed"], dh1_b, ta=True, name="d_w_out_attn")
            dcat = _mm(dh1_b, w["w_out_attn"][j], tb=True, name="d_cat")
            hosted = exchange_early is not None and layer == 0
            dhn, gl, got = _attn_layer_bwd(dcat, tp["hn"], tp["wl"], tp["sv"], rope,
                                           exchange_early(g) if hosted else ())
            if hosted:
                exchanged = got
            g["w_in_attn"][j] = _unpermute_in_attn(gl.pop("w_in"))
            for n, val in gl.items():
                g[n][j] = val
        else:
            g["w_out_conv"][j] = _mm(tp["mixed"], dh1_b, ta=True, name="d_w_out_conv")
            dyb = _mm(dh1_b, w["w_out_conv"][j], tb=True, name="d_yb")
            dz3, dcw = _conv_bwd(tp["sv"]["z3"], w["conv_w"][j], dyb)
            g["conv_w"][j] = dcw
            g["w_in_conv"][j] = _mm(tp["hn"], dz3, ta=True, name="d_w_in_conv")
            dhn = _mm(dz3, w["w_in_conv"][j], tb=True, name="d_hn_conv")
        dh, dh_b, dg = _rms_bwd(tp["h"], w["g_mix"][layer], dhn, dres=dh1, want_bf16=True)
        g["g_mix"][layer] = dg[0]
    g["meta_tokens"] = [dh[PAD:BLOCK]]
    return loss, dh[BLOCK:], g, exchanged


COMM_ROWS = 2048
ADAMW_BLOCK_BYTES = 1 << 20


def kernel(x, meta_tokens, g_mix, g_mlp, w_in_attn, g_cq, w_uq, g_ckv, w_ukv, g_q_mla, g_k_mla, g_q_fox, g_k_fox, b_forget, w_out_attn, w_in_conv, conv_w, w_out_conv, w_mlp_up, w_mlp_down, loss_target, m_meta_tokens, m_g_mix, m_g_mlp, m_w_in_attn, m_g_cq, m_w_uq, m_g_ckv, m_w_ukv, m_g_q_mla, m_g_k_mla, m_g_q_fox, m_g_k_fox, m_b_forget, m_w_out_attn, m_w_in_conv, m_conv_w, m_w_out_conv, m_w_mlp_up, m_w_mlp_down, v_meta_tokens, v_g_mix, v_g_mlp, v_w_in_attn, v_g_cq, v_w_uq, v_g_ckv, v_w_ukv, v_g_q_mla, v_g_k_mla, v_g_q_fox, v_g_k_fox, v_b_forget, v_w_out_attn, v_w_in_conv, v_conv_w, v_w_out_conv, v_w_mlp_up, v_w_mlp_down):
    args = dict(locals())
    local = {n: args[n] for n in WEIGHTS}
    mom = {n: args["m_" + n] for n in WEIGHTS}
    var = {n: args["v_" + n] for n in WEIGHTS}
    axis = dict(SHARDED)
    count = {n: local[n].shape[0] for n, _ in SHARDED if n != "meta_tokens"}
    piece = lambda src, p: src[p[0]] if p[1] is None else src[p[0]][p[1]]
    piece_axis = lambda p: axis[p[0]] - (0 if p[1] is None else 1)
    shape_of = lambda p: piece(local, p).shape
    layers = lambda n, ls: [(n, l) for l in ls]
    first_bf = [("w_uq", 0), ("w_ukv", 0), ("w_out_attn", 0), ("w_mlp_up", 0), ("w_mlp_down", 0)]
    first_f32 = [("meta_tokens", None), ("conv_w", 0), ("conv_w", 1)]
    late_bf = ([("w_uq", 1), ("w_ukv", 1), ("w_out_attn", 1)] + layers("w_in_conv", (0, 1))
               + layers("w_out_conv", (0, 1)) + layers("w_mlp_up", (1, 2, 3)) + layers("w_mlp_down", (1, 2, 3)))
    early_bf = ([("w_uq", 1), ("w_ukv", 1)] + layers("w_out_attn", (0, 1)) + layers("w_in_conv", (0, 1))
                + layers("conv_w", (0, 1)) + layers("w_out_conv", (0, 1)) + layers("w_mlp_up", range(DEPTH))
                + layers("w_mlp_down", range(DEPTH)))
    last_bf = [("w_uq", 0), ("w_ukv", 0), ("meta_tokens", None)]

    w = {n: local[n] for n in REPLICATED}
    w.update({n: [None] * c for n, c in count.items()})

    def install(w, pieces, gathered, in_layer, gathered_in):
        for p, blocks in zip(pieces, _unpack(gathered, [shape_of(p) for p in pieces], (N_DEV,))):
            value = _from_shards(blocks, piece_axis(p))
            if p[1] is None:
                w[p[0]] = value
            else:
                w[p[0]][p[1]] = value
        w["w_in_attn"][in_layer] = _permute_in_attn(_from_shards(gathered_in, piece_axis(("w_in_attn", 0))))

    got_bf, got_in, got_f32 = _all_gather([
        _pack([piece(local, p) for p in first_bf], 16, BF16), local["w_in_attn"][0].astype(BF16),
        _pack([piece(local, p) for p in first_f32], 8, F32)])
    install(w, first_bf, got_bf, 0, got_in)
    for p, blocks in zip(first_f32, _unpack(got_f32, [shape_of(p) for p in first_f32], (N_DEV,))):
        if p[1] is None:
            w[p[0]] = _from_shards(blocks, piece_axis(p))
        else:
            w[p[0]][p[1]] = _from_shards(blocks, piece_axis(p))
    gather_late = ([_pack([piece(local, p) for p in late_bf], 16, BF16), local["w_in_attn"][1].astype(BF16)],
                   lambda w_, got: install(w_, late_bf, got[0], 1, got[1]))

    def pack_grads(g, pieces, in_layer):
        sent = _pack_rows([_to_shards(piece(g, p), piece_axis(p)) for p in pieces], COMM_ROWS, BF16)
        return [sent, _to_shards(g["w_in_attn"][in_layer], piece_axis(("w_in_attn", 0))).astype(BF16)]

    loss_part, dx, grads, early = _local_step(x[0], loss_target[0], w, gather_late,
                                              lambda g: pack_grads(g, early_bf, 1))

    grads["meta_tokens"] = grads["meta_tokens"][0]
    last = _all_to_all(pack_grads(grads, last_bf, 0))
    g_piece = {}
    for pieces, (got, got_in), in_layer in ((early_bf, early, 1), (last_bf, last, 0)):
        summed = _unpack(_sum_blocks(got, COMM_ROWS), [shape_of(p) for p in pieces])
        g_piece.update(zip(pieces, summed))
        g_piece[("w_in_attn", in_layer)] = _sum_blocks(got_in, 256)
    g_local = {n: jnp.stack([g_piece[(n, l)] for l in range(c)]) for n, c in count.items()}
    g_local["meta_tokens"] = g_piece[("meta_tokens", None)]
    rep, = _all_gather([_pack([jnp.stack(grads[n]) for n in REPLICATED] + [loss_part], 8, F32)])
    g_rep = _sum_blocks(rep, rep.shape[1])
    *g_reps, loss = _unpack(g_rep, [local[n].shape for n in REPLICATED] + [()])
    g_local.update(zip(REPLICATED, g_reps))

    def flat(src, names, rows_multiple):
        return _pack([src[n] for n in names], rows_multiple, F32)

    upd = {}
    rows = g_rep.shape[0]
    outs = _adamw(flat(local, REPLICATED, rows), g_rep, flat(mom, REPLICATED, rows), flat(var, REPLICATED, rows), rows)
    for kind, buf in zip(("delta", "m", "v"), outs):
        upd.update({(kind, n): a for n, a in zip(REPLICATED, _unpack(buf, [local[n].shape for n in REPLICATED]))})
    for n, _ in SHARDED:
        as_rows = lambda a: a.reshape(-1, a.shape[-1])
        n_rows, n_cols = as_rows(local[n]).shape
        tiles = [r for r in (1024, 512, 256, 128, 64, 32, 16, 8) if r * n_cols * 4 <= ADAMW_BLOCK_BYTES]
        outs = _adamw(as_rows(local[n]), as_rows(g_local[n]), as_rows(mom[n]), as_rows(var[n]), _pick(n_rows, tiles))
        for kind, buf in zip(("delta", "m", "v"), outs):
            upd[(kind, n)] = buf.reshape(local[n].shape)

    return (loss, dx[None], *[g_local[n] for n in WEIGHTS], *[upd[("delta", n)] for n in WEIGHTS],
            *[upd[("m", n)] for n in WEIGHTS], *[upd[("v", n)] for n in WEIGHTS])
```

```python
import functools
import math

import jax
import jax.numpy as jnp
import numpy as np
from jax import lax
from jax.experimental import pallas as pl
from jax.experimental.pallas import tpu as pltpu

F32 = jnp.float32
BF16 = jnp.bfloat16

N_DEV = 8
D_MODEL = 1024
DEPTH = 4
N_META = 16
BLOCK = 128
PAD = BLOCK - N_META
HEADS = 8
MLA_NOPE = 64
MLA_ROPE = 32
MLA_QK = MLA_NOPE + MLA_ROPE
MLA_V = 64
Q_LORA = 384
KV_LORA = 256
ROPE_BASE = 10000.0
FOX_DIM = 64
D_FF = 4 * D_MODEL
EPS = 1e-6
NEG = -1e30
LANES = 128
ATTN_IN_PAD = 2304
C_CQ, C_CKV, C_FQ, C_FK, C_FV, C_TAIL = 0, 384, 640, 1152, 1664, 2176
TAIL_F = MLA_ROPE

ADAM_LR = 0.001
ADAM_B1 = 0.9
ADAM_B2 = 0.999
ADAM_EPS = 1e-08
ADAM_WD = 0.01
ADAM_STEP = 10

VMEM_LIMIT_BYTES = 48 * 1024 * 1024
MM_VMEM_BUDGET_BYTES = 28 * 1024 * 1024
MESH = pl.DeviceIdType.MESH

SHARDED = (
    ("meta_tokens", 1), ("w_in_attn", 2), ("w_uq", 2), ("w_ukv", 2), ("w_out_attn", 1),
    ("w_in_conv", 2), ("conv_w", 2), ("w_out_conv", 1), ("w_mlp_up", 2), ("w_mlp_down", 1))
F32_GATHERED = ("meta_tokens", "conv_w")
REPLICATED = ("g_mix", "g_mlp", "g_cq", "g_ckv", "g_q_mla", "g_k_mla", "g_q_fox", "g_k_fox", "b_forget")
WEIGHTS = ("meta_tokens", "g_mix", "g_mlp", "w_in_attn", "g_cq", "w_uq", "g_ckv", "w_ukv", "g_q_mla",
           "g_k_mla", "g_q_fox", "g_k_fox", "b_forget", "w_out_attn", "w_in_conv", "conv_w",
           "w_out_conv", "w_mlp_up", "w_mlp_down")


def _params(*sem):
    return pltpu.CompilerParams(dimension_semantics=sem, vmem_limit_bytes=VMEM_LIMIT_BYTES)


def _row_tile(t):
    return 640 if (t % 640 == 0 and t > 640) else 128


def _padded_rows(n, rows_multiple):
    rows = -(-n // LANES)
    return -(-rows // rows_multiple) * rows_multiple


def _pack(parts, rows_multiple, dtype):
    n = sum(p.size for p in parts)
    rows = _padded_rows(n, rows_multiple)
    fill = [jnp.zeros((rows * LANES - n,), dtype)]
    return jnp.concatenate([p.reshape(-1).astype(dtype) for p in parts] + fill).reshape(rows, LANES)


def _pack_rows(parts, rows_multiple, dtype):
    n = sum(p.size for p in parts) // N_DEV
    rows = _padded_rows(n, rows_multiple)
    fill = [jnp.zeros((N_DEV, rows * LANES - n), dtype)]
    flat = jnp.concatenate([p.reshape(N_DEV, -1).astype(dtype) for p in parts] + fill, axis=1)
    return flat.reshape(N_DEV, rows, LANES)


def _unpack(buf, shapes, lead=()):
    flat = buf.reshape(lead + (-1,))
    out, off = [], 0
    for s in shapes:
        n = math.prod(s)
        out.append(flat[..., off:off + n].reshape(lead + tuple(s)))
        off += n
    return out


def _to_shards(full, axis):
    s = full.shape
    return jnp.moveaxis(full.reshape(s[:axis] + (N_DEV, s[axis] // N_DEV) + s[axis + 1:]), axis, 0)


def _from_shards(g8, axis):
    m = jnp.moveaxis(g8, 0, axis)
    s = m.shape
    return m.reshape(s[:axis] + (s[axis] * s[axis + 1],) + s[axis + 2:])


def _comm_call(body, name, xs, out_shapes):
    n = len(xs)
    hbm = pl.BlockSpec(memory_space=pltpu.HBM)
    return pl.pallas_call(
        body, name=name, out_shape=out_shapes, in_specs=[hbm] * n, out_specs=[hbm] * n,
        scratch_shapes=[pltpu.SemaphoreType.DMA((n, 7)), pltpu.SemaphoreType.DMA((n, 7)),
                        pltpu.SemaphoreType.DMA((n,))],
    )(*xs)


def _gather_ops(x_refs, out_refs, send_sems, recv_sems, local_sems):
    n = len(x_refs)
    x_, y_, c = lax.axis_index("x"), lax.axis_index("y"), lax.axis_index("c")
    me, sibling = (x_, y_, c), (x_, y_, 1 - c)
    chips = [(1 - x_, y_), (x_, 1 - y_), (1 - x_, 1 - y_)]

    def rows(a, px, py, pc):
        return out_refs[a].at[4 * px + 2 * py + pc]

    def copy(a, k, block, to, src=None):
        return pltpu.make_async_remote_copy(
            src_ref=rows(a, *block) if src is None else src, dst_ref=rows(a, *block),
            send_sem=send_sems.at[a, k], recv_sem=recv_sems.at[a, k], device_id=to, device_id_type=MESH)

    def mine():
        return [pltpu.make_async_copy(x_refs[a], rows(a, *me), local_sems.at[a]) for a in range(n)]

    def first():
        cps = []
        for a in range(n):
            cps.append(copy(a, 0, me, sibling, src=x_refs[a]))
            cps += [copy(a, 1 + j, me, (*chip, c), src=x_refs[a]) for j, chip in enumerate(chips)]
        return cps

    def start():
        for cp in mine() + first():
            cp.start()

    def finish():
        passed = []
        for j, chip in enumerate(chips):
            for a in range(n):
                copy(a, 1 + j, (*chip, c), me).wait_recv()
                passed.append(copy(a, 4 + j, (*chip, c), sibling))
                passed[-1].start()
        for a in range(n):
            copy(a, 0, sibling, me).wait_recv()
            for j, chip in enumerate(chips):
                copy(a, 4 + j, (*chip, 1 - c), me).wait_recv()
        for cp in first() + passed:
            cp.wait_send()
        for cp in mine():
            cp.wait()

    return start, finish


def _exchange_ops(x_refs, out_refs, send_sems, recv_sems, local_sems):
    n = len(x_refs)
    x_, y_, c = lax.axis_index("x"), lax.axis_index("y"), lax.axis_index("c")
    me = 4 * x_ + 2 * y_ + c

    def peer(k):
        px = 1 - x_ if k & 4 else x_
        py = 1 - y_ if k & 2 else y_
        pc = 1 - c if k & 1 else c
        return px, py, pc

    def copy(a, k):
        px, py, pc = peer(k)
        return pltpu.make_async_remote_copy(
            src_ref=x_refs[a].at[4 * px + 2 * py + pc], dst_ref=out_refs[a].at[me],
            send_sem=send_sems.at[a, k - 1], recv_sem=recv_sems.at[a, k - 1], device_id=(px, py, pc),
            device_id_type=MESH)

    def arrival(a, k):
        px, py, pc = peer(k)
        slot = 4 * px + 2 * py + pc
        return pltpu.make_async_remote_copy(
            src_ref=x_refs[a].at[slot], dst_ref=out_refs[a].at[slot],
            send_sem=send_sems.at[a, k - 1], recv_sem=recv_sems.at[a, k - 1], device_id=(px, py, pc),
            device_id_type=MESH)

    def mine():
        return [pltpu.make_async_copy(x_refs[a].at[me], out_refs[a].at[me], local_sems.at[a]) for a in range(n)]

    def sends():
        return [copy(a, k) for k in range(1, N_DEV) for a in range(n)]

    def start():
        for cp in mine() + sends():
            cp.start()

    def finish():
        for k in range(1, N_DEV):
            for a in range(n):
                arrival(a, k).wait_recv()
        for cp in sends():
            cp.wait_send()
        for cp in mine():
            cp.wait()

    return start, finish


def _comm_parts(xs, gather):
    n = len(xs)
    hbm = pl.BlockSpec(memory_space=pltpu.HBM)
    shapes = [jax.ShapeDtypeStruct(((N_DEV,) + x.shape) if gather else x.shape, x.dtype) for x in xs]
    sems = [pltpu.SemaphoreType.DMA((n, 7)), pltpu.SemaphoreType.DMA((n, 7)), pltpu.SemaphoreType.DMA((n,))]
    return [hbm] * n, [hbm] * n, shapes, sems


def _all_gather(xs):
    n = len(xs)

    def body(*refs):
        start, finish = _gather_ops(refs[:n], refs[n:2 * n], *refs[2 * n:])
        start()
        finish()

    return _comm_call(body, "all_gather", xs, [jax.ShapeDtypeStruct((N_DEV,) + x.shape, x.dtype) for x in xs])


def _all_to_all(xs):
    n = len(xs)

    def body(*refs):
        start, finish = _exchange_ops(refs[:n], refs[n:2 * n], *refs[2 * n:])
        start()
        finish()

    return _comm_call(body, "all_to_all", xs, [jax.ShapeDtypeStruct(x.shape, x.dtype) for x in xs])


def _sum_blocks(x, rows_tile):
    _, r, c_ = x.shape

    def body(x_ref, o_ref):
        acc = x_ref[0].astype(F32)
        for d in range(1, N_DEV):
            acc = acc + x_ref[d].astype(F32)
        o_ref[...] = acc

    return pl.pallas_call(
        body, name="sum_blocks", grid=(r // rows_tile,),
        in_specs=[pl.BlockSpec((N_DEV, rows_tile, c_), lambda i: (0, i, 0))],
        out_specs=pl.BlockSpec((rows_tile, c_), lambda i: (i, 0)),
        out_shape=jax.ShapeDtypeStruct((r, c_), F32),
        compiler_params=_params("parallel"),
    )(x)


def _adamw(w, g, m, v, rows_tile):
    r, c_ = w.shape
    c1 = 1.0 - ADAM_B1 ** ADAM_STEP
    c2 = 1.0 - ADAM_B2 ** ADAM_STEP

    def body(w_ref, g_ref, m_ref, v_ref, d_ref, mo_ref, vo_ref):
        g_ = g_ref[...]
        m_ = ADAM_B1 * m_ref[...] + (1.0 - ADAM_B1) * g_
        v_ = ADAM_B2 * v_ref[...] + (1.0 - ADAM_B2) * (g_ * g_)
        m_hat = m_ / c1
        v_hat = v_ / c2
        d_ref[...] = -ADAM_LR * (m_hat / (jnp.sqrt(v_hat) + ADAM_EPS) + ADAM_WD * w_ref[...])
        mo_ref[...] = m_
        vo_ref[...] = v_

    spec = pl.BlockSpec((rows_tile, c_), lambda i: (i, 0))
    shape = jax.ShapeDtypeStruct((r, c_), F32)
    return pl.pallas_call(
        body, name="adamw", grid=(r // rows_tile,), in_specs=[spec] * 4, out_specs=[spec] * 3,
        out_shape=[shape] * 3, compiler_params=_params("parallel"),
    )(w, g, m, v)


def _pick(n, prefs):
    for p in prefs:
        if n % p == 0:
            return p
    return n


def _mat_spec(arr, tr, tc, r_of, c_of):
    if arr.ndim == 2:
        return pl.BlockSpec((tr, tc), lambda i, j, k: (r_of(i, j, k), c_of(i, j, k)))
    per = arr.shape[2] // tc
    return pl.BlockSpec((None, tr, tc), lambda i, j, k: (c_of(i, j, k) // per, r_of(i, j, k), c_of(i, j, k) % per))


def _mm(a, b, *, ta=False, tb=False, out_dtype=F32, out_seg=None, res=None, epi=None, aux=None,
        tm=None, tn=None, tk=None, name="mm"):
    def dims(x):
        return (x.shape[0], x.shape[1]) if x.ndim == 2 else (x.shape[1], x.shape[0] * x.shape[2])
    ar, ac = dims(a)
    br, bc = dims(b)
    m, k = (ac, ar) if ta else (ar, ac)
    n, kb = (br, bc) if tb else (bc, br)
    assert k == kb, (a.shape, b.shape, ta, tb)
    tn = tn or _pick(n, (1024, 768, 512, 384, 256, 128))
    tk = tk or _pick(k, ((1664,) if ta else (2048,)) + (1024, 768, 640, 512, 384, 256, 128))

    def vmem_bytes(rows):
        out_bytes = 2 * 2 if epi == "relu2" else jnp.dtype(out_dtype).itemsize
        x_bytes = sum(x.dtype.itemsize for x in (res, aux) if x is not None)
        return 2 * (rows * tk * 2 + tk * tn * 2 + rows * tn * (out_bytes + x_bytes)) + (rows * tn * 4 if k > tk else 0)

    tall = (1664,) if (not ta and m % 1664 == 0 and vmem_bytes(1664) <= MM_VMEM_BUDGET_BYTES) else ()
    tm = tm or _pick(m, (1024, 512, 384, 256, 128) if ta else tall + (640, 512, 384, 256, 128))
    if out_seg:
        assert (n // out_seg) % tn == 0
    for x, t in ((a, tm if ta else tk), (b, tk if tb else tn)):
        if x.ndim == 3:
            assert x.shape[2] % t == 0
    nk = k // tk
    gi, gj, gk = (lambda j, i, kk: i), (lambda j, i, kk: j), (lambda j, i, kk: kk)
    a_spec = _mat_spec(a, tk, tm, gk, gi) if ta else _mat_spec(a, tm, tk, gi, gk)
    b_spec = _mat_spec(b, tn, tk, gj, gk) if tb else _mat_spec(b, tk, tn, gk, gj)
    out_like = jax.ShapeDtypeStruct((out_seg, m, n // out_seg) if out_seg else (m, n), out_dtype)
    o_spec = _mat_spec(out_like, tm, tn, gi, gj)
    o_shape = (out_seg, m, n // out_seg) if out_seg else (m, n)
    dn = (((0 if ta else 1,), (1 if tb else 0,)), ((), ()))
    extra = [x for x in (res, aux) if x is not None]
    assert not (res is not None and aux is not None)
    n_out = 2 if epi == "relu2" else 1

    def body(*refs):
        a_ref, b_ref = refs[0], refs[1]
        x_ref = refs[2] if extra else None
        outs = refs[2 + len(extra):2 + len(extra) + n_out]
        acc_ref = refs[-1] if nk > 1 else None
        part = lax.dot_general(a_ref[...], b_ref[...], dn, preferred_element_type=F32)

        def finish(acc):
            if epi == "relu2":
                r = jnp.maximum(acc, 0.0)
                outs[0][...] = r.astype(BF16)
                outs[1][...] = (r * r).astype(BF16)
            elif epi == "relu2_bwd":
                outs[0][...] = (acc * (2.0 * x_ref[...].astype(F32))).astype(out_dtype)
            elif res is not None:
                outs[0][...] = (acc + x_ref[...]).astype(out_dtype)
            else:
                outs[0][...] = acc.astype(out_dtype)

        if nk == 1:
            finish(part)
        else:
            kk = pl.program_id(2)

            @pl.when(kk == 0)
            def _():
                acc_ref[...] = part

            @pl.when(kk > 0)
            def _():
                acc_ref[...] += part

            @pl.when(kk == nk - 1)
            def _():
                finish(acc_ref[...])

    if epi == "relu2":
        out_shape = [jax.ShapeDtypeStruct(o_shape, BF16), jax.ShapeDtypeStruct(o_shape, BF16)]
        out_specs = [o_spec, o_spec]
    else:
        out_shape = jax.ShapeDtypeStruct(o_shape, out_dtype)
        out_specs = o_spec
    x_specs = [pl.BlockSpec((tm, tn), lambda j, i, kk: (i, j))] * len(extra)
    res_ = pl.pallas_call(
        body, name=name, grid=(n // tn, m // tm, nk),
        in_specs=[a_spec, b_spec] + x_specs, out_specs=out_specs, out_shape=out_shape,
        scratch_shapes=[pltpu.VMEM((tm, tn), F32)] if nk > 1 else [],
        compiler_params=_params("parallel", "parallel", "arbitrary"),
    )(a, b, *extra)
    return res_


def _rms_fwd(x, g):
    t, d = x.shape
    tm = _row_tile(t)

    def body(x_ref, g_ref, o_ref):
        x_ = x_ref[...]
        rstd = lax.rsqrt(jnp.mean(x_ * x_, axis=-1, keepdims=True) + EPS)
        o_ref[...] = (x_ * rstd * g_ref[...]).astype(BF16)

    return pl.pallas_call(
        body, name="rms_fwd", grid=(t // tm,),
        in_specs=[pl.BlockSpec((tm, d), lambda i: (i, 0)), pl.BlockSpec((1, d), lambda i: (0, 0))],
        out_specs=pl.BlockSpec((tm, d), lambda i: (i, 0)),
        out_shape=jax.ShapeDtypeStruct((t, d), BF16), compiler_params=_params("parallel"),
    )(x, g.reshape(1, d))


def _rms_bwd(x, g, dy, dres=None, want_bf16=False):
    t, d = x.shape
    tm = _row_tile(t)
    has_res = dres is not None

    def body(*refs):
        x_ref, g_ref, dy_ref = refs[:3]
        r_ref = refs[3] if has_res else None
        outs = refs[3 + has_res:]
        x_ = x_ref[...]
        rstd = lax.rsqrt(jnp.mean(x_ * x_, axis=-1, keepdims=True) + EPS)
        xh = x_ * rstd
        dy_ = dy_ref[...].astype(F32)
        dxh = dy_ * g_ref[...]
        dx = rstd * (dxh - xh * jnp.mean(dxh * xh, axis=-1, keepdims=True))
        if has_res:
            dx = dx + r_ref[...]
        outs[0][...] = dx
        if want_bf16:
            outs[1][...] = dx.astype(BF16)
        dg_ref = outs[-1]

        @pl.when(pl.program_id(0) == 0)
        def _():
            dg_ref[...] = jnp.zeros_like(dg_ref)

        dg_ref[...] += jnp.sum(dy_ * xh, axis=0, keepdims=True)

    row = pl.BlockSpec((tm, d), lambda i: (i, 0))
    vec = pl.BlockSpec((1, d), lambda i: (0, 0))
    out_shape = [jax.ShapeDtypeStruct((t, d), F32)] + ([jax.ShapeDtypeStruct((t, d), BF16)] if want_bf16 else []) \
        + [jax.ShapeDtypeStruct((1, d), F32)]
    out_specs = [row] + ([row] if want_bf16 else []) + [vec]
    return pl.pallas_call(
        body, name="rms_bwd", grid=(t // tm,),
        in_specs=[row, vec, row] + ([row] if has_res else []), out_specs=out_specs, out_shape=out_shape,
        compiler_params=_params("arbitrary"),
    )(x, g.reshape(1, d), dy, *([dres] if has_res else []))


def _swap_rope_halves(y):
    lane = lax.broadcasted_iota(jnp.int32, y.shape, 1)
    half = MLA_ROPE // 2
    swapped = jnp.where(lane < MLA_NOPE + half, pltpu.roll(y, LANES - half, axis=1), pltpu.roll(y, half, axis=1))
    return jnp.where((lane >= MLA_NOPE) & (lane < MLA_QK), swapped, 0.0)


def _head_norm_fwd(x, g, n_valid, scale, rope=None):
    t, hw = x.shape
    w = g.shape[1]
    tm = _row_tile(t)

    def body(*refs):
        x_ref, g_ref = refs[:2]
        o_ref = refs[-1]
        gain = g_ref[...] * scale
        for n in range(hw // w):
            cols = slice(n * w, (n + 1) * w)
            x_ = x_ref[:, cols]
            rstd = lax.rsqrt(jnp.sum(x_ * x_, axis=-1, keepdims=True) * (1.0 / n_valid) + EPS)
            y = x_ * rstd * gain
            if rope is not None:
                y = y * refs[2][...] + _swap_rope_halves(y) * refs[3][...]
            o_ref[:, cols] = y.astype(BF16)

    row = pl.BlockSpec((tm, hw), lambda i: (i, 0))
    tab = pl.BlockSpec((tm, w), lambda i: (i, 0))
    return pl.pallas_call(
        body, name="head_norm_fwd", grid=(t // tm,),
        in_specs=[row, pl.BlockSpec((1, w), lambda i: (0, 0))] + ([tab, tab] if rope is not None else []),
        out_specs=row, out_shape=jax.ShapeDtypeStruct((t, hw), BF16),
        compiler_params=_params("parallel"),
    )(x, g, *(rope if rope is not None else ()))


def _head_norm_bwd(x, g, dout, n_valid, scale, rope=None):
    t, hw = x.shape
    w = g.shape[1]
    tm = _row_tile(t)

    def body(*refs):
        x_ref, g_ref, do_ref = refs[:3]
        dx_ref, dsum_ref, dg_ref = refs[-3:]
        gain = g_ref[...] * scale
        dsum = jnp.zeros((tm, w), F32)
        dg = jnp.zeros((1, w), F32)
        for n in range(hw // w):
            cols = slice(n * w, (n + 1) * w)
            dy = do_ref[:, cols]
            if rope is not None:
                dy = dy * refs[3][...] + _swap_rope_halves(dy * refs[4][...])
            x_ = x_ref[:, cols]
            rstd = lax.rsqrt(jnp.sum(x_ * x_, axis=-1, keepdims=True) * (1.0 / n_valid) + EPS)
            xh = x_ * rstd
            dxh = dy * gain
            dx = rstd * (dxh - xh * (jnp.sum(dxh * xh, axis=-1, keepdims=True) * (1.0 / n_valid)))
            dx_ref[:, cols] = dx.astype(BF16)
            dsum = dsum + dx
            dg = dg + jnp.sum(dy * xh, axis=0, keepdims=True)
        dsum_ref[...] = dsum

        @pl.when(pl.program_id(0) == 0)
        def _():
            dg_ref[...] = jnp.zeros_like(dg_ref)

        dg_ref[...] += scale * dg

    row = pl.BlockSpec((tm, hw), lambda i: (i, 0))
    tab = pl.BlockSpec((tm, w), lambda i: (i, 0))
    vec = pl.BlockSpec((1, w), lambda i: (0, 0))
    return pl.pallas_call(
        body, name="head_norm_bwd", grid=(t // tm,),
        in_specs=[row, vec, row] + ([tab, tab] if rope is not None else []),
        out_specs=[row, tab, vec],
        out_shape=[jax.ShapeDtypeStruct((t, hw), BF16), jax.ShapeDtypeStruct((t, w), F32),
                   jax.ShapeDtypeStruct((1, w), F32)],
        compiler_params=_params("arbitrary"),
    )(x, g, dout, *(rope if rope is not None else ()))


def _tri(n, upper):
    r = lax.broadcasted_iota(jnp.int32, (n, n), 0)
    c = lax.broadcasted_iota(jnp.int32, (n, n), 1)
    return ((r <= c) if upper else (r >= c)).astype(F32)


def _gate_mask(shape, row0):
    lane = lax.broadcasted_iota(jnp.int32, shape, 1)
    row = row0 + lax.broadcasted_iota(jnp.int32, shape, 0)
    return (lane >= TAIL_F) & (lane < TAIL_F + HEADS) & (row >= PAD)


def _gate_fwd(z, bias):
    t = z.shape[0]
    tm = BLOCK
    tail = C_TAIL // LANES

    def body(z_ref, b_ref, o_ref, carry):
        i = pl.program_id(0)

        @pl.when(i == 0)
        def _():
            carry[...] = jnp.zeros_like(carry)

        x_ = z_ref[...] + b_ref[...]
        logf = jnp.minimum(x_, 0.0) - jnp.log1p(jnp.exp(-jnp.abs(x_)))
        logf = jnp.where(_gate_mask(logf.shape, i * tm), logf, 0.0)
        cum = jnp.dot(_tri(tm, False), logf, preferred_element_type=F32, precision=lax.Precision.HIGHEST) + carry[...]
        o_ref[...] = cum
        carry[...] = cum[tm - 1:tm, :]

    return pl.pallas_call(
        body, name="gate_fwd", grid=(t // tm,),
        in_specs=[pl.BlockSpec((tm, LANES), lambda i: (i, tail)), pl.BlockSpec((1, LANES), lambda i: (0, 0))],
        out_specs=pl.BlockSpec((tm, LANES), lambda i: (i, 0)),
        out_shape=jax.ShapeDtypeStruct((t, LANES), F32),
        scratch_shapes=[pltpu.VMEM((1, LANES), F32)], compiler_params=_params("arbitrary"),
    )(z, bias)


def _gate_bwd(z, bias, dcum):
    t = z.shape[0]
    tm = BLOCK
    nb = t // tm
    tail = C_TAIL // LANES

    def body(z_ref, b_ref, d_ref, o_ref, db_ref, carry):
        i = pl.program_id(0)

        @pl.when(i == 0)
        def _():
            carry[...] = jnp.zeros_like(carry)
            db_ref[...] = jnp.zeros_like(db_ref)

        rc = jnp.dot(_tri(tm, True), d_ref[...], preferred_element_type=F32, precision=lax.Precision.HIGHEST) + carry[...]
        carry[...] = rc[0:1, :]
        x_ = z_ref[...] + b_ref[...]
        sig_neg = 1.0 / (1.0 + jnp.exp(x_))
        dl = jnp.where(_gate_mask(rc.shape, (nb - 1 - i) * tm), rc * sig_neg, 0.0)
        o_ref[...] = dl
        db_ref[...] += jnp.sum(dl, axis=0, keepdims=True)

    return pl.pallas_call(
        body, name="gate_bwd", grid=(nb,),
        in_specs=[pl.BlockSpec((tm, LANES), lambda i: (nb - 1 - i, tail)), pl.BlockSpec((1, LANES), lambda i: (0, 0)),
                  pl.BlockSpec((tm, LANES), lambda i: (nb - 1 - i, 0))],
        out_specs=[pl.BlockSpec((tm, LANES), lambda i: (nb - 1 - i, 0)), pl.BlockSpec((1, LANES), lambda i: (0, 0))],
        out_shape=[jax.ShapeDtypeStruct((t, LANES), F32), jax.ShapeDtypeStruct((1, LANES), F32)],
        scratch_shapes=[pltpu.VMEM((1, LANES), F32)], compiler_params=_params("arbitrary"),
    )(z, bias, dcum)


def _pairs(nb, by_query):
    if by_query:
        pr = [(i, j) for i in range(nb) for j in range(i + 1)]
    else:
        pr = [(i, j) for j in range(nb) for i in range(j, nb)]
    return (jnp.asarray(np.array([p[0] for p in pr], np.int32)),
            jnp.asarray(np.array([p[1] for p in pr], np.int32)))


HEADS_PER_STEP = 8


def _mask_scores(s, i, j, tile):
    qp = i * tile + lax.broadcasted_iota(jnp.int32, s.shape, 0)
    kp = j * tile + lax.broadcasted_iota(jnp.int32, s.shape, 1)
    return jnp.where((kp <= qp) & (kp >= PAD), s, NEG)


def _pipelined(n, front, back):
    nxt = front(0)
    for h in range(n):
        cur = nxt
        if h + 1 < n:
            nxt = front(h + 1)
        back(h, cur)


def _nt_dot(a, b):
    return lax.dot_general(a, b, (((1,), (1,)), ((), ())), preferred_element_type=F32)


def _tn_dot(a, b):
    return lax.dot_general(a, b, (((0,), (0,)), ((), ())), preferred_element_type=F32)


def _attn_specs(hb, tile, dk, dv):
    q_of = lambda g, p, it, jt: (it[p], g)
    k_of = lambda g, p, it, jt: (jt[p], g)
    return dict(
        q=pl.BlockSpec((tile, hb * dk), q_of), k=pl.BlockSpec((tile, hb * dk), k_of),
        v=pl.BlockSpec((tile, hb * dv), k_of), ov=pl.BlockSpec((tile, hb * dv), q_of),
        fr=pl.BlockSpec((hb, 1, tile), lambda g, p, it, jt: (g, 0, jt[p])),
        rowq=pl.BlockSpec((hb, 1, tile), lambda g, p, it, jt: (g, 0, it[p])))


def _head(ref, n, d):
    return ref[:, n * d:(n + 1) * d]


def _attn_fwd(q, k, v, key_bias=None, gather=()):
    h = HEADS
    t, dk, dv = q.shape[0], q.shape[1] // h, v.shape[1] // h
    tile = _row_tile(t)
    nb = t // tile
    hb = HEADS_PER_STEP
    biased = key_bias is not None
    it, jt = _pairs(nb, True)
    sp = _attn_specs(hb, tile, dk, dv)

    n_in = 4 if biased else 3
    n_g = len(gather)
    g_in, g_out, g_shapes, g_sems = _comm_parts(gather, True)
    n_steps = int(it.shape[0])

    def body(it_ref, jt_ref, *refs):
        q_ref, k_ref, v_ref = refs[:3]
        b_ref = refs[3] if biased else None
        o_ref, lse_ref = refs[n_in + 2 * n_g:n_in + 2 * n_g + 2]
        m_sc, l_sc, acc_sc = refs[n_in + 2 * n_g + 2:n_in + 2 * n_g + 5]
        p = pl.program_id(1)
        i, j = it_ref[p], jt_ref[p]
        if n_g:
            g_start, g_finish = _gather_ops(refs[n_in:n_in + n_g], refs[n_in + n_g:n_in + 2 * n_g],
                                            *refs[n_in + 2 * n_g + 5:])
            first = (pl.program_id(0) == 0) & (p == 0)
            last = (pl.program_id(0) == h // hb - 1) & (p == n_steps - 1)
            pl.when(first)(g_start)

        @pl.when(j == 0)
        def _():
            m_sc[...] = jnp.full_like(m_sc, NEG)
            l_sc[...] = jnp.zeros_like(l_sc)
            acc_sc[...] = jnp.zeros_like(acc_sc)

        def step(masked):
            def front(n):
                return _nt_dot(_head(q_ref, n, dk), _head(k_ref, n, dk))

            def back(n, s):
                if biased:
                    s = s + b_ref[n]
                if masked:
                    s = _mask_scores(s, i, j, tile)
                m_prev = m_sc[n]
                m_new = jnp.maximum(m_prev, jnp.max(s, axis=-1, keepdims=True))
                alpha = jnp.exp2(m_prev - m_new)
                e = jnp.exp2(s - m_new)
                l_sc[n] = alpha * l_sc[n] + jnp.sum(e, axis=-1, keepdims=True)
                acc_sc[n] = alpha * acc_sc[n] + jnp.dot(e.astype(BF16), _head(v_ref, n, dv),
                                                        preferred_element_type=F32)
                m_sc[n] = m_new

            _pipelined(hb, front, back)

        edge = (j == i) | (j == 0)
        pl.when(edge)(lambda: step(True))
        pl.when(jnp.logical_not(edge))(lambda: step(False))

        @pl.when(j == i)
        def _():
            row = i * tile + lax.broadcasted_iota(jnp.int32, (tile, 1), 0)
            for n in range(hb):
                o_ref[:, n * dv:(n + 1) * dv] = jnp.where(row >= PAD, acc_sc[n] / l_sc[n], 0.0)
                lse_ref[n] = jnp.transpose(m_sc[n] + jnp.log2(l_sc[n]))

        if n_g:
            pl.when(last)(g_finish)

    grid_spec = pltpu.PrefetchScalarGridSpec(
        num_scalar_prefetch=2, grid=(h // hb, n_steps),
        in_specs=[sp["q"], sp["k"], sp["v"]] + ([sp["fr"]] if biased else []) + g_in,
        out_specs=g_out + [sp["ov"], sp["rowq"]],
        scratch_shapes=[pltpu.VMEM((hb, tile, 1), F32), pltpu.VMEM((hb, tile, 1), F32),
                        pltpu.VMEM((hb, tile, dv), F32)] + (g_sems if n_g else []))
    outs = pl.pallas_call(
        body, name="attn_fwd_gather" if n_g else "attn_fwd", grid_spec=grid_spec,
        out_shape=g_shapes + [jax.ShapeDtypeStruct((t, h * dv), F32), jax.ShapeDtypeStruct((h, 1, t), F32)],
        compiler_params=_params("arbitrary" if n_g else "parallel", "arbitrary"),
    )(it, jt, q, k, v, *((key_bias,) if biased else ()), *gather)
    return (outs[n_g], outs[n_g + 1], outs[:n_g]) if n_g else tuple(outs)


LOG2E = 1.4426950408889634
LN2 = 0.6931471805599453
MLA_SCALE = MLA_QK ** -0.5 * LOG2E
FOX_SCALE = FOX_DIM ** -0.5 * LOG2E


def _attn_delta(o, do):
    h = HEADS
    t, dv = o.shape[0], o.shape[1] // h
    tm = _row_tile(t)

    def body(o_ref, do_ref, d_ref):
        prod = o_ref[...] * do_ref[...]
        for n in range(h):
            d_ref[n] = jnp.transpose(jnp.sum(prod[:, n * dv:(n + 1) * dv], axis=-1, keepdims=True) * LN2)

    blk = pl.BlockSpec((tm, h * dv), lambda i: (i, 0))
    return pl.pallas_call(
        body, name="attn_delta", grid=(t // tm,), in_specs=[blk, blk],
        out_specs=pl.BlockSpec((h, 1, tm), lambda i: (0, 0, i)),
        out_shape=jax.ShapeDtypeStruct((h, 1, t), F32), compiler_params=_params("parallel"),
    )(o, do)


BWD_HEADS_PER_STEP = (4, 4)


def _attn_bwd(q, k, v, do, lse_row, delta_row, key_bias=None, exchange=()):
    h = HEADS
    t, dk, dv = q.shape[0], q.shape[1] // h, v.shape[1] // h
    tile = _row_tile(t)
    nb = t // tile
    decay = key_bias is not None
    hb = BWD_HEADS_PER_STEP[int(decay)]
    it, jt = _pairs(nb, False)
    sp = _attn_specs(hb, tile, dk, dv)
    n_in = 7 if decay else 6
    n_out = 5 if decay else 3
    n_x = len(exchange)
    x_in, x_out, x_shapes, x_sems = _comm_parts(exchange, False)
    n_steps = int(it.shape[0])

    def body(it_ref, jt_ref, *refs):
        q_ref, k_ref, v_ref, do_ref, lse_ref, delta_ref = refs[:6]
        b_ref = refs[6] if decay else None
        outs = refs[n_in + n_x:]
        dq_ref, dk_ref, dv_ref = outs[:3]
        rs_ref, ks_ref = (outs[3], outs[4]) if decay else (None, None)
        scratch = outs[n_out + n_x:]
        dq_sc, dk_sc, dv_sc = scratch[:3]
        ks_sc, b_sc = (scratch[3], scratch[4]) if decay else (None, None)
        p = pl.program_id(1)
        i, j = it_ref[p], jt_ref[p]
        if n_x:
            x_start, x_finish = _exchange_ops(refs[n_in:n_in + n_x], outs[n_out:n_out + n_x],
                                              *scratch[5 if decay else 3:])
            first = (pl.program_id(0) == 0) & (p == 0)
            last = (pl.program_id(0) == h // hb - 1) & (p == n_steps - 1)
            pl.when(first)(x_start)

        @pl.when(p == 0)
        def _():
            dq_sc[...] = jnp.zeros_like(dq_sc)
            if decay:
                rs_ref[...] = jnp.zeros_like(rs_ref)

        @pl.when(i == j)
        def _():
            dk_sc[...] = jnp.zeros_like(dk_sc)
            dv_sc[...] = jnp.zeros_like(dv_sc)
            if decay:
                ks_sc[...] = jnp.zeros_like(ks_sc)
                for n in range(hb):
                    b_sc[n] = jnp.transpose(b_ref[n])

        def step(masked):
            def front(n):
                return (_nt_dot(_head(k_ref, n, dk), _head(q_ref, n, dk)),
                        _nt_dot(_head(v_ref, n, dv), (_head(do_ref, n, dv) * LN2).astype(BF16)))

            def back(n, s_dp):
                s, dp = s_dp
                if decay:
                    s = s + b_sc[n]
                if masked:
                    kp = j * tile + lax.broadcasted_iota(jnp.int32, s.shape, 0)
                    qp = i * tile + lax.broadcasted_iota(jnp.int32, s.shape, 1)
                    s = jnp.where((kp <= qp) & (kp >= PAD), s, NEG)
                pr = jnp.exp2(s - lse_ref[n])
                ds = pr * (dp - delta_ref[n])
                ds_b = ds.astype(BF16)
                dv_sc[n] += jnp.dot(pr.astype(BF16), _head(do_ref, n, dv).astype(BF16), preferred_element_type=F32)
                dk_sc[n] += jnp.dot(ds_b, _head(q_ref, n, dk), preferred_element_type=F32)
                dq_sc[n, i] += _tn_dot(ds_b, _head(k_ref, n, dk))
                if decay:
                    rs_ref[n, i] += jnp.sum(ds, axis=0, keepdims=True)
                    ks_sc[n] += jnp.sum(ds, axis=-1, keepdims=True)

            _pipelined(hb, front, back)

        edge = (j == i) | (j == 0)
        pl.when(edge)(lambda: step(True))
        pl.when(jnp.logical_not(edge))(lambda: step(False))

        @pl.when(i == j)
        def _():
            for n in range(hb):
                dq_ref[:, n * dk:(n + 1) * dk] = dq_sc[n, j]

        @pl.when(i == nb - 1)
        def _():
            for n in range(hb):
                dk_ref[:, n * dk:(n + 1) * dk] = dk_sc[n]
                dv_ref[:, n * dv:(n + 1) * dv] = dv_sc[n]
                if decay:
                    ks_ref[n] = jnp.transpose(ks_sc[n])

        if n_x:
            pl.when(last)(x_finish)

    rows_out = pl.BlockSpec((hb, nb, 1, tile), lambda hh, p, it, jt: (hh, 0, 0, 0))
    grid_spec = pltpu.PrefetchScalarGridSpec(
        num_scalar_prefetch=2, grid=(h // hb, n_steps),
        in_specs=[sp["q"], sp["k"], sp["v"], sp["ov"], sp["rowq"], sp["rowq"]] + ([sp["fr"]] if decay else [])
        + x_in,
        out_specs=[sp["k"], sp["k"], sp["v"]] + ([rows_out, sp["fr"]] if decay else []) + x_out,
        scratch_shapes=[pltpu.VMEM((hb, nb, tile, dk), F32), pltpu.VMEM((hb, tile, dk), F32),
                        pltpu.VMEM((hb, tile, dv), F32)] + ([pltpu.VMEM((hb, tile, 1), F32)] * 2 if decay else [])
        + (x_sems if n_x else []))
    out_shape = [jax.ShapeDtypeStruct((t, h * dk), F32), jax.ShapeDtypeStruct((t, h * dk), F32),
                 jax.ShapeDtypeStruct((t, h * dv), F32)] \
        + ([jax.ShapeDtypeStruct((h, nb, 1, tile), F32), jax.ShapeDtypeStruct((h, 1, t), F32)] if decay else []) \
        + x_shapes
    outs = pl.pallas_call(
        body, name="attn_bwd_exchange" if n_x else "attn_bwd", grid_spec=grid_spec, out_shape=out_shape,
        compiler_params=_params("arbitrary" if n_x else "parallel", "arbitrary"),
    )(it, jt, q, k, v, do, lse_row, delta_row, *((key_bias,) if decay else ()), *exchange)
    return tuple(outs[:n_out]) + ((list(outs[n_out:]),) if n_x else ())


CONV_COLS = 512


def _shift_down(g, prev, n):
    out = pltpu.roll(g, n, axis=0)
    row = lax.broadcasted_iota(jnp.int32, g.shape, 0)
    for r in range(n):
        out = jnp.where(row == r, prev[8 - n + r:8 - n + r + 1, :], out)
    return out


def _shift_up(g, nxt, n):
    tm = g.shape[0]
    out = pltpu.roll(g, tm - n, axis=0)
    row = lax.broadcasted_iota(jnp.int32, g.shape, 0)
    for r in range(n):
        out = jnp.where(row == tm - n + r, nxt[r:r + 1, :], out)
    return out


def _conv_fwd(z3, w):
    _, t, d = z3.shape
    tm = _row_tile(t)
    tc = CONV_COLS

    def body(z_ref, zp_ref, w_ref, o_ref):
        i = pl.program_id(1)
        g = z_ref[1] * z_ref[2]
        gp = jnp.where(i > 0, zp_ref[1] * zp_ref[2], 0.0)
        w_ = w_ref[...]
        y = w_[2:3] * g + w_[1:2] * _shift_down(g, gp, 1) + w_[0:1] * _shift_down(g, gp, 2)
        o_ref[...] = (z_ref[0] * y).astype(BF16)

    return pl.pallas_call(
        body, name="conv_fwd", grid=(d // tc, t // tm),
        in_specs=[pl.BlockSpec((3, tm, tc), lambda j, i: (0, i, j)),
                  pl.BlockSpec((3, 8, tc), lambda j, i: (0, jnp.maximum(i * (tm // 8) - 1, 0), j)),
                  pl.BlockSpec((3, tc), lambda j, i: (0, j))],
        out_specs=pl.BlockSpec((tm, tc), lambda j, i: (i, j)),
        out_shape=jax.ShapeDtypeStruct((t, d), BF16), compiler_params=_params("parallel", "parallel"),
    )(z3, z3, w)


def _conv_bwd(z3, w, dyb):
    _, t, d = z3.shape
    tm = _row_tile(t)
    tc = CONV_COLS
    ni = t // tm

    def body(z_ref, zp_ref, zn_ref, d_ref, dn_ref, w_ref, dz_ref, dw_ref):
        i = pl.program_id(1)
        gb, gc, u = z_ref[0], z_ref[1], z_ref[2]
        g = gc * u
        gp = jnp.where(i > 0, zp_ref[1] * zp_ref[2], 0.0)
        w_ = w_ref[...]
        g1, g2 = _shift_down(g, gp, 1), _shift_down(g, gp, 2)
        y = w_[2:3] * g + w_[1:2] * g1 + w_[0:1] * g2
        dyb_ = d_ref[...]
        dy = dyb_ * gb
        dyn = jnp.where(i < ni - 1, dn_ref[...] * zn_ref[0], 0.0)
        dg = w_[2:3] * dy + w_[1:2] * _shift_up(dy, dyn, 1) + w_[0:1] * _shift_up(dy, dyn, 2)
        dz_ref[0] = (dyb_ * y).astype(BF16)
        dz_ref[1] = (dg * u).astype(BF16)
        dz_ref[2] = (dg * gc).astype(BF16)

        @pl.when(i == 0)
        def _():
            dw_ref[...] = jnp.zeros_like(dw_ref)

        dw_ref[...] += jnp.concatenate([jnp.sum(dy * g2, axis=0, keepdims=True),
                                        jnp.sum(dy * g1, axis=0, keepdims=True),
                                        jnp.sum(dy * g, axis=0, keepdims=True)], axis=0)

    cur = pl.BlockSpec((3, tm, tc), lambda j, i: (0, i, j))
    return pl.pallas_call(
        body, name="conv_bwd", grid=(d // tc, ni),
        in_specs=[cur,
                  pl.BlockSpec((3, 8, tc), lambda j, i: (0, jnp.maximum(i * (tm // 8) - 1, 0), j)),
                  pl.BlockSpec((3, 8, tc), lambda j, i: (0, jnp.minimum((i + 1) * (tm // 8), t // 8 - 1), j)),
                  pl.BlockSpec((tm, tc), lambda j, i: (i, j)),
                  pl.BlockSpec((8, tc), lambda j, i: (jnp.minimum((i + 1) * (tm // 8), t // 8 - 1), j)),
                  pl.BlockSpec((3, tc), lambda j, i: (0, j))],
        out_specs=[cur, pl.BlockSpec((3, tc), lambda j, i: (0, j))],
        out_shape=[jax.ShapeDtypeStruct((3, t, d), BF16), jax.ShapeDtypeStruct((3, d), F32)],
        compiler_params=_params("parallel", "arbitrary"),
    )(z3, z3, z3, dyb, dyb, w)


def _loss_head(h, target):
    t, d = h.shape
    tm = BLOCK

    def body(h_ref, t_ref, dh_ref, dhb_ref, loss_ref):
        i = pl.program_id(0)

        @pl.when(i == 0)
        def _():
            loss_ref[...] = jnp.zeros_like(loss_ref)

        err = jnp.where(i > 0, h_ref[...] - t_ref[...], 0.0)
        dh = err * (1.0 / d)
        dh_ref[...] = dh
        dhb_ref[...] = dh.astype(BF16)
        loss_ref[...] += 0.5 * jnp.sum(jnp.sum(err * err, axis=-1, keepdims=True) * (1.0 / d), axis=0, keepdims=True)

    row = pl.BlockSpec((tm, d), lambda i: (i, 0))
    return pl.pallas_call(
        body, name="loss_head", grid=(t // tm,),
        in_specs=[row, pl.BlockSpec((tm, d), lambda i: (jnp.maximum(i - 1, 0), 0))],
        out_specs=[row, row, pl.BlockSpec((1, 1), lambda i: (0, 0))],
        out_shape=[jax.ShapeDtypeStruct((t, d), F32), jax.ShapeDtypeStruct((t, d), BF16),
                   jax.ShapeDtypeStruct((1, 1), F32)],
        compiler_params=_params("arbitrary"),
    )(h, target)


def _rope_tables(t):
    pos = jnp.arange(t, dtype=F32) - PAD
    inv_freq = ROPE_BASE ** (-jnp.arange(0, MLA_ROPE, 2, dtype=F32) / MLA_ROPE)
    ang = pos[:, None] * inv_freq[None, :]
    cos, sin = jnp.cos(ang), jnp.sin(ang)
    one, zero = jnp.ones((t, MLA_NOPE), F32), jnp.zeros((t, MLA_NOPE), F32)
    tail = jnp.zeros((t, LANES - MLA_QK), F32)
    return (jnp.concatenate([one, cos, cos, tail], axis=1), jnp.concatenate([zero, -sin, sin, tail], axis=1))


def _pad_lanes(x, width=LANES):
    return jnp.pad(x, [(0, 0)] * (x.ndim - 1) + [(0, width - x.shape[-1])])


def _permute_in_attn(w):
    return jnp.concatenate([w[:, :640], w[:, 672:2208], w[:, 640:672], w[:, 2208:2216],
                            jnp.zeros((w.shape[0], ATTN_IN_PAD - 2216), w.dtype)], axis=1)


def _unpermute_in_attn(dw):
    return jnp.concatenate([dw[:, :640], dw[:, 2176:2208], dw[:, 640:2176], dw[:, 2208:2216]], axis=1)


def _attn_layer_fwd(hn, wl, rope, gather=()):
    t = hn.shape[0]
    z = _mm(hn, wl["w_in"], name="attn_in")
    cqn = _rms_fwd(z[:, C_CQ:C_CQ + Q_LORA], wl["g_cq"])
    ckvn = _rms_fwd(z[:, C_CKV:C_CKV + KV_LORA], wl["g_ckv"])
    w_uq = _pad_lanes(wl["w_uq"].reshape(Q_LORA, HEADS, MLA_QK)).reshape(Q_LORA, HEADS * LANES)
    xq = _mm(cqn, w_uq, name="mla_uq")
    kvf = _mm(ckvn, wl["w_ukv"], name="mla_ukv")
    kv3 = kvf.reshape(t, HEADS, MLA_NOPE + MLA_V)
    k_pe = jnp.broadcast_to(z[:, None, C_TAIL:C_TAIL + MLA_ROPE], (t, HEADS, MLA_ROPE))
    xk = _pad_lanes(jnp.concatenate([kv3[:, :, :MLA_NOPE], k_pe], axis=-1)).reshape(t, HEADS * LANES)
    v_mla = kv3[:, :, MLA_NOPE:].reshape(t, HEADS * MLA_V).astype(BF16)
    gq, gk = _pad_lanes(wl["g_q_mla"].reshape(1, -1)), _pad_lanes(wl["g_k_mla"].reshape(1, -1))
    q_mla = _head_norm_fwd(xq, gq, MLA_QK, MLA_SCALE, rope)
    k_mla = _head_norm_fwd(xk, gk, MLA_QK, 1.0, rope)
    o_mla, lse_mla, *gathered = _attn_fwd(q_mla, k_mla, v_mla, gather=gather)
    xfq = z[:, C_FQ:C_FQ + HEADS * FOX_DIM]
    xfk = z[:, C_FK:C_FK + HEADS * FOX_DIM]
    v_fox = z[:, C_FV:C_FV + HEADS * FOX_DIM].astype(BF16)
    bias = jnp.pad(wl["b_forget"].reshape(1, -1), ((0, 0), (TAIL_F, LANES - TAIL_F - HEADS)))
    cum = _gate_fwd(z, bias)
    neg_f = (-LOG2E * jnp.transpose(cum[:, TAIL_F:TAIL_F + HEADS]))[:, None, :]
    q_fox = _head_norm_fwd(xfq, wl["g_q_fox"].reshape(1, -1), FOX_DIM, FOX_SCALE)
    k_fox = _head_norm_fwd(xfk, wl["g_k_fox"].reshape(1, -1), FOX_DIM, 1.0)
    o_fox, lse_fox = _attn_fwd(q_fox, k_fox, v_fox, neg_f)
    cat = jnp.concatenate([o_mla, o_fox], axis=1).astype(BF16)
    saved = dict(z=z, cqn=cqn, ckvn=ckvn, xq=xq, xk=xk, v_mla=v_mla, q_mla=q_mla, k_mla=k_mla, o_mla=o_mla,
                 lse_mla=lse_mla, xfq=xfq, xfk=xfk, v_fox=v_fox, q_fox=q_fox, k_fox=k_fox, bias=bias,
                 neg_f=neg_f, o_fox=o_fox, lse_fox=lse_fox, cat=cat, gq=gq, gk=gk, w_uq=w_uq)
    return cat, saved, (gathered[0] if gathered else ())


def _attn_layer_bwd(dcat, hn, wl, sv, rope, exchange=()):
    t = hn.shape[0]
    g = {}
    do_mla = dcat[:, :HEADS * MLA_V]
    do_fox = dcat[:, HEADS * MLA_V:]
    qkv = (sv["q_fox"], sv["k_fox"], sv["v_fox"])
    delta = _attn_delta(sv["o_fox"], do_fox)
    dq_fox, dk_fox, dv_fox, row_sums, key_sums, *exchanged = _attn_bwd(
        *qkv, do_fox, sv["lse_fox"], delta, sv["neg_f"], exchange=exchange)
    dcum = jnp.pad(jnp.transpose(LOG2E * (row_sums.reshape(HEADS, t) - key_sums.reshape(HEADS, t))),
                   ((0, 0), (TAIL_F, LANES - TAIL_F - HEADS)))
    dtail_f, dbias = _gate_bwd(sv["z"], sv["bias"], dcum)
    g["b_forget"] = dbias[0, TAIL_F:TAIL_F + HEADS]
    dxfq, _, dgq = _head_norm_bwd(sv["xfq"], wl["g_q_fox"].reshape(1, -1), dq_fox, FOX_DIM, FOX_SCALE)
    dxfk, _, dgk = _head_norm_bwd(sv["xfk"], wl["g_k_fox"].reshape(1, -1), dk_fox, FOX_DIM, 1.0)
    g["g_q_fox"], g["g_k_fox"] = dgq[0], dgk[0]
    qkv = (sv["q_mla"], sv["k_mla"], sv["v_mla"])
    delta = _attn_delta(sv["o_mla"], do_mla)
    dq_mla, dk_mla, dv_mla = _attn_bwd(*qkv, do_mla, sv["lse_mla"], delta)
    dxq, _, dgq = _head_norm_bwd(sv["xq"], sv["gq"], dq_mla, MLA_QK, MLA_SCALE, rope)
    dxk, dxk_sum, dgk = _head_norm_bwd(sv["xk"], sv["gk"], dk_mla, MLA_QK, 1.0, rope)
    g["g_q_mla"], g["g_k_mla"] = dgq[0, :MLA_QK], dgk[0, :MLA_QK]
    dqf = dxq.astype(BF16)
    dkvf = jnp.concatenate([dxk.reshape(t, HEADS, LANES)[:, :, :MLA_NOPE], dv_mla.reshape(t, HEADS, MLA_V)],
                           axis=-1).reshape(t, HEADS * (MLA_NOPE + MLA_V)).astype(BF16)
    g["w_uq"] = _mm(sv["cqn"], dqf, ta=True, name="d_w_uq").reshape(Q_LORA, HEADS, LANES)[:, :, :MLA_QK].reshape(
        Q_LORA, HEADS * MLA_QK)
    g["w_ukv"] = _mm(sv["ckvn"], dkvf, ta=True, name="d_w_ukv")
    dcqn = _mm(dqf, sv["w_uq"], tb=True, name="d_cqn")
    dckvn = _mm(dkvf, wl["w_ukv"], tb=True, name="d_ckvn")
    z = sv["z"]
    dcq, dg_cq = _rms_bwd(z[:, C_CQ:C_CQ + Q_LORA], wl["g_cq"], dcqn)
    dckv, dg_ckv = _rms_bwd(z[:, C_CKV:C_CKV + KV_LORA], wl["g_ckv"], dckvn)
    g["g_cq"], g["g_ckv"] = dg_cq[0], dg_ckv[0]
    tail = jnp.concatenate([dxk_sum[:, MLA_NOPE:MLA_QK], dtail_f[:, TAIL_F:]], axis=1)
    dz = jnp.concatenate([dcq, dckv, dxfq, dxfk, dv_fox, tail], axis=1).astype(BF16)
    g["w_in"] = _mm(hn, dz, ta=True, name="d_w_in_attn")
    dhn = _mm(dz, wl["w_in"], tb=True, out_dtype=BF16, name="d_hn_attn")
    return dhn, g, (exchanged[0] if exchanged else ())


def _local_step(x, target, w, gather_late=None, exchange_early=None):
    seq = x.shape[0]
    t = seq + BLOCK
    rope = _rope_tables(t)
    h = jnp.concatenate([jnp.zeros((PAD, D_MODEL), F32), w["meta_tokens"], x], axis=0)
    tape = []
    for layer in range(DEPTH):
        j = layer // 2
        hn = _rms_fwd(h, w["g_mix"][layer])
        if layer % 2 == 0:
            wl = dict(w_in=w["w_in_attn"][j], g_cq=w["g_cq"][j], w_uq=w["w_uq"][j], g_ckv=w["g_ckv"][j],
                      w_ukv=w["w_ukv"][j], g_q_mla=w["g_q_mla"][j], g_k_mla=w["g_k_mla"][j],
                      g_q_fox=w["g_q_fox"][j], g_k_fox=w["g_k_fox"][j], b_forget=w["b_forget"][j])
            hosted = gather_late is not None and layer == 0
            mixed, sv, gathered = _attn_layer_fwd(hn, wl, rope, gather_late[0] if hosted else ())
            if hosted:
                gather_late[1](w, gathered)
            h1 = _mm(mixed, w["w_out_attn"][j], res=h, name="attn_out")
        else:
            wl = None
            z3 = _mm(hn, w["w_in_conv"][j], out_seg=3, name="conv_in")
            mixed = _conv_fwd(z3, w["conv_w"][j])
            sv = dict(z3=z3)
            h1 = _mm(mixed, w["w_out_conv"][j], res=h, name="conv_out")
        hn2 = _rms_fwd(h1, w["g_mlp"][layer])
        u, act = _mm(hn2, w["w_mlp_up"][layer], epi="relu2", name="mlp_up")
        h2 = _mm(act, w["w_mlp_down"][layer], res=h1, name="mlp_down")
        tape.append(dict(h=h, hn=hn, wl=wl, sv=sv, mixed=mixed, h1=h1, hn2=hn2, u=u, act=act))
        h = h2

    dh, dh_b, loss = _loss_head(h, target)
    exchanged = ()
    g = {n: [None] * (DEPTH if n in ("g_mix", "g_mlp", "w_mlp_up", "w_mlp_down") else DEPTH // 2)
         for n in WEIGHTS if n != "meta_tokens"}
    for layer in reversed(range(DEPTH)):
        j = layer // 2
        tp = tape[layer]
        g["w_mlp_down"][layer] = _mm(tp["act"], dh_b, ta=True, name="d_w_down")
        du = _mm(dh_b, w["w_mlp_down"][layer], tb=True, epi="relu2_bwd", aux=tp["u"], out_dtype=BF16, name="d_u")
        g["w_mlp_up"][layer] = _mm(tp["hn2"], du, ta=True, name="d_w_up")
        dhn2 = _mm(du, w["w_mlp_up"][layer], tb=True, out_dtype=BF16, name="d_hn2")
        dh1, dh1_b, dg = _rms_bwd(tp["h1"], w["g_mlp"][layer], dhn2, dres=dh, want_bf16=True)
        g["g_mlp"][layer] = dg[0]
        if layer % 2 == 0:
            g["w_out_attn"][j] = _mm(tp["mixed"], dh1_b, ta=True, name="d_w_out_attn")
            dcat = _mm(dh1_b, w["w_out_attn"][j], tb=True, name="d_cat")
            hosted = exchange_early is not None and layer == 0
            dhn, gl, got = _attn_layer_bwd(dcat, tp["hn"], tp["wl"], tp["sv"], rope,
                                           exchange_early(g) if hosted else ())
            if hosted:
                exchanged = got
            g["w_in_attn"][j] = _unpermute_in_attn(gl.pop("w_in"))
            for n, val in gl.items():
                g[n][j] = val
        else:
            g["w_out_conv"][j] = _mm(tp["mixed"], dh1_b, ta=True, name="d_w_out_conv")
            dyb = _mm(dh1_b, w["w_out_conv"][j], tb=True, name="d_yb")
            dz3, dcw = _conv_bwd(tp["sv"]["z3"], w["conv_w"][j], dyb)
            g["conv_w"][j] = dcw
            g["w_in_conv"][j] = _mm(tp["hn"], dz3, ta=True, name="d_w_in_conv")
            dhn = _mm(dz3, w["w_in_conv"][j], tb=True, out_dtype=BF16, name="d_hn_conv")
        dh, dh_b, dg = _rms_bwd(tp["h"], w["g_mix"][layer], dhn, dres=dh1, want_bf16=True)
        g["g_mix"][layer] = dg[0]
    g["meta_tokens"] = [dh[PAD:BLOCK]]
    return loss, dh[BLOCK:], g, exchanged


COMM_ROWS = 2048
ADAMW_BLOCK_BYTES = 1 << 20


def kernel(x, meta_tokens, g_mix, g_mlp, w_in_attn, g_cq, w_uq, g_ckv, w_ukv, g_q_mla, g_k_mla, g_q_fox, g_k_fox, b_forget, w_out_attn, w_in_conv, conv_w, w_out_conv, w_mlp_up, w_mlp_down, loss_target, m_meta_tokens, m_g_mix, m_g_mlp, m_w_in_attn, m_g_cq, m_w_uq, m_g_ckv, m_w_ukv, m_g_q_mla, m_g_k_mla, m_g_q_fox, m_g_k_fox, m_b_forget, m_w_out_attn, m_w_in_conv, m_conv_w, m_w_out_conv, m_w_mlp_up, m_w_mlp_down, v_meta_tokens, v_g_mix, v_g_mlp, v_w_in_attn, v_g_cq, v_w_uq, v_g_ckv, v_w_ukv, v_g_q_mla, v_g_k_mla, v_g_q_fox, v_g_k_fox, v_b_forget, v_w_out_attn, v_w_in_conv, v_conv_w, v_w_out_conv, v_w_mlp_up, v_w_mlp_down):
    args = dict(locals())
    local = {n: args[n] for n in WEIGHTS}
    mom = {n: args["m_" + n] for n in WEIGHTS}
    var = {n: args["v_" + n] for n in WEIGHTS}
    axis = dict(SHARDED)
    count = {n: local[n].shape[0] for n, _ in SHARDED if n != "meta_tokens"}
    piece = lambda src, p: src[p[0]] if p[1] is None else src[p[0]][p[1]]
    piece_axis = lambda p: axis[p[0]] - (0 if p[1] is None else 1)
    shape_of = lambda p: piece(local, p).shape
    layers = lambda n, ls: [(n, l) for l in ls]
    first_bf = [("w_uq", 0), ("w_ukv", 0)]
    first_f32 = [("meta_tokens", None), ("conv_w", 0), ("conv_w", 1)]
    late_bf = ([("w_uq", 1), ("w_ukv", 1)] + layers("w_out_attn", (0, 1)) + layers("w_in_conv", (0, 1))
               + layers("w_out_conv", (0, 1)) + layers("w_mlp_up", range(DEPTH)) + layers("w_mlp_down", range(DEPTH)))
    early_bf = ([("w_uq", 1), ("w_ukv", 1)] + layers("w_out_attn", (0, 1)) + layers("w_in_conv", (0, 1))
                + layers("conv_w", (0, 1)) + layers("w_out_conv", (0, 1)) + layers("w_mlp_up", range(DEPTH))
                + layers("w_mlp_down", range(DEPTH)))
    last_bf = [("w_uq", 0), ("w_ukv", 0), ("meta_tokens", None)]

    w = {n: local[n] for n in REPLICATED}
    w.update({n: [None] * c for n, c in count.items()})

    def install(w, pieces, gathered, in_layer, gathered_in):
        for p, blocks in zip(pieces, _unpack(gathered, [shape_of(p) for p in pieces], (N_DEV,))):
            value = _from_shards(blocks, piece_axis(p))
            if p[1] is None:
                w[p[0]] = value
            else:
                w[p[0]][p[1]] = value
        w["w_in_attn"][in_layer] = _permute_in_attn(_from_shards(gathered_in, piece_axis(("w_in_attn", 0))))

    got_bf, got_in, got_f32 = _all_gather([
        _pack([piece(local, p) for p in first_bf], 16, BF16), local["w_in_attn"][0].astype(BF16),
        _pack([piece(local, p) for p in first_f32], 8, F32)])
    install(w, first_bf, got_bf, 0, got_in)
    for p, blocks in zip(first_f32, _unpack(got_f32, [shape_of(p) for p in first_f32], (N_DEV,))):
        if p[1] is None:
            w[p[0]] = _from_shards(blocks, piece_axis(p))
        else:
            w[p[0]][p[1]] = _from_shards(blocks, piece_axis(p))
    gather_late = ([_pack([piece(local, p) for p in late_bf], 16, BF16), local["w_in_attn"][1].astype(BF16)],
                   lambda w_, got: install(w_, late_bf, got[0], 1, got[1]))

    def pack_grads(g, pieces, in_layer):
        sent = _pack_rows([_to_shards(piece(g, p), piece_axis(p)) for p in pieces], COMM_ROWS, BF16)
        return [sent, _to_shards(g["w_in_attn"][in_layer], piece_axis(("w_in_attn", 0))).astype(BF16)]

    loss_part, dx, grads, early = _local_step(x[0], loss_target[0], w, gather_late,
                                              lambda g: pack_grads(g, early_bf, 1))

    grads["meta_tokens"] = grads["meta_tokens"][0]
    last = _all_to_all(pack_grads(grads, last_bf, 0))
    g_piece = {}
    for pieces, (got, got_in), in_layer in ((early_bf, early, 1), (last_bf, last, 0)):
        summed = _unpack(_sum_blocks(got, COMM_ROWS), [shape_of(p) for p in pieces])
        g_piece.update(zip(pieces, summed))
        g_piece[("w_in_attn", in_layer)] = _sum_blocks(got_in, 256)
    g_local = {n: jnp.stack([g_piece[(n, l)] for l in range(c)]) for n, c in count.items()}
    g_local["meta_tokens"] = g_piece[("meta_tokens", None)]
    rep, = _all_gather([_pack([jnp.stack(grads[n]) for n in REPLICATED] + [loss_part], 8, F32)])
    g_rep = _sum_blocks(rep, rep.shape[1])
    *g_reps, loss = _unpack(g_rep, [local[n].shape for n in REPLICATED] + [()])
    g_local.update(zip(REPLICATED, g_reps))

    def flat(src, names, rows_multiple):
        return _pack([src[n] for n in names], rows_multiple, F32)

    upd = {}
    rows = g_rep.shape[0]
    outs = _adamw(flat(local, REPLICATED, rows), g_rep, flat(mom, REPLICATED, rows), flat(var, REPLICATED, rows), rows)
    for kind, buf in zip(("delta", "m", "v"), outs):
        upd.update({(kind, n): a for n, a in zip(REPLICATED, _unpack(buf, [local[n].shape for n in REPLICATED]))})
    for n, _ in SHARDED:
        as_rows = lambda a: a.reshape(-1, a.shape[-1])
        n_rows, n_cols = as_rows(local[n]).shape
        tiles = [r for r in (1024, 512, 256, 128, 64, 32, 16, 8) if r * n_cols * 4 <= ADAMW_BLOCK_BYTES]
        outs = _adamw(as_rows(local[n]), as_rows(g_local[n]), as_rows(mom[n]), as_rows(var[n]), _pick(n_rows, tiles))
        for kind, buf in zip(("delta", "m", "v"), outs):
            upd[(kind, n)] = buf.reshape(local[n].shape)

    return (loss, dx[None], *[g_local[n] for n in WEIGHTS], *[upd[("delta", n)] for n in WEIGHTS],
            *[upd[("m", n)] for n in WEIGHTS], *[upd[("v", n)] for n in WEIGHTS])
```

```python
import functools
import math

import jax
import jax.numpy as jnp
import numpy as np
from jax import lax
from jax.experimental import pallas as pl
from jax.experimental.pallas import tpu as pltpu

F32 = jnp.float32
BF16 = jnp.bfloat16

N_DEV = 8
D_MODEL = 1024
DEPTH = 4
N_META = 16
BLOCK = 128
PAD = BLOCK - N_META
HEADS = 8
MLA_NOPE = 64
MLA_ROPE = 32
MLA_QK = MLA_NOPE + MLA_ROPE
MLA_V = 64
Q_LORA = 384
KV_LORA = 256
ROPE_BASE = 10000.0
FOX_DIM = 64
D_FF = 4 * D_MODEL
EPS = 1e-6
NEG = -1e30
LANES = 128
ATTN_IN_PAD = 2304
C_CQ, C_CKV, C_FQ, C_FK, C_FV, C_TAIL = 0, 384, 640, 1152, 1664, 2176
TAIL_F = MLA_ROPE

ADAM_LR = 0.001
ADAM_B1 = 0.9
ADAM_B2 = 0.999
ADAM_EPS = 1e-08
ADAM_WD = 0.01
ADAM_STEP = 10

VMEM_LIMIT_BYTES = 48 * 1024 * 1024
MM_VMEM_BUDGET_BYTES = 28 * 1024 * 1024
MESH = pl.DeviceIdType.MESH

SHARDED = (
    ("meta_tokens", 1), ("w_in_attn", 2), ("w_uq", 2), ("w_ukv", 2), ("w_out_attn", 1),
    ("w_in_conv", 2), ("conv_w", 2), ("w_out_conv", 1), ("w_mlp_up", 2), ("w_mlp_down", 1))
F32_GATHERED = ("meta_tokens", "conv_w")
REPLICATED = ("g_mix", "g_mlp", "g_cq", "g_ckv", "g_q_mla", "g_k_mla", "g_q_fox", "g_k_fox", "b_forget")
WEIGHTS = ("meta_tokens", "g_mix", "g_mlp", "w_in_attn", "g_cq", "w_uq", "g_ckv", "w_ukv", "g_q_mla",
           "g_k_mla", "g_q_fox", "g_k_fox", "b_forget", "w_out_attn", "w_in_conv", "conv_w",
           "w_out_conv", "w_mlp_up", "w_mlp_down")


def _params(*sem):
    return pltpu.CompilerParams(dimension_semantics=sem, vmem_limit_bytes=VMEM_LIMIT_BYTES)


def _row_tile(t):
    return 640 if (t % 640 == 0 and t > 640) else 128


def _padded_rows(n, rows_multiple):
    rows = -(-n // LANES)
    return -(-rows // rows_multiple) * rows_multiple


def _pack(parts, rows_multiple, dtype):
    n = sum(p.size for p in parts)
    rows = _padded_rows(n, rows_multiple)
    fill = [jnp.zeros((rows * LANES - n,), dtype)]
    return jnp.concatenate([p.reshape(-1).astype(dtype) for p in parts] + fill).reshape(rows, LANES)


def _pack_rows(parts, rows_multiple, dtype):
    n = sum(p.size for p in parts) // N_DEV
    rows = _padded_rows(n, rows_multiple)
    fill = [jnp.zeros((N_DEV, rows * LANES - n), dtype)]
    flat = jnp.concatenate([p.reshape(N_DEV, -1).astype(dtype) for p in parts] + fill, axis=1)
    return flat.reshape(N_DEV, rows, LANES)


def _unpack(buf, shapes, lead=()):
    flat = buf.reshape(lead + (-1,))
    out, off = [], 0
    for s in shapes:
        n = math.prod(s)
        out.append(flat[..., off:off + n].reshape(lead + tuple(s)))
        off += n
    return out


def _to_shards(full, axis):
    s = full.shape
    return jnp.moveaxis(full.reshape(s[:axis] + (N_DEV, s[axis] // N_DEV) + s[axis + 1:]), axis, 0)


def _from_shards(g8, axis):
    m = jnp.moveaxis(g8, 0, axis)
    s = m.shape
    return m.reshape(s[:axis] + (s[axis] * s[axis + 1],) + s[axis + 2:])


def _comm_call(body, name, xs, out_shapes):
    n = len(xs)
    hbm = pl.BlockSpec(memory_space=pltpu.HBM)
    return pl.pallas_call(
        body, name=name, out_shape=out_shapes, in_specs=[hbm] * n, out_specs=[hbm] * n,
        scratch_shapes=[pltpu.SemaphoreType.DMA((n, 7)), pltpu.SemaphoreType.DMA((n, 7)),
                        pltpu.SemaphoreType.DMA((n,))],
    )(*xs)


def _gather_ops(x_refs, out_refs, send_sems, recv_sems, local_sems):
    n = len(x_refs)
    x_, y_, c = lax.axis_index("x"), lax.axis_index("y"), lax.axis_index("c")
    me, sibling = (x_, y_, c), (x_, y_, 1 - c)
    chips = [(1 - x_, y_), (x_, 1 - y_), (1 - x_, 1 - y_)]

    def rows(a, px, py, pc):
        return out_refs[a].at[4 * px + 2 * py + pc]

    def copy(a, k, block, to, src=None):
        return pltpu.make_async_remote_copy(
            src_ref=rows(a, *block) if src is None else src, dst_ref=rows(a, *block),
            send_sem=send_sems.at[a, k], recv_sem=recv_sems.at[a, k], device_id=to, device_id_type=MESH)

    def mine():
        return [pltpu.make_async_copy(x_refs[a], rows(a, *me), local_sems.at[a]) for a in range(n)]

    def first():
        cps = []
        for a in range(n):
            cps.append(copy(a, 0, me, sibling, src=x_refs[a]))
            cps += [copy(a, 1 + j, me, (*chip, c), src=x_refs[a]) for j, chip in enumerate(chips)]
        return cps

    def start():
        for cp in mine() + first():
            cp.start()

    def finish():
        passed = []
        for j, chip in enumerate(chips):
            for a in range(n):
                copy(a, 1 + j, (*chip, c), me).wait_recv()
                passed.append(copy(a, 4 + j, (*chip, c), sibling))
                passed[-1].start()
        for a in range(n):
            copy(a, 0, sibling, me).wait_recv()
            for j, chip in enumerate(chips):
                copy(a, 4 + j, (*chip, 1 - c), me).wait_recv()
        for cp in first() + passed:
            cp.wait_send()
        for cp in mine():
            cp.wait()

    return start, finish


def _exchange_ops(x_refs, out_refs, send_sems, recv_sems, local_sems):
    n = len(x_refs)
    x_, y_, c = lax.axis_index("x"), lax.axis_index("y"), lax.axis_index("c")
    me = 4 * x_ + 2 * y_ + c

    def peer(k):
        px = 1 - x_ if k & 4 else x_
        py = 1 - y_ if k & 2 else y_
        pc = 1 - c if k & 1 else c
        return px, py, pc

    def copy(a, k):
        px, py, pc = peer(k)
        return pltpu.make_async_remote_copy(
            src_ref=x_refs[a].at[4 * px + 2 * py + pc], dst_ref=out_refs[a].at[me],
            send_sem=send_sems.at[a, k - 1], recv_sem=recv_sems.at[a, k - 1], device_id=(px, py, pc),
            device_id_type=MESH)

    def arrival(a, k):
        px, py, pc = peer(k)
        slot = 4 * px + 2 * py + pc
        return pltpu.make_async_remote_copy(
            src_ref=x_refs[a].at[slot], dst_ref=out_refs[a].at[slot],
            send_sem=send_sems.at[a, k - 1], recv_sem=recv_sems.at[a, k - 1], device_id=(px, py, pc),
            device_id_type=MESH)

    def mine():
        return [pltpu.make_async_copy(x_refs[a].at[me], out_refs[a].at[me], local_sems.at[a]) for a in range(n)]

    def sends():
        return [copy(a, k) for k in range(1, N_DEV) for a in range(n)]

    def start():
        for cp in mine() + sends():
            cp.start()

    def finish():
        for k in range(1, N_DEV):
            for a in range(n):
                arrival(a, k).wait_recv()
        for cp in sends():
            cp.wait_send()
        for cp in mine():
            cp.wait()

    return start, finish


def _comm_parts(xs, gather):
    n = len(xs)
    hbm = pl.BlockSpec(memory_space=pltpu.HBM)
    shapes = [jax.ShapeDtypeStruct(((N_DEV,) + x.shape) if gather else x.shape, x.dtype) for x in xs]
    sems = [pltpu.SemaphoreType.DMA((n, 7)), pltpu.SemaphoreType.DMA((n, 7)), pltpu.SemaphoreType.DMA((n,))]
    return [hbm] * n, [hbm] * n, shapes, sems


def _all_gather(xs):
    n = len(xs)

    def body(*refs):
        start, finish = _gather_ops(refs[:n], refs[n:2 * n], *refs[2 * n:])
        start()
        finish()

    return _comm_call(body, "all_gather", xs, [jax.ShapeDtypeStruct((N_DEV,) + x.shape, x.dtype) for x in xs])


def _all_to_all(xs):
    n = len(xs)

    def body(*refs):
        start, finish = _exchange_ops(refs[:n], refs[n:2 * n], *refs[2 * n:])
        start()
        finish()

    return _comm_call(body, "all_to_all", xs, [jax.ShapeDtypeStruct(x.shape, x.dtype) for x in xs])


def _sum_blocks(x, rows_tile):
    _, r, c_ = x.shape

    def body(x_ref, o_ref):
        acc = x_ref[0].astype(F32)
        for d in range(1, N_DEV):
            acc = acc + x_ref[d].astype(F32)
        o_ref[...] = acc

    return pl.pallas_call(
        body, name="sum_blocks", grid=(r // rows_tile,),
        in_specs=[pl.BlockSpec((N_DEV, rows_tile, c_), lambda i: (0, i, 0))],
        out_specs=pl.BlockSpec((rows_tile, c_), lambda i: (i, 0)),
        out_shape=jax.ShapeDtypeStruct((r, c_), F32),
        compiler_params=_params("parallel"),
    )(x)


def _adamw(w, g, m, v, rows_tile):
    r, c_ = w.shape
    c1 = 1.0 - ADAM_B1 ** ADAM_STEP
    c2 = 1.0 - ADAM_B2 ** ADAM_STEP

    def body(w_ref, g_ref, m_ref, v_ref, d_ref, mo_ref, vo_ref):
        g_ = g_ref[...]
        m_ = ADAM_B1 * m_ref[...] + (1.0 - ADAM_B1) * g_
        v_ = ADAM_B2 * v_ref[...] + (1.0 - ADAM_B2) * (g_ * g_)
        m_hat = m_ / c1
        v_hat = v_ / c2
        d_ref[...] = -ADAM_LR * (m_hat / (jnp.sqrt(v_hat) + ADAM_EPS) + ADAM_WD * w_ref[...])
        mo_ref[...] = m_
        vo_ref[...] = v_

    spec = pl.BlockSpec((rows_tile, c_), lambda i: (i, 0))
    shape = jax.ShapeDtypeStruct((r, c_), F32)
    return pl.pallas_call(
        body, name="adamw", grid=(r // rows_tile,), in_specs=[spec] * 4, out_specs=[spec] * 3,
        out_shape=[shape] * 3, compiler_params=_params("parallel"),
    )(w, g, m, v)


def _pick(n, prefs):
    for p in prefs:
        if n % p == 0:
            return p
    return n


def _mat_spec(arr, tr, tc, r_of, c_of):
    if arr.ndim == 2:
        return pl.BlockSpec((tr, tc), lambda i, j, k: (r_of(i, j, k), c_of(i, j, k)))
    per = arr.shape[2] // tc
    return pl.BlockSpec((None, tr, tc), lambda i, j, k: (c_of(i, j, k) // per, r_of(i, j, k), c_of(i, j, k) % per))


def _mm(a, b, *, ta=False, tb=False, out_dtype=F32, out_seg=None, res=None, epi=None, aux=None, norm_gain=None,
        tm=None, tn=None, tk=None, name="mm"):
    def dims(x):
        return (x.shape[0], x.shape[1]) if x.ndim == 2 else (x.shape[1], x.shape[0] * x.shape[2])
    ar, ac = dims(a)
    br, bc = dims(b)
    m, k = (ac, ar) if ta else (ar, ac)
    n, kb = (br, bc) if tb else (bc, br)
    assert k == kb, (a.shape, b.shape, ta, tb)
    tn = tn or _pick(n, (1024, 768, 512, 384, 256, 128))
    tk = tk or _pick(k, ((1664,) if ta else (2048,)) + (1024, 768, 640, 512, 384, 256, 128))

    def vmem_bytes(rows):
        out_bytes = 2 * 2 if epi == "relu2" else jnp.dtype(out_dtype).itemsize + (2 if norm_gain is not None else 0)
        x_bytes = sum(x.dtype.itemsize for x in (res, aux) if x is not None)
        return 2 * (rows * tk * 2 + tk * tn * 2 + rows * tn * (out_bytes + x_bytes)) + (rows * tn * 4 if k > tk else 0)

    tall = (1664,) if (not ta and m % 1664 == 0 and vmem_bytes(1664) <= MM_VMEM_BUDGET_BYTES) else ()
    tm = tm or _pick(m, (1024, 512, 384, 256, 128) if ta else tall + (640, 512, 384, 256, 128))
    if out_seg:
        assert (n // out_seg) % tn == 0
    for x, t in ((a, tm if ta else tk), (b, tk if tb else tn)):
        if x.ndim == 3:
            assert x.shape[2] % t == 0
    nk = k // tk
    gi, gj, gk = (lambda j, i, kk: i), (lambda j, i, kk: j), (lambda j, i, kk: kk)
    a_spec = _mat_spec(a, tk, tm, gk, gi) if ta else _mat_spec(a, tm, tk, gi, gk)
    b_spec = _mat_spec(b, tn, tk, gj, gk) if tb else _mat_spec(b, tk, tn, gk, gj)
    out_like = jax.ShapeDtypeStruct((out_seg, m, n // out_seg) if out_seg else (m, n), out_dtype)
    o_spec = _mat_spec(out_like, tm, tn, gi, gj)
    o_shape = (out_seg, m, n // out_seg) if out_seg else (m, n)
    dn = (((0 if ta else 1,), (1 if tb else 0,)), ((), ()))
    extra = [x for x in (res, aux) if x is not None]
    assert not (res is not None and aux is not None)
    normed = norm_gain is not None
    assert not normed or (res is not None and tn == n and not out_seg)
    n_out = 2 if (epi == "relu2" or normed) else 1
    n_extra = len(extra)

    def body(*refs):
        a_ref, b_ref = refs[0], refs[1]
        x_ref = refs[2] if n_extra else None
        g_ref = refs[2 + n_extra] if normed else None
        outs = refs[2 + n_extra + normed:2 + n_extra + normed + n_out]
        acc_ref = refs[-1] if nk > 1 else None
        part = lax.dot_general(a_ref[...], b_ref[...], dn, preferred_element_type=F32)

        def finish(acc):
            if epi == "relu2":
                r = jnp.maximum(acc, 0.0)
                outs[0][...] = r.astype(BF16)
                outs[1][...] = (r * r).astype(BF16)
            elif epi == "relu2_bwd":
                outs[0][...] = (acc * (2.0 * x_ref[...].astype(F32))).astype(out_dtype)
            elif res is not None:
                h = acc + x_ref[...]
                outs[0][...] = h.astype(out_dtype)
                if normed:
                    rstd = lax.rsqrt(jnp.mean(h * h, axis=-1, keepdims=True) + EPS)
                    outs[1][...] = (h * rstd * g_ref[...]).astype(BF16)
            else:
                outs[0][...] = acc.astype(out_dtype)

        if nk == 1:
            finish(part)
        else:
            kk = pl.program_id(2)

            @pl.when(kk == 0)
            def _():
                acc_ref[...] = part

            @pl.when(kk > 0)
            def _():
                acc_ref[...] += part

            @pl.when(kk == nk - 1)
            def _():
                finish(acc_ref[...])

    if epi == "relu2":
        out_shape = [jax.ShapeDtypeStruct(o_shape, BF16), jax.ShapeDtypeStruct(o_shape, BF16)]
        out_specs = [o_spec, o_spec]
    elif normed:
        out_shape = [jax.ShapeDtypeStruct(o_shape, out_dtype), jax.ShapeDtypeStruct(o_shape, BF16)]
        out_specs = [o_spec, o_spec]
    else:
        out_shape = jax.ShapeDtypeStruct(o_shape, out_dtype)
        out_specs = o_spec
    x_specs = [pl.BlockSpec((tm, tn), lambda j, i, kk: (i, j))] * len(extra)
    if normed:
        x_specs = x_specs + [pl.BlockSpec((1, tn), lambda j, i, kk: (0, 0))]
        extra = extra + [norm_gain.reshape(1, n)]
    res_ = pl.pallas_call(
        body, name=name, grid=(n // tn, m // tm, nk),
        in_specs=[a_spec, b_spec] + x_specs, out_specs=out_specs, out_shape=out_shape,
        scratch_shapes=[pltpu.VMEM((tm, tn), F32)] if nk > 1 else [],
        compiler_params=_params("parallel", "parallel", "arbitrary"),
    )(a, b, *extra)
    return res_


def _rms_fwd(x, g):
    t, d = x.shape
    tm = _row_tile(t)

    def body(x_ref, g_ref, o_ref):
        x_ = x_ref[...]
        rstd = lax.rsqrt(jnp.mean(x_ * x_, axis=-1, keepdims=True) + EPS)
        o_ref[...] = (x_ * rstd * g_ref[...]).astype(BF16)

    return pl.pallas_call(
        body, name="rms_fwd", grid=(t // tm,),
        in_specs=[pl.BlockSpec((tm, d), lambda i: (i, 0)), pl.BlockSpec((1, d), lambda i: (0, 0))],
        out_specs=pl.BlockSpec((tm, d), lambda i: (i, 0)),
        out_shape=jax.ShapeDtypeStruct((t, d), BF16), compiler_params=_params("parallel"),
    )(x, g.reshape(1, d))


def _rms_bwd(x, g, dy, dres=None, want_bf16=False):
    t, d = x.shape
    tm = _row_tile(t)
    has_res = dres is not None

    def body(*refs):
        x_ref, g_ref, dy_ref = refs[:3]
        r_ref = refs[3] if has_res else None
        outs = refs[3 + has_res:]
        x_ = x_ref[...]
        rstd = lax.rsqrt(jnp.mean(x_ * x_, axis=-1, keepdims=True) + EPS)
        xh = x_ * rstd
        dy_ = dy_ref[...].astype(F32)
        dxh = dy_ * g_ref[...]
        dx = rstd * (dxh - xh * jnp.mean(dxh * xh, axis=-1, keepdims=True))
        if has_res:
            dx = dx + r_ref[...]
        outs[0][...] = dx
        if want_bf16:
            outs[1][...] = dx.astype(BF16)
        dg_ref = outs[-1]

        @pl.when(pl.program_id(0) == 0)
        def _():
            dg_ref[...] = jnp.zeros_like(dg_ref)

        dg_ref[...] += jnp.sum(dy_ * xh, axis=0, keepdims=True)

    row = pl.BlockSpec((tm, d), lambda i: (i, 0))
    vec = pl.BlockSpec((1, d), lambda i: (0, 0))
    out_shape = [jax.ShapeDtypeStruct((t, d), F32)] + ([jax.ShapeDtypeStruct((t, d), BF16)] if want_bf16 else []) \
        + [jax.ShapeDtypeStruct((1, d), F32)]
    out_specs = [row] + ([row] if want_bf16 else []) + [vec]
    return pl.pallas_call(
        body, name="rms_bwd", grid=(t // tm,),
        in_specs=[row, vec, row] + ([row] if has_res else []), out_specs=out_specs, out_shape=out_shape,
        compiler_params=_params("arbitrary"),
    )(x, g.reshape(1, d), dy, *([dres] if has_res else []))


def _swap_rope_halves(y):
    lane = lax.broadcasted_iota(jnp.int32, y.shape, 1)
    half = MLA_ROPE // 2
    swapped = jnp.where(lane < MLA_NOPE + half, pltpu.roll(y, LANES - half, axis=1), pltpu.roll(y, half, axis=1))
    return jnp.where((lane >= MLA_NOPE) & (lane < MLA_QK), swapped, 0.0)


def _head_norm_fwd(x, g, n_valid, scale, rope=None):
    t, hw = x.shape
    w = g.shape[1]
    tm = _row_tile(t)

    def body(*refs):
        x_ref, g_ref = refs[:2]
        o_ref = refs[-1]
        gain = g_ref[...] * scale
        for n in range(hw // w):
            cols = slice(n * w, (n + 1) * w)
            x_ = x_ref[:, cols]
            rstd = lax.rsqrt(jnp.sum(x_ * x_, axis=-1, keepdims=True) * (1.0 / n_valid) + EPS)
            y = x_ * rstd * gain
            if rope is not None:
                y = y * refs[2][...] + _swap_rope_halves(y) * refs[3][...]
            o_ref[:, cols] = y.astype(BF16)

    row = pl.BlockSpec((tm, hw), lambda i: (i, 0))
    tab = pl.BlockSpec((tm, w), lambda i: (i, 0))
    return pl.pallas_call(
        body, name="head_norm_fwd", grid=(t // tm,),
        in_specs=[row, pl.BlockSpec((1, w), lambda i: (0, 0))] + ([tab, tab] if rope is not None else []),
        out_specs=row, out_shape=jax.ShapeDtypeStruct((t, hw), BF16),
        compiler_params=_params("parallel"),
    )(x, g, *(rope if rope is not None else ()))


def _head_norm_bwd(x, g, dout, n_valid, scale, rope=None):
    t, hw = x.shape
    w = g.shape[1]
    tm = _row_tile(t)

    def body(*refs):
        x_ref, g_ref, do_ref = refs[:3]
        dx_ref, dsum_ref, dg_ref = refs[-3:]
        gain = g_ref[...] * scale
        dsum = jnp.zeros((tm, w), F32)
        dg = jnp.zeros((1, w), F32)
        for n in range(hw // w):
            cols = slice(n * w, (n + 1) * w)
            dy = do_ref[:, cols]
            if rope is not None:
                dy = dy * refs[3][...] + _swap_rope_halves(dy * refs[4][...])
            x_ = x_ref[:, cols]
            rstd = lax.rsqrt(jnp.sum(x_ * x_, axis=-1, keepdims=True) * (1.0 / n_valid) + EPS)
            xh = x_ * rstd
            dxh = dy * gain
            dx = rstd * (dxh - xh * (jnp.sum(dxh * xh, axis=-1, keepdims=True) * (1.0 / n_valid)))
            dx_ref[:, cols] = dx.astype(BF16)
            dsum = dsum + dx
            dg = dg + jnp.sum(dy * xh, axis=0, keepdims=True)
        dsum_ref[...] = dsum

        @pl.when(pl.program_id(0) == 0)
        def _():
            dg_ref[...] = jnp.zeros_like(dg_ref)

        dg_ref[...] += scale * dg

    row = pl.BlockSpec((tm, hw), lambda i: (i, 0))
    tab = pl.BlockSpec((tm, w), lambda i: (i, 0))
    vec = pl.BlockSpec((1, w), lambda i: (0, 0))
    return pl.pallas_call(
        body, name="head_norm_bwd", grid=(t // tm,),
        in_specs=[row, vec, row] + ([tab, tab] if rope is not None else []),
        out_specs=[row, tab, vec],
        out_shape=[jax.ShapeDtypeStruct((t, hw), BF16), jax.ShapeDtypeStruct((t, w), F32),
                   jax.ShapeDtypeStruct((1, w), F32)],
        compiler_params=_params("arbitrary"),
    )(x, g, dout, *(rope if rope is not None else ()))


def _tri(n, upper):
    r = lax.broadcasted_iota(jnp.int32, (n, n), 0)
    c = lax.broadcasted_iota(jnp.int32, (n, n), 1)
    return ((r <= c) if upper else (r >= c)).astype(F32)


def _gate_mask(shape, row0):
    lane = lax.broadcasted_iota(jnp.int32, shape, 1)
    row = row0 + lax.broadcasted_iota(jnp.int32, shape, 0)
    return (lane >= TAIL_F) & (lane < TAIL_F + HEADS) & (row >= PAD)


def _gate_fwd(z, bias):
    t = z.shape[0]
    tm = BLOCK
    tail = C_TAIL // LANES

    def body(z_ref, b_ref, o_ref, carry):
        i = pl.program_id(0)

        @pl.when(i == 0)
        def _():
            carry[...] = jnp.zeros_like(carry)

        x_ = z_ref[...] + b_ref[...]
        logf = jnp.minimum(x_, 0.0) - jnp.log1p(jnp.exp(-jnp.abs(x_)))
        logf = jnp.where(_gate_mask(logf.shape, i * tm), logf, 0.0)
        cum = jnp.dot(_tri(tm, False), logf, preferred_element_type=F32, precision=lax.Precision.HIGHEST) + carry[...]
        o_ref[...] = cum
        carry[...] = cum[tm - 1:tm, :]

    return pl.pallas_call(
        body, name="gate_fwd", grid=(t // tm,),
        in_specs=[pl.BlockSpec((tm, LANES), lambda i: (i, tail)), pl.BlockSpec((1, LANES), lambda i: (0, 0))],
        out_specs=pl.BlockSpec((tm, LANES), lambda i: (i, 0)),
        out_shape=jax.ShapeDtypeStruct((t, LANES), F32),
        scratch_shapes=[pltpu.VMEM((1, LANES), F32)], compiler_params=_params("arbitrary"),
    )(z, bias)


def _gate_bwd(z, bias, dcum):
    t = z.shape[0]
    tm = BLOCK
    nb = t // tm
    tail = C_TAIL // LANES

    def body(z_ref, b_ref, d_ref, o_ref, db_ref, carry):
        i = pl.program_id(0)

        @pl.when(i == 0)
        def _():
            carry[...] = jnp.zeros_like(carry)
            db_ref[...] = jnp.zeros_like(db_ref)

        rc = jnp.dot(_tri(tm, True), d_ref[...], preferred_element_type=F32, precision=lax.Precision.HIGHEST) + carry[...]
        carry[...] = rc[0:1, :]
        x_ = z_ref[...] + b_ref[...]
        sig_neg = 1.0 / (1.0 + jnp.exp(x_))
        dl = jnp.where(_gate_mask(rc.shape, (nb - 1 - i) * tm), rc * sig_neg, 0.0)
        o_ref[...] = dl
        db_ref[...] += jnp.sum(dl, axis=0, keepdims=True)

    return pl.pallas_call(
        body, name="gate_bwd", grid=(nb,),
        in_specs=[pl.BlockSpec((tm, LANES), lambda i: (nb - 1 - i, tail)), pl.BlockSpec((1, LANES), lambda i: (0, 0)),
                  pl.BlockSpec((tm, LANES), lambda i: (nb - 1 - i, 0))],
        out_specs=[pl.BlockSpec((tm, LANES), lambda i: (nb - 1 - i, 0)), pl.BlockSpec((1, LANES), lambda i: (0, 0))],
        out_shape=[jax.ShapeDtypeStruct((t, LANES), F32), jax.ShapeDtypeStruct((1, LANES), F32)],
        scratch_shapes=[pltpu.VMEM((1, LANES), F32)], compiler_params=_params("arbitrary"),
    )(z, bias, dcum)


def _pairs(nb, by_query):
    if by_query:
        pr = [(i, j) for i in range(nb) for j in range(i + 1)]
    else:
        pr = [(i, j) for j in range(nb) for i in range(j, nb)]
    return (jnp.asarray(np.array([p[0] for p in pr], np.int32)),
            jnp.asarray(np.array([p[1] for p in pr], np.int32)))


HEADS_PER_STEP = 8


def _mask_scores(s, i, j, tile):
    qp = i * tile + lax.broadcasted_iota(jnp.int32, s.shape, 0)
    kp = j * tile + lax.broadcasted_iota(jnp.int32, s.shape, 1)
    return jnp.where((kp <= qp) & (kp >= PAD), s, NEG)


def _pipelined(n, front, back):
    nxt = front(0)
    for h in range(n):
        cur = nxt
        if h + 1 < n:
            nxt = front(h + 1)
        back(h, cur)


def _nt_dot(a, b):
    return lax.dot_general(a, b, (((1,), (1,)), ((), ())), preferred_element_type=F32)


def _tn_dot(a, b):
    return lax.dot_general(a, b, (((0,), (0,)), ((), ())), preferred_element_type=F32)


def _attn_specs(hb, tile, dk, dv):
    q_of = lambda g, p, it, jt: (it[p], g)
    k_of = lambda g, p, it, jt: (jt[p], g)
    return dict(
        q=pl.BlockSpec((tile, hb * dk), q_of), k=pl.BlockSpec((tile, hb * dk), k_of),
        v=pl.BlockSpec((tile, hb * dv), k_of), ov=pl.BlockSpec((tile, hb * dv), q_of),
        fr=pl.BlockSpec((hb, 1, tile), lambda g, p, it, jt: (g, 0, jt[p])),
        rowq=pl.BlockSpec((hb, 1, tile), lambda g, p, it, jt: (g, 0, it[p])))


def _head(ref, n, d):
    return ref[:, n * d:(n + 1) * d]


def _attn_fwd(q, k, v, key_bias=None, gather=()):
    h = HEADS
    t, dk, dv = q.shape[0], q.shape[1] // h, v.shape[1] // h
    tile = _row_tile(t)
    nb = t // tile
    hb = HEADS_PER_STEP
    biased = key_bias is not None
    it, jt = _pairs(nb, True)
    sp = _attn_specs(hb, tile, dk, dv)

    n_in = 4 if biased else 3
    n_g = len(gather)
    g_in, g_out, g_shapes, g_sems = _comm_parts(gather, True)
    n_steps = int(it.shape[0])

    def body(it_ref, jt_ref, *refs):
        q_ref, k_ref, v_ref = refs[:3]
        b_ref = refs[3] if biased else None
        o_ref, lse_ref = refs[n_in + 2 * n_g:n_in + 2 * n_g + 2]
        m_sc, l_sc, acc_sc = refs[n_in + 2 * n_g + 2:n_in + 2 * n_g + 5]
        p = pl.program_id(1)
        i, j = it_ref[p], jt_ref[p]
        if n_g:
            g_start, g_finish = _gather_ops(refs[n_in:n_in + n_g], refs[n_in + n_g:n_in + 2 * n_g],
                                            *refs[n_in + 2 * n_g + 5:])
            first = (pl.program_id(0) == 0) & (p == 0)
            last = (pl.program_id(0) == h // hb - 1) & (p == n_steps - 1)
            pl.when(first)(g_start)

        @pl.when(j == 0)
        def _():
            m_sc[...] = jnp.full_like(m_sc, NEG)
            l_sc[...] = jnp.zeros_like(l_sc)
            acc_sc[...] = jnp.zeros_like(acc_sc)

        def step(masked):
            def front(n):
                return _nt_dot(_head(q_ref, n, dk), _head(k_ref, n, dk))

            def back(n, s):
                if biased:
                    s = s + b_ref[n]
                if masked:
                    s = _mask_scores(s, i, j, tile)
                m_prev = m_sc[n]
                m_new = jnp.maximum(m_prev, jnp.max(s, axis=-1, keepdims=True))
                alpha = jnp.exp2(m_prev - m_new)
                e = jnp.exp2(s - m_new)
                l_sc[n] = alpha * l_sc[n] + jnp.sum(e, axis=-1, keepdims=True)
                acc_sc[n] = alpha * acc_sc[n] + jnp.dot(e.astype(BF16), _head(v_ref, n, dv),
                                                        preferred_element_type=F32)
                m_sc[n] = m_new

            _pipelined(hb, front, back)

        edge = (j == i) | (j == 0)
        pl.when(edge)(lambda: step(True))
        pl.when(jnp.logical_not(edge))(lambda: step(False))

        @pl.when(j == i)
        def _():
            row = i * tile + lax.broadcasted_iota(jnp.int32, (tile, 1), 0)
            for n in range(hb):
                o_ref[:, n * dv:(n + 1) * dv] = jnp.where(row >= PAD, acc_sc[n] / l_sc[n], 0.0)
                lse_ref[n] = jnp.transpose(m_sc[n] + jnp.log2(l_sc[n]))

        if n_g:
            pl.when(last)(g_finish)

    grid_spec = pltpu.PrefetchScalarGridSpec(
        num_scalar_prefetch=2, grid=(h // hb, n_steps),
        in_specs=[sp["q"], sp["k"], sp["v"]] + ([sp["fr"]] if biased else []) + g_in,
        out_specs=g_out + [sp["ov"], sp["rowq"]],
        scratch_shapes=[pltpu.VMEM((hb, tile, 1), F32), pltpu.VMEM((hb, tile, 1), F32),
                        pltpu.VMEM((hb, tile, dv), F32)] + (g_sems if n_g else []))
    outs = pl.pallas_call(
        body, name="attn_fwd_gather" if n_g else "attn_fwd", grid_spec=grid_spec,
        out_shape=g_shapes + [jax.ShapeDtypeStruct((t, h * dv), F32), jax.ShapeDtypeStruct((h, 1, t), F32)],
        compiler_params=_params("arbitrary" if n_g else "parallel", "arbitrary"),
    )(it, jt, q, k, v, *((key_bias,) if biased else ()), *gather)
    return (outs[n_g], outs[n_g + 1], outs[:n_g]) if n_g else tuple(outs)


LOG2E = 1.4426950408889634
LN2 = 0.6931471805599453
MLA_SCALE = MLA_QK ** -0.5 * LOG2E
FOX_SCALE = FOX_DIM ** -0.5 * LOG2E


def _attn_delta(o, do):
    h = HEADS
    t, dv = o.shape[0], o.shape[1] // h
    tm = _row_tile(t)

    def body(o_ref, do_ref, d_ref):
        prod = o_ref[...] * do_ref[...]
        for n in range(h):
            d_ref[n] = jnp.transpose(jnp.sum(prod[:, n * dv:(n + 1) * dv], axis=-1, keepdims=True) * LN2)

    blk = pl.BlockSpec((tm, h * dv), lambda i: (i, 0))
    return pl.pallas_call(
        body, name="attn_delta", grid=(t // tm,), in_specs=[blk, blk],
        out_specs=pl.BlockSpec((h, 1, tm), lambda i: (0, 0, i)),
        out_shape=jax.ShapeDtypeStruct((h, 1, t), F32), compiler_params=_params("parallel"),
    )(o, do)


BWD_HEADS_PER_STEP = (4, 4)


def _attn_bwd(q, k, v, do, lse_row, delta_row, key_bias=None, exchange=()):
    h = HEADS
    t, dk, dv = q.shape[0], q.shape[1] // h, v.shape[1] // h
    tile = _row_tile(t)
    nb = t // tile
    decay = key_bias is not None
    hb = BWD_HEADS_PER_STEP[int(decay)]
    it, jt = _pairs(nb, False)
    sp = _attn_specs(hb, tile, dk, dv)
    n_in = 7 if decay else 6
    n_out = 5 if decay else 3
    n_x = len(exchange)
    x_in, x_out, x_shapes, x_sems = _comm_parts(exchange, False)
    n_steps = int(it.shape[0])

    def body(it_ref, jt_ref, *refs):
        q_ref, k_ref, v_ref, do_ref, lse_ref, delta_ref = refs[:6]
        b_ref = refs[6] if decay else None
        outs = refs[n_in + n_x:]
        dq_ref, dk_ref, dv_ref = outs[:3]
        rs_ref, ks_ref = (outs[3], outs[4]) if decay else (None, None)
        scratch = outs[n_out + n_x:]
        dq_sc, dk_sc, dv_sc = scratch[:3]
        ks_sc, b_sc = (scratch[3], scratch[4]) if decay else (None, None)
        p = pl.program_id(1)
        i, j = it_ref[p], jt_ref[p]
        if n_x:
            x_start, x_finish = _exchange_ops(refs[n_in:n_in + n_x], outs[n_out:n_out + n_x],
                                              *scratch[5 if decay else 3:])
            first = (pl.program_id(0) == 0) & (p == 0)
            last = (pl.program_id(0) == h // hb - 1) & (p == n_steps - 1)
            pl.when(first)(x_start)

        @pl.when(p == 0)
        def _():
            dq_sc[...] = jnp.zeros_like(dq_sc)
            if decay:
                rs_ref[...] = jnp.zeros_like(rs_ref)

        @pl.when(i == j)
        def _():
            dk_sc[...] = jnp.zeros_like(dk_sc)
            dv_sc[...] = jnp.zeros_like(dv_sc)
            if decay:
                ks_sc[...] = jnp.zeros_like(ks_sc)
                for n in range(hb):
                    b_sc[n] = jnp.transpose(b_ref[n])

        def step(masked):
            def front(n):
                return (_nt_dot(_head(k_ref, n, dk), _head(q_ref, n, dk)),
                        _nt_dot(_head(v_ref, n, dv), (_head(do_ref, n, dv) * LN2).astype(BF16)))

            def back(n, s_dp):
                s, dp = s_dp
                if decay:
                    s = s + b_sc[n]
                if masked:
                    kp = j * tile + lax.broadcasted_iota(jnp.int32, s.shape, 0)
                    qp = i * tile + lax.broadcasted_iota(jnp.int32, s.shape, 1)
                    s = jnp.where((kp <= qp) & (kp >= PAD), s, NEG)
                pr = jnp.exp2(s - lse_ref[n])
                ds = pr * (dp - delta_ref[n])
                ds_b = ds.astype(BF16)
                dv_sc[n] += jnp.dot(pr.astype(BF16), _head(do_ref, n, dv).astype(BF16), preferred_element_type=F32)
                dk_sc[n] += jnp.dot(ds_b, _head(q_ref, n, dk), preferred_element_type=F32)
                dq_sc[n, i] += _tn_dot(ds_b, _head(k_ref, n, dk))
                if decay:
                    rs_ref[n, i] += jnp.sum(ds, axis=0, keepdims=True)
                    ks_sc[n] += jnp.sum(ds, axis=-1, keepdims=True)

            _pipelined(hb, front, back)

        edge = (j == i) | (j == 0)
        pl.when(edge)(lambda: step(True))
        pl.when(jnp.logical_not(edge))(lambda: step(False))

        @pl.when(i == j)
        def _():
            for n in range(hb):
                dq_ref[:, n * dk:(n + 1) * dk] = dq_sc[n, j]

        @pl.when(i == nb - 1)
        def _():
            for n in range(hb):
                dk_ref[:, n * dk:(n + 1) * dk] = dk_sc[n]
                dv_ref[:, n * dv:(n + 1) * dv] = dv_sc[n]
                if decay:
                    ks_ref[n] = jnp.transpose(ks_sc[n])

        if n_x:
            pl.when(last)(x_finish)

    rows_out = pl.BlockSpec((hb, nb, 1, tile), lambda hh, p, it, jt: (hh, 0, 0, 0))
    grid_spec = pltpu.PrefetchScalarGridSpec(
        num_scalar_prefetch=2, grid=(h // hb, n_steps),
        in_specs=[sp["q"], sp["k"], sp["v"], sp["ov"], sp["rowq"], sp["rowq"]] + ([sp["fr"]] if decay else [])
        + x_in,
        out_specs=[sp["k"], sp["k"], sp["v"]] + ([rows_out, sp["fr"]] if decay else []) + x_out,
        scratch_shapes=[pltpu.VMEM((hb, nb, tile, dk), F32), pltpu.VMEM((hb, tile, dk), F32),
                        pltpu.VMEM((hb, tile, dv), F32)] + ([pltpu.VMEM((hb, tile, 1), F32)] * 2 if decay else [])
        + (x_sems if n_x else []))
    out_shape = [jax.ShapeDtypeStruct((t, h * dk), F32), jax.ShapeDtypeStruct((t, h * dk), F32),
                 jax.ShapeDtypeStruct((t, h * dv), F32)] \
        + ([jax.ShapeDtypeStruct((h, nb, 1, tile), F32), jax.ShapeDtypeStruct((h, 1, t), F32)] if decay else []) \
        + x_shapes
    outs = pl.pallas_call(
        body, name="attn_bwd_exchange" if n_x else "attn_bwd", grid_spec=grid_spec, out_shape=out_shape,
        compiler_params=_params("arbitrary" if n_x else "parallel", "arbitrary"),
    )(it, jt, q, k, v, do, lse_row, delta_row, *((key_bias,) if decay else ()), *exchange)
    return tuple(outs[:n_out]) + ((list(outs[n_out:]),) if n_x else ())


CONV_COLS = 512


def _shift_down(g, prev, n):
    out = pltpu.roll(g, n, axis=0)
    row = lax.broadcasted_iota(jnp.int32, g.shape, 0)
    for r in range(n):
        out = jnp.where(row == r, prev[8 - n + r:8 - n + r + 1, :], out)
    return out


def _shift_up(g, nxt, n):
    tm = g.shape[0]
    out = pltpu.roll(g, tm - n, axis=0)
    row = lax.broadcasted_iota(jnp.int32, g.shape, 0)
    for r in range(n):
        out = jnp.where(row == tm - n + r, nxt[r:r + 1, :], out)
    return out


def _conv_fwd(z3, w):
    _, t, d = z3.shape
    tm = _row_tile(t)
    tc = CONV_COLS

    def body(z_ref, zp_ref, w_ref, o_ref):
        i = pl.program_id(1)
        g = z_ref[1] * z_ref[2]
        gp = jnp.where(i > 0, zp_ref[1] * zp_ref[2], 0.0)
        w_ = w_ref[...]
        y = w_[2:3] * g + w_[1:2] * _shift_down(g, gp, 1) + w_[0:1] * _shift_down(g, gp, 2)
        o_ref[...] = (z_ref[0] * y).astype(BF16)

    return pl.pallas_call(
        body, name="conv_fwd", grid=(d // tc, t // tm),
        in_specs=[pl.BlockSpec((3, tm, tc), lambda j, i: (0, i, j)),
                  pl.BlockSpec((3, 8, tc), lambda j, i: (0, jnp.maximum(i * (tm // 8) - 1, 0), j)),
                  pl.BlockSpec((3, tc), lambda j, i: (0, j))],
        out_specs=pl.BlockSpec((tm, tc), lambda j, i: (i, j)),
        out_shape=jax.ShapeDtypeStruct((t, d), BF16), compiler_params=_params("parallel", "parallel"),
    )(z3, z3, w)


def _conv_bwd(z3, w, dyb):
    _, t, d = z3.shape
    tm = _row_tile(t)
    tc = CONV_COLS
    ni = t // tm

    def body(z_ref, zp_ref, zn_ref, d_ref, dn_ref, w_ref, dz_ref, dw_ref):
        i = pl.program_id(1)
        gb, gc, u = z_ref[0], z_ref[1], z_ref[2]
        g = gc * u
        gp = jnp.where(i > 0, zp_ref[1] * zp_ref[2], 0.0)
        w_ = w_ref[...]
        g1, g2 = _shift_down(g, gp, 1), _shift_down(g, gp, 2)
        y = w_[2:3] * g + w_[1:2] * g1 + w_[0:1] * g2
        dyb_ = d_ref[...]
        dy = dyb_ * gb
        dyn = jnp.where(i < ni - 1, dn_ref[...] * zn_ref[0], 0.0)
        dg = w_[2:3] * dy + w_[1:2] * _shift_up(dy, dyn, 1) + w_[0:1] * _shift_up(dy, dyn, 2)
        dz_ref[0] = (dyb_ * y).astype(BF16)
        dz_ref[1] = (dg * u).astype(BF16)
        dz_ref[2] = (dg * gc).astype(BF16)

        @pl.when(i == 0)
        def _():
            dw_ref[...] = jnp.zeros_like(dw_ref)

        dw_ref[...] += jnp.concatenate([jnp.sum(dy * g2, axis=0, keepdims=True),
                                        jnp.sum(dy * g1, axis=0, keepdims=True),
                                        jnp.sum(dy * g, axis=0, keepdims=True)], axis=0)

    cur = pl.BlockSpec((3, tm, tc), lambda j, i: (0, i, j))
    return pl.pallas_call(
        body, name="conv_bwd", grid=(d // tc, ni),
        in_specs=[cur,
                  pl.BlockSpec((3, 8, tc), lambda j, i: (0, jnp.maximum(i * (tm // 8) - 1, 0), j)),
                  pl.BlockSpec((3, 8, tc), lambda j, i: (0, jnp.minimum((i + 1) * (tm // 8), t // 8 - 1), j)),
                  pl.BlockSpec((tm, tc), lambda j, i: (i, j)),
                  pl.BlockSpec((8, tc), lambda j, i: (jnp.minimum((i + 1) * (tm // 8), t // 8 - 1), j)),
                  pl.BlockSpec((3, tc), lambda j, i: (0, j))],
        out_specs=[cur, pl.BlockSpec((3, tc), lambda j, i: (0, j))],
        out_shape=[jax.ShapeDtypeStruct((3, t, d), BF16), jax.ShapeDtypeStruct((3, d), F32)],
        compiler_params=_params("parallel", "arbitrary"),
    )(z3, z3, z3, dyb, dyb, w)


def _loss_head(h, target):
    t, d = h.shape
    tm = BLOCK

    def body(h_ref, t_ref, dh_ref, dhb_ref, loss_ref):
        i = pl.program_id(0)

        @pl.when(i == 0)
        def _():
            loss_ref[...] = jnp.zeros_like(loss_ref)

        err = jnp.where(i > 0, h_ref[...] - t_ref[...], 0.0)
        dh = err * (1.0 / d)
        dh_ref[...] = dh
        dhb_ref[...] = dh.astype(BF16)
        loss_ref[...] += 0.5 * jnp.sum(jnp.sum(err * err, axis=-1, keepdims=True) * (1.0 / d), axis=0, keepdims=True)

    row = pl.BlockSpec((tm, d), lambda i: (i, 0))
    return pl.pallas_call(
        body, name="loss_head", grid=(t // tm,),
        in_specs=[row, pl.BlockSpec((tm, d), lambda i: (jnp.maximum(i - 1, 0), 0))],
        out_specs=[row, row, pl.BlockSpec((1, 1), lambda i: (0, 0))],
        out_shape=[jax.ShapeDtypeStruct((t, d), F32), jax.ShapeDtypeStruct((t, d), BF16),
                   jax.ShapeDtypeStruct((1, 1), F32)],
        compiler_params=_params("arbitrary"),
    )(h, target)


def _rope_tables(t):
    pos = jnp.arange(t, dtype=F32) - PAD
    inv_freq = ROPE_BASE ** (-jnp.arange(0, MLA_ROPE, 2, dtype=F32) / MLA_ROPE)
    ang = pos[:, None] * inv_freq[None, :]
    cos, sin = jnp.cos(ang), jnp.sin(ang)
    one, zero = jnp.ones((t, MLA_NOPE), F32), jnp.zeros((t, MLA_NOPE), F32)
    tail = jnp.zeros((t, LANES - MLA_QK), F32)
    return (jnp.concatenate([one, cos, cos, tail], axis=1), jnp.concatenate([zero, -sin, sin, tail], axis=1))


def _pad_lanes(x, width=LANES):
    return jnp.pad(x, [(0, 0)] * (x.ndim - 1) + [(0, width - x.shape[-1])])


def _permute_in_attn(w):
    return jnp.concatenate([w[:, :640], w[:, 672:2208], w[:, 640:672], w[:, 2208:2216],
                            jnp.zeros((w.shape[0], ATTN_IN_PAD - 2216), w.dtype)], axis=1)


def _unpermute_in_attn(dw):
    return jnp.concatenate([dw[:, :640], dw[:, 2176:2208], dw[:, 640:2176], dw[:, 2208:2216]], axis=1)


def _attn_layer_fwd(hn, wl, rope, gather=()):
    t = hn.shape[0]
    z = _mm(hn, wl["w_in"], name="attn_in")
    cqn = _rms_fwd(z[:, C_CQ:C_CQ + Q_LORA], wl["g_cq"])
    ckvn = _rms_fwd(z[:, C_CKV:C_CKV + KV_LORA], wl["g_ckv"])
    w_uq = _pad_lanes(wl["w_uq"].reshape(Q_LORA, HEADS, MLA_QK)).reshape(Q_LORA, HEADS * LANES)
    xq = _mm(cqn, w_uq, name="mla_uq")
    kvf = _mm(ckvn, wl["w_ukv"], name="mla_ukv")
    kv3 = kvf.reshape(t, HEADS, MLA_NOPE + MLA_V)
    k_pe = jnp.broadcast_to(z[:, None, C_TAIL:C_TAIL + MLA_ROPE], (t, HEADS, MLA_ROPE))
    xk = _pad_lanes(jnp.concatenate([kv3[:, :, :MLA_NOPE], k_pe], axis=-1)).reshape(t, HEADS * LANES)
    v_mla = kv3[:, :, MLA_NOPE:].reshape(t, HEADS * MLA_V).astype(BF16)
    gq, gk = _pad_lanes(wl["g_q_mla"].reshape(1, -1)), _pad_lanes(wl["g_k_mla"].reshape(1, -1))
    q_mla = _head_norm_fwd(xq, gq, MLA_QK, MLA_SCALE, rope)
    k_mla = _head_norm_fwd(xk, gk, MLA_QK, 1.0, rope)
    o_mla, lse_mla, *gathered = _attn_fwd(q_mla, k_mla, v_mla, gather=gather)
    xfq = z[:, C_FQ:C_FQ + HEADS * FOX_DIM]
    xfk = z[:, C_FK:C_FK + HEADS * FOX_DIM]
    v_fox = z[:, C_FV:C_FV + HEADS * FOX_DIM].astype(BF16)
    bias = jnp.pad(wl["b_forget"].reshape(1, -1), ((0, 0), (TAIL_F, LANES - TAIL_F - HEADS)))
    cum = _gate_fwd(z, bias)
    neg_f = (-LOG2E * jnp.transpose(cum[:, TAIL_F:TAIL_F + HEADS]))[:, None, :]
    q_fox = _head_norm_fwd(xfq, wl["g_q_fox"].reshape(1, -1), FOX_DIM, FOX_SCALE)
    k_fox = _head_norm_fwd(xfk, wl["g_k_fox"].reshape(1, -1), FOX_DIM, 1.0)
    o_fox, lse_fox = _attn_fwd(q_fox, k_fox, v_fox, neg_f)
    cat = jnp.concatenate([o_mla, o_fox], axis=1).astype(BF16)
    saved = dict(z=z, cqn=cqn, ckvn=ckvn, xq=xq, xk=xk, v_mla=v_mla, q_mla=q_mla, k_mla=k_mla, o_mla=o_mla,
                 lse_mla=lse_mla, xfq=xfq, xfk=xfk, v_fox=v_fox, q_fox=q_fox, k_fox=k_fox, bias=bias,
                 neg_f=neg_f, o_fox=o_fox, lse_fox=lse_fox, cat=cat, gq=gq, gk=gk, w_uq=w_uq)
    return cat, saved, (gathered[0] if gathered else ())


def _attn_layer_bwd(dcat, hn, wl, sv, rope, exchange=()):
    t = hn.shape[0]
    g = {}
    do_mla = dcat[:, :HEADS * MLA_V]
    do_fox = dcat[:, HEADS * MLA_V:]
    qkv = (sv["q_fox"], sv["k_fox"], sv["v_fox"])
    delta = _attn_delta(sv["o_fox"], do_fox)
    dq_fox, dk_fox, dv_fox, row_sums, key_sums, *exchanged = _attn_bwd(
        *qkv, do_fox, sv["lse_fox"], delta, sv["neg_f"], exchange=exchange)
    dcum = jnp.pad(jnp.transpose(LOG2E * (row_sums.reshape(HEADS, t) - key_sums.reshape(HEADS, t))),
                   ((0, 0), (TAIL_F, LANES - TAIL_F - HEADS)))
    dtail_f, dbias = _gate_bwd(sv["z"], sv["bias"], dcum)
    g["b_forget"] = dbias[0, TAIL_F:TAIL_F + HEADS]
    dxfq, _, dgq = _head_norm_bwd(sv["xfq"], wl["g_q_fox"].reshape(1, -1), dq_fox, FOX_DIM, FOX_SCALE)
    dxfk, _, dgk = _head_norm_bwd(sv["xfk"], wl["g_k_fox"].reshape(1, -1), dk_fox, FOX_DIM, 1.0)
    g["g_q_fox"], g["g_k_fox"] = dgq[0], dgk[0]
    qkv = (sv["q_mla"], sv["k_mla"], sv["v_mla"])
    delta = _attn_delta(sv["o_mla"], do_mla)
    dq_mla, dk_mla, dv_mla = _attn_bwd(*qkv, do_mla, sv["lse_mla"], delta)
    dxq, _, dgq = _head_norm_bwd(sv["xq"], sv["gq"], dq_mla, MLA_QK, MLA_SCALE, rope)
    dxk, dxk_sum, dgk = _head_norm_bwd(sv["xk"], sv["gk"], dk_mla, MLA_QK, 1.0, rope)
    g["g_q_mla"], g["g_k_mla"] = dgq[0, :MLA_QK], dgk[0, :MLA_QK]
    dqf = dxq.astype(BF16)
    dkvf = jnp.concatenate([dxk.reshape(t, HEADS, LANES)[:, :, :MLA_NOPE], dv_mla.reshape(t, HEADS, MLA_V)],
                           axis=-1).reshape(t, HEADS * (MLA_NOPE + MLA_V)).astype(BF16)
    g["w_uq"] = _mm(sv["cqn"], dqf, ta=True, name="d_w_uq").reshape(Q_LORA, HEADS, LANES)[:, :, :MLA_QK].reshape(
        Q_LORA, HEADS * MLA_QK)
    g["w_ukv"] = _mm(sv["ckvn"], dkvf, ta=True, name="d_w_ukv")
    dcqn = _mm(dqf, sv["w_uq"], tb=True, name="d_cqn")
    dckvn = _mm(dkvf, wl["w_ukv"], tb=True, name="d_ckvn")
    z = sv["z"]
    dcq, dg_cq = _rms_bwd(z[:, C_CQ:C_CQ + Q_LORA], wl["g_cq"], dcqn)
    dckv, dg_ckv = _rms_bwd(z[:, C_CKV:C_CKV + KV_LORA], wl["g_ckv"], dckvn)
    g["g_cq"], g["g_ckv"] = dg_cq[0], dg_ckv[0]
    tail = jnp.concatenate([dxk_sum[:, MLA_NOPE:MLA_QK], dtail_f[:, TAIL_F:]], axis=1)
    dz = jnp.concatenate([dcq, dckv, dxfq, dxfk, dv_fox, tail], axis=1).astype(BF16)
    g["w_in"] = _mm(hn, dz, ta=True, name="d_w_in_attn")
    dhn = _mm(dz, wl["w_in"], tb=True, out_dtype=BF16, name="d_hn_attn")
    return dhn, g, (exchanged[0] if exchanged else ())


def _local_step(x, target, w, gather_late=None, exchange_early=None):
    seq = x.shape[0]
    t = seq + BLOCK
    rope = _rope_tables(t)
    h = jnp.concatenate([jnp.zeros((PAD, D_MODEL), F32), w["meta_tokens"], x], axis=0)
    tape = []
    hn = _rms_fwd(h, w["g_mix"][0])
    for layer in range(DEPTH):
        j = layer // 2
        if layer % 2 == 0:
            wl = dict(w_in=w["w_in_attn"][j], g_cq=w["g_cq"][j], w_uq=w["w_uq"][j], g_ckv=w["g_ckv"][j],
                      w_ukv=w["w_ukv"][j], g_q_mla=w["g_q_mla"][j], g_k_mla=w["g_k_mla"][j],
                      g_q_fox=w["g_q_fox"][j], g_k_fox=w["g_k_fox"][j], b_forget=w["b_forget"][j])
            hosted = gather_late is not None and layer == 0
            mixed, sv, gathered = _attn_layer_fwd(hn, wl, rope, gather_late[0] if hosted else ())
            if hosted:
                gather_late[1](w, gathered)
            h1, hn2 = _mm(mixed, w["w_out_attn"][j], res=h, norm_gain=w["g_mlp"][layer], name="attn_out")
        else:
            wl = None
            z3 = _mm(hn, w["w_in_conv"][j], out_seg=3, name="conv_in")
            mixed = _conv_fwd(z3, w["conv_w"][j])
            sv = dict(z3=z3)
            h1, hn2 = _mm(mixed, w["w_out_conv"][j], res=h, norm_gain=w["g_mlp"][layer], name="conv_out")
        u, act = _mm(hn2, w["w_mlp_up"][layer], epi="relu2", name="mlp_up")
        tape.append(dict(h=h, hn=hn, wl=wl, sv=sv, mixed=mixed, h1=h1, hn2=hn2, u=u, act=act))
        if layer + 1 < DEPTH:
            h, hn = _mm(act, w["w_mlp_down"][layer], res=h1, norm_gain=w["g_mix"][layer + 1], name="mlp_down")
        else:
            h = _mm(act, w["w_mlp_down"][layer], res=h1, name="mlp_down")

    dh, dh_b, loss = _loss_head(h, target)
    exchanged = ()
    g = {n: [None] * (DEPTH if n in ("g_mix", "g_mlp", "w_mlp_up", "w_mlp_down") else DEPTH // 2)
         for n in WEIGHTS if n != "meta_tokens"}
    for layer in reversed(range(DEPTH)):
        j = layer // 2
        tp = tape[layer]
        g["w_mlp_down"][layer] = _mm(tp["act"], dh_b, ta=True, name="d_w_down")
        du = _mm(dh_b, w["w_mlp_down"][layer], tb=True, epi="relu2_bwd", aux=tp["u"], out_dtype=BF16, name="d_u")
        g["w_mlp_up"][layer] = _mm(tp["hn2"], du, ta=True, name="d_w_up")
        dhn2 = _mm(du, w["w_mlp_up"][layer], tb=True, out_dtype=BF16, name="d_hn2")
        dh1, dh1_b, dg = _rms_bwd(tp["h1"], w["g_mlp"][layer], dhn2, dres=dh, want_bf16=True)
        g["g_mlp"][layer] = dg[0]
        if layer % 2 == 0:
            g["w_out_attn"][j] = _mm(tp["mixed"], dh1_b, ta=True, name="d_w_out_attn")
            dcat = _mm(dh1_b, w["w_out_attn"][j], tb=True, name="d_cat")
            hosted = exchange_early is not None and layer == 0
            dhn, gl, got = _attn_layer_bwd(dcat, tp["hn"], tp["wl"], tp["sv"], rope,
                                           exchange_early(g) if hosted else ())
            if hosted:
                exchanged = got
            g["w_in_attn"][j] = _unpermute_in_attn(gl.pop("w_in"))
            for n, val in gl.items():
                g[n][j] = val
        else:
            g["w_out_conv"][j] = _mm(tp["mixed"], dh1_b, ta=True, name="d_w_out_conv")
            dyb = _mm(dh1_b, w["w_out_conv"][j], tb=True, name="d_yb")
            dz3, dcw = _conv_bwd(tp["sv"]["z3"], w["conv_w"][j], dyb)
            g["conv_w"][j] = dcw
            g["w_in_conv"][j] = _mm(tp["hn"], dz3, ta=True, name="d_w_in_conv")
            dhn = _mm(dz3, w["w_in_conv"][j], tb=True, out_dtype=BF16, name="d_hn_conv")
        dh, dh_b, dg = _rms_bwd(tp["h"], w["g_mix"][layer], dhn, dres=dh1, want_bf16=True)
        g["g_mix"][layer] = dg[0]
    g["meta_tokens"] = [dh[PAD:BLOCK]]
    return loss, dh[BLOCK:], g, exchanged


COMM_ROWS = 2048
ADAMW_BLOCK_BYTES = 1 << 20


def kernel(x, meta_tokens, g_mix, g_mlp, w_in_attn, g_cq, w_uq, g_ckv, w_ukv, g_q_mla, g_k_mla, g_q_fox, g_k_fox, b_forget, w_out_attn, w_in_conv, conv_w, w_out_conv, w_mlp_up, w_mlp_down, loss_target, m_meta_tokens, m_g_mix, m_g_mlp, m_w_in_attn, m_g_cq, m_w_uq, m_g_ckv, m_w_ukv, m_g_q_mla, m_g_k_mla, m_g_q_fox, m_g_k_fox, m_b_forget, m_w_out_attn, m_w_in_conv, m_conv_w, m_w_out_conv, m_w_mlp_up, m_w_mlp_down, v_meta_tokens, v_g_mix, v_g_mlp, v_w_in_attn, v_g_cq, v_w_uq, v_g_ckv, v_w_ukv, v_g_q_mla, v_g_k_mla, v_g_q_fox, v_g_k_fox, v_b_forget, v_w_out_attn, v_w_in_conv, v_conv_w, v_w_out_conv, v_w_mlp_up, v_w_mlp_down):
    args = dict(locals())
    local = {n: args[n] for n in WEIGHTS}
    mom = {n: args["m_" + n] for n in WEIGHTS}
    var = {n: args["v_" + n] for n in WEIGHTS}
    axis = dict(SHARDED)
    count = {n: local[n].shape[0] for n, _ in SHARDED if n != "meta_tokens"}
    piece = lambda src, p: src[p[0]] if p[1] is None else src[p[0]][p[1]]
    piece_axis = lambda p: axis[p[0]] - (0 if p[1] is None else 1)
    shape_of = lambda p: piece(local, p).shape
    layers = lambda n, ls: [(n, l) for l in ls]
    first_bf = [("w_uq", 0), ("w_ukv", 0)]
    first_f32 = [("meta_tokens", None), ("conv_w", 0), ("conv_w", 1)]
    late_bf = ([("w_uq", 1), ("w_ukv", 1)] + layers("w_out_attn", (0, 1)) + layers("w_in_conv", (0, 1))
               + layers("w_out_conv", (0, 1)) + layers("w_mlp_up", range(DEPTH)) + layers("w_mlp_down", range(DEPTH)))
    early_bf = ([("w_uq", 1), ("w_ukv", 1)] + layers("w_out_attn", (0, 1)) + layers("w_in_conv", (0, 1))
                + layers("conv_w", (0, 1)) + layers("w_out_conv", (0, 1)) + layers("w_mlp_up", range(DEPTH))
                + layers("w_mlp_down", range(DEPTH)))
    last_bf = [("w_uq", 0), ("w_ukv", 0), ("meta_tokens", None)]

    w = {n: local[n] for n in REPLICATED}
    w.update({n: [None] * c for n, c in count.items()})

    def install(w, pieces, gathered, in_layer, gathered_in):
        for p, blocks in zip(pieces, _unpack(gathered, [shape_of(p) for p in pieces], (N_DEV,))):
            value = _from_shards(blocks, piece_axis(p))
            if p[1] is None:
                w[p[0]] = value
            else:
                w[p[0]][p[1]] = value
        w["w_in_attn"][in_layer] = _permute_in_attn(_from_shards(gathered_in, piece_axis(("w_in_attn", 0))))

    got_bf, got_in, got_f32 = _all_gather([
        _pack([piece(local, p) for p in first_bf], 16, BF16), local["w_in_attn"][0].astype(BF16),
        _pack([piece(local, p) for p in first_f32], 8, F32)])
    install(w, first_bf, got_bf, 0, got_in)
    for p, blocks in zip(first_f32, _unpack(got_f32, [shape_of(p) for p in first_f32], (N_DEV,))):
        if p[1] is None:
            w[p[0]] = _from_shards(blocks, piece_axis(p))
        else:
            w[p[0]][p[1]] = _from_shards(blocks, piece_axis(p))
    gather_late = ([_pack([piece(local, p) for p in late_bf], 16, BF16), local["w_in_attn"][1].astype(BF16)],
                   lambda w_, got: install(w_, late_bf, got[0], 1, got[1]))

    def pack_grads(g, pieces, in_layer):
        sent = _pack_rows([_to_shards(piece(g, p), piece_axis(p)) for p in pieces], COMM_ROWS, BF16)
        return [sent, _to_shards(g["w_in_attn"][in_layer], piece_axis(("w_in_attn", 0))).astype(BF16)]

    loss_part, dx, grads, early = _local_step(x[0], loss_target[0], w, gather_late,
                                              lambda g: pack_grads(g, early_bf, 1))

    grads["meta_tokens"] = grads["meta_tokens"][0]
    last = _all_to_all(pack_grads(grads, last_bf, 0))
    g_piece = {}
    for pieces, (got, got_in), in_layer in ((early_bf, early, 1), (last_bf, last, 0)):
        summed = _unpack(_sum_blocks(got, COMM_ROWS), [shape_of(p) for p in pieces])
        g_piece.update(zip(pieces, summed))
        g_piece[("w_in_attn", in_layer)] = _sum_blocks(got_in, 256)
    g_local = {n: jnp.stack([g_piece[(n, l)] for l in range(c)]) for n, c in count.items()}
    g_local["meta_tokens"] = g_piece[("meta_tokens", None)]
    rep, = _all_gather([_pack([jnp.stack(grads[n]) for n in REPLICATED] + [loss_part], 8, F32)])
    g_rep = _sum_blocks(rep, rep.shape[1])
    *g_reps, loss = _unpack(g_rep, [local[n].shape for n in REPLICATED] + [()])
    g_local.update(zip(REPLICATED, g_reps))

    def flat(src, names, rows_multiple):
        return _pack([src[n] for n in names], rows_multiple, F32)

    upd = {}
    rows = g_rep.shape[0]
    outs = _adamw(flat(local, REPLICATED, rows), g_rep, flat(mom, REPLICATED, rows), flat(var, REPLICATED, rows), rows)
    for kind, buf in zip(("delta", "m", "v"), outs):
        upd.update({(kind, n): a for n, a in zip(REPLICATED, _unpack(buf, [local[n].shape for n in REPLICATED]))})
    for n, _ in SHARDED:
        as_rows = lambda a: a.reshape(-1, a.shape[-1])
        n_rows, n_cols = as_rows(local[n]).shape
        tiles = [r for r in (1024, 512, 256, 128, 64, 32, 16, 8) if r * n_cols * 4 <= ADAMW_BLOCK_BYTES]
        outs = _adamw(as_rows(local[n]), as_rows(g_local[n]), as_rows(mom[n]), as_rows(var[n]), _pick(n_rows, tiles))
        for kind, buf in zip(("delta", "m", "v"), outs):
            upd[(kind, n)] = buf.reshape(local[n].shape)

    return (loss, dx[None], *[g_local[n] for n in WEIGHTS], *[upd[("delta", n)] for n in WEIGHTS],
            *[upd[("m", n)] for n in WEIGHTS], *[upd[("v", n)] for n in WEIGHTS])
```

```python
import functools
import math

import jax
import jax.numpy as jnp
import numpy as np
from jax import lax
from jax.experimental import pallas as pl
from jax.experimental.pallas import tpu as pltpu

F32 = jnp.float32
BF16 = jnp.bfloat16

N_DEV = 8
D_MODEL = 1024
DEPTH = 4
N_META = 16
BLOCK = 128
PAD = BLOCK - N_META
HEADS = 8
MLA_NOPE = 64
MLA_ROPE = 32
MLA_QK = MLA_NOPE + MLA_ROPE
MLA_V = 64
Q_LORA = 384
KV_LORA = 256
ROPE_BASE = 10000.0
FOX_DIM = 64
D_FF = 4 * D_MODEL
EPS = 1e-6
NEG = -1e30
LANES = 128
ATTN_IN_PAD = 2304
C_CQ, C_CKV, C_FQ, C_FK, C_FV, C_TAIL = 0, 384, 640, 1152, 1664, 2176
TAIL_F = MLA_ROPE

ADAM_LR = 0.001
ADAM_B1 = 0.9
ADAM_B2 = 0.999
ADAM_EPS = 1e-08
ADAM_WD = 0.01
ADAM_STEP = 10

VMEM_LIMIT_BYTES = 48 * 1024 * 1024
MM_VMEM_BUDGET_BYTES = 28 * 1024 * 1024
MESH = pl.DeviceIdType.MESH

SHARDED = (
    ("meta_tokens", 1), ("w_in_attn", 2), ("w_uq", 2), ("w_ukv", 2), ("w_out_attn", 1),
    ("w_in_conv", 2), ("conv_w", 2), ("w_out_conv", 1), ("w_mlp_up", 2), ("w_mlp_down", 1))
REPLICATED = ("g_mix", "g_mlp", "g_cq", "g_ckv", "g_q_mla", "g_k_mla", "g_q_fox", "g_k_fox", "b_forget")
WEIGHTS = ("meta_tokens", "g_mix", "g_mlp", "w_in_attn", "g_cq", "w_uq", "g_ckv", "w_ukv", "g_q_mla",
           "g_k_mla", "g_q_fox", "g_k_fox", "b_forget", "w_out_attn", "w_in_conv", "conv_w",
           "w_out_conv", "w_mlp_up", "w_mlp_down")


def _params(*sem):
    return pltpu.CompilerParams(dimension_semantics=sem, vmem_limit_bytes=VMEM_LIMIT_BYTES)


def _row_tile(t):
    return 640 if (t % 640 == 0 and t > 640) else 128


def _padded_rows(n, rows_multiple):
    rows = -(-n // LANES)
    return -(-rows // rows_multiple) * rows_multiple


def _pack(parts, rows_multiple, dtype):
    n = sum(p.size for p in parts)
    rows = _padded_rows(n, rows_multiple)
    fill = [jnp.zeros((rows * LANES - n,), dtype)]
    return jnp.concatenate([p.reshape(-1).astype(dtype) for p in parts] + fill).reshape(rows, LANES)


def _pack_rows(parts, rows_multiple, dtype):
    n = sum(p.size for p in parts) // N_DEV
    rows = _padded_rows(n, rows_multiple)
    fill = [jnp.zeros((N_DEV, rows * LANES - n), dtype)]
    flat = jnp.concatenate([p.reshape(N_DEV, -1).astype(dtype) for p in parts] + fill, axis=1)
    return flat.reshape(N_DEV, rows, LANES)


def _unpack(buf, shapes, lead=()):
    flat = buf.reshape(lead + (-1,))
    out, off = [], 0
    for s in shapes:
        n = math.prod(s)
        out.append(flat[..., off:off + n].reshape(lead + tuple(s)))
        off += n
    return out


def _to_shards(full, axis):
    s = full.shape
    return jnp.moveaxis(full.reshape(s[:axis] + (N_DEV, s[axis] // N_DEV) + s[axis + 1:]), axis, 0)


def _from_shards(g8, axis):
    m = jnp.moveaxis(g8, 0, axis)
    s = m.shape
    return m.reshape(s[:axis] + (s[axis] * s[axis + 1],) + s[axis + 2:])


def _comm_call(body, name, xs, out_shapes):
    n = len(xs)
    hbm = pl.BlockSpec(memory_space=pltpu.HBM)
    return pl.pallas_call(
        body, name=name, out_shape=out_shapes, in_specs=[hbm] * n, out_specs=[hbm] * n,
        scratch_shapes=[pltpu.SemaphoreType.DMA((n, 7)), pltpu.SemaphoreType.DMA((n, 7)),
                        pltpu.SemaphoreType.DMA((n,))],
    )(*xs)


def _gather_ops(x_refs, out_refs, send_sems, recv_sems, local_sems):
    n = len(x_refs)
    x_, y_, c = lax.axis_index("x"), lax.axis_index("y"), lax.axis_index("c")
    me, sibling = (x_, y_, c), (x_, y_, 1 - c)
    chips = [(1 - x_, y_), (x_, 1 - y_), (1 - x_, 1 - y_)]

    def rows(a, px, py, pc):
        return out_refs[a].at[4 * px + 2 * py + pc]

    def copy(a, k, block, to, src=None):
        return pltpu.make_async_remote_copy(
            src_ref=rows(a, *block) if src is None else src, dst_ref=rows(a, *block),
            send_sem=send_sems.at[a, k], recv_sem=recv_sems.at[a, k], device_id=to, device_id_type=MESH)

    def mine():
        return [pltpu.make_async_copy(x_refs[a], rows(a, *me), local_sems.at[a]) for a in range(n)]

    def first():
        cps = []
        for a in range(n):
            cps.append(copy(a, 0, me, sibling, src=x_refs[a]))
            cps += [copy(a, 1 + j, me, (*chip, c), src=x_refs[a]) for j, chip in enumerate(chips)]
        return cps

    def start():
        for cp in mine() + first():
            cp.start()

    def finish():
        passed = []
        for j, chip in enumerate(chips):
            for a in range(n):
                copy(a, 1 + j, (*chip, c), me).wait_recv()
                passed.append(copy(a, 4 + j, (*chip, c), sibling))
                passed[-1].start()
        for a in range(n):
            copy(a, 0, sibling, me).wait_recv()
            for j, chip in enumerate(chips):
                copy(a, 4 + j, (*chip, 1 - c), me).wait_recv()
        for cp in first() + passed:
            cp.wait_send()
        for cp in mine():
            cp.wait()

    return start, finish


def _exchange_ops(x_refs, out_refs, send_sems, recv_sems, local_sems):
    n = len(x_refs)
    x_, y_, c = lax.axis_index("x"), lax.axis_index("y"), lax.axis_index("c")
    me = 4 * x_ + 2 * y_ + c

    def peer(k):
        px = 1 - x_ if k & 4 else x_
        py = 1 - y_ if k & 2 else y_
        pc = 1 - c if k & 1 else c
        return px, py, pc

    def copy(a, k):
        px, py, pc = peer(k)
        return pltpu.make_async_remote_copy(
            src_ref=x_refs[a].at[4 * px + 2 * py + pc], dst_ref=out_refs[a].at[me],
            send_sem=send_sems.at[a, k - 1], recv_sem=recv_sems.at[a, k - 1], device_id=(px, py, pc),
            device_id_type=MESH)

    def arrival(a, k):
        px, py, pc = peer(k)
        slot = 4 * px + 2 * py + pc
        return pltpu.make_async_remote_copy(
            src_ref=x_refs[a].at[slot], dst_ref=out_refs[a].at[slot],
            send_sem=send_sems.at[a, k - 1], recv_sem=recv_sems.at[a, k - 1], device_id=(px, py, pc),
            device_id_type=MESH)

    def mine():
        return [pltpu.make_async_copy(x_refs[a].at[me], out_refs[a].at[me], local_sems.at[a]) for a in range(n)]

    def sends():
        return [copy(a, k) for k in range(1, N_DEV) for a in range(n)]

    def start():
        for cp in mine() + sends():
            cp.start()

    def finish():
        for k in range(1, N_DEV):
            for a in range(n):
                arrival(a, k).wait_recv()
        for cp in sends():
            cp.wait_send()
        for cp in mine():
            cp.wait()

    return start, finish


def _comm_parts(xs, gather):
    n = len(xs)
    hbm = pl.BlockSpec(memory_space=pltpu.HBM)
    shapes = [jax.ShapeDtypeStruct(((N_DEV,) + x.shape) if gather else x.shape, x.dtype) for x in xs]
    sems = [pltpu.SemaphoreType.DMA((n, 7)), pltpu.SemaphoreType.DMA((n, 7)), pltpu.SemaphoreType.DMA((n,))]
    return [hbm] * n, [hbm] * n, shapes, sems


def _all_gather(xs):
    n = len(xs)

    def body(*refs):
        start, finish = _gather_ops(refs[:n], refs[n:2 * n], *refs[2 * n:])
        start()
        finish()

    return _comm_call(body, "all_gather", xs, [jax.ShapeDtypeStruct((N_DEV,) + x.shape, x.dtype) for x in xs])


def _all_to_all(xs):
    n = len(xs)

    def body(*refs):
        start, finish = _exchange_ops(refs[:n], refs[n:2 * n], *refs[2 * n:])
        start()
        finish()

    return _comm_call(body, "all_to_all", xs, [jax.ShapeDtypeStruct(x.shape, x.dtype) for x in xs])


def _sum_blocks(x, rows_tile):
    _, r, c_ = x.shape

    def body(x_ref, o_ref):
        acc = x_ref[0].astype(F32)
        for d in range(1, N_DEV):
            acc = acc + x_ref[d].astype(F32)
        o_ref[...] = acc

    return pl.pallas_call(
        body, name="sum_blocks", grid=(r // rows_tile,),
        in_specs=[pl.BlockSpec((N_DEV, rows_tile, c_), lambda i: (0, i, 0))],
        out_specs=pl.BlockSpec((rows_tile, c_), lambda i: (i, 0)),
        out_shape=jax.ShapeDtypeStruct((r, c_), F32),
        compiler_params=_params("parallel"),
    )(x)


def _adamw(w, g, m, v, rows_tile):
    r, c_ = w.shape
    c1 = 1.0 - ADAM_B1 ** ADAM_STEP
    c2 = 1.0 - ADAM_B2 ** ADAM_STEP

    def body(w_ref, g_ref, m_ref, v_ref, d_ref, mo_ref, vo_ref):
        g_ = g_ref[...]
        m_ = ADAM_B1 * m_ref[...] + (1.0 - ADAM_B1) * g_
        v_ = ADAM_B2 * v_ref[...] + (1.0 - ADAM_B2) * (g_ * g_)
        m_hat = m_ / c1
        v_hat = v_ / c2
        d_ref[...] = -ADAM_LR * (m_hat / (jnp.sqrt(v_hat) + ADAM_EPS) + ADAM_WD * w_ref[...])
        mo_ref[...] = m_
        vo_ref[...] = v_

    spec = pl.BlockSpec((rows_tile, c_), lambda i: (i, 0))
    shape = jax.ShapeDtypeStruct((r, c_), F32)
    return pl.pallas_call(
        body, name="adamw", grid=(r // rows_tile,), in_specs=[spec] * 4, out_specs=[spec] * 3,
        out_shape=[shape] * 3, compiler_params=_params("parallel"),
    )(w, g, m, v)


def _pick(n, prefs):
    for p in prefs:
        if n % p == 0:
            return p
    return n


def _mat_spec(arr, tr, tc, r_of, c_of):
    if arr.ndim == 2:
        return pl.BlockSpec((tr, tc), lambda i, j, k: (r_of(i, j, k), c_of(i, j, k)))
    per = arr.shape[2] // tc
    return pl.BlockSpec((None, tr, tc), lambda i, j, k: (c_of(i, j, k) // per, r_of(i, j, k), c_of(i, j, k) % per))


def _mm(a, b, *, ta=False, tb=False, out_dtype=F32, out_seg=None, res=None, epi=None, aux=None, norm_gain=None,
        tm=None, tn=None, tk=None, name="mm"):
    def dims(x):
        return (x.shape[0], x.shape[1]) if x.ndim == 2 else (x.shape[1], x.shape[0] * x.shape[2])
    ar, ac = dims(a)
    br, bc = dims(b)
    m, k = (ac, ar) if ta else (ar, ac)
    n, kb = (br, bc) if tb else (bc, br)
    assert k == kb, (a.shape, b.shape, ta, tb)
    tn = tn or _pick(n, (1024, 768, 512, 384, 256, 128))
    tk = tk or _pick(k, ((1664,) if ta else (2048,)) + (1024, 768, 640, 512, 384, 256, 128))

    def vmem_bytes(rows):
        out_bytes = 2 * 2 if epi == "relu2" else jnp.dtype(out_dtype).itemsize + (2 if norm_gain is not None else 0)
        x_bytes = sum(x.dtype.itemsize for x in (res, aux) if x is not None)
        return 2 * (rows * tk * 2 + tk * tn * 2 + rows * tn * (out_bytes + x_bytes)) + (rows * tn * 4 if k > tk else 0)

    tall = (1664,) if (not ta and m % 1664 == 0 and vmem_bytes(1664) <= MM_VMEM_BUDGET_BYTES) else ()
    tm = tm or _pick(m, (1024, 512, 384, 256, 128) if ta else tall + (640, 512, 384, 256, 128))
    if out_seg:
        assert (n // out_seg) % tn == 0
    for x, t in ((a, tm if ta else tk), (b, tk if tb else tn)):
        if x.ndim == 3:
            assert x.shape[2] % t == 0
    nk = k // tk
    gi, gj, gk = (lambda j, i, kk: i), (lambda j, i, kk: j), (lambda j, i, kk: kk)
    a_spec = _mat_spec(a, tk, tm, gk, gi) if ta else _mat_spec(a, tm, tk, gi, gk)
    b_spec = _mat_spec(b, tn, tk, gj, gk) if tb else _mat_spec(b, tk, tn, gk, gj)
    out_like = jax.ShapeDtypeStruct((out_seg, m, n // out_seg) if out_seg else (m, n), out_dtype)
    o_spec = _mat_spec(out_like, tm, tn, gi, gj)
    o_shape = (out_seg, m, n // out_seg) if out_seg else (m, n)
    dn = (((0 if ta else 1,), (1 if tb else 0,)), ((), ()))
    extra = [x for x in (res, aux) if x is not None]
    assert not (res is not None and aux is not None)
    normed = norm_gain is not None
    assert not normed or (res is not None and tn == n and not out_seg)
    n_out = 2 if (epi == "relu2" or normed) else 1
    n_extra = len(extra)

    def body(*refs):
        a_ref, b_ref = refs[0], refs[1]
        x_ref = refs[2] if n_extra else None
        g_ref = refs[2 + n_extra] if normed else None
        outs = refs[2 + n_extra + normed:2 + n_extra + normed + n_out]
        acc_ref = refs[-1] if nk > 1 else None
        part = lax.dot_general(a_ref[...], b_ref[...], dn, preferred_element_type=F32)

        def finish(acc):
            if epi == "relu2":
                r = jnp.maximum(acc, 0.0)
                outs[0][...] = r.astype(BF16)
                outs[1][...] = (r * r).astype(BF16)
            elif epi == "relu2_bwd":
                outs[0][...] = (acc * (2.0 * x_ref[...].astype(F32))).astype(out_dtype)
            elif res is not None:
                h = acc + x_ref[...]
                outs[0][...] = h.astype(out_dtype)
                if normed:
                    rstd = lax.rsqrt(jnp.mean(h * h, axis=-1, keepdims=True) + EPS)
                    outs[1][...] = (h * rstd * g_ref[...]).astype(BF16)
            else:
                outs[0][...] = acc.astype(out_dtype)

        if nk == 1:
            finish(part)
        else:
            kk = pl.program_id(2)

            @pl.when(kk == 0)
            def _():
                acc_ref[...] = part

            @pl.when(kk > 0)
            def _():
                acc_ref[...] += part

            @pl.when(kk == nk - 1)
            def _():
                finish(acc_ref[...])

    if epi == "relu2":
        out_shape = [jax.ShapeDtypeStruct(o_shape, BF16), jax.ShapeDtypeStruct(o_shape, BF16)]
        out_specs = [o_spec, o_spec]
    elif normed:
        out_shape = [jax.ShapeDtypeStruct(o_shape, out_dtype), jax.ShapeDtypeStruct(o_shape, BF16)]
        out_specs = [o_spec, o_spec]
    else:
        out_shape = jax.ShapeDtypeStruct(o_shape, out_dtype)
        out_specs = o_spec
    x_specs = [pl.BlockSpec((tm, tn), lambda j, i, kk: (i, j))] * len(extra)
    if normed:
        x_specs = x_specs + [pl.BlockSpec((1, tn), lambda j, i, kk: (0, 0))]
        extra = extra + [norm_gain.reshape(1, n)]
    res_ = pl.pallas_call(
        body, name=name, grid=(n // tn, m // tm, nk),
        in_specs=[a_spec, b_spec] + x_specs, out_specs=out_specs, out_shape=out_shape,
        scratch_shapes=[pltpu.VMEM((tm, tn), F32)] if nk > 1 else [],
        compiler_params=_params("parallel", "parallel", "arbitrary"),
    )(a, b, *extra)
    return res_


def _rms_fwd(x, g):
    t, d = x.shape
    tm = _row_tile(t)

    def body(x_ref, g_ref, o_ref):
        x_ = x_ref[...]
        rstd = lax.rsqrt(jnp.mean(x_ * x_, axis=-1, keepdims=True) + EPS)
        o_ref[...] = (x_ * rstd * g_ref[...]).astype(BF16)

    return pl.pallas_call(
        body, name="rms_fwd", grid=(t // tm,),
        in_specs=[pl.BlockSpec((tm, d), lambda i: (i, 0)), pl.BlockSpec((1, d), lambda i: (0, 0))],
        out_specs=pl.BlockSpec((tm, d), lambda i: (i, 0)),
        out_shape=jax.ShapeDtypeStruct((t, d), BF16), compiler_params=_params("parallel"),
    )(x, g.reshape(1, d))


def _rms_bwd(x, g, dy, dres=None, want_bf16=False):
    t, d = x.shape
    tm = _row_tile(t)
    has_res = dres is not None

    def body(*refs):
        x_ref, g_ref, dy_ref = refs[:3]
        r_ref = refs[3] if has_res else None
        outs = refs[3 + has_res:]
        x_ = x_ref[...]
        rstd = lax.rsqrt(jnp.mean(x_ * x_, axis=-1, keepdims=True) + EPS)
        xh = x_ * rstd
        dy_ = dy_ref[...].astype(F32)
        dxh = dy_ * g_ref[...]
        dx = rstd * (dxh - xh * jnp.mean(dxh * xh, axis=-1, keepdims=True))
        if has_res:
            dx = dx + r_ref[...]
        outs[0][...] = dx
        if want_bf16:
            outs[1][...] = dx.astype(BF16)
        dg_ref = outs[-1]

        @pl.when(pl.program_id(0) == 0)
        def _():
            dg_ref[...] = jnp.zeros_like(dg_ref)

        dg_ref[...] += jnp.sum(dy_ * xh, axis=0, keepdims=True)

    row = pl.BlockSpec((tm, d), lambda i: (i, 0))
    vec = pl.BlockSpec((1, d), lambda i: (0, 0))
    out_shape = [jax.ShapeDtypeStruct((t, d), F32)] + ([jax.ShapeDtypeStruct((t, d), BF16)] if want_bf16 else []) \
        + [jax.ShapeDtypeStruct((1, d), F32)]
    out_specs = [row] + ([row] if want_bf16 else []) + [vec]
    return pl.pallas_call(
        body, name="rms_bwd", grid=(t // tm,),
        in_specs=[row, vec, row] + ([row] if has_res else []), out_specs=out_specs, out_shape=out_shape,
        compiler_params=_params("arbitrary"),
    )(x, g.reshape(1, d), dy, *([dres] if has_res else []))


def _swap_rope_halves(y):
    lane = lax.broadcasted_iota(jnp.int32, y.shape, 1)
    half = MLA_ROPE // 2
    swapped = jnp.where(lane < MLA_NOPE + half, pltpu.roll(y, LANES - half, axis=1), pltpu.roll(y, half, axis=1))
    return jnp.where((lane >= MLA_NOPE) & (lane < MLA_QK), swapped, 0.0)


def _head_norm_fwd(x, g, n_valid, scale, rope=None):
    t, hw = x.shape
    w = g.shape[1]
    tm = _row_tile(t)

    def body(*refs):
        x_ref, g_ref = refs[:2]
        o_ref = refs[-1]
        gain = g_ref[...] * scale
        for n in range(hw // w):
            cols = slice(n * w, (n + 1) * w)
            x_ = x_ref[:, cols]
            rstd = lax.rsqrt(jnp.sum(x_ * x_, axis=-1, keepdims=True) * (1.0 / n_valid) + EPS)
            y = x_ * rstd * gain
            if rope is not None:
                y = y * refs[2][...] + _swap_rope_halves(y) * refs[3][...]
            o_ref[:, cols] = y.astype(BF16)

    row = pl.BlockSpec((tm, hw), lambda i: (i, 0))
    tab = pl.BlockSpec((tm, w), lambda i: (i, 0))
    return pl.pallas_call(
        body, name="head_norm_fwd", grid=(t // tm,),
        in_specs=[row, pl.BlockSpec((1, w), lambda i: (0, 0))] + ([tab, tab] if rope is not None else []),
        out_specs=row, out_shape=jax.ShapeDtypeStruct((t, hw), BF16),
        compiler_params=_params("parallel"),
    )(x, g, *(rope if rope is not None else ()))


def _head_norm_bwd(x, g, dout, n_valid, scale, rope=None):
    t, hw = x.shape
    w = g.shape[1]
    tm = _row_tile(t)

    def body(*refs):
        x_ref, g_ref, do_ref = refs[:3]
        dx_ref, dsum_ref, dg_ref = refs[-3:]
        gain = g_ref[...] * scale
        dsum = jnp.zeros((tm, w), F32)
        dg = jnp.zeros((1, w), F32)
        for n in range(hw // w):
            cols = slice(n * w, (n + 1) * w)
            dy = do_ref[:, cols]
            if rope is not None:
                dy = dy * refs[3][...] + _swap_rope_halves(dy * refs[4][...])
            x_ = x_ref[:, cols]
            rstd = lax.rsqrt(jnp.sum(x_ * x_, axis=-1, keepdims=True) * (1.0 / n_valid) + EPS)
            xh = x_ * rstd
            dxh = dy * gain
            dx = rstd * (dxh - xh * (jnp.sum(dxh * xh, axis=-1, keepdims=True) * (1.0 / n_valid)))
            dx_ref[:, cols] = dx.astype(BF16)
            dsum = dsum + dx
            dg = dg + jnp.sum(dy * xh, axis=0, keepdims=True)
        dsum_ref[...] = dsum

        @pl.when(pl.program_id(0) == 0)
        def _():
            dg_ref[...] = jnp.zeros_like(dg_ref)

        dg_ref[...] += scale * dg

    row = pl.BlockSpec((tm, hw), lambda i: (i, 0))
    tab = pl.BlockSpec((tm, w), lambda i: (i, 0))
    vec = pl.BlockSpec((1, w), lambda i: (0, 0))
    return pl.pallas_call(
        body, name="head_norm_bwd", grid=(t // tm,),
        in_specs=[row, vec, row] + ([tab, tab] if rope is not None else []),
        out_specs=[row, tab, vec],
        out_shape=[jax.ShapeDtypeStruct((t, hw), BF16), jax.ShapeDtypeStruct((t, w), F32),
                   jax.ShapeDtypeStruct((1, w), F32)],
        compiler_params=_params("arbitrary"),
    )(x, g, dout, *(rope if rope is not None else ()))


def _tri(n, upper):
    r = lax.broadcasted_iota(jnp.int32, (n, n), 0)
    c = lax.broadcasted_iota(jnp.int32, (n, n), 1)
    return ((r <= c) if upper else (r >= c)).astype(F32)


def _gate_mask(shape, row0):
    lane = lax.broadcasted_iota(jnp.int32, shape, 1)
    row = row0 + lax.broadcasted_iota(jnp.int32, shape, 0)
    return (lane >= TAIL_F) & (lane < TAIL_F + HEADS) & (row >= PAD)


def _gate_fwd(z, bias):
    t = z.shape[0]
    tm = _row_tile(t)
    tail = C_TAIL // LANES

    def body(z_ref, b_ref, o_ref, carry):
        i = pl.program_id(0)

        @pl.when(i == 0)
        def _():
            carry[...] = jnp.zeros_like(carry)

        x_ = z_ref[...] + b_ref[...]
        logf = jnp.minimum(x_, 0.0) - jnp.log1p(jnp.exp(-jnp.abs(x_)))
        logf = jnp.where(_gate_mask(logf.shape, i * tm), logf, 0.0)
        run = carry[...]
        for c in range(tm // BLOCK):
            rows = slice(c * BLOCK, (c + 1) * BLOCK)
            cum = jnp.dot(_tri(BLOCK, False), logf[rows], preferred_element_type=F32,
                          precision=lax.Precision.HIGHEST) + run
            o_ref[rows, :] = cum
            run = cum[BLOCK - 1:BLOCK, :]
        carry[...] = run

    return pl.pallas_call(
        body, name="gate_fwd", grid=(t // tm,),
        in_specs=[pl.BlockSpec((tm, LANES), lambda i: (i, tail)), pl.BlockSpec((1, LANES), lambda i: (0, 0))],
        out_specs=pl.BlockSpec((tm, LANES), lambda i: (i, 0)),
        out_shape=jax.ShapeDtypeStruct((t, LANES), F32),
        scratch_shapes=[pltpu.VMEM((1, LANES), F32)], compiler_params=_params("arbitrary"),
    )(z, bias)


def _gate_bwd(z, bias, dcum):
    t = z.shape[0]
    tm = _row_tile(t)
    nb = t // tm
    tail = C_TAIL // LANES

    def body(z_ref, b_ref, d_ref, o_ref, db_ref, carry):
        i = pl.program_id(0)

        @pl.when(i == 0)
        def _():
            carry[...] = jnp.zeros_like(carry)
            db_ref[...] = jnp.zeros_like(db_ref)

        run = carry[...]
        db = jnp.zeros((1, LANES), F32)
        for c in reversed(range(tm // BLOCK)):
            rows = slice(c * BLOCK, (c + 1) * BLOCK)
            rc = jnp.dot(_tri(BLOCK, True), d_ref[rows, :], preferred_element_type=F32,
                         precision=lax.Precision.HIGHEST) + run
            run = rc[0:1, :]
            x_ = z_ref[rows, :] + b_ref[...]
            sig_neg = 1.0 / (1.0 + jnp.exp(x_))
            dl = jnp.where(_gate_mask(rc.shape, (nb - 1 - i) * tm + c * BLOCK), rc * sig_neg, 0.0)
            o_ref[rows, :] = dl
            db = db + jnp.sum(dl, axis=0, keepdims=True)
        carry[...] = run
        db_ref[...] += db

    return pl.pallas_call(
        body, name="gate_bwd", grid=(nb,),
        in_specs=[pl.BlockSpec((tm, LANES), lambda i: (nb - 1 - i, tail)), pl.BlockSpec((1, LANES), lambda i: (0, 0)),
                  pl.BlockSpec((tm, LANES), lambda i: (nb - 1 - i, 0))],
        out_specs=[pl.BlockSpec((tm, LANES), lambda i: (nb - 1 - i, 0)), pl.BlockSpec((1, LANES), lambda i: (0, 0))],
        out_shape=[jax.ShapeDtypeStruct((t, LANES), F32), jax.ShapeDtypeStruct((1, LANES), F32)],
        scratch_shapes=[pltpu.VMEM((1, LANES), F32)], compiler_params=_params("arbitrary"),
    )(z, bias, dcum)


def _pairs(nb, by_query):
    if by_query:
        pr = [(i, j) for i in range(nb) for j in range(i + 1)]
    else:
        pr = [(i, j) for j in range(nb) for i in range(j, nb)]
    return (jnp.asarray(np.array([p[0] for p in pr], np.int32)),
            jnp.asarray(np.array([p[1] for p in pr], np.int32)))


HEADS_PER_STEP = 8


def _mask_scores(s, i, j, tile):
    qp = i * tile + lax.broadcasted_iota(jnp.int32, s.shape, 0)
    kp = j * tile + lax.broadcasted_iota(jnp.int32, s.shape, 1)
    return jnp.where((kp <= qp) & (kp >= PAD), s, NEG)


def _pipelined(n, front, back):
    nxt = front(0)
    for h in range(n):
        cur = nxt
        if h + 1 < n:
            nxt = front(h + 1)
        back(h, cur)


def _nt_dot(a, b):
    return lax.dot_general(a, b, (((1,), (1,)), ((), ())), preferred_element_type=F32)


def _tn_dot(a, b):
    return lax.dot_general(a, b, (((0,), (0,)), ((), ())), preferred_element_type=F32)


def _attn_specs(hb, tile, dk, dv):
    q_of = lambda g, p, it, jt: (it[p], g)
    k_of = lambda g, p, it, jt: (jt[p], g)
    return dict(
        q=pl.BlockSpec((tile, hb * dk), q_of), k=pl.BlockSpec((tile, hb * dk), k_of),
        v=pl.BlockSpec((tile, hb * dv), k_of), ov=pl.BlockSpec((tile, hb * dv), q_of),
        fr=pl.BlockSpec((hb, 1, tile), lambda g, p, it, jt: (g, 0, jt[p])),
        rowq=pl.BlockSpec((hb, 1, tile), lambda g, p, it, jt: (g, 0, it[p])))


def _head(ref, n, d):
    return ref[:, n * d:(n + 1) * d]


def _attn_fwd(q, k, v, key_bias=None, gather=()):
    h = HEADS
    t, dk, dv = q.shape[0], q.shape[1] // h, v.shape[1] // h
    tile = _row_tile(t)
    nb = t // tile
    hb = HEADS_PER_STEP
    biased = key_bias is not None
    it, jt = _pairs(nb, True)
    sp = _attn_specs(hb, tile, dk, dv)

    n_in = 4 if biased else 3
    n_g = len(gather)
    g_in, g_out, g_shapes, g_sems = _comm_parts(gather, True)
    n_steps = int(it.shape[0])

    def body(it_ref, jt_ref, *refs):
        q_ref, k_ref, v_ref = refs[:3]
        b_ref = refs[3] if biased else None
        o_ref, lse_ref = refs[n_in + 2 * n_g:n_in + 2 * n_g + 2]
        m_sc, l_sc, acc_sc = refs[n_in + 2 * n_g + 2:n_in + 2 * n_g + 5]
        p = pl.program_id(1)
        i, j = it_ref[p], jt_ref[p]
        if n_g:
            g_start, g_finish = _gather_ops(refs[n_in:n_in + n_g], refs[n_in + n_g:n_in + 2 * n_g],
                                            *refs[n_in + 2 * n_g + 5:])
            first = (pl.program_id(0) == 0) & (p == 0)
            last = (pl.program_id(0) == h // hb - 1) & (p == n_steps - 1)
            pl.when(first)(g_start)

        @pl.when(j == 0)
        def _():
            m_sc[...] = jnp.full_like(m_sc, NEG)
            l_sc[...] = jnp.zeros_like(l_sc)
            acc_sc[...] = jnp.zeros_like(acc_sc)

        def step(masked):
            def front(n):
                return _nt_dot(_head(q_ref, n, dk), _head(k_ref, n, dk))

            def back(n, s):
                if biased:
                    s = s + b_ref[n]
                if masked:
                    s = _mask_scores(s, i, j, tile)
                m_prev = m_sc[n]
                m_new = jnp.maximum(m_prev, jnp.max(s, axis=-1, keepdims=True))
                alpha = jnp.exp2(m_prev - m_new)
                e = jnp.exp2(s - m_new)
                l_sc[n] = alpha * l_sc[n] + jnp.sum(e, axis=-1, keepdims=True)
                acc_sc[n] = alpha * acc_sc[n] + jnp.dot(e.astype(BF16), _head(v_ref, n, dv),
                                                        preferred_element_type=F32)
                m_sc[n] = m_new

            _pipelined(hb, front, back)

        edge = (j == i) | (j == 0)
        pl.when(edge)(lambda: step(True))
        pl.when(jnp.logical_not(edge))(lambda: step(False))

        @pl.when(j == i)
        def _():
            row = i * tile + lax.broadcasted_iota(jnp.int32, (tile, 1), 0)
            for n in range(hb):
                o_ref[:, n * dv:(n + 1) * dv] = jnp.where(row >= PAD, acc_sc[n] / l_sc[n], 0.0)
                lse_ref[n] = jnp.transpose(m_sc[n] + jnp.log2(l_sc[n]))

        if n_g:
            pl.when(last)(g_finish)

    grid_spec = pltpu.PrefetchScalarGridSpec(
        num_scalar_prefetch=2, grid=(h // hb, n_steps),
        in_specs=[sp["q"], sp["k"], sp["v"]] + ([sp["fr"]] if biased else []) + g_in,
        out_specs=g_out + [sp["ov"], sp["rowq"]],
        scratch_shapes=[pltpu.VMEM((hb, tile, 1), F32), pltpu.VMEM((hb, tile, 1), F32),
                        pltpu.VMEM((hb, tile, dv), F32)] + (g_sems if n_g else []))
    outs = pl.pallas_call(
        body, name="attn_fwd_gather" if n_g else "attn_fwd", grid_spec=grid_spec,
        out_shape=g_shapes + [jax.ShapeDtypeStruct((t, h * dv), F32), jax.ShapeDtypeStruct((h, 1, t), F32)],
        compiler_params=_params("arbitrary" if n_g else "parallel", "arbitrary"),
    )(it, jt, q, k, v, *((key_bias,) if biased else ()), *gather)
    return (outs[n_g], outs[n_g + 1], outs[:n_g]) if n_g else tuple(outs)


LOG2E = 1.4426950408889634
LN2 = 0.6931471805599453
MLA_SCALE = MLA_QK ** -0.5 * LOG2E
FOX_SCALE = FOX_DIM ** -0.5 * LOG2E


def _attn_delta(o, do):
    h = HEADS
    t, dv = o.shape[0], o.shape[1] // h
    tm = _row_tile(t)

    def body(o_ref, do_ref, d_ref):
        prod = o_ref[...] * do_ref[...]
        for n in range(h):
            d_ref[n] = jnp.transpose(jnp.sum(prod[:, n * dv:(n + 1) * dv], axis=-1, keepdims=True) * LN2)

    blk = pl.BlockSpec((tm, h * dv), lambda i: (i, 0))
    return pl.pallas_call(
        body, name="attn_delta", grid=(t // tm,), in_specs=[blk, blk],
        out_specs=pl.BlockSpec((h, 1, tm), lambda i: (0, 0, i)),
        out_shape=jax.ShapeDtypeStruct((h, 1, t), F32), compiler_params=_params("parallel"),
    )(o, do)


BWD_HEADS_PER_STEP = (4, 4)


def _attn_bwd(q, k, v, do, lse_row, delta_row, key_bias=None, exchange=()):
    h = HEADS
    t, dk, dv = q.shape[0], q.shape[1] // h, v.shape[1] // h
    tile = _row_tile(t)
    nb = t // tile
    decay = key_bias is not None
    hb = BWD_HEADS_PER_STEP[int(decay)]
    it, jt = _pairs(nb, False)
    sp = _attn_specs(hb, tile, dk, dv)
    n_in = 7 if decay else 6
    n_out = 5 if decay else 3
    n_x = len(exchange)
    x_in, x_out, x_shapes, x_sems = _comm_parts(exchange, False)
    n_steps = int(it.shape[0])

    def body(it_ref, jt_ref, *refs):
        q_ref, k_ref, v_ref, do_ref, lse_ref, delta_ref = refs[:6]
        b_ref = refs[6] if decay else None
        outs = refs[n_in + n_x:]
        dq_ref, dk_ref, dv_ref = outs[:3]
        rs_ref, ks_ref = (outs[3], outs[4]) if decay else (None, None)
        scratch = outs[n_out + n_x:]
        dq_sc, dk_sc, dv_sc = scratch[:3]
        ks_sc, b_sc = (scratch[3], scratch[4]) if decay else (None, None)
        p = pl.program_id(1)
        i, j = it_ref[p], jt_ref[p]
        if n_x:
            x_start, x_finish = _exchange_ops(refs[n_in:n_in + n_x], outs[n_out:n_out + n_x],
                                              *scratch[5 if decay else 3:])
            first = (pl.program_id(0) == 0) & (p == 0)
            last = (pl.program_id(0) == h // hb - 1) & (p == n_steps - 1)
            pl.when(first)(x_start)

        @pl.when(p == 0)
        def _():
            dq_sc[...] = jnp.zeros_like(dq_sc)
            if decay:
                rs_ref[...] = jnp.zeros_like(rs_ref)

        @pl.when(i == j)
        def _():
            dk_sc[...] = jnp.zeros_like(dk_sc)
            dv_sc[...] = jnp.zeros_like(dv_sc)
            if decay:
                ks_sc[...] = jnp.zeros_like(ks_sc)
                for n in range(hb):
                    b_sc[n] = jnp.transpose(b_ref[n])

        def step(masked):
            def front(n):
                return (_nt_dot(_head(k_ref, n, dk), _head(q_ref, n, dk)),
                        _nt_dot(_head(v_ref, n, dv), (_head(do_ref, n, dv) * LN2).astype(BF16)))

            def back(n, s_dp):
                s, dp = s_dp
                if decay:
                    s = s + b_sc[n]
                if masked:
                    kp = j * tile + lax.broadcasted_iota(jnp.int32, s.shape, 0)
                    qp = i * tile + lax.broadcasted_iota(jnp.int32, s.shape, 1)
                    s = jnp.where((kp <= qp) & (kp >= PAD), s, NEG)
                pr = jnp.exp2(s - lse_ref[n])
                ds = pr * (dp - delta_ref[n])
                ds_b = ds.astype(BF16)
                dv_sc[n] += jnp.dot(pr.astype(BF16), _head(do_ref, n, dv).astype(BF16), preferred_element_type=F32)
                dk_sc[n] += jnp.dot(ds_b, _head(q_ref, n, dk), preferred_element_type=F32)
                dq_sc[n, i] += _tn_dot(ds_b, _head(k_ref, n, dk))
                if decay:
                    rs_ref[n, i] += jnp.sum(ds, axis=0, keepdims=True)
                    ks_sc[n] += jnp.sum(ds, axis=-1, keepdims=True)

            _pipelined(hb, front, back)

        edge = (j == i) | (j == 0)
        pl.when(edge)(lambda: step(True))
        pl.when(jnp.logical_not(edge))(lambda: step(False))

        @pl.when(i == j)
        def _():
            for n in range(hb):
                dq_ref[:, n * dk:(n + 1) * dk] = dq_sc[n, j]

        @pl.when(i == nb - 1)
        def _():
            for n in range(hb):
                dk_ref[:, n * dk:(n + 1) * dk] = dk_sc[n]
                dv_ref[:, n * dv:(n + 1) * dv] = dv_sc[n]
                if decay:
                    ks_ref[n] = jnp.transpose(ks_sc[n])

        if n_x:
            pl.when(last)(x_finish)

    rows_out = pl.BlockSpec((hb, nb, 1, tile), lambda hh, p, it, jt: (hh, 0, 0, 0))
    grid_spec = pltpu.PrefetchScalarGridSpec(
        num_scalar_prefetch=2, grid=(h // hb, n_steps),
        in_specs=[sp["q"], sp["k"], sp["v"], sp["ov"], sp["rowq"], sp["rowq"]] + ([sp["fr"]] if decay else [])
        + x_in,
        out_specs=[sp["k"], sp["k"], sp["v"]] + ([rows_out, sp["fr"]] if decay else []) + x_out,
        scratch_shapes=[pltpu.VMEM((hb, nb, tile, dk), F32), pltpu.VMEM((hb, tile, dk), F32),
                        pltpu.VMEM((hb, tile, dv), F32)] + ([pltpu.VMEM((hb, tile, 1), F32)] * 2 if decay else [])
        + (x_sems if n_x else []))
    out_shape = [jax.ShapeDtypeStruct((t, h * dk), F32), jax.ShapeDtypeStruct((t, h * dk), F32),
                 jax.ShapeDtypeStruct((t, h * dv), F32)] \
        + ([jax.ShapeDtypeStruct((h, nb, 1, tile), F32), jax.ShapeDtypeStruct((h, 1, t), F32)] if decay else []) \
        + x_shapes
    outs = pl.pallas_call(
        body, name="attn_bwd_exchange" if n_x else "attn_bwd", grid_spec=grid_spec, out_shape=out_shape,
        compiler_params=_params("arbitrary" if n_x else "parallel", "arbitrary"),
    )(it, jt, q, k, v, do, lse_row, delta_row, *((key_bias,) if decay else ()), *exchange)
    return tuple(outs[:n_out]) + ((list(outs[n_out:]),) if n_x else ())


CONV_COLS = 512


def _shift_down(g, prev, n):
    out = pltpu.roll(g, n, axis=0)
    row = lax.broadcasted_iota(jnp.int32, g.shape, 0)
    for r in range(n):
        out = jnp.where(row == r, prev[8 - n + r:8 - n + r + 1, :], out)
    return out


def _shift_up(g, nxt, n):
    tm = g.shape[0]
    out = pltpu.roll(g, tm - n, axis=0)
    row = lax.broadcasted_iota(jnp.int32, g.shape, 0)
    for r in range(n):
        out = jnp.where(row == tm - n + r, nxt[r:r + 1, :], out)
    return out


def _conv_fwd(z3, w):
    _, t, d = z3.shape
    tm = _row_tile(t)
    tc = CONV_COLS

    def body(z_ref, zp_ref, w_ref, o_ref):
        i = pl.program_id(1)
        g = z_ref[1] * z_ref[2]
        gp = jnp.where(i > 0, zp_ref[1] * zp_ref[2], 0.0)
        w_ = w_ref[...]
        y = w_[2:3] * g + w_[1:2] * _shift_down(g, gp, 1) + w_[0:1] * _shift_down(g, gp, 2)
        o_ref[...] = (z_ref[0] * y).astype(BF16)

    return pl.pallas_call(
        body, name="conv_fwd", grid=(d // tc, t // tm),
        in_specs=[pl.BlockSpec((3, tm, tc), lambda j, i: (0, i, j)),
                  pl.BlockSpec((3, 8, tc), lambda j, i: (0, jnp.maximum(i * (tm // 8) - 1, 0), j)),
                  pl.BlockSpec((3, tc), lambda j, i: (0, j))],
        out_specs=pl.BlockSpec((tm, tc), lambda j, i: (i, j)),
        out_shape=jax.ShapeDtypeStruct((t, d), BF16), compiler_params=_params("parallel", "parallel"),
    )(z3, z3, w)


def _conv_bwd(z3, w, dyb):
    _, t, d = z3.shape
    tm = _row_tile(t)
    tc = CONV_COLS
    ni = t // tm

    def body(z_ref, zp_ref, zn_ref, d_ref, dn_ref, w_ref, dz_ref, dw_ref):
        i = pl.program_id(1)
        gb, gc, u = z_ref[0], z_ref[1], z_ref[2]
        g = gc * u
        gp = jnp.where(i > 0, zp_ref[1] * zp_ref[2], 0.0)
        w_ = w_ref[...]
        g1, g2 = _shift_down(g, gp, 1), _shift_down(g, gp, 2)
        y = w_[2:3] * g + w_[1:2] * g1 + w_[0:1] * g2
        dyb_ = d_ref[...]
        dy = dyb_ * gb
        dyn = jnp.where(i < ni - 1, dn_ref[...] * zn_ref[0], 0.0)
        dg = w_[2:3] * dy + w_[1:2] * _shift_up(dy, dyn, 1) + w_[0:1] * _shift_up(dy, dyn, 2)
        dz_ref[0] = (dyb_ * y).astype(BF16)
        dz_ref[1] = (dg * u).astype(BF16)
        dz_ref[2] = (dg * gc).astype(BF16)

        @pl.when(i == 0)
        def _():
            dw_ref[...] = jnp.zeros_like(dw_ref)

        dw_ref[...] += jnp.concatenate([jnp.sum(dy * g2, axis=0, keepdims=True),
                                        jnp.sum(dy * g1, axis=0, keepdims=True),
                                        jnp.sum(dy * g, axis=0, keepdims=True)], axis=0)

    cur = pl.BlockSpec((3, tm, tc), lambda j, i: (0, i, j))
    return pl.pallas_call(
        body, name="conv_bwd", grid=(d // tc, ni),
        in_specs=[cur,
                  pl.BlockSpec((3, 8, tc), lambda j, i: (0, jnp.maximum(i * (tm // 8) - 1, 0), j)),
                  pl.BlockSpec((3, 8, tc), lambda j, i: (0, jnp.minimum((i + 1) * (tm // 8), t // 8 - 1), j)),
                  pl.BlockSpec((tm, tc), lambda j, i: (i, j)),
                  pl.BlockSpec((8, tc), lambda j, i: (jnp.minimum((i + 1) * (tm // 8), t // 8 - 1), j)),
                  pl.BlockSpec((3, tc), lambda j, i: (0, j))],
        out_specs=[cur, pl.BlockSpec((3, tc), lambda j, i: (0, j))],
        out_shape=[jax.ShapeDtypeStruct((3, t, d), BF16), jax.ShapeDtypeStruct((3, d), F32)],
        compiler_params=_params("parallel", "arbitrary"),
    )(z3, z3, z3, dyb, dyb, w)


def _loss_head(h, target):
    t, d = h.shape
    tm = BLOCK

    def body(h_ref, t_ref, dh_ref, dhb_ref, loss_ref):
        i = pl.program_id(0)

        @pl.when(i == 0)
        def _():
            loss_ref[...] = jnp.zeros_like(loss_ref)

        err = jnp.where(i > 0, h_ref[...] - t_ref[...], 0.0)
        dh = err * (1.0 / d)
        dh_ref[...] = dh
        dhb_ref[...] = dh.astype(BF16)
        loss_ref[...] += 0.5 * jnp.sum(jnp.sum(err * err, axis=-1, keepdims=True) * (1.0 / d), axis=0, keepdims=True)

    row = pl.BlockSpec((tm, d), lambda i: (i, 0))
    return pl.pallas_call(
        body, name="loss_head", grid=(t // tm,),
        in_specs=[row, pl.BlockSpec((tm, d), lambda i: (jnp.maximum(i - 1, 0), 0))],
        out_specs=[row, row, pl.BlockSpec((1, 1), lambda i: (0, 0))],
        out_shape=[jax.ShapeDtypeStruct((t, d), F32), jax.ShapeDtypeStruct((t, d), BF16),
                   jax.ShapeDtypeStruct((1, 1), F32)],
        compiler_params=_params("arbitrary"),
    )(h, target)


def _rope_tables(t):
    pos = jnp.arange(t, dtype=F32) - PAD
    inv_freq = ROPE_BASE ** (-jnp.arange(0, MLA_ROPE, 2, dtype=F32) / MLA_ROPE)
    ang = pos[:, None] * inv_freq[None, :]
    cos, sin = jnp.cos(ang), jnp.sin(ang)
    one, zero = jnp.ones((t, MLA_NOPE), F32), jnp.zeros((t, MLA_NOPE), F32)
    tail = jnp.zeros((t, LANES - MLA_QK), F32)
    return (jnp.concatenate([one, cos, cos, tail], axis=1), jnp.concatenate([zero, -sin, sin, tail], axis=1))


def _pad_lanes(x, width=LANES):
    return jnp.pad(x, [(0, 0)] * (x.ndim - 1) + [(0, width - x.shape[-1])])


def _permute_in_attn(w):
    return jnp.concatenate([w[:, :640], w[:, 672:2208], w[:, 640:672], w[:, 2208:2216],
                            jnp.zeros((w.shape[0], ATTN_IN_PAD - 2216), w.dtype)], axis=1)


def _unpermute_in_attn(dw):
    return jnp.concatenate([dw[:, :640], dw[:, 2176:2208], dw[:, 640:2176], dw[:, 2208:2216]], axis=1)


def _attn_layer_fwd(hn, wl, rope, gather=()):
    t = hn.shape[0]
    z = _mm(hn, wl["w_in"], name="attn_in")
    cqn = _rms_fwd(z[:, C_CQ:C_CQ + Q_LORA], wl["g_cq"])
    ckvn = _rms_fwd(z[:, C_CKV:C_CKV + KV_LORA], wl["g_ckv"])
    w_uq = _pad_lanes(wl["w_uq"].reshape(Q_LORA, HEADS, MLA_QK)).reshape(Q_LORA, HEADS * LANES)
    xq = _mm(cqn, w_uq, name="mla_uq")
    kvf = _mm(ckvn, wl["w_ukv"], name="mla_ukv")
    kv3 = kvf.reshape(t, HEADS, MLA_NOPE + MLA_V)
    k_pe = jnp.broadcast_to(z[:, None, C_TAIL:C_TAIL + MLA_ROPE], (t, HEADS, MLA_ROPE))
    xk = _pad_lanes(jnp.concatenate([kv3[:, :, :MLA_NOPE], k_pe], axis=-1)).reshape(t, HEADS * LANES)
    v_mla = kv3[:, :, MLA_NOPE:].reshape(t, HEADS * MLA_V).astype(BF16)
    gq, gk = _pad_lanes(wl["g_q_mla"].reshape(1, -1)), _pad_lanes(wl["g_k_mla"].reshape(1, -1))
    q_mla = _head_norm_fwd(xq, gq, MLA_QK, MLA_SCALE, rope)
    k_mla = _head_norm_fwd(xk, gk, MLA_QK, 1.0, rope)
    o_mla, lse_mla, *gathered = _attn_fwd(q_mla, k_mla, v_mla, gather=gather)
    xfq = z[:, C_FQ:C_FQ + HEADS * FOX_DIM]
    xfk = z[:, C_FK:C_FK + HEADS * FOX_DIM]
    v_fox = z[:, C_FV:C_FV + HEADS * FOX_DIM].astype(BF16)
    bias = jnp.pad(wl["b_forget"].reshape(1, -1), ((0, 0), (TAIL_F, LANES - TAIL_F - HEADS)))
    cum = _gate_fwd(z, bias)
    neg_f = (-LOG2E * jnp.transpose(cum[:, TAIL_F:TAIL_F + HEADS]))[:, None, :]
    q_fox = _head_norm_fwd(xfq, wl["g_q_fox"].reshape(1, -1), FOX_DIM, FOX_SCALE)
    k_fox = _head_norm_fwd(xfk, wl["g_k_fox"].reshape(1, -1), FOX_DIM, 1.0)
    o_fox, lse_fox = _attn_fwd(q_fox, k_fox, v_fox, neg_f)
    cat = jnp.concatenate([o_mla, o_fox], axis=1).astype(BF16)
    saved = dict(z=z, cqn=cqn, ckvn=ckvn, xq=xq, xk=xk, v_mla=v_mla, q_mla=q_mla, k_mla=k_mla, o_mla=o_mla,
                 lse_mla=lse_mla, xfq=xfq, xfk=xfk, v_fox=v_fox, q_fox=q_fox, k_fox=k_fox, bias=bias,
                 neg_f=neg_f, o_fox=o_fox, lse_fox=lse_fox, cat=cat, gq=gq, gk=gk, w_uq=w_uq)
    return cat, saved, (gathered[0] if gathered else ())


def _attn_layer_bwd(dcat, hn, wl, sv, rope, exchange=()):
    t = hn.shape[0]
    g = {}
    do_mla = dcat[:, :HEADS * MLA_V]
    do_fox = dcat[:, HEADS * MLA_V:]
    qkv = (sv["q_fox"], sv["k_fox"], sv["v_fox"])
    delta = _attn_delta(sv["o_fox"], do_fox)
    dq_fox, dk_fox, dv_fox, row_sums, key_sums, *exchanged = _attn_bwd(
        *qkv, do_fox, sv["lse_fox"], delta, sv["neg_f"], exchange=exchange)
    dcum = jnp.pad(jnp.transpose(LOG2E * (row_sums.reshape(HEADS, t) - key_sums.reshape(HEADS, t))),
                   ((0, 0), (TAIL_F, LANES - TAIL_F - HEADS)))
    dtail_f, dbias = _gate_bwd(sv["z"], sv["bias"], dcum)
    g["b_forget"] = dbias[0, TAIL_F:TAIL_F + HEADS]
    dxfq, _, dgq = _head_norm_bwd(sv["xfq"], wl["g_q_fox"].reshape(1, -1), dq_fox, FOX_DIM, FOX_SCALE)
    dxfk, _, dgk = _head_norm_bwd(sv["xfk"], wl["g_k_fox"].reshape(1, -1), dk_fox, FOX_DIM, 1.0)
    g["g_q_fox"], g["g_k_fox"] = dgq[0], dgk[0]
    qkv = (sv["q_mla"], sv["k_mla"], sv["v_mla"])
    delta = _attn_delta(sv["o_mla"], do_mla)
    dq_mla, dk_mla, dv_mla = _attn_bwd(*qkv, do_mla, sv["lse_mla"], delta)
    dxq, _, dgq = _head_norm_bwd(sv["xq"], sv["gq"], dq_mla, MLA_QK, MLA_SCALE, rope)
    dxk, dxk_sum, dgk = _head_norm_bwd(sv["xk"], sv["gk"], dk_mla, MLA_QK, 1.0, rope)
    g["g_q_mla"], g["g_k_mla"] = dgq[0, :MLA_QK], dgk[0, :MLA_QK]
    dqf = dxq.astype(BF16)
    dkvf = jnp.concatenate([dxk.reshape(t, HEADS, LANES)[:, :, :MLA_NOPE], dv_mla.reshape(t, HEADS, MLA_V)],
                           axis=-1).reshape(t, HEADS * (MLA_NOPE + MLA_V)).astype(BF16)
    g["w_uq"] = _mm(sv["cqn"], dqf, ta=True, name="d_w_uq").reshape(Q_LORA, HEADS, LANES)[:, :, :MLA_QK].reshape(
        Q_LORA, HEADS * MLA_QK)
    g["w_ukv"] = _mm(sv["ckvn"], dkvf, ta=True, name="d_w_ukv")
    dcqn = _mm(dqf, sv["w_uq"], tb=True, name="d_cqn")
    dckvn = _mm(dkvf, wl["w_ukv"], tb=True, name="d_ckvn")
    z = sv["z"]
    dcq, dg_cq = _rms_bwd(z[:, C_CQ:C_CQ + Q_LORA], wl["g_cq"], dcqn)
    dckv, dg_ckv = _rms_bwd(z[:, C_CKV:C_CKV + KV_LORA], wl["g_ckv"], dckvn)
    g["g_cq"], g["g_ckv"] = dg_cq[0], dg_ckv[0]
    tail = jnp.concatenate([dxk_sum[:, MLA_NOPE:MLA_QK], dtail_f[:, TAIL_F:]], axis=1)
    dz = jnp.concatenate([dcq, dckv, dxfq, dxfk, dv_fox, tail], axis=1).astype(BF16)
    g["w_in"] = _mm(hn, dz, ta=True, name="d_w_in_attn")
    dhn = _mm(dz, wl["w_in"], tb=True, out_dtype=BF16, name="d_hn_attn")
    return dhn, g, (exchanged[0] if exchanged else ())


def _local_step(x, target, w, gather_late=None, exchange_early=None):
    seq = x.shape[0]
    t = seq + BLOCK
    rope = _rope_tables(t)
    h = jnp.concatenate([jnp.zeros((PAD, D_MODEL), F32), w["meta_tokens"], x], axis=0)
    tape = []
    hn = _rms_fwd(h, w["g_mix"][0])
    for layer in range(DEPTH):
        j = layer // 2
        if layer % 2 == 0:
            wl = dict(w_in=w["w_in_attn"][j], g_cq=w["g_cq"][j], w_uq=w["w_uq"][j], g_ckv=w["g_ckv"][j],
                      w_ukv=w["w_ukv"][j], g_q_mla=w["g_q_mla"][j], g_k_mla=w["g_k_mla"][j],
                      g_q_fox=w["g_q_fox"][j], g_k_fox=w["g_k_fox"][j], b_forget=w["b_forget"][j])
            hosted = gather_late is not None and layer == 0
            mixed, sv, gathered = _attn_layer_fwd(hn, wl, rope, gather_late[0] if hosted else ())
            if hosted:
                gather_late[1](w, gathered)
            h1, hn2 = _mm(mixed, w["w_out_attn"][j], res=h, norm_gain=w["g_mlp"][layer], name="attn_out")
        else:
            wl = None
            z3 = _mm(hn, w["w_in_conv"][j], out_seg=3, name="conv_in")
            mixed = _conv_fwd(z3, w["conv_w"][j])
            sv = dict(z3=z3)
            h1, hn2 = _mm(mixed, w["w_out_conv"][j], res=h, norm_gain=w["g_mlp"][layer], name="conv_out")
        u, act = _mm(hn2, w["w_mlp_up"][layer], epi="relu2", name="mlp_up")
        tape.append(dict(h=h, hn=hn, wl=wl, sv=sv, mixed=mixed, h1=h1, hn2=hn2, u=u, act=act))
        if layer + 1 < DEPTH:
            h, hn = _mm(act, w["w_mlp_down"][layer], res=h1, norm_gain=w["g_mix"][layer + 1], name="mlp_down")
        else:
            h = _mm(act, w["w_mlp_down"][layer], res=h1, name="mlp_down")

    dh, dh_b, loss = _loss_head(h, target)
    exchanged = ()
    g = {n: [None] * (DEPTH if n in ("g_mix", "g_mlp", "w_mlp_up", "w_mlp_down") else DEPTH // 2)
         for n in WEIGHTS if n != "meta_tokens"}
    for layer in reversed(range(DEPTH)):
        j = layer // 2
        tp = tape[layer]
        g["w_mlp_down"][layer] = _mm(tp["act"], dh_b, ta=True, name="d_w_down")
        du = _mm(dh_b, w["w_mlp_down"][layer], tb=True, epi="relu2_bwd", aux=tp["u"], out_dtype=BF16, name="d_u")
        g["w_mlp_up"][layer] = _mm(tp["hn2"], du, ta=True, name="d_w_up")
        dhn2 = _mm(du, w["w_mlp_up"][layer], tb=True, out_dtype=BF16, name="d_hn2")
        dh1, dh1_b, dg = _rms_bwd(tp["h1"], w["g_mlp"][layer], dhn2, dres=dh, want_bf16=True)
        g["g_mlp"][layer] = dg[0]
        if layer % 2 == 0:
            g["w_out_attn"][j] = _mm(tp["mixed"], dh1_b, ta=True, name="d_w_out_attn")
            dcat = _mm(dh1_b, w["w_out_attn"][j], tb=True, name="d_cat")
            hosted = exchange_early is not None and layer == 0
            dhn, gl, got = _attn_layer_bwd(dcat, tp["hn"], tp["wl"], tp["sv"], rope,
                                           exchange_early(g) if hosted else ())
            if hosted:
                exchanged = got
            g["w_in_attn"][j] = _unpermute_in_attn(gl.pop("w_in"))
            for n, val in gl.items():
                g[n][j] = val
        else:
            g["w_out_conv"][j] = _mm(tp["mixed"], dh1_b, ta=True, name="d_w_out_conv")
            dyb = _mm(dh1_b, w["w_out_conv"][j], tb=True, name="d_yb")
            dz3, dcw = _conv_bwd(tp["sv"]["z3"], w["conv_w"][j], dyb)
            g["conv_w"][j] = dcw
            g["w_in_conv"][j] = _mm(tp["hn"], dz3, ta=True, name="d_w_in_conv")
            dhn = _mm(dz3, w["w_in_conv"][j], tb=True, out_dtype=BF16, name="d_hn_conv")
        dh, dh_b, dg = _rms_bwd(tp["h"], w["g_mix"][layer], dhn, dres=dh1, want_bf16=True)
        g["g_mix"][layer] = dg[0]
    g["meta_tokens"] = [dh[PAD:BLOCK]]
    return loss, dh[BLOCK:], g, exchanged


COMM_ROWS = 2048
ADAMW_BLOCK_BYTES = 1 << 20


def kernel(x, meta_tokens, g_mix, g_mlp, w_in_attn, g_cq, w_uq, g_ckv, w_ukv, g_q_mla, g_k_mla, g_q_fox, g_k_fox, b_forget, w_out_attn, w_in_conv, conv_w, w_out_conv, w_mlp_up, w_mlp_down, loss_target, m_meta_tokens, m_g_mix, m_g_mlp, m_w_in_attn, m_g_cq, m_w_uq, m_g_ckv, m_w_ukv, m_g_q_mla, m_g_k_mla, m_g_q_fox, m_g_k_fox, m_b_forget, m_w_out_attn, m_w_in_conv, m_conv_w, m_w_out_conv, m_w_mlp_up, m_w_mlp_down, v_meta_tokens, v_g_mix, v_g_mlp, v_w_in_attn, v_g_cq, v_w_uq, v_g_ckv, v_w_ukv, v_g_q_mla, v_g_k_mla, v_g_q_fox, v_g_k_fox, v_b_forget, v_w_out_attn, v_w_in_conv, v_conv_w, v_w_out_conv, v_w_mlp_up, v_w_mlp_down):
    args = dict(locals())
    local = {n: args[n] for n in WEIGHTS}
    mom = {n: args["m_" + n] for n in WEIGHTS}
    var = {n: args["v_" + n] for n in WEIGHTS}
    axis = dict(SHARDED)
    count = {n: local[n].shape[0] for n, _ in SHARDED if n != "meta_tokens"}
    piece = lambda src, p: src[p[0]] if p[1] is None else src[p[0]][p[1]]
    piece_axis = lambda p: axis[p[0]] - (0 if p[1] is None else 1)
    shape_of = lambda p: piece(local, p).shape
    layers = lambda n, ls: [(n, l) for l in ls]
    first_bf = [("w_uq", 0), ("w_ukv", 0)]
    first_f32 = [("meta_tokens", None), ("conv_w", 0), ("conv_w", 1)]
    late_bf = ([("w_uq", 1), ("w_ukv", 1)] + layers("w_out_attn", (0, 1)) + layers("w_in_conv", (0, 1))
               + layers("w_out_conv", (0, 1)) + layers("w_mlp_up", range(DEPTH)) + layers("w_mlp_down", range(DEPTH)))
    early_bf = ([("w_uq", 1), ("w_ukv", 1)] + layers("w_out_attn", (0, 1)) + layers("w_in_conv", (0, 1))
                + layers("conv_w", (0, 1)) + layers("w_out_conv", (0, 1)) + layers("w_mlp_up", range(DEPTH))
                + layers("w_mlp_down", range(DEPTH)))
    last_bf = [("w_uq", 0), ("w_ukv", 0), ("meta_tokens", None)]

    w = {n: local[n] for n in REPLICATED}
    w.update({n: [None] * c for n, c in count.items()})

    def install(w, pieces, gathered, in_layer, gathered_in):
        for p, blocks in zip(pieces, _unpack(gathered, [shape_of(p) for p in pieces], (N_DEV,))):
            value = _from_shards(blocks, piece_axis(p))
            if p[1] is None:
                w[p[0]] = value
            else:
                w[p[0]][p[1]] = value
        w["w_in_attn"][in_layer] = _permute_in_attn(_from_shards(gathered_in, piece_axis(("w_in_attn", 0))))

    got_bf, got_in, got_f32 = _all_gather([
        _pack([piece(local, p) for p in first_bf], 16, BF16), local["w_in_attn"][0].astype(BF16),
        _pack([piece(local, p) for p in first_f32], 8, F32)])
    install(w, first_bf, got_bf, 0, got_in)
    for p, blocks in zip(first_f32, _unpack(got_f32, [shape_of(p) for p in first_f32], (N_DEV,))):
        if p[1] is None:
            w[p[0]] = _from_shards(blocks, piece_axis(p))
        else:
            w[p[0]][p[1]] = _from_shards(blocks, piece_axis(p))
    gather_late = ([_pack([piece(local, p) for p in late_bf], 16, BF16), local["w_in_attn"][1].astype(BF16)],
                   lambda w_, got: install(w_, late_bf, got[0], 1, got[1]))

    def pack_grads(g, pieces, in_layer):
        sent = _pack_rows([_to_shards(piece(g, p), piece_axis(p)) for p in pieces], COMM_ROWS, BF16)
        return [sent, _to_shards(g["w_in_attn"][in_layer], piece_axis(("w_in_attn", 0))).astype(BF16)]

    loss_part, dx, grads, early = _local_step(x[0], loss_target[0], w, gather_late,
                                              lambda g: pack_grads(g, early_bf, 1))

    grads["meta_tokens"] = grads["meta_tokens"][0]
    last = _all_to_all(pack_grads(grads, last_bf, 0))
    g_piece = {}
    for pieces, (got, got_in), in_layer in ((early_bf, early, 1), (last_bf, last, 0)):
        summed = _unpack(_sum_blocks(got, COMM_ROWS), [shape_of(p) for p in pieces])
        g_piece.update(zip(pieces, summed))
        g_piece[("w_in_attn", in_layer)] = _sum_blocks(got_in, 256)
    g_local = {n: jnp.stack([g_piece[(n, l)] for l in range(c)]) for n, c in count.items()}
    g_local["meta_tokens"] = g_piece[("meta_tokens", None)]
    rep, = _all_gather([_pack([jnp.stack(grads[n]) for n in REPLICATED] + [loss_part], 8, F32)])
    g_rep = _sum_blocks(rep, rep.shape[1])
    *g_reps, loss = _unpack(g_rep, [local[n].shape for n in REPLICATED] + [()])
    g_local.update(zip(REPLICATED, g_reps))

    def flat(src, names, rows_multiple):
        return _pack([src[n] for n in names], rows_multiple, F32)

    upd = {}
    rows = g_rep.shape[0]
    outs = _adamw(flat(local, REPLICATED, rows), g_rep, flat(mom, REPLICATED, rows), flat(var, REPLICATED, rows), rows)
    for kind, buf in zip(("delta", "m", "v"), outs):
        upd.update({(kind, n): a for n, a in zip(REPLICATED, _unpack(buf, [local[n].shape for n in REPLICATED]))})
    for n, _ in SHARDED:
        as_rows = lambda a: a.reshape(-1, a.shape[-1])
        n_rows, n_cols = as_rows(local[n]).shape
        tiles = [r for r in (1024, 512, 256, 128, 64, 32, 16, 8) if r * n_cols * 4 <= ADAMW_BLOCK_BYTES]
        outs = _adamw(as_rows(local[n]), as_rows(g_local[n]), as_rows(mom[n]), as_rows(var[n]), _pick(n_rows, tiles))
        for kind, buf in zip(("delta", "m", "v"), outs):
            upd[(kind, n)] = buf.reshape(local[n].shape)

    return (loss, dx[None], *[g_local[n] for n in WEIGHTS], *[upd[("delta", n)] for n in WEIGHTS],
            *[upd[("m", n)] for n in WEIGHTS], *[upd[("v", n)] for n in WEIGHTS])
```

```python
import functools
import math

import jax
import jax.numpy as jnp
import numpy as np
from jax import lax
from jax.experimental import pallas as pl
from jax.experimental.pallas import tpu as pltpu

F32 = jnp.float32
BF16 = jnp.bfloat16

N_DEV = 8
D_MODEL = 1024
DEPTH = 4
N_META = 16
BLOCK = 128
PAD = BLOCK - N_META
HEADS = 8
MLA_NOPE = 64
MLA_ROPE = 32
MLA_QK = MLA_NOPE + MLA_ROPE
MLA_V = 64
Q_LORA = 384
KV_LORA = 256
ROPE_BASE = 10000.0
FOX_DIM = 64
D_FF = 4 * D_MODEL
EPS = 1e-6
NEG = -1e30
LANES = 128
ATTN_IN_PAD = 2304
C_CQ, C_CKV, C_FQ, C_FK, C_FV, C_TAIL = 0, 384, 640, 1152, 1664, 2176
TAIL_F = MLA_ROPE

ADAM_LR = 0.001
ADAM_B1 = 0.9
ADAM_B2 = 0.999
ADAM_EPS = 1e-08
ADAM_WD = 0.01
ADAM_STEP = 10

VMEM_LIMIT_BYTES = 48 * 1024 * 1024
MM_VMEM_BUDGET_BYTES = 28 * 1024 * 1024
MESH = pl.DeviceIdType.MESH

SHARDED = (
    ("meta_tokens", 1), ("w_in_attn", 2), ("w_uq", 2), ("w_ukv", 2), ("w_out_attn", 1),
    ("w_in_conv", 2), ("conv_w", 2), ("w_out_conv", 1), ("w_mlp_up", 2), ("w_mlp_down", 1))
REPLICATED = ("g_mix", "g_mlp", "g_cq", "g_ckv", "g_q_mla", "g_k_mla", "g_q_fox", "g_k_fox", "b_forget")
WEIGHTS = ("meta_tokens", "g_mix", "g_mlp", "w_in_attn", "g_cq", "w_uq", "g_ckv", "w_ukv", "g_q_mla",
           "g_k_mla", "g_q_fox", "g_k_fox", "b_forget", "w_out_attn", "w_in_conv", "conv_w",
           "w_out_conv", "w_mlp_up", "w_mlp_down")


def _params(*sem):
    return pltpu.CompilerParams(dimension_semantics=sem, vmem_limit_bytes=VMEM_LIMIT_BYTES)


def _row_tile(t):
    return 640 if (t % 640 == 0 and t > 640) else 128


def _padded_rows(n, rows_multiple):
    rows = -(-n // LANES)
    return -(-rows // rows_multiple) * rows_multiple


def _pack(parts, rows_multiple, dtype):
    n = sum(p.size for p in parts)
    rows = _padded_rows(n, rows_multiple)
    fill = [jnp.zeros((rows * LANES - n,), dtype)]
    return jnp.concatenate([p.reshape(-1).astype(dtype) for p in parts] + fill).reshape(rows, LANES)


def _pack_rows(parts, rows_multiple, dtype):
    n = sum(p.size for p in parts) // N_DEV
    rows = _padded_rows(n, rows_multiple)
    fill = [jnp.zeros((N_DEV, rows * LANES - n), dtype)]
    flat = jnp.concatenate([p.reshape(N_DEV, -1).astype(dtype) for p in parts] + fill, axis=1)
    return flat.reshape(N_DEV, rows, LANES)


def _unpack(buf, shapes, lead=()):
    flat = buf.reshape(lead + (-1,))
    out, off = [], 0
    for s in shapes:
        n = math.prod(s)
        out.append(flat[..., off:off + n].reshape(lead + tuple(s)))
        off += n
    return out


def _to_shards(full, axis):
    s = full.shape
    return jnp.moveaxis(full.reshape(s[:axis] + (N_DEV, s[axis] // N_DEV) + s[axis + 1:]), axis, 0)


def _from_shards(g8, axis):
    m = jnp.moveaxis(g8, 0, axis)
    s = m.shape
    return m.reshape(s[:axis] + (s[axis] * s[axis + 1],) + s[axis + 2:])


def _comm_call(body, name, xs, out_shapes):
    n = len(xs)
    hbm = pl.BlockSpec(memory_space=pltpu.HBM)
    return pl.pallas_call(
        body, name=name, out_shape=out_shapes, in_specs=[hbm] * n, out_specs=[hbm] * n,
        scratch_shapes=[pltpu.SemaphoreType.DMA((n, 7)), pltpu.SemaphoreType.DMA((n, 7)),
                        pltpu.SemaphoreType.DMA((n,))],
    )(*xs)


def _gather_ops(x_refs, out_refs, send_sems, recv_sems, local_sems):
    n = len(x_refs)
    x_, y_, c = lax.axis_index("x"), lax.axis_index("y"), lax.axis_index("c")
    me, sibling = (x_, y_, c), (x_, y_, 1 - c)
    chips = [(1 - x_, y_), (x_, 1 - y_), (1 - x_, 1 - y_)]

    def rows(a, px, py, pc):
        return out_refs[a].at[4 * px + 2 * py + pc]

    def copy(a, k, block, to, src=None):
        return pltpu.make_async_remote_copy(
            src_ref=rows(a, *block) if src is None else src, dst_ref=rows(a, *block),
            send_sem=send_sems.at[a, k], recv_sem=recv_sems.at[a, k], device_id=to, device_id_type=MESH)

    def mine():
        return [pltpu.make_async_copy(x_refs[a], rows(a, *me), local_sems.at[a]) for a in range(n)]

    def first():
        cps = []
        for a in range(n):
            cps.append(copy(a, 0, me, sibling, src=x_refs[a]))
            cps += [copy(a, 1 + j, me, (*chip, c), src=x_refs[a]) for j, chip in enumerate(chips)]
        return cps

    def start():
        for cp in mine() + first():
            cp.start()

    def finish():
        passed = []
        for j, chip in enumerate(chips):
            for a in range(n):
                copy(a, 1 + j, (*chip, c), me).wait_recv()
                passed.append(copy(a, 4 + j, (*chip, c), sibling))
                passed[-1].start()
        for a in range(n):
            copy(a, 0, sibling, me).wait_recv()
            for j, chip in enumerate(chips):
                copy(a, 4 + j, (*chip, 1 - c), me).wait_recv()
        for cp in first() + passed:
            cp.wait_send()
        for cp in mine():
            cp.wait()

    return start, finish


def _exchange_ops(x_refs, out_refs, send_sems, recv_sems, local_sems):
    n = len(x_refs)
    x_, y_, c = lax.axis_index("x"), lax.axis_index("y"), lax.axis_index("c")
    me = 4 * x_ + 2 * y_ + c

    def peer(k):
        px = 1 - x_ if k & 4 else x_
        py = 1 - y_ if k & 2 else y_
        pc = 1 - c if k & 1 else c
        return px, py, pc

    def copy(a, k):
        px, py, pc = peer(k)
        return pltpu.make_async_remote_copy(
            src_ref=x_refs[a].at[4 * px + 2 * py + pc], dst_ref=out_refs[a].at[me],
            send_sem=send_sems.at[a, k - 1], recv_sem=recv_sems.at[a, k - 1], device_id=(px, py, pc),
            device_id_type=MESH)

    def arrival(a, k):
        px, py, pc = peer(k)
        slot = 4 * px + 2 * py + pc
        return pltpu.make_async_remote_copy(
            src_ref=x_refs[a].at[slot], dst_ref=out_refs[a].at[slot],
            send_sem=send_sems.at[a, k - 1], recv_sem=recv_sems.at[a, k - 1], device_id=(px, py, pc),
            device_id_type=MESH)

    def mine():
        return [pltpu.make_async_copy(x_refs[a].at[me], out_refs[a].at[me], local_sems.at[a]) for a in range(n)]

    def sends():
        return [copy(a, k) for k in range(1, N_DEV) for a in range(n)]

    def start():
        for cp in mine() + sends():
            cp.start()

    def finish():
        for k in range(1, N_DEV):
            for a in range(n):
                arrival(a, k).wait_recv()
        for cp in sends():
            cp.wait_send()
        for cp in mine():
            cp.wait()

    return start, finish


def _comm_parts(xs, gather):
    n = len(xs)
    hbm = pl.BlockSpec(memory_space=pltpu.HBM)
    shapes = [jax.ShapeDtypeStruct(((N_DEV,) + x.shape) if gather else x.shape, x.dtype) for x in xs]
    sems = [pltpu.SemaphoreType.DMA((n, 7)), pltpu.SemaphoreType.DMA((n, 7)), pltpu.SemaphoreType.DMA((n,))]
    return [hbm] * n, [hbm] * n, shapes, sems


def _all_gather(xs):
    n = len(xs)

    def body(*refs):
        start, finish = _gather_ops(refs[:n], refs[n:2 * n], *refs[2 * n:])
        start()
        finish()

    return _comm_call(body, "all_gather", xs, [jax.ShapeDtypeStruct((N_DEV,) + x.shape, x.dtype) for x in xs])


def _all_to_all(xs):
    n = len(xs)

    def body(*refs):
        start, finish = _exchange_ops(refs[:n], refs[n:2 * n], *refs[2 * n:])
        start()
        finish()

    return _comm_call(body, "all_to_all", xs, [jax.ShapeDtypeStruct(x.shape, x.dtype) for x in xs])


def _exchange_and_gather(xs, ys):
    nx, ny = len(xs), len(ys)
    x_in, x_out, x_shapes, x_sems = _comm_parts(xs, False)
    y_in, y_out, y_shapes, y_sems = _comm_parts(ys, True)

    def body(*refs):
        ins, outs, sems = refs[:nx + ny], refs[nx + ny:2 * (nx + ny)], refs[2 * (nx + ny):]
        x_start, x_finish = _exchange_ops(ins[:nx], outs[:nx], *sems[:3])
        y_start, y_finish = _gather_ops(ins[nx:], outs[nx:], *sems[3:])
        x_start()
        y_start()
        x_finish()
        y_finish()

    outs = pl.pallas_call(
        body, name="exchange_and_gather", out_shape=x_shapes + y_shapes, in_specs=x_in + y_in,
        out_specs=x_out + y_out, scratch_shapes=x_sems + y_sems,
    )(*xs, *ys)
    return list(outs[:nx]), list(outs[nx:])


def _sum_blocks(x, rows_tile):
    _, r, c_ = x.shape

    def body(x_ref, o_ref):
        acc = x_ref[0].astype(F32)
        for d in range(1, N_DEV):
            acc = acc + x_ref[d].astype(F32)
        o_ref[...] = acc

    return pl.pallas_call(
        body, name="sum_blocks", grid=(r // rows_tile,),
        in_specs=[pl.BlockSpec((N_DEV, rows_tile, c_), lambda i: (0, i, 0))],
        out_specs=pl.BlockSpec((rows_tile, c_), lambda i: (i, 0)),
        out_shape=jax.ShapeDtypeStruct((r, c_), F32),
        compiler_params=_params("parallel"),
    )(x)


def _adamw(w, g, m, v, rows_tile):
    r, c_ = w.shape
    c1 = 1.0 - ADAM_B1 ** ADAM_STEP
    c2 = 1.0 - ADAM_B2 ** ADAM_STEP

    def body(w_ref, g_ref, m_ref, v_ref, d_ref, mo_ref, vo_ref):
        g_ = g_ref[...]
        m_ = ADAM_B1 * m_ref[...] + (1.0 - ADAM_B1) * g_
        v_ = ADAM_B2 * v_ref[...] + (1.0 - ADAM_B2) * (g_ * g_)
        m_hat = m_ / c1
        v_hat = v_ / c2
        d_ref[...] = -ADAM_LR * (m_hat / (jnp.sqrt(v_hat) + ADAM_EPS) + ADAM_WD * w_ref[...])
        mo_ref[...] = m_
        vo_ref[...] = v_

    spec = pl.BlockSpec((rows_tile, c_), lambda i: (i, 0))
    shape = jax.ShapeDtypeStruct((r, c_), F32)
    return pl.pallas_call(
        body, name="adamw", grid=(r // rows_tile,), in_specs=[spec] * 4, out_specs=[spec] * 3,
        out_shape=[shape] * 3, compiler_params=_params("parallel"),
    )(w, g, m, v)


def _pick(n, prefs):
    for p in prefs:
        if n % p == 0:
            return p
    return n


def _mat_spec(arr, tr, tc, r_of, c_of):
    if arr.ndim == 2:
        return pl.BlockSpec((tr, tc), lambda i, j, k: (r_of(i, j, k), c_of(i, j, k)))
    per = arr.shape[2] // tc
    return pl.BlockSpec((None, tr, tc), lambda i, j, k: (c_of(i, j, k) // per, r_of(i, j, k), c_of(i, j, k) % per))


def _mm(a, b, *, ta=False, tb=False, out_dtype=F32, out_seg=None, res=None, epi=None, aux=None, norm_gain=None,
        tm=None, tn=None, tk=None, name="mm"):
    def dims(x):
        return (x.shape[0], x.shape[1]) if x.ndim == 2 else (x.shape[1], x.shape[0] * x.shape[2])
    ar, ac = dims(a)
    br, bc = dims(b)
    m, k = (ac, ar) if ta else (ar, ac)
    n, kb = (br, bc) if tb else (bc, br)
    assert k == kb, (a.shape, b.shape, ta, tb)
    tn = tn or _pick(n, (1024, 768, 512, 384, 256, 128))
    tk = tk or _pick(k, ((1664,) if ta else (2048,)) + (1024, 768, 640, 512, 384, 256, 128))

    def vmem_bytes(rows):
        out_bytes = 2 * 2 if epi == "relu2" else jnp.dtype(out_dtype).itemsize + (2 if norm_gain is not None else 0)
        x_bytes = sum(x.dtype.itemsize for x in (res, aux) if x is not None)
        return 2 * (rows * tk * 2 + tk * tn * 2 + rows * tn * (out_bytes + x_bytes)) + (rows * tn * 4 if k > tk else 0)

    tall = (1664,) if (not ta and m % 1664 == 0 and vmem_bytes(1664) <= MM_VMEM_BUDGET_BYTES) else ()
    tm = tm or _pick(m, (1024, 512, 384, 256, 128) if ta else tall + (640, 512, 384, 256, 128))
    if out_seg:
        assert (n // out_seg) % tn == 0
    for x, t in ((a, tm if ta else tk), (b, tk if tb else tn)):
        if x.ndim == 3:
            assert x.shape[2] % t == 0
    nk = k // tk
    gi, gj, gk = (lambda j, i, kk: i), (lambda j, i, kk: j), (lambda j, i, kk: kk)
    a_spec = _mat_spec(a, tk, tm, gk, gi) if ta else _mat_spec(a, tm, tk, gi, gk)
    b_spec = _mat_spec(b, tn, tk, gj, gk) if tb else _mat_spec(b, tk, tn, gk, gj)
    out_like = jax.ShapeDtypeStruct((out_seg, m, n // out_seg) if out_seg else (m, n), out_dtype)
    o_spec = _mat_spec(out_like, tm, tn, gi, gj)
    o_shape = (out_seg, m, n // out_seg) if out_seg else (m, n)
    dn = (((0 if ta else 1,), (1 if tb else 0,)), ((), ()))
    extra = [x for x in (res, aux) if x is not None]
    assert not (res is not None and aux is not None)
    normed = norm_gain is not None
    assert not normed or (res is not None and tn == n and not out_seg)
    n_out = 2 if (epi == "relu2" or normed) else 1
    n_extra = len(extra)

    def body(*refs):
        a_ref, b_ref = refs[0], refs[1]
        x_ref = refs[2] if n_extra else None
        g_ref = refs[2 + n_extra] if normed else None
        outs = refs[2 + n_extra + normed:2 + n_extra + normed + n_out]
        acc_ref = refs[-1] if nk > 1 else None
        part = lax.dot_general(a_ref[...], b_ref[...], dn, preferred_element_type=F32)

        def finish(acc):
            if epi == "relu2":
                r = jnp.maximum(acc, 0.0)
                outs[0][...] = r.astype(BF16)
                outs[1][...] = (r * r).astype(BF16)
            elif epi == "relu2_bwd":
                outs[0][...] = (acc * (2.0 * x_ref[...].astype(F32))).astype(out_dtype)
            elif res is not None:
                h = acc + x_ref[...]
                outs[0][...] = h.astype(out_dtype)
                if normed:
                    rstd = lax.rsqrt(jnp.mean(h * h, axis=-1, keepdims=True) + EPS)
                    outs[1][...] = (h * rstd * g_ref[...]).astype(BF16)
            else:
                outs[0][...] = acc.astype(out_dtype)

        if nk == 1:
            finish(part)
        else:
            kk = pl.program_id(2)

            @pl.when(kk == 0)
            def _():
                acc_ref[...] = part

            @pl.when(kk > 0)
            def _():
                acc_ref[...] += part

            @pl.when(kk == nk - 1)
            def _():
                finish(acc_ref[...])

    if epi == "relu2":
        out_shape = [jax.ShapeDtypeStruct(o_shape, BF16), jax.ShapeDtypeStruct(o_shape, BF16)]
        out_specs = [o_spec, o_spec]
    elif normed:
        out_shape = [jax.ShapeDtypeStruct(o_shape, out_dtype), jax.ShapeDtypeStruct(o_shape, BF16)]
        out_specs = [o_spec, o_spec]
    else:
        out_shape = jax.ShapeDtypeStruct(o_shape, out_dtype)
        out_specs = o_spec
    x_specs = [pl.BlockSpec((tm, tn), lambda j, i, kk: (i, j))] * len(extra)
    if normed:
        x_specs = x_specs + [pl.BlockSpec((1, tn), lambda j, i, kk: (0, 0))]
        extra = extra + [norm_gain.reshape(1, n)]
    res_ = pl.pallas_call(
        body, name=name, grid=(n // tn, m // tm, nk),
        in_specs=[a_spec, b_spec] + x_specs, out_specs=out_specs, out_shape=out_shape,
        scratch_shapes=[pltpu.VMEM((tm, tn), F32)] if nk > 1 else [],
        compiler_params=_params("parallel", "parallel", "arbitrary"),
    )(a, b, *extra)
    return res_


def _rms_fwd(x, g):
    t, d = x.shape
    tm = _row_tile(t)

    def body(x_ref, g_ref, o_ref):
        x_ = x_ref[...]
        rstd = lax.rsqrt(jnp.mean(x_ * x_, axis=-1, keepdims=True) + EPS)
        o_ref[...] = (x_ * rstd * g_ref[...]).astype(BF16)

    return pl.pallas_call(
        body, name="rms_fwd", grid=(t // tm,),
        in_specs=[pl.BlockSpec((tm, d), lambda i: (i, 0)), pl.BlockSpec((1, d), lambda i: (0, 0))],
        out_specs=pl.BlockSpec((tm, d), lambda i: (i, 0)),
        out_shape=jax.ShapeDtypeStruct((t, d), BF16), compiler_params=_params("parallel"),
    )(x, g.reshape(1, d))


def _rms_bwd(x, g, dy, dres=None, want_bf16=False):
    t, d = x.shape
    tm = _row_tile(t)
    has_res = dres is not None

    def body(*refs):
        x_ref, g_ref, dy_ref = refs[:3]
        r_ref = refs[3] if has_res else None
        outs = refs[3 + has_res:]
        x_ = x_ref[...]
        rstd = lax.rsqrt(jnp.mean(x_ * x_, axis=-1, keepdims=True) + EPS)
        xh = x_ * rstd
        dy_ = dy_ref[...].astype(F32)
        dxh = dy_ * g_ref[...]
        dx = rstd * (dxh - xh * jnp.mean(dxh * xh, axis=-1, keepdims=True))
        if has_res:
            dx = dx + r_ref[...]
        outs[0][...] = dx
        if want_bf16:
            outs[1][...] = dx.astype(BF16)
        dg_ref = outs[-1]

        @pl.when(pl.program_id(0) == 0)
        def _():
            dg_ref[...] = jnp.zeros_like(dg_ref)

        dg_ref[...] += jnp.sum(dy_ * xh, axis=0, keepdims=True)

    row = pl.BlockSpec((tm, d), lambda i: (i, 0))
    vec = pl.BlockSpec((1, d), lambda i: (0, 0))
    out_shape = [jax.ShapeDtypeStruct((t, d), F32)] + ([jax.ShapeDtypeStruct((t, d), BF16)] if want_bf16 else []) \
        + [jax.ShapeDtypeStruct((1, d), F32)]
    out_specs = [row] + ([row] if want_bf16 else []) + [vec]
    return pl.pallas_call(
        body, name="rms_bwd", grid=(t // tm,),
        in_specs=[row, vec, row] + ([row] if has_res else []), out_specs=out_specs, out_shape=out_shape,
        compiler_params=_params("arbitrary"),
    )(x, g.reshape(1, d), dy, *([dres] if has_res else []))


def _swap_rope_halves(y):
    lane = lax.broadcasted_iota(jnp.int32, y.shape, 1)
    half = MLA_ROPE // 2
    swapped = jnp.where(lane < MLA_NOPE + half, pltpu.roll(y, LANES - half, axis=1), pltpu.roll(y, half, axis=1))
    return jnp.where((lane >= MLA_NOPE) & (lane < MLA_QK), swapped, 0.0)


def _head_norm_fwd(x, g, n_valid, scale, rope=None):
    t, hw = x.shape
    w = g.shape[1]
    tm = _row_tile(t)

    def body(*refs):
        x_ref, g_ref = refs[:2]
        o_ref = refs[-1]
        gain = g_ref[...] * scale
        for n in range(hw // w):
            cols = slice(n * w, (n + 1) * w)
            x_ = x_ref[:, cols]
            rstd = lax.rsqrt(jnp.sum(x_ * x_, axis=-1, keepdims=True) * (1.0 / n_valid) + EPS)
            y = x_ * rstd * gain
            if rope is not None:
                y = y * refs[2][...] + _swap_rope_halves(y) * refs[3][...]
            o_ref[:, cols] = y.astype(BF16)

    row = pl.BlockSpec((tm, hw), lambda i: (i, 0))
    tab = pl.BlockSpec((tm, w), lambda i: (i, 0))
    return pl.pallas_call(
        body, name="head_norm_fwd", grid=(t // tm,),
        in_specs=[row, pl.BlockSpec((1, w), lambda i: (0, 0))] + ([tab, tab] if rope is not None else []),
        out_specs=row, out_shape=jax.ShapeDtypeStruct((t, hw), BF16),
        compiler_params=_params("parallel"),
    )(x, g, *(rope if rope is not None else ()))


def _head_norm_bwd(x, g, dout, n_valid, scale, rope=None):
    t, hw = x.shape
    w = g.shape[1]
    tm = _row_tile(t)

    def body(*refs):
        x_ref, g_ref, do_ref = refs[:3]
        dx_ref, dsum_ref, dg_ref = refs[-3:]
        gain = g_ref[...] * scale
        dsum = jnp.zeros((tm, w), F32)
        dg = jnp.zeros((1, w), F32)
        for n in range(hw // w):
            cols = slice(n * w, (n + 1) * w)
            dy = do_ref[:, cols]
            if rope is not None:
                dy = dy * refs[3][...] + _swap_rope_halves(dy * refs[4][...])
            x_ = x_ref[:, cols]
            rstd = lax.rsqrt(jnp.sum(x_ * x_, axis=-1, keepdims=True) * (1.0 / n_valid) + EPS)
            xh = x_ * rstd
            dxh = dy * gain
            dx = rstd * (dxh - xh * (jnp.sum(dxh * xh, axis=-1, keepdims=True) * (1.0 / n_valid)))
            dx_ref[:, cols] = dx.astype(BF16)
            dsum = dsum + dx
            dg = dg + jnp.sum(dy * xh, axis=0, keepdims=True)
        dsum_ref[...] = dsum

        @pl.when(pl.program_id(0) == 0)
        def _():
            dg_ref[...] = jnp.zeros_like(dg_ref)

        dg_ref[...] += scale * dg

    row = pl.BlockSpec((tm, hw), lambda i: (i, 0))
    tab = pl.BlockSpec((tm, w), lambda i: (i, 0))
    vec = pl.BlockSpec((1, w), lambda i: (0, 0))
    return pl.pallas_call(
        body, name="head_norm_bwd", grid=(t // tm,),
        in_specs=[row, vec, row] + ([tab, tab] if rope is not None else []),
        out_specs=[row, tab, vec],
        out_shape=[jax.ShapeDtypeStruct((t, hw), BF16), jax.ShapeDtypeStruct((t, w), F32),
                   jax.ShapeDtypeStruct((1, w), F32)],
        compiler_params=_params("arbitrary"),
    )(x, g, dout, *(rope if rope is not None else ()))


def _tri(n, upper):
    r = lax.broadcasted_iota(jnp.int32, (n, n), 0)
    c = lax.broadcasted_iota(jnp.int32, (n, n), 1)
    return ((r <= c) if upper else (r >= c)).astype(F32)


def _gate_mask(shape, row0):
    lane = lax.broadcasted_iota(jnp.int32, shape, 1)
    row = row0 + lax.broadcasted_iota(jnp.int32, shape, 0)
    return (lane >= TAIL_F) & (lane < TAIL_F + HEADS) & (row >= PAD)


def _gate_fwd(z, bias):
    t = z.shape[0]
    tm = _row_tile(t)
    tail = C_TAIL // LANES

    def body(z_ref, b_ref, o_ref, carry):
        i = pl.program_id(0)

        @pl.when(i == 0)
        def _():
            carry[...] = jnp.zeros_like(carry)

        x_ = z_ref[...] + b_ref[...]
        logf = jnp.minimum(x_, 0.0) - jnp.log1p(jnp.exp(-jnp.abs(x_)))
        logf = jnp.where(_gate_mask(logf.shape, i * tm), logf, 0.0)
        run = carry[...]
        for c in range(tm // BLOCK):
            rows = slice(c * BLOCK, (c + 1) * BLOCK)
            cum = jnp.dot(_tri(BLOCK, False), logf[rows], preferred_element_type=F32,
                          precision=lax.Precision.HIGHEST) + run
            o_ref[rows, :] = cum
            run = cum[BLOCK - 1:BLOCK, :]
        carry[...] = run

    return pl.pallas_call(
        body, name="gate_fwd", grid=(t // tm,),
        in_specs=[pl.BlockSpec((tm, LANES), lambda i: (i, tail)), pl.BlockSpec((1, LANES), lambda i: (0, 0))],
        out_specs=pl.BlockSpec((tm, LANES), lambda i: (i, 0)),
        out_shape=jax.ShapeDtypeStruct((t, LANES), F32),
        scratch_shapes=[pltpu.VMEM((1, LANES), F32)], compiler_params=_params("arbitrary"),
    )(z, bias)


def _gate_bwd(z, bias, dcum):
    t = z.shape[0]
    tm = _row_tile(t)
    nb = t // tm
    tail = C_TAIL // LANES

    def body(z_ref, b_ref, d_ref, o_ref, db_ref, carry):
        i = pl.program_id(0)

        @pl.when(i == 0)
        def _():
            carry[...] = jnp.zeros_like(carry)
            db_ref[...] = jnp.zeros_like(db_ref)

        run = carry[...]
        db = jnp.zeros((1, LANES), F32)
        for c in reversed(range(tm // BLOCK)):
            rows = slice(c * BLOCK, (c + 1) * BLOCK)
            rc = jnp.dot(_tri(BLOCK, True), d_ref[rows, :], preferred_element_type=F32,
                         precision=lax.Precision.HIGHEST) + run
            run = rc[0:1, :]
            x_ = z_ref[rows, :] + b_ref[...]
            sig_neg = 1.0 / (1.0 + jnp.exp(x_))
            dl = jnp.where(_gate_mask(rc.shape, (nb - 1 - i) * tm + c * BLOCK), rc * sig_neg, 0.0)
            o_ref[rows, :] = dl
            db = db + jnp.sum(dl, axis=0, keepdims=True)
        carry[...] = run
        db_ref[...] += db

    return pl.pallas_call(
        body, name="gate_bwd", grid=(nb,),
        in_specs=[pl.BlockSpec((tm, LANES), lambda i: (nb - 1 - i, tail)), pl.BlockSpec((1, LANES), lambda i: (0, 0)),
                  pl.BlockSpec((tm, LANES), lambda i: (nb - 1 - i, 0))],
        out_specs=[pl.BlockSpec((tm, LANES), lambda i: (nb - 1 - i, 0)), pl.BlockSpec((1, LANES), lambda i: (0, 0))],
        out_shape=[jax.ShapeDtypeStruct((t, LANES), F32), jax.ShapeDtypeStruct((1, LANES), F32)],
        scratch_shapes=[pltpu.VMEM((1, LANES), F32)], compiler_params=_params("arbitrary"),
    )(z, bias, dcum)


def _pairs(nb, by_query):
    if by_query:
        pr = [(i, j) for i in range(nb) for j in range(i + 1)]
    else:
        pr = [(i, j) for j in range(nb) for i in range(j, nb)]
    return (jnp.asarray(np.array([p[0] for p in pr], np.int32)),
            jnp.asarray(np.array([p[1] for p in pr], np.int32)))


HEADS_PER_STEP = 8


def _mask_scores(s, i, j, tile):
    qp = i * tile + lax.broadcasted_iota(jnp.int32, s.shape, 0)
    kp = j * tile + lax.broadcasted_iota(jnp.int32, s.shape, 1)
    return jnp.where((kp <= qp) & (kp >= PAD), s, NEG)


def _pipelined(n, front, back):
    nxt = front(0)
    for h in range(n):
        cur = nxt
        if h + 1 < n:
            nxt = front(h + 1)
        back(h, cur)


def _nt_dot(a, b):
    return lax.dot_general(a, b, (((1,), (1,)), ((), ())), preferred_element_type=F32)


def _tn_dot(a, b):
    return lax.dot_general(a, b, (((0,), (0,)), ((), ())), preferred_element_type=F32)


def _attn_specs(hb, tile, dk, dv):
    q_of = lambda g, p, it, jt: (it[p], g)
    k_of = lambda g, p, it, jt: (jt[p], g)
    return dict(
        q=pl.BlockSpec((tile, hb * dk), q_of), k=pl.BlockSpec((tile, hb * dk), k_of),
        v=pl.BlockSpec((tile, hb * dv), k_of), ov=pl.BlockSpec((tile, hb * dv), q_of),
        fr=pl.BlockSpec((hb, 1, tile), lambda g, p, it, jt: (g, 0, jt[p])),
        rowq=pl.BlockSpec((hb, 1, tile), lambda g, p, it, jt: (g, 0, it[p])))


def _head(ref, n, d):
    return ref[:, n * d:(n + 1) * d]


def _attn_fwd(q, k, v, key_bias=None, gather=()):
    h = HEADS
    t, dk, dv = q.shape[0], q.shape[1] // h, v.shape[1] // h
    tile = _row_tile(t)
    nb = t // tile
    hb = HEADS_PER_STEP
    biased = key_bias is not None
    it, jt = _pairs(nb, True)
    sp = _attn_specs(hb, tile, dk, dv)

    n_in = 4 if biased else 3
    n_g = len(gather)
    g_in, g_out, g_shapes, g_sems = _comm_parts(gather, True)
    n_steps = int(it.shape[0])

    def body(it_ref, jt_ref, *refs):
        q_ref, k_ref, v_ref = refs[:3]
        b_ref = refs[3] if biased else None
        o_ref, lse_ref = refs[n_in + 2 * n_g:n_in + 2 * n_g + 2]
        m_sc, l_sc, acc_sc = refs[n_in + 2 * n_g + 2:n_in + 2 * n_g + 5]
        p = pl.program_id(1)
        i, j = it_ref[p], jt_ref[p]
        if n_g:
            g_start, g_finish = _gather_ops(refs[n_in:n_in + n_g], refs[n_in + n_g:n_in + 2 * n_g],
                                            *refs[n_in + 2 * n_g + 5:])
            first = (pl.program_id(0) == 0) & (p == 0)
            last = (pl.program_id(0) == h // hb - 1) & (p == n_steps - 1)
            pl.when(first)(g_start)

        @pl.when(j == 0)
        def _():
            m_sc[...] = jnp.full_like(m_sc, NEG)
            l_sc[...] = jnp.zeros_like(l_sc)
            acc_sc[...] = jnp.zeros_like(acc_sc)

        def step(masked):
            def front(n):
                return _nt_dot(_head(q_ref, n, dk), _head(k_ref, n, dk))

            def back(n, s):
                if biased:
                    s = s + b_ref[n]
                if masked:
                    s = _mask_scores(s, i, j, tile)
                m_prev = m_sc[n]
                m_new = jnp.maximum(m_prev, jnp.max(s, axis=-1, keepdims=True))
                alpha = jnp.exp2(m_prev - m_new)
                e = jnp.exp2(s - m_new)
                l_sc[n] = alpha * l_sc[n] + jnp.sum(e, axis=-1, keepdims=True)
                acc_sc[n] = alpha * acc_sc[n] + jnp.dot(e.astype(BF16), _head(v_ref, n, dv),
                                                        preferred_element_type=F32)
                m_sc[n] = m_new

            _pipelined(hb, front, back)

        edge = (j == i) | (j == 0)
        pl.when(edge)(lambda: step(True))
        pl.when(jnp.logical_not(edge))(lambda: step(False))

        @pl.when(j == i)
        def _():
            row = i * tile + lax.broadcasted_iota(jnp.int32, (tile, 1), 0)
            for n in range(hb):
                o_ref[:, n * dv:(n + 1) * dv] = jnp.where(row >= PAD, acc_sc[n] / l_sc[n], 0.0)
                lse_ref[n] = jnp.transpose(m_sc[n] + jnp.log2(l_sc[n]))

        if n_g:
            pl.when(last)(g_finish)

    grid_spec = pltpu.PrefetchScalarGridSpec(
        num_scalar_prefetch=2, grid=(h // hb, n_steps),
        in_specs=[sp["q"], sp["k"], sp["v"]] + ([sp["fr"]] if biased else []) + g_in,
        out_specs=g_out + [sp["ov"], sp["rowq"]],
        scratch_shapes=[pltpu.VMEM((hb, tile, 1), F32), pltpu.VMEM((hb, tile, 1), F32),
                        pltpu.VMEM((hb, tile, dv), F32)] + (g_sems if n_g else []))
    outs = pl.pallas_call(
        body, name="attn_fwd_gather" if n_g else "attn_fwd", grid_spec=grid_spec,
        out_shape=g_shapes + [jax.ShapeDtypeStruct((t, h * dv), F32), jax.ShapeDtypeStruct((h, 1, t), F32)],
        compiler_params=_params("arbitrary" if n_g else "parallel", "arbitrary"),
    )(it, jt, q, k, v, *((key_bias,) if biased else ()), *gather)
    return (outs[n_g], outs[n_g + 1], outs[:n_g]) if n_g else tuple(outs)


LOG2E = 1.4426950408889634
LN2 = 0.6931471805599453
MLA_SCALE = MLA_QK ** -0.5 * LOG2E
FOX_SCALE = FOX_DIM ** -0.5 * LOG2E


def _attn_delta(o, do):
    h = HEADS
    t, dv = o.shape[0], o.shape[1] // h
    tm = _row_tile(t)

    def body(o_ref, do_ref, d_ref):
        prod = o_ref[...] * do_ref[...]
        for n in range(h):
            d_ref[n] = jnp.transpose(jnp.sum(prod[:, n * dv:(n + 1) * dv], axis=-1, keepdims=True) * LN2)

    blk = pl.BlockSpec((tm, h * dv), lambda i: (i, 0))
    return pl.pallas_call(
        body, name="attn_delta", grid=(t // tm,), in_specs=[blk, blk],
        out_specs=pl.BlockSpec((h, 1, tm), lambda i: (0, 0, i)),
        out_shape=jax.ShapeDtypeStruct((h, 1, t), F32), compiler_params=_params("parallel"),
    )(o, do)


BWD_HEADS_PER_STEP = (4, 4)


def _attn_bwd(q, k, v, do, lse_row, delta_row, key_bias=None, exchange=()):
    h = HEADS
    t, dk, dv = q.shape[0], q.shape[1] // h, v.shape[1] // h
    tile = _row_tile(t)
    nb = t // tile
    decay = key_bias is not None
    hb = BWD_HEADS_PER_STEP[int(decay)]
    it, jt = _pairs(nb, False)
    sp = _attn_specs(hb, tile, dk, dv)
    n_in = 7 if decay else 6
    n_out = 5 if decay else 3
    n_x = len(exchange)
    x_in, x_out, x_shapes, x_sems = _comm_parts(exchange, False)
    n_steps = int(it.shape[0])

    def body(it_ref, jt_ref, *refs):
        q_ref, k_ref, v_ref, do_ref, lse_ref, delta_ref = refs[:6]
        b_ref = refs[6] if decay else None
        outs = refs[n_in + n_x:]
        dq_ref, dk_ref, dv_ref = outs[:3]
        rs_ref, ks_ref = (outs[3], outs[4]) if decay else (None, None)
        scratch = outs[n_out + n_x:]
        dq_sc, dk_sc, dv_sc = scratch[:3]
        ks_sc, b_sc = (scratch[3], scratch[4]) if decay else (None, None)
        p = pl.program_id(1)
        i, j = it_ref[p], jt_ref[p]
        if n_x:
            x_start, x_finish = _exchange_ops(refs[n_in:n_in + n_x], outs[n_out:n_out + n_x],
                                              *scratch[5 if decay else 3:])
            first = (pl.program_id(0) == 0) & (p == 0)
            last = (pl.program_id(0) == h // hb - 1) & (p == n_steps - 1)
            pl.when(first)(x_start)

        @pl.when(p == 0)
        def _():
            dq_sc[...] = jnp.zeros_like(dq_sc)
            if decay:
                rs_ref[...] = jnp.zeros_like(rs_ref)

        @pl.when(i == j)
        def _():
            dk_sc[...] = jnp.zeros_like(dk_sc)
            dv_sc[...] = jnp.zeros_like(dv_sc)
            if decay:
                ks_sc[...] = jnp.zeros_like(ks_sc)
                for n in range(hb):
                    b_sc[n] = jnp.transpose(b_ref[n])

        def step(masked):
            def front(n):
                return (_nt_dot(_head(k_ref, n, dk), _head(q_ref, n, dk)),
                        _nt_dot(_head(v_ref, n, dv), (_head(do_ref, n, dv) * LN2).astype(BF16)))

            def back(n, s_dp):
                s, dp = s_dp
                if decay:
                    s = s + b_sc[n]
                if masked:
                    kp = j * tile + lax.broadcasted_iota(jnp.int32, s.shape, 0)
                    qp = i * tile + lax.broadcasted_iota(jnp.int32, s.shape, 1)
                    s = jnp.where((kp <= qp) & (kp >= PAD), s, NEG)
                pr = jnp.exp2(s - lse_ref[n])
                ds = pr * (dp - delta_ref[n])
                ds_b = ds.astype(BF16)
                dv_sc[n] += jnp.dot(pr.astype(BF16), _head(do_ref, n, dv).astype(BF16), preferred_element_type=F32)
                dk_sc[n] += jnp.dot(ds_b, _head(q_ref, n, dk), preferred_element_type=F32)
                dq_sc[n, i] += _tn_dot(ds_b, _head(k_ref, n, dk))
                if decay:
                    rs_ref[n, i] += jnp.sum(ds, axis=0, keepdims=True)
                    ks_sc[n] += jnp.sum(ds, axis=-1, keepdims=True)

            _pipelined(hb, front, back)

        edge = (j == i) | (j == 0)
        pl.when(edge)(lambda: step(True))
        pl.when(jnp.logical_not(edge))(lambda: step(False))

        @pl.when(i == j)
        def _():
            for n in range(hb):
                dq_ref[:, n * dk:(n + 1) * dk] = dq_sc[n, j]

        @pl.when(i == nb - 1)
        def _():
            for n in range(hb):
                dk_ref[:, n * dk:(n + 1) * dk] = dk_sc[n]
                dv_ref[:, n * dv:(n + 1) * dv] = dv_sc[n]
                if decay:
                    ks_ref[n] = jnp.transpose(ks_sc[n])

        if n_x:
            pl.when(last)(x_finish)

    rows_out = pl.BlockSpec((hb, nb, 1, tile), lambda hh, p, it, jt: (hh, 0, 0, 0))
    grid_spec = pltpu.PrefetchScalarGridSpec(
        num_scalar_prefetch=2, grid=(h // hb, n_steps),
        in_specs=[sp["q"], sp["k"], sp["v"], sp["ov"], sp["rowq"], sp["rowq"]] + ([sp["fr"]] if decay else [])
        + x_in,
        out_specs=[sp["k"], sp["k"], sp["v"]] + ([rows_out, sp["fr"]] if decay else []) + x_out,
        scratch_shapes=[pltpu.VMEM((hb, nb, tile, dk), F32), pltpu.VMEM((hb, tile, dk), F32),
                        pltpu.VMEM((hb, tile, dv), F32)] + ([pltpu.VMEM((hb, tile, 1), F32)] * 2 if decay else [])
        + (x_sems if n_x else []))
    out_shape = [jax.ShapeDtypeStruct((t, h * dk), F32), jax.ShapeDtypeStruct((t, h * dk), F32),
                 jax.ShapeDtypeStruct((t, h * dv), F32)] \
        + ([jax.ShapeDtypeStruct((h, nb, 1, tile), F32), jax.ShapeDtypeStruct((h, 1, t), F32)] if decay else []) \
        + x_shapes
    outs = pl.pallas_call(
        body, name="attn_bwd_exchange" if n_x else "attn_bwd", grid_spec=grid_spec, out_shape=out_shape,
        compiler_params=_params("arbitrary" if n_x else "parallel", "arbitrary"),
    )(it, jt, q, k, v, do, lse_row, delta_row, *((key_bias,) if decay else ()), *exchange)
    return tuple(outs[:n_out]) + ((list(outs[n_out:]),) if n_x else ())


CONV_COLS = 512


def _shift_down(g, prev, n):
    out = pltpu.roll(g, n, axis=0)
    row = lax.broadcasted_iota(jnp.int32, g.shape, 0)
    for r in range(n):
        out = jnp.where(row == r, prev[8 - n + r:8 - n + r + 1, :], out)
    return out


def _shift_up(g, nxt, n):
    tm = g.shape[0]
    out = pltpu.roll(g, tm - n, axis=0)
    row = lax.broadcasted_iota(jnp.int32, g.shape, 0)
    for r in range(n):
        out = jnp.where(row == tm - n + r, nxt[r:r + 1, :], out)
    return out


def _conv_fwd(z3, w):
    _, t, d = z3.shape
    tm = _row_tile(t)
    tc = CONV_COLS

    def body(z_ref, zp_ref, w_ref, o_ref):
        i = pl.program_id(1)
        g = z_ref[1] * z_ref[2]
        gp = jnp.where(i > 0, zp_ref[1] * zp_ref[2], 0.0)
        w_ = w_ref[...]
        y = w_[2:3] * g + w_[1:2] * _shift_down(g, gp, 1) + w_[0:1] * _shift_down(g, gp, 2)
        o_ref[...] = (z_ref[0] * y).astype(BF16)

    return pl.pallas_call(
        body, name="conv_fwd", grid=(d // tc, t // tm),
        in_specs=[pl.BlockSpec((3, tm, tc), lambda j, i: (0, i, j)),
                  pl.BlockSpec((3, 8, tc), lambda j, i: (0, jnp.maximum(i * (tm // 8) - 1, 0), j)),
                  pl.BlockSpec((3, tc), lambda j, i: (0, j))],
        out_specs=pl.BlockSpec((tm, tc), lambda j, i: (i, j)),
        out_shape=jax.ShapeDtypeStruct((t, d), BF16), compiler_params=_params("parallel", "parallel"),
    )(z3, z3, w)


def _conv_bwd(z3, w, dyb):
    _, t, d = z3.shape
    tm = _row_tile(t)
    tc = CONV_COLS
    ni = t // tm

    def body(z_ref, zp_ref, zn_ref, d_ref, dn_ref, w_ref, dz_ref, dw_ref):
        i = pl.program_id(1)
        gb, gc, u = z_ref[0], z_ref[1], z_ref[2]
        g = gc * u
        gp = jnp.where(i > 0, zp_ref[1] * zp_ref[2], 0.0)
        w_ = w_ref[...]
        g1, g2 = _shift_down(g, gp, 1), _shift_down(g, gp, 2)
        y = w_[2:3] * g + w_[1:2] * g1 + w_[0:1] * g2
        dyb_ = d_ref[...]
        dy = dyb_ * gb
        dyn = jnp.where(i < ni - 1, dn_ref[...] * zn_ref[0], 0.0)
        dg = w_[2:3] * dy + w_[1:2] * _shift_up(dy, dyn, 1) + w_[0:1] * _shift_up(dy, dyn, 2)
        dz_ref[0] = (dyb_ * y).astype(BF16)
        dz_ref[1] = (dg * u).astype(BF16)
        dz_ref[2] = (dg * gc).astype(BF16)

        @pl.when(i == 0)
        def _():
            dw_ref[...] = jnp.zeros_like(dw_ref)

        dw_ref[...] += jnp.concatenate([jnp.sum(dy * g2, axis=0, keepdims=True),
                                        jnp.sum(dy * g1, axis=0, keepdims=True),
                                        jnp.sum(dy * g, axis=0, keepdims=True)], axis=0)

    cur = pl.BlockSpec((3, tm, tc), lambda j, i: (0, i, j))
    return pl.pallas_call(
        body, name="conv_bwd", grid=(d // tc, ni),
        in_specs=[cur,
                  pl.BlockSpec((3, 8, tc), lambda j, i: (0, jnp.maximum(i * (tm // 8) - 1, 0), j)),
                  pl.BlockSpec((3, 8, tc), lambda j, i: (0, jnp.minimum((i + 1) * (tm // 8), t // 8 - 1), j)),
                  pl.BlockSpec((tm, tc), lambda j, i: (i, j)),
                  pl.BlockSpec((8, tc), lambda j, i: (jnp.minimum((i + 1) * (tm // 8), t // 8 - 1), j)),
                  pl.BlockSpec((3, tc), lambda j, i: (0, j))],
        out_specs=[cur, pl.BlockSpec((3, tc), lambda j, i: (0, j))],
        out_shape=[jax.ShapeDtypeStruct((3, t, d), BF16), jax.ShapeDtypeStruct((3, d), F32)],
        compiler_params=_params("parallel", "arbitrary"),
    )(z3, z3, z3, dyb, dyb, w)


def _loss_head(h, target):
    t, d = h.shape
    tm = BLOCK

    def body(h_ref, t_ref, dh_ref, dhb_ref, loss_ref):
        i = pl.program_id(0)

        @pl.when(i == 0)
        def _():
            loss_ref[...] = jnp.zeros_like(loss_ref)

        err = jnp.where(i > 0, h_ref[...] - t_ref[...], 0.0)
        dh = err * (1.0 / d)
        dh_ref[...] = dh
        dhb_ref[...] = dh.astype(BF16)
        loss_ref[...] += 0.5 * jnp.sum(jnp.sum(err * err, axis=-1, keepdims=True) * (1.0 / d), axis=0, keepdims=True)

    row = pl.BlockSpec((tm, d), lambda i: (i, 0))
    return pl.pallas_call(
        body, name="loss_head", grid=(t // tm,),
        in_specs=[row, pl.BlockSpec((tm, d), lambda i: (jnp.maximum(i - 1, 0), 0))],
        out_specs=[row, row, pl.BlockSpec((1, 1), lambda i: (0, 0))],
        out_shape=[jax.ShapeDtypeStruct((t, d), F32), jax.ShapeDtypeStruct((t, d), BF16),
                   jax.ShapeDtypeStruct((1, 1), F32)],
        compiler_params=_params("arbitrary"),
    )(h, target)


def _rope_tables(t):
    pos = jnp.arange(t, dtype=F32) - PAD
    inv_freq = ROPE_BASE ** (-jnp.arange(0, MLA_ROPE, 2, dtype=F32) / MLA_ROPE)
    ang = pos[:, None] * inv_freq[None, :]
    cos, sin = jnp.cos(ang), jnp.sin(ang)
    one, zero = jnp.ones((t, MLA_NOPE), F32), jnp.zeros((t, MLA_NOPE), F32)
    tail = jnp.zeros((t, LANES - MLA_QK), F32)
    return (jnp.concatenate([one, cos, cos, tail], axis=1), jnp.concatenate([zero, -sin, sin, tail], axis=1))


def _pad_lanes(x, width=LANES):
    return jnp.pad(x, [(0, 0)] * (x.ndim - 1) + [(0, width - x.shape[-1])])


def _permute_in_attn(w):
    return jnp.concatenate([w[:, :640], w[:, 672:2208], w[:, 640:672], w[:, 2208:2216],
                            jnp.zeros((w.shape[0], ATTN_IN_PAD - 2216), w.dtype)], axis=1)


def _unpermute_in_attn(dw):
    return jnp.concatenate([dw[:, :640], dw[:, 2176:2208], dw[:, 640:2176], dw[:, 2208:2216]], axis=1)


def _attn_layer_fwd(hn, wl, rope, gather=()):
    t = hn.shape[0]
    z = _mm(hn, wl["w_in"], name="attn_in")
    cqn = _rms_fwd(z[:, C_CQ:C_CQ + Q_LORA], wl["g_cq"])
    ckvn = _rms_fwd(z[:, C_CKV:C_CKV + KV_LORA], wl["g_ckv"])
    w_uq = _pad_lanes(wl["w_uq"].reshape(Q_LORA, HEADS, MLA_QK)).reshape(Q_LORA, HEADS * LANES)
    xq = _mm(cqn, w_uq, name="mla_uq")
    kvf = _mm(ckvn, wl["w_ukv"], name="mla_ukv")
    kv3 = kvf.reshape(t, HEADS, MLA_NOPE + MLA_V)
    k_pe = jnp.broadcast_to(z[:, None, C_TAIL:C_TAIL + MLA_ROPE], (t, HEADS, MLA_ROPE))
    xk = _pad_lanes(jnp.concatenate([kv3[:, :, :MLA_NOPE], k_pe], axis=-1)).reshape(t, HEADS * LANES)
    v_mla = kv3[:, :, MLA_NOPE:].reshape(t, HEADS * MLA_V).astype(BF16)
    gq, gk = _pad_lanes(wl["g_q_mla"].reshape(1, -1)), _pad_lanes(wl["g_k_mla"].reshape(1, -1))
    q_mla = _head_norm_fwd(xq, gq, MLA_QK, MLA_SCALE, rope)
    k_mla = _head_norm_fwd(xk, gk, MLA_QK, 1.0, rope)
    o_mla, lse_mla, *gathered = _attn_fwd(q_mla, k_mla, v_mla, gather=gather)
    xfq = z[:, C_FQ:C_FQ + HEADS * FOX_DIM]
    xfk = z[:, C_FK:C_FK + HEADS * FOX_DIM]
    v_fox = z[:, C_FV:C_FV + HEADS * FOX_DIM].astype(BF16)
    bias = jnp.pad(wl["b_forget"].reshape(1, -1), ((0, 0), (TAIL_F, LANES - TAIL_F - HEADS)))
    cum = _gate_fwd(z, bias)
    neg_f = (-LOG2E * jnp.transpose(cum[:, TAIL_F:TAIL_F + HEADS]))[:, None, :]
    q_fox = _head_norm_fwd(xfq, wl["g_q_fox"].reshape(1, -1), FOX_DIM, FOX_SCALE)
    k_fox = _head_norm_fwd(xfk, wl["g_k_fox"].reshape(1, -1), FOX_DIM, 1.0)
    o_fox, lse_fox = _attn_fwd(q_fox, k_fox, v_fox, neg_f)
    cat = jnp.concatenate([o_mla, o_fox], axis=1).astype(BF16)
    saved = dict(z=z, cqn=cqn, ckvn=ckvn, xq=xq, xk=xk, v_mla=v_mla, q_mla=q_mla, k_mla=k_mla, o_mla=o_mla,
                 lse_mla=lse_mla, xfq=xfq, xfk=xfk, v_fox=v_fox, q_fox=q_fox, k_fox=k_fox, bias=bias,
                 neg_f=neg_f, o_fox=o_fox, lse_fox=lse_fox, cat=cat, gq=gq, gk=gk, w_uq=w_uq)
    return cat, saved, (gathered[0] if gathered else ())


def _attn_layer_bwd(dcat, hn, wl, sv, rope, exchange=()):
    t = hn.shape[0]
    g = {}
    do_mla = dcat[:, :HEADS * MLA_V]
    do_fox = dcat[:, HEADS * MLA_V:]
    qkv = (sv["q_fox"], sv["k_fox"], sv["v_fox"])
    delta = _attn_delta(sv["o_fox"], do_fox)
    dq_fox, dk_fox, dv_fox, row_sums, key_sums, *exchanged = _attn_bwd(
        *qkv, do_fox, sv["lse_fox"], delta, sv["neg_f"], exchange=exchange)
    dcum = jnp.pad(jnp.transpose(LOG2E * (row_sums.reshape(HEADS, t) - key_sums.reshape(HEADS, t))),
                   ((0, 0), (TAIL_F, LANES - TAIL_F - HEADS)))
    dtail_f, dbias = _gate_bwd(sv["z"], sv["bias"], dcum)
    g["b_forget"] = dbias[0, TAIL_F:TAIL_F + HEADS]
    dxfq, _, dgq = _head_norm_bwd(sv["xfq"], wl["g_q_fox"].reshape(1, -1), dq_fox, FOX_DIM, FOX_SCALE)
    dxfk, _, dgk = _head_norm_bwd(sv["xfk"], wl["g_k_fox"].reshape(1, -1), dk_fox, FOX_DIM, 1.0)
    g["g_q_fox"], g["g_k_fox"] = dgq[0], dgk[0]
    qkv = (sv["q_mla"], sv["k_mla"], sv["v_mla"])
    delta = _attn_delta(sv["o_mla"], do_mla)
    dq_mla, dk_mla, dv_mla = _attn_bwd(*qkv, do_mla, sv["lse_mla"], delta)
    dxq, _, dgq = _head_norm_bwd(sv["xq"], sv["gq"], dq_mla, MLA_QK, MLA_SCALE, rope)
    dxk, dxk_sum, dgk = _head_norm_bwd(sv["xk"], sv["gk"], dk_mla, MLA_QK, 1.0, rope)
    g["g_q_mla"], g["g_k_mla"] = dgq[0, :MLA_QK], dgk[0, :MLA_QK]
    dqf = dxq.astype(BF16)
    dkvf = jnp.concatenate([dxk.reshape(t, HEADS, LANES)[:, :, :MLA_NOPE], dv_mla.reshape(t, HEADS, MLA_V)],
                           axis=-1).reshape(t, HEADS * (MLA_NOPE + MLA_V)).astype(BF16)
    g["w_uq"] = _mm(sv["cqn"], dqf, ta=True, name="d_w_uq").reshape(Q_LORA, HEADS, LANES)[:, :, :MLA_QK].reshape(
        Q_LORA, HEADS * MLA_QK)
    g["w_ukv"] = _mm(sv["ckvn"], dkvf, ta=True, name="d_w_ukv")
    dcqn = _mm(dqf, sv["w_uq"], tb=True, name="d_cqn")
    dckvn = _mm(dkvf, wl["w_ukv"], tb=True, name="d_ckvn")
    z = sv["z"]
    dcq, dg_cq = _rms_bwd(z[:, C_CQ:C_CQ + Q_LORA], wl["g_cq"], dcqn)
    dckv, dg_ckv = _rms_bwd(z[:, C_CKV:C_CKV + KV_LORA], wl["g_ckv"], dckvn)
    g["g_cq"], g["g_ckv"] = dg_cq[0], dg_ckv[0]
    tail = jnp.concatenate([dxk_sum[:, MLA_NOPE:MLA_QK], dtail_f[:, TAIL_F:]], axis=1)
    dz = jnp.concatenate([dcq, dckv, dxfq, dxfk, dv_fox, tail], axis=1).astype(BF16)
    g["w_in"] = _mm(hn, dz, ta=True, name="d_w_in_attn")
    dhn = _mm(dz, wl["w_in"], tb=True, out_dtype=BF16, name="d_hn_attn")
    return dhn, g, (exchanged[0] if exchanged else ())


def _local_step(x, target, w, gather_late=None, exchange_early=None):
    seq = x.shape[0]
    t = seq + BLOCK
    rope = _rope_tables(t)
    h = jnp.concatenate([jnp.zeros((PAD, D_MODEL), F32), w["meta_tokens"], x], axis=0)
    tape = []
    hn = _rms_fwd(h, w["g_mix"][0])
    for layer in range(DEPTH):
        j = layer // 2
        if layer % 2 == 0:
            wl = dict(w_in=w["w_in_attn"][j], g_cq=w["g_cq"][j], w_uq=w["w_uq"][j], g_ckv=w["g_ckv"][j],
                      w_ukv=w["w_ukv"][j], g_q_mla=w["g_q_mla"][j], g_k_mla=w["g_k_mla"][j],
                      g_q_fox=w["g_q_fox"][j], g_k_fox=w["g_k_fox"][j], b_forget=w["b_forget"][j])
            hosted = gather_late is not None and layer == 0
            mixed, sv, gathered = _attn_layer_fwd(hn, wl, rope, gather_late[0] if hosted else ())
            if hosted:
                gather_late[1](w, gathered)
            h1, hn2 = _mm(mixed, w["w_out_attn"][j], res=h, norm_gain=w["g_mlp"][layer], name="attn_out")
        else:
            wl = None
            z3 = _mm(hn, w["w_in_conv"][j], out_seg=3, name="conv_in")
            mixed = _conv_fwd(z3, w["conv_w"][j])
            sv = dict(z3=z3)
            h1, hn2 = _mm(mixed, w["w_out_conv"][j], res=h, norm_gain=w["g_mlp"][layer], name="conv_out")
        u, act = _mm(hn2, w["w_mlp_up"][layer], epi="relu2", name="mlp_up")
        tape.append(dict(h=h, hn=hn, wl=wl, sv=sv, mixed=mixed, h1=h1, hn2=hn2, u=u, act=act))
        if layer + 1 < DEPTH:
            h, hn = _mm(act, w["w_mlp_down"][layer], res=h1, norm_gain=w["g_mix"][layer + 1], name="mlp_down")
        else:
            h = _mm(act, w["w_mlp_down"][layer], res=h1, name="mlp_down")

    dh, dh_b, loss = _loss_head(h, target)
    exchanged = ()
    g = {n: [None] * (DEPTH if n in ("g_mix", "g_mlp", "w_mlp_up", "w_mlp_down") else DEPTH // 2)
         for n in WEIGHTS if n != "meta_tokens"}
    for layer in reversed(range(DEPTH)):
        j = layer // 2
        tp = tape[layer]
        g["w_mlp_down"][layer] = _mm(tp["act"], dh_b, ta=True, name="d_w_down")
        du = _mm(dh_b, w["w_mlp_down"][layer], tb=True, epi="relu2_bwd", aux=tp["u"], out_dtype=BF16, name="d_u")
        g["w_mlp_up"][layer] = _mm(tp["hn2"], du, ta=True, name="d_w_up")
        dhn2 = _mm(du, w["w_mlp_up"][layer], tb=True, out_dtype=BF16, name="d_hn2")
        dh1, dh1_b, dg = _rms_bwd(tp["h1"], w["g_mlp"][layer], dhn2, dres=dh, want_bf16=True)
        g["g_mlp"][layer] = dg[0]
        if layer % 2 == 0:
            g["w_out_attn"][j] = _mm(tp["mixed"], dh1_b, ta=True, name="d_w_out_attn")
            dcat = _mm(dh1_b, w["w_out_attn"][j], tb=True, name="d_cat")
            hosted = exchange_early is not None and layer == 0
            dhn, gl, got = _attn_layer_bwd(dcat, tp["hn"], tp["wl"], tp["sv"], rope,
                                           exchange_early(g) if hosted else ())
            if hosted:
                exchanged = got
            g["w_in_attn"][j] = _unpermute_in_attn(gl.pop("w_in"))
            for n, val in gl.items():
                g[n][j] = val
        else:
            g["w_out_conv"][j] = _mm(tp["mixed"], dh1_b, ta=True, name="d_w_out_conv")
            dyb = _mm(dh1_b, w["w_out_conv"][j], tb=True, name="d_yb")
            dz3, dcw = _conv_bwd(tp["sv"]["z3"], w["conv_w"][j], dyb)
            g["conv_w"][j] = dcw
            g["w_in_conv"][j] = _mm(tp["hn"], dz3, ta=True, name="d_w_in_conv")
            dhn = _mm(dz3, w["w_in_conv"][j], tb=True, out_dtype=BF16, name="d_hn_conv")
        dh, dh_b, dg = _rms_bwd(tp["h"], w["g_mix"][layer], dhn, dres=dh1, want_bf16=True)
        g["g_mix"][layer] = dg[0]
    g["meta_tokens"] = [dh[PAD:BLOCK]]
    return loss, dh[BLOCK:], g, exchanged


COMM_ROWS = 2048
ADAMW_BLOCK_BYTES = 1 << 20


def kernel(x, meta_tokens, g_mix, g_mlp, w_in_attn, g_cq, w_uq, g_ckv, w_ukv, g_q_mla, g_k_mla, g_q_fox, g_k_fox, b_forget, w_out_attn, w_in_conv, conv_w, w_out_conv, w_mlp_up, w_mlp_down, loss_target, m_meta_tokens, m_g_mix, m_g_mlp, m_w_in_attn, m_g_cq, m_w_uq, m_g_ckv, m_w_ukv, m_g_q_mla, m_g_k_mla, m_g_q_fox, m_g_k_fox, m_b_forget, m_w_out_attn, m_w_in_conv, m_conv_w, m_w_out_conv, m_w_mlp_up, m_w_mlp_down, v_meta_tokens, v_g_mix, v_g_mlp, v_w_in_attn, v_g_cq, v_w_uq, v_g_ckv, v_w_ukv, v_g_q_mla, v_g_k_mla, v_g_q_fox, v_g_k_fox, v_b_forget, v_w_out_attn, v_w_in_conv, v_conv_w, v_w_out_conv, v_w_mlp_up, v_w_mlp_down):
    args = dict(locals())
    local = {n: args[n] for n in WEIGHTS}
    mom = {n: args["m_" + n] for n in WEIGHTS}
    var = {n: args["v_" + n] for n in WEIGHTS}
    axis = dict(SHARDED)
    count = {n: local[n].shape[0] for n, _ in SHARDED if n != "meta_tokens"}
    piece = lambda src, p: src[p[0]] if p[1] is None else src[p[0]][p[1]]
    piece_axis = lambda p: axis[p[0]] - (0 if p[1] is None else 1)
    shape_of = lambda p: piece(local, p).shape
    layers = lambda n, ls: [(n, l) for l in ls]
    first_bf = [("w_uq", 0), ("w_ukv", 0)]
    first_f32 = [("meta_tokens", None), ("conv_w", 0), ("conv_w", 1)]
    late_bf = ([("w_uq", 1), ("w_ukv", 1)] + layers("w_out_attn", (0, 1)) + layers("w_in_conv", (0, 1))
               + layers("w_out_conv", (0, 1)) + layers("w_mlp_up", range(DEPTH)) + layers("w_mlp_down", range(DEPTH)))
    early_bf = ([("w_uq", 1), ("w_ukv", 1)] + layers("w_out_attn", (0, 1)) + layers("w_in_conv", (0, 1))
                + layers("conv_w", (0, 1)) + layers("w_out_conv", (0, 1)) + layers("w_mlp_up", range(DEPTH))
                + layers("w_mlp_down", range(DEPTH)))
    last_bf = [("w_uq", 0), ("w_ukv", 0), ("meta_tokens", None)]

    w = {n: local[n] for n in REPLICATED}
    w.update({n: [None] * c for n, c in count.items()})

    def install(w, pieces, gathered, in_layer, gathered_in):
        for p, blocks in zip(pieces, _unpack(gathered, [shape_of(p) for p in pieces], (N_DEV,))):
            value = _from_shards(blocks, piece_axis(p))
            if p[1] is None:
                w[p[0]] = value
            else:
                w[p[0]][p[1]] = value
        w["w_in_attn"][in_layer] = _permute_in_attn(_from_shards(gathered_in, piece_axis(("w_in_attn", 0))))

    got_bf, got_in, got_f32 = _all_gather([
        _pack([piece(local, p) for p in first_bf], 16, BF16), local["w_in_attn"][0].astype(BF16),
        _pack([piece(local, p) for p in first_f32], 8, F32)])
    install(w, first_bf, got_bf, 0, got_in)
    for p, blocks in zip(first_f32, _unpack(got_f32, [shape_of(p) for p in first_f32], (N_DEV,))):
        if p[1] is None:
            w[p[0]] = _from_shards(blocks, piece_axis(p))
        else:
            w[p[0]][p[1]] = _from_shards(blocks, piece_axis(p))
    gather_late = ([_pack([piece(local, p) for p in late_bf], 16, BF16), local["w_in_attn"][1].astype(BF16)],
                   lambda w_, got: install(w_, late_bf, got[0], 1, got[1]))

    def pack_grads(g, pieces, in_layer):
        sent = _pack_rows([_to_shards(piece(g, p), piece_axis(p)) for p in pieces], COMM_ROWS, BF16)
        return [sent, _to_shards(g["w_in_attn"][in_layer], piece_axis(("w_in_attn", 0))).astype(BF16)]

    loss_part, dx, grads, early = _local_step(x[0], loss_target[0], w, gather_late,
                                              lambda g: pack_grads(g, early_bf, 1))

    grads["meta_tokens"] = grads["meta_tokens"][0]
    last, (rep,) = _exchange_and_gather(
        pack_grads(grads, last_bf, 0), [_pack([jnp.stack(grads[n]) for n in REPLICATED] + [loss_part], 8, F32)])
    g_piece = {}
    for pieces, (got, got_in), in_layer in ((early_bf, early, 1), (last_bf, last, 0)):
        summed = _unpack(_sum_blocks(got, COMM_ROWS), [shape_of(p) for p in pieces])
        g_piece.update(zip(pieces, summed))
        g_piece[("w_in_attn", in_layer)] = _sum_blocks(got_in, 256)
    g_local = {n: jnp.stack([g_piece[(n, l)] for l in range(c)]) for n, c in count.items()}
    g_local["meta_tokens"] = g_piece[("meta_tokens", None)]
    g_rep = _sum_blocks(rep, rep.shape[1])
    *g_reps, loss = _unpack(g_rep, [local[n].shape for n in REPLICATED] + [()])
    g_local.update(zip(REPLICATED, g_reps))

    def flat(src, names, rows_multiple):
        return _pack([src[n] for n in names], rows_multiple, F32)

    upd = {}
    rows = g_rep.shape[0]
    outs = _adamw(flat(local, REPLICATED, rows), g_rep, flat(mom, REPLICATED, rows), flat(var, REPLICATED, rows), rows)
    for kind, buf in zip(("delta", "m", "v"), outs):
        upd.update({(kind, n): a for n, a in zip(REPLICATED, _unpack(buf, [local[n].shape for n in REPLICATED]))})
    for n, _ in SHARDED:
        as_rows = lambda a: a.reshape(-1, a.shape[-1])
        n_rows, n_cols = as_rows(local[n]).shape
        tiles = [r for r in (1024, 512, 256, 128, 64, 32, 16, 8) if r * n_cols * 4 <= ADAMW_BLOCK_BYTES]
        outs = _adamw(as_rows(local[n]), as_rows(g_local[n]), as_rows(mom[n]), as_rows(var[n]), _pick(n_rows, tiles))
        for kind, buf in zip(("delta", "m", "v"), outs):
            upd[(kind, n)] = buf.reshape(local[n].shape)

    return (loss, dx[None], *[g_local[n] for n in WEIGHTS], *[upd[("delta", n)] for n in WEIGHTS],
            *[upd[("m", n)] for n in WEIGHTS], *[upd[("v", n)] for n in WEIGHTS])
```
